```python
import jax
import jax.numpy as jnp
from jax import lax
import numpy as np


D_MODEL = 1024
BATCH = 4
SEQ = 4096
DEPTH = 1

MEM_LEN = 256
MIX_W = D_MODEL
LRU_W = MIX_W // 2
RWKV_W = MIX_W - LRU_W
LRU_BLOCKS = 8
LRU_BLOCK = LRU_W // LRU_BLOCKS
CONV_W = 4
LRU_C = 8.0
RWKV_HEAD = 64
RWKV_HEADS = RWKV_W // RWKV_HEAD
DECAY_LORA = 64
AAA_LORA = 64
GATE_LORA = 128
RWKV_IN = 3 * RWKV_W + DECAY_LORA + AAA_LORA + GATE_LORA
IN_W = 2 * LRU_W + RWKV_IN
XA_HEADS = 4
XA_HEAD = D_MODEL // XA_HEADS
N_EXPERTS = 32
TOP_K = 4
D_FF = D_MODEL
SWIGLU_LIMIT = 7.0
SWIGLU_ALPHA = 1.702
MOE_BLOCK = 256
EPS = 1e-6
GN_EPS = 64e-5

kernel_name = 'hymba_rglru_rwkv7_xattn_moe'


def rmsnorm(x, g):
    xf = x.astype(jnp.float32)
    y = xf * lax.rsqrt(jnp.mean(xf * xf, axis=-1, keepdims=True) + EPS)
    return (y * g.astype(jnp.float32)).astype(x.dtype)


def token_shift(u):
    return jnp.pad(u, ((0, 0), (1, 0), (0, 0)))[:, :-1]


def causal_dwconv(x, w, b):
    c = x.shape[-1]
    y = lax.conv_general_dilated(x, w[:, None, :].astype(x.dtype), window_strides=(1,),
                                 padding=[(CONV_W - 1, 0)],
                                 dimension_numbers=('NWC', 'WIO', 'NWC'),
                                 feature_group_count=c)
    return y + b


def rg_lru(xc, wx, bx, wa, ba, lam):
    bsz, s, _ = xc.shape
    xb = xc.reshape(bsz, s, LRU_BLOCKS, LRU_BLOCK)
    gx = jax.nn.sigmoid(jnp.einsum('bsni,nij->bsnj', xb, wx).reshape(bsz, s, LRU_W) + bx)
    ga = jax.nn.sigmoid(jnp.einsum('bsni,nij->bsnj', xb, wa).reshape(bsz, s, LRU_W) + ba)
    log_a = -LRU_C * ga.astype(jnp.float32) * jax.nn.softplus(-lam.astype(jnp.float32))
    a = jnp.exp(log_a)
    bt = jnp.sqrt(-jnp.expm1(2.0 * log_a)) * gx.astype(jnp.float32) * xc.astype(jnp.float32)

    def combine(left, right):
        a1, b1 = left
        a2, b2 = right
        return a1 * a2, a2 * b1 + b2

    _, h = lax.associative_scan(combine, (a, bt), axis=1)
    return h.astype(xc.dtype)


def rwkv7_mix(u, mu, w0, w_up, a0, a_up, g_up, k_k, k_a, r_k, lnx_g, lnx_b):
    bsz, s, _ = u.shape
    u = u + (token_shift(u) - u) * mu
    c1 = RWKV_W
    c2 = 2 * RWKV_W
    c3 = 3 * RWKV_W
    c4 = c3 + DECAY_LORA
    c5 = c4 + AAA_LORA
    r, k, v, dw, da, dg = jnp.split(u, [c1, c2, c3, c4, c5], axis=-1)
    w = -jax.nn.softplus(-(w0 + jnp.tanh(dw) @ w_up)) - 0.5
    decay = jnp.exp(-jnp.exp(w.astype(jnp.float32)))
    a = jax.nn.sigmoid(a0 + da @ a_up)
    g = jax.nn.sigmoid(dg) @ g_up

    def hs(t):
        return t.reshape(bsz, s, RWKV_HEADS, RWKV_HEAD).astype(jnp.float32)

    kk = hs(k * k_k)
    kk = kk / jnp.maximum(jnp.sqrt(jnp.sum(kk * kk, axis=-1, keepdims=True)), 1e-12)
    k = k * (1.0 + (a - 1.0) * k_a)
    rh, kh, vh, ah, wh = hs(r), hs(k), hs(v), hs(a), hs(decay)

    def step(state, inp):
        r_t, w_t, k_t, v_t, kk_t, a_t = inp
        sa = jnp.einsum('bhvk,bhk->bhv', state, -kk_t)
        state = (state * w_t[:, :, None, :]
                 + sa[..., None] * (kk_t * a_t)[:, :, None, :]
                 + v_t[..., None] * k_t[:, :, None, :])
        y_t = jnp.einsum('bhvk,bhk->bhv', state, r_t)
        return state, y_t

    def tm(t):
        return jnp.swapaxes(t, 0, 1)

    s0 = jnp.zeros((bsz, RWKV_HEADS, RWKV_HEAD, RWKV_HEAD), jnp.float32)
    _, y = lax.scan(step, s0, (tm(rh), tm(wh), tm(kh), tm(vh), tm(kk), tm(ah)))
    y = tm(y)
    mean = jnp.mean(y, axis=-1, keepdims=True)
    var = jnp.mean(jnp.square(y - mean), axis=-1, keepdims=True)
    y = ((y - mean) * lax.rsqrt(var + GN_EPS) * lnx_g.reshape(RWKV_HEADS, RWKV_HEAD).astype(jnp.float32)
         + lnx_b.reshape(RWKV_HEADS, RWKV_HEAD).astype(jnp.float32))
    y = y + jnp.sum(rh * kh * r_k.astype(jnp.float32), axis=-1, keepdims=True) * vh
    return (y.reshape(bsz, s, RWKV_W) * g.astype(jnp.float32)).astype(u.dtype)


def cross_attn(h, mem_h, wq, wk, wv, wo):
    bsz, s, d = h.shape
    m = mem_h.shape[1]
    q = (h @ wq).reshape(bsz, s, XA_HEADS, XA_HEAD)
    k = (mem_h @ wk).reshape(bsz, m, XA_HEADS, XA_HEAD)
    v = (mem_h @ wv).reshape(bsz, m, XA_HEADS, XA_HEAD)
    sc = jnp.einsum('bshd,bmhd->bhsm', q, k).astype(jnp.float32) * (XA_HEAD ** -0.5)
    p = jax.nn.softmax(sc, axis=-1).astype(h.dtype)
    o = jnp.einsum('bhsm,bmhd->bshd', p, v).reshape(bsz, s, d)
    return o @ wo


def clamped_swiglu(gu):
    gate, up = jnp.split(gu, 2, axis=-1)
    gate = jnp.minimum(gate, SWIGLU_LIMIT)
    up = jnp.clip(up, -SWIGLU_LIMIT, SWIGLU_LIMIT)
    return (up + 1.0) * (gate * jax.nn.sigmoid(SWIGLU_ALPHA * gate))


def moe(h, w_router, b_router, w_gu, b_gu, w_dn, b_dn):
    bsz, s, d = h.shape
    t = bsz * s
    xt = h.reshape(t, d)
    logits = (xt @ w_router + b_router).astype(jnp.float32)
    top_val, top_idx = lax.top_k(logits, TOP_K)
    gates = jax.nn.softmax(top_val, axis=-1)
    n_assign = t * TOP_K
    e_flat = top_idx.reshape(n_assign)
    tok_flat = jnp.repeat(jnp.arange(t, dtype=jnp.int32), TOP_K)
    g_flat = gates.reshape(n_assign)
    order = jnp.argsort(e_flat)
    e_s = e_flat[order]
    tok_s = tok_flat[order]
    g_s = g_flat[order]
    counts = jnp.bincount(e_flat, length=N_EXPERTS)
    starts = jnp.cumsum(counts) - counts
    padded = (counts + MOE_BLOCK - 1) // MOE_BLOCK * MOE_BLOCK
    pends = jnp.cumsum(padded)
    pstarts = pends - padded
    dest = pstarts[e_s] + (jnp.arange(n_assign) - starts[e_s])
    n_blocks = (n_assign + N_EXPERTS * (MOE_BLOCK - 1) + MOE_BLOCK - 1) // MOE_BLOCK
    rows = n_blocks * MOE_BLOCK
    buf_tok = jnp.full((rows,), t, jnp.int32).at[dest].set(tok_s)
    buf_g = jnp.zeros((rows,), jnp.float32).at[dest].set(g_s)
    block_e = jnp.minimum(jnp.searchsorted(pends, jnp.arange(n_blocks) * MOE_BLOCK, side='right'),
                          N_EXPERTS - 1).astype(jnp.int32)
    x_pad = jnp.concatenate([xt, jnp.zeros((1, d), xt.dtype)], axis=0)

    def block_fn(args):
        tok_b, g_b, e = args
        xb = x_pad[tok_b]
        act = clamped_swiglu(xb @ w_gu[e] + b_gu[e])
        y = act @ w_dn[e] + b_dn[e]
        return y * g_b[:, None].astype(y.dtype)

    ys = lax.map(block_fn, (buf_tok.reshape(n_blocks, MOE_BLOCK),
                            buf_g.reshape(n_blocks, MOE_BLOCK), block_e))
    out = jnp.zeros((t + 1, d), ys.dtype).at[buf_tok].add(ys.reshape(rows, d))[:t]
    return out.reshape(bsz, s, d).astype(h.dtype)


def setup_inputs(seed: int = 0) -> dict:
    key = jax.random.key(seed)
    ks = iter(jax.random.split(key, 48))
    f32 = jnp.float32

    def nrm(shape, scale):
        return jax.random.normal(next(ks), shape, f32) * scale

    def gain(shape):
        return 1.0 + 0.05 * jax.random.normal(next(ks), shape, f32)

    L = DEPTH
    u_lam = jax.random.uniform(next(ks), (L, LRU_W), f32, 0.9, 0.999)
    a_base = u_lam ** (1.0 / LRU_C)
    lru_lambda = jnp.log(a_base) - jnp.log1p(-a_base)
    return {
        'x': nrm((BATCH, SEQ, D_MODEL), 1.0),
        'mem': nrm((BATCH, MEM_LEN, D_MODEL), 1.0),
        'norm_mix_g': gain((L, D_MODEL)),
        'w_in': nrm((L, D_MODEL, IN_W), D_MODEL ** -0.5),
        'conv_w': nrm((L, CONV_W, LRU_W), CONV_W ** -0.5),
        'conv_b': nrm((L, LRU_W), 0.02),
        'lru_wx': nrm((L, LRU_BLOCKS, LRU_BLOCK, LRU_BLOCK), LRU_BLOCK ** -0.5),
        'lru_bx': nrm((L, LRU_W), 0.02),
        'lru_wa': nrm((L, LRU_BLOCKS, LRU_BLOCK, LRU_BLOCK), LRU_BLOCK ** -0.5),
        'lru_ba': nrm((L, LRU_W), 0.02),
        'lru_lambda': lru_lambda,
        'rwkv_mu': jax.random.uniform(next(ks), (L, RWKV_IN), f32, 0.0, 1.0),
        'rwkv_w0': jax.random.uniform(next(ks), (L, RWKV_W), f32, -4.0, 1.0),
        'rwkv_w_up': nrm((L, DECAY_LORA, RWKV_W), 0.1 * DECAY_LORA ** -0.5),
        'rwkv_a0': nrm((L, RWKV_W), 0.5),
        'rwkv_a_up': nrm((L, AAA_LORA, RWKV_W), 0.1 * AAA_LORA ** -0.5),
        'rwkv_g_up': nrm((L, GATE_LORA, RWKV_W), GATE_LORA ** -0.5),
        'rwkv_k_k': 0.85 + 0.05 * jax.random.normal(next(ks), (L, RWKV_W), f32),
        'rwkv_k_a': gain((L, RWKV_W)),
        'rwkv_r_k': nrm((L, RWKV_HEADS, RWKV_HEAD), 0.1),
        'rwkv_lnx_g': gain((L, RWKV_W)),
        'rwkv_lnx_b': nrm((L, RWKV_W), 0.02),
        'w_out': nrm((L, MIX_W, D_MODEL), MIX_W ** -0.5),
        'norm_xa_g': gain((L, D_MODEL)),
        'norm_mem_g': gain((L, D_MODEL)),
        'xa_wq': nrm((L, D_MODEL, D_MODEL), D_MODEL ** -0.5),
        'xa_wk': nrm((L, D_MODEL, D_MODEL), D_MODEL ** -0.5),
        'xa_wv': nrm((L, D_MODEL, D_MODEL), D_MODEL ** -0.5),
        'xa_wo': nrm((L, D_MODEL, D_MODEL), D_MODEL ** -0.5),
        'norm_ffn_g': gain((L, D_MODEL)),
        'w_router': nrm((L, D_MODEL, N_EXPERTS), D_MODEL ** -0.5),
        'b_router': nrm((L, N_EXPERTS), 0.01),
        'w_gu': nrm((L, N_EXPERTS, D_MODEL, 2 * D_FF), D_MODEL ** -0.5),
        'b_gu': nrm((L, N_EXPERTS, 2 * D_FF), 0.02),
        'w_dn': nrm((L, N_EXPERTS, D_FF, D_MODEL), D_FF ** -0.5),
        'b_dn': nrm((L, N_EXPERTS, D_MODEL), 0.02),
        'final_norm_g': gain((D_MODEL,)),
    }


def reference(x, mem, norm_mix_g, w_in, conv_w, conv_b, lru_wx, lru_bx, lru_wa, lru_ba,
              lru_lambda, rwkv_mu, rwkv_w0, rwkv_w_up, rwkv_a0, rwkv_a_up, rwkv_g_up,
              rwkv_k_k, rwkv_k_a, rwkv_r_k, rwkv_lnx_g, rwkv_lnx_b, w_out, norm_xa_g,
              norm_mem_g, xa_wq, xa_wk, xa_wv, xa_wo, norm_ffn_g, w_router, b_router,
              w_gu, b_gu, w_dn, b_dn, final_norm_g):
    for l in range(DEPTH):
        h = rmsnorm(x, norm_mix_g[l])
        u = h @ w_in[l]
        x_lru, g_lru, u_rwkv = jnp.split(u, [LRU_W, 2 * LRU_W], axis=-1)
        xc = causal_dwconv(x_lru, conv_w[l], conv_b[l])
        y_lru = rg_lru(xc, lru_wx[l], lru_bx[l], lru_wa[l], lru_ba[l], lru_lambda[l]) * jax.nn.gelu(g_lru)
        y_rwkv = rwkv7_mix(u_rwkv, rwkv_mu[l], rwkv_w0[l], rwkv_w_up[l], rwkv_a0[l], rwkv_a_up[l],
                           rwkv_g_up[l], rwkv_k_k[l], rwkv_k_a[l], rwkv_r_k[l],
                           rwkv_lnx_g[l], rwkv_lnx_b[l])
        x = x + jnp.concatenate([y_lru, y_rwkv], axis=-1) @ w_out[l]
        x = x + cross_attn(rmsnorm(x, norm_xa_g[l]), rmsnorm(mem, norm_mem_g[l]),
                           xa_wq[l], xa_wk[l], xa_wv[l], xa_wo[l])
        x = x + moe(rmsnorm(x, norm_ffn_g[l]), w_router[l], b_router[l],
                    w_gu[l], b_gu[l], w_dn[l], b_dn[l])
    return rmsnorm(x, final_norm_g)
```

```python
import functools

import jax
import jax.numpy as jnp
from jax import lax
from jax.experimental import pallas as pl
from jax.experimental.pallas import tpu as pltpu

f32 = jnp.float32
bf16 = jnp.bfloat16
i32 = jnp.int32
HI = lax.Precision.HIGHEST

D_MODEL = 1024
LRU_W = 512
RWKV_W = 512
LRU_BLOCKS = 8
LRU_BLOCK = 64
CONV_W = 4
LRU_C = 8.0
RWKV_HEAD = 64
DECAY_LORA = 64
AAA_LORA = 64
GATE_LORA = 128
RWKV_IN = 3 * RWKV_W + DECAY_LORA + AAA_LORA + GATE_LORA
XA_HEADS = 4
XA_HEAD = D_MODEL // XA_HEADS
N_EXPERTS = 32
TOP_K = 4
D_FF = D_MODEL
SWIGLU_LIMIT = 7.0
SWIGLU_ALPHA = 1.702
EPS = 1e-6
GN_EPS = 64e-5

LANES = 128
SUBLANES = 8
RWKV_CHUNK = 64
PAIR = 2 * RWKV_HEAD
N_PAIRS = RWKV_W // PAIR
MOE_BLOCK = 256
VMEM_LIMIT = 52 * 1024 * 1024


def _cparams(sem):
    return pltpu.CompilerParams(dimension_semantics=sem, vmem_limit_bytes=VMEM_LIMIT)


def _rms(x, g):
    return x * lax.rsqrt(jnp.mean(x * x, axis=-1, keepdims=True) + EPS) * g


def _full(shape):
    n = len(shape)
    return pl.BlockSpec(shape, lambda *a: (0,) * n)


def _shift_rows(x, prev8, d):
    xr = pltpu.roll(x, d, 0)
    tr = pltpu.roll(prev8, d, 0)
    row = lax.broadcasted_iota(i32, prev8.shape, 0)
    head = jnp.where(row < d, tr, xr[:SUBLANES])
    return jnp.concatenate([head, xr[SUBLANES:]], axis=0)


def _inproj_body(x_ref, g_ref, w_ref, xl_ref, gl_ref, ur_ref):
    h = _rms(x_ref[...], g_ref[...])
    u = jnp.dot(h.astype(bf16), w_ref[...], preferred_element_type=f32)
    xl_ref[...] = u[:, :LRU_W]
    gl_ref[...] = u[:, LRU_W:2 * LRU_W]
    ur_ref[...] = u[:, 2 * LRU_W:]


def _inproj(x2d, g, w_in_bf, tm):
    t = x2d.shape[0]
    return pl.pallas_call(
        _inproj_body,
        grid=(t // tm,),
        in_specs=[pl.BlockSpec((tm, D_MODEL), lambda i: (i, 0)), _full((1, D_MODEL)),
                  _full(w_in_bf.shape)],
        out_specs=[pl.BlockSpec((tm, LRU_W), lambda i: (i, 0)),
                   pl.BlockSpec((tm, LRU_W), lambda i: (i, 0)),
                   pl.BlockSpec((tm, RWKV_IN), lambda i: (i, 0))],
        out_shape=[jax.ShapeDtypeStruct((t, LRU_W), f32), jax.ShapeDtypeStruct((t, LRU_W), f32),
                   jax.ShapeDtypeStruct((t, RWKV_IN), f32)],
        compiler_params=_cparams(("parallel",)),
        name="inproj",
    )(x2d, g, w_in_bf)


def _lru_body(xl_ref, gl_ref, cw_ref, cb_ref, wg_ref, bg_ref, lam_ref, o_ref, tail_ref, h_ref):
    ts = xl_ref.shape[0]

    @pl.when(pl.program_id(1) == 0)
    def _():
        tail_ref[...] = jnp.zeros_like(tail_ref)
        h_ref[...] = jnp.zeros_like(h_ref)

    x = xl_ref[...]
    tail = tail_ref[...]
    cw = cw_ref[...]
    xc = cb_ref[...] + cw[CONV_W - 1:CONV_W] * x
    for d in range(1, CONV_W):
        xc = xc + cw[CONV_W - 1 - d:CONV_W - d] * _shift_rows(x, tail, d)
    tail_ref[...] = x[ts - SUBLANES:]

    gates = jax.nn.sigmoid(jnp.dot(xc.astype(bf16), wg_ref[...], preferred_element_type=f32) + bg_ref[...])
    gx = gates[:, :LRU_W]
    ga = gates[:, LRU_W:]
    log_a = -LRU_C * ga * jax.nn.softplus(-lam_ref[...])
    a = jnp.exp(log_a)
    b = jnp.sqrt(-jnp.tanh(log_a) * (a * a + 1.0)) * gx * xc

    row = lax.broadcasted_iota(i32, (ts, LRU_W), 0)
    d = 1
    while d < ts:
        keep = row >= d
        a_s = jnp.where(keep, pltpu.roll(a, d, 0), 1.0)
        b_s = jnp.where(keep, pltpu.roll(b, d, 0), 0.0)
        b = a * b_s + b
        a = a * a_s
        d *= 2
    h = b + a * h_ref[SUBLANES - 1:SUBLANES, :]
    h_ref[...] = h[ts - SUBLANES:]
    o_ref[...] = (h * jax.nn.gelu(gl_ref[...])).astype(o_ref.dtype)


def _lru(xl, gl, cw, cb, wg_bf, bg, lam, bsz, s, ts):
    nt = s // ts
    blk = pl.BlockSpec((ts, LRU_W), lambda b, i: (b * nt + i, 0))
    return pl.pallas_call(
        _lru_body,
        grid=(bsz, nt),
        in_specs=[blk, blk, _full(cw.shape), _full(cb.shape), _full(wg_bf.shape), _full(bg.shape),
                  _full(lam.shape)],
        out_specs=blk,
        out_shape=jax.ShapeDtypeStruct((bsz * s, LRU_W), bf16),
        scratch_shapes=[pltpu.VMEM((SUBLANES, LRU_W), f32), pltpu.VMEM((SUBLANES, LRU_W), f32)],
        compiler_params=_cparams(("parallel", "arbitrary")),
        name="lru",
    )(xl, gl, cw, cb, wg_bf, bg, lam)


def _rwkv_prep_body(ur_ref, mu_ref, w0_ref, a0_ref, wl_ref, gup_ref, kk_ref, ka_ref, rk_ref, ones_ref,
                    at_ref, bt_ref, kt_ref, rt_ref, v_ref, cl_ref, bon_ref, g_ref, prev_ref):
    ts = ur_ref.shape[0]

    @pl.when(pl.program_id(1) == 0)
    def _():
        prev_ref[...] = jnp.zeros_like(prev_ref)

    u0 = ur_ref[...]
    us = _shift_rows(u0, prev_ref[...], 1)
    prev_ref[...] = u0[ts - SUBLANES:]
    u = u0 + (us - u0) * mu_ref[...]
    c1, c2, c3 = RWKV_W, 2 * RWKV_W, 3 * RWKV_W
    r = u[:, :c1]
    k = u[:, c1:c2]
    v = u[:, c2:c3]
    lora = u[:, c3:c3 + LANES]
    dg = u[:, c3 + LANES:]
    lane = lax.broadcasted_iota(i32, lora.shape, 1)
    lora = jnp.where(lane < DECAY_LORA, jnp.tanh(lora), lora)
    proj = jnp.dot(lora.astype(bf16), wl_ref[...], preferred_element_type=f32)
    w = -jax.nn.softplus(-(w0_ref[...] + proj[:, :RWKV_W])) - 0.5
    lw = -jnp.exp(w)
    a = jax.nn.sigmoid(a0_ref[...] + proj[:, RWKV_W:])
    g_ref[...] = jnp.dot(jax.nn.sigmoid(dg).astype(bf16), gup_ref[...], preferred_element_type=f32)

    ones = ones_ref[...]
    kk = k * kk_ref[...]
    ss = jnp.dot(kk * kk, ones, precision=HI, preferred_element_type=f32)
    kk = kk / jnp.maximum(jnp.sqrt(ss), 1e-12)
    k2 = k * (1.0 + (a - 1.0) * ka_ref[...])
    bon_ref[...] = jnp.dot(r * k2 * rk_ref[...], ones, precision=HI, preferred_element_type=f32)

    ri = lax.broadcasted_iota(i32, (ts, ts), 0)
    ci = lax.broadcasted_iota(i32, (ts, ts), 1)
    tri = jnp.where((ci <= ri) & (ri // RWKV_CHUNK == ci // RWKV_CHUNK), 1.0, 0.0).astype(f32)
    cl = jnp.dot(tri, lw, precision=HI, preferred_element_type=f32)
    e_neg = jnp.exp(-cl)
    at_ref[...] = -kk * jnp.exp(cl - lw)
    bt_ref[...] = kk * a * e_neg
    kt_ref[...] = k2 * e_neg
    rt_ref[...] = r * jnp.exp(cl)
    v_ref[...] = v
    cl_ref[...] = cl


def _rwkv_prep(ur, mu, w0, a0, wl_bf, gup_bf, k_k, k_a, r_k, ones_blk, bsz, s, ts):
    nt = s // ts
    t = bsz * s
    oblk = pl.BlockSpec((ts, RWKV_W), lambda b, i: (b * nt + i, 0))
    osh = jax.ShapeDtypeStruct((t, RWKV_W), f32)
    return pl.pallas_call(
        _rwkv_prep_body,
        grid=(bsz, nt),
        in_specs=[pl.BlockSpec((ts, RWKV_IN), lambda b, i: (b * nt + i, 0)), _full(mu.shape),
                  _full(w0.shape), _full(a0.shape), _full(wl_bf.shape), _full(gup_bf.shape),
                  _full(k_k.shape), _full(k_a.shape), _full(r_k.shape), _full(ones_blk.shape)],
        out_specs=[oblk] * 8,
        out_shape=[osh] * 8,
        scratch_shapes=[pltpu.VMEM((SUBLANES, RWKV_IN), f32)],
        compiler_params=_cparams(("parallel", "arbitrary")),
        name="rwkv_prep",
    )(ur, mu, w0, a0, wl_bf, gup_bf, k_k, k_a, r_k, ones_blk)


def _mm(a, b):
    return jnp.dot(a, b, precision=HI, preferred_element_type=f32)


def _mm_nt(a, b):
    return lax.dot_general(a, b, (((1,), (1,)), ((), ())), precision=HI, preferred_element_type=f32)


def _mm_tn(a, b):
    return lax.dot_general(a, b, (((0,), (0,)), ((), ())), precision=HI, preferred_element_type=f32)


def _stack_heads(x, m0):
    return jnp.concatenate([jnp.where(m0, x, 0.0), jnp.where(m0, 0.0, x)], axis=0)


def _unstack(x):
    c = x.shape[0] // 2
    return x[:c] + x[c:]


def _chunk_maps(at, bt, kt, rt, v, cl):
    c = RWKV_CHUNK
    n = 2 * c
    lane = lax.broadcasted_iota(i32, (c, PAIR), 1)
    m0 = lane < RWKV_HEAD
    pc = jnp.exp(cl[c - 1:c, :])
    a_s = _stack_heads(at, m0)
    r_s = _stack_heads(rt, m0)
    b_s = _stack_heads(bt, m0)
    k_s = _stack_heads(kt, m0)
    v_s = _stack_heads(v, m0)
    ri = lax.broadcasted_iota(i32, (n, n), 0)
    ci = lax.broadcasted_iota(i32, (n, n), 1)
    strict = ci < ri
    incl = ci <= ri
    eye = jnp.where(ri == ci, 1.0, 0.0).astype(f32)

    aa = _mm_nt(jnp.concatenate([a_s, r_s], axis=0), jnp.concatenate([b_s, k_s], axis=0))
    l_ab = jnp.where(strict, aa[:n, :n], 0.0)
    a_ak = jnp.where(strict, aa[:n, n:], 0.0)
    a_rb = jnp.where(incl, aa[n:, :n], 0.0)
    a_rk = jnp.where(incl, aa[n:, n:], 0.0)

    tinv = eye + l_ab
    lp = l_ab
    p = 2
    while p < c:
        lp = _mm(lp, lp)
        tinv = tinv + _mm(tinv, lp)
        p *= 2

    w_pre = _mm(a_ak, v_s)
    za = _mm(tinv, jnp.concatenate([w_pre, a_s], axis=1))
    zp_s = za[:, :PAIR]
    ac_s = za[:, PAIR:]
    yp = _unstack(_mm(jnp.concatenate([a_rk, a_rb], axis=1), jnp.concatenate([v_s, zp_s], axis=0)))
    rc = rt + _unstack(_mm(a_rb, ac_s))
    bh = b_s * pc
    kh = k_s * pc
    sm = _mm_tn(bh, jnp.concatenate([zp_s, ac_s], axis=1))
    sp = _mm_tn(kh, v_s) + sm[:, :PAIR]
    g = jnp.where(ri == ci, jnp.broadcast_to(pc, (n, n)), 0.0) + sm[:, PAIR:]
    return g, sp, rc, yp


def _rwkv_chunk_body(at_ref, bt_ref, kt_ref, rt_ref, v_ref, cl_ref, g_ref, sp_ref, rc_ref, yp_ref):
    ts = at_ref.shape[0]
    c = RWKV_CHUNK
    for j in range(ts // c):
        sl = slice(j * c, (j + 1) * c)
        g, sp, rc, yp = _chunk_maps(at_ref[sl, :], bt_ref[sl, :], kt_ref[sl, :], rt_ref[sl, :],
                                    v_ref[sl, :], cl_ref[sl, :])
        g_ref[j, 0] = g
        sp_ref[j, 0] = sp
        rc_ref[sl, :] = rc
        yp_ref[sl, :] = yp


def _rwkv_chunk(at, bt, kt, rt, v, cl, ts):
    t = at.shape[0]
    nck = ts // RWKV_CHUNK
    iblk = pl.BlockSpec((ts, PAIR), lambda i, p: (i, p))
    mblk = pl.BlockSpec((nck, 1, PAIR, PAIR), lambda i, p: (i, p, 0, 0))
    msh = jax.ShapeDtypeStruct((t // RWKV_CHUNK, N_PAIRS, PAIR, PAIR), f32)
    osh = jax.ShapeDtypeStruct((t, RWKV_W), f32)
    return pl.pallas_call(
        _rwkv_chunk_body,
        grid=(t // ts, N_PAIRS),
        in_specs=[iblk] * 6,
        out_specs=[mblk, mblk, iblk, iblk],
        out_shape=[msh, msh, osh, osh],
        compiler_params=_cparams(("parallel", "parallel")),
        name="rwkv_chunk",
    )(at, bt, kt, rt, v, cl)


def _rwkv_state_body(g_ref, sp_ref, rc_ref, yp_ref, y_ref, s_ref):
    @pl.when(pl.program_id(1) == 0)
    def _():
        s_ref[...] = jnp.zeros_like(s_ref)

    c = RWKV_CHUNK
    nck = g_ref.shape[0]
    for j in range(nck):
        sl = slice(j * c, (j + 1) * c)
        for p in range(N_PAIRS):
            ls = slice(p * PAIR, (p + 1) * PAIR)
            s = s_ref[p]
            y_ref[sl, ls] = yp_ref[sl, ls] + _mm(rc_ref[sl, ls], s)
            s_ref[p] = _mm(g_ref[j, p], s) + sp_ref[j, p]


def _rwkv_state(g, sp, rc, yp, bsz, s, nck):
    t = bsz * s
    ns = s // (nck * RWKV_CHUNK)
    rows = nck * RWKV_CHUNK
    mblk = pl.BlockSpec((nck, N_PAIRS, PAIR, PAIR), lambda b, i: (b * ns + i, 0, 0, 0))
    rblk = pl.BlockSpec((rows, RWKV_W), lambda b, i: (b * ns + i, 0))
    return pl.pallas_call(
        _rwkv_state_body,
        grid=(bsz, ns),
        in_specs=[mblk, mblk, rblk, rblk],
        out_specs=rblk,
        out_shape=jax.ShapeDtypeStruct((t, RWKV_W), f32),
        scratch_shapes=[pltpu.VMEM((N_PAIRS, PAIR, PAIR), f32)],
        compiler_params=_cparams(("parallel", "arbitrary")),
        name="rwkv_state",
    )(g, sp, rc, yp)


def _outproj_body(x_ref, yl_ref, ys_ref, bon_ref, v_ref, g_ref, lg_ref, lb_ref, ones_ref, w_ref, o_ref):
    y = ys_ref[...]
    ones = ones_ref[...]
    inv_n = 1.0 / RWKV_HEAD
    mean = jnp.dot(y, ones, precision=HI, preferred_element_type=f32) * inv_n
    yc = y - mean
    var = jnp.dot(yc * yc, ones, precision=HI, preferred_element_type=f32) * inv_n
    yn = yc * lax.rsqrt(var + GN_EPS) * lg_ref[...] + lb_ref[...]
    yr = (yn + bon_ref[...] * v_ref[...]) * g_ref[...]
    cat = jnp.concatenate([yl_ref[...], yr.astype(bf16)], axis=1)
    o_ref[...] = x_ref[...] + jnp.dot(cat, w_ref[...], preferred_element_type=f32)


def _outproj(x2d, y_lru, y_scan, bon, v, g, lnx_g, lnx_b, ones_blk, w_out_bf, tm):
    t = x2d.shape[0]
    xb = pl.BlockSpec((tm, D_MODEL), lambda i: (i, 0))
    hb = pl.BlockSpec((tm, RWKV_W), lambda i: (i, 0))
    return pl.pallas_call(
        _outproj_body,
        grid=(t // tm,),
        in_specs=[xb, hb, hb, hb, hb, hb, _full(lnx_g.shape), _full(lnx_b.shape), _full(ones_blk.shape),
                  _full(w_out_bf.shape)],
        out_specs=xb,
        out_shape=jax.ShapeDtypeStruct((t, D_MODEL), f32),
        compiler_params=_cparams(("parallel",)),
        name="outproj",
    )(x2d, y_lru, y_scan, bon, v, g, lnx_g, lnx_b, ones_blk, w_out_bf)


def _memkv_body(m_ref, g_ref, wk_ref, wv_ref, k_ref, v_ref):
    h = _rms(m_ref[...], g_ref[...]).astype(bf16)
    k_ref[...] = jnp.dot(h, wk_ref[...], preferred_element_type=f32).astype(bf16)
    v_ref[...] = jnp.dot(h, wv_ref[...], preferred_element_type=f32).astype(bf16)


def _memkv(mem2d, g, wk_bf, wv_bf, tm):
    t = mem2d.shape[0]
    blk = pl.BlockSpec((tm, D_MODEL), lambda i: (i, 0))
    sh = jax.ShapeDtypeStruct((t, D_MODEL), bf16)
    return pl.pallas_call(
        _memkv_body,
        grid=(t // tm,),
        in_specs=[blk, _full(g.shape), _full(wk_bf.shape), _full(wv_bf.shape)],
        out_specs=[blk, blk],
        out_shape=[sh, sh],
        compiler_params=_cparams(("parallel",)),
        name="memkv",
    )(mem2d, g, wk_bf, wv_bf)


def _xattn_body(x_ref, k_ref, v_ref, gx_ref, wq_ref, wo_ref, gf_ref, wr_ref, br_ref,
                x2_ref, hf_ref, idx_ref, gate_ref):
    x = x_ref[...]
    h = _rms(x, gx_ref[...]).astype(bf16)
    q = jnp.dot(h, wq_ref[...], preferred_element_type=f32).astype(bf16)
    k = k_ref[...]
    v = v_ref[...]
    outs = []
    for hd in range(XA_HEADS):
        sl = slice(hd * XA_HEAD, (hd + 1) * XA_HEAD)
        sc = lax.dot_general(q[:, sl], k[:, sl], (((1,), (1,)), ((), ())),
                             preferred_element_type=f32) * (XA_HEAD ** -0.5)
        sc = sc - jnp.max(sc, axis=-1, keepdims=True)
        e = jnp.exp(sc)
        p = e / jnp.sum(e, axis=-1, keepdims=True)
        outs.append(jnp.dot(p.astype(bf16), v[:, sl], preferred_element_type=f32).astype(bf16))
    o = jnp.concatenate(outs, axis=1)
    x2 = x + jnp.dot(o, wo_ref[...], preferred_element_type=f32)
    x2_ref[...] = x2

    hf = _rms(x2, gf_ref[...])
    hf_ref[...] = hf
    logits = jnp.dot(hf, wr_ref[...], precision=HI, preferred_element_type=f32) + br_ref[...]
    lane = lax.broadcasted_iota(i32, logits.shape, 1)
    neg = jnp.float32(-jnp.inf)
    cur = jnp.where(lane < N_EXPERTS, logits, neg)
    vals = []
    idxs = []
    for _ in range(TOP_K):
        m = jnp.max(cur, axis=-1, keepdims=True)
        am = jnp.min(jnp.where(cur == m, lane, LANES), axis=-1, keepdims=True)
        vals.append(m)
        idxs.append(am)
        cur = jnp.where(lane == am, neg, cur)
    es = [jnp.exp(vk - vals[0]) for vk in vals]
    den = es[0] + es[1] + es[2] + es[3]
    idx_out = jnp.zeros(logits.shape, i32)
    gate_out = jnp.zeros(logits.shape, f32)
    for kq in range(TOP_K):
        idx_out = jnp.where(lane == kq, idxs[kq], idx_out)
        gate_out = jnp.where(lane == kq, es[kq] / den, gate_out)
    idx_ref[...] = idx_out
    gate_ref[...] = gate_out


def _xattn(x1, kmem, vmem, g_xa, wq_bf, wo_bf, g_ffn, wr_pad, br_pad, bsz, s, mlen, tm):
    t = bsz * s
    nt = s // tm
    xb = pl.BlockSpec((tm, D_MODEL), lambda i: (i, 0))
    mb = pl.BlockSpec((mlen, D_MODEL), lambda i: (i // nt, 0))
    lb = pl.BlockSpec((tm, LANES), lambda i: (i, 0))
    return pl.pallas_call(
        _xattn_body,
        grid=(t // tm,),
        in_specs=[xb, mb, mb, _full(g_xa.shape), _full(wq_bf.shape), _full(wo_bf.shape),
                  _full(g_ffn.shape), _full(wr_pad.shape), _full(br_pad.shape)],
        out_specs=[xb, xb, lb, lb],
        out_shape=[jax.ShapeDtypeStruct((t, D_MODEL), f32), jax.ShapeDtypeStruct((t, D_MODEL), f32),
                   jax.ShapeDtypeStruct((t, LANES), i32), jax.ShapeDtypeStruct((t, LANES), f32)],
        compiler_params=_cparams(("parallel",)),
        name="xattn",
    )(x1, kmem, vmem, g_xa, wq_bf, wo_bf, g_ffn, wr_pad, br_pad)


def _row_gather_start(tok_ref, base, src_hbm, dst, sem, n):
    def body(r, carry):
        pltpu.make_async_copy(src_hbm.at[pl.ds(tok_ref[base + r], 1), :], dst.at[pl.ds(r, 1), :], sem).start()
        return carry
    lax.fori_loop(0, n, body, 0)


def _moe_body(be_ref, nb_ref, tok_ref, x_hbm, g_ref, wgu_ref, bgu_ref, wdn_ref, bdn_ref, o_ref,
              xbuf, wgu_bf, wdn_bf, sem):
    i = pl.program_id(0)
    n_used = nb_ref[0]
    slot = i % 2

    @pl.when(i == 0)
    def _():
        _row_gather_start(tok_ref, 0, x_hbm, xbuf.at[0], sem.at[0], MOE_BLOCK)

    @pl.when(i + 1 < n_used)
    def _():
        _row_gather_start(tok_ref, (i + 1) * MOE_BLOCK, x_hbm, xbuf.at[1 - slot], sem.at[1 - slot], MOE_BLOCK)

    @pl.when(i >= n_used)
    def _():
        o_ref[...] = jnp.zeros_like(o_ref)

    @pl.when(i < n_used)
    def _():
        prev_e = be_ref[jnp.maximum(i - 1, 0)]

        @pl.when((i == 0) | (be_ref[i] != prev_e))
        def _():
            wgu_bf[...] = wgu_ref[0].astype(bf16)
            wdn_bf[...] = wdn_ref[0].astype(bf16)

        pltpu.make_async_copy(x_hbm.at[pl.ds(0, MOE_BLOCK), :], xbuf.at[slot], sem.at[slot]).wait()
        xb = xbuf[slot].astype(bf16)
        gu = jnp.dot(xb, wgu_bf[...], preferred_element_type=f32) + bgu_ref[0]
        gate = jnp.minimum(gu[:, :D_FF], SWIGLU_LIMIT)
        up = jnp.clip(gu[:, D_FF:], -SWIGLU_LIMIT, SWIGLU_LIMIT)
        act = (up + 1.0) * (gate * jax.nn.sigmoid(SWIGLU_ALPHA * gate))
        y = jnp.dot(act.astype(bf16), wdn_bf[...], preferred_element_type=f32) + bdn_ref[0]
        o_ref[...] = y * g_ref[...]


def _moe(block_e, n_used, buf_tok, hf_pad, buf_g, w_gu, b_gu, w_dn, b_dn, n_blocks):
    rows = n_blocks * MOE_BLOCK
    grid_spec = pltpu.PrefetchScalarGridSpec(
        num_scalar_prefetch=3,
        grid=(n_blocks,),
        in_specs=[
            pl.BlockSpec(memory_space=pl.ANY),
            pl.BlockSpec((MOE_BLOCK, 1), lambda i, be, nb, tk: (i, 0)),
            pl.BlockSpec((1, D_MODEL, 2 * D_FF), lambda i, be, nb, tk: (be[i], 0, 0)),
            pl.BlockSpec((1, 1, 2 * D_FF), lambda i, be, nb, tk: (be[i], 0, 0)),
            pl.BlockSpec((1, D_FF, D_MODEL), lambda i, be, nb, tk: (be[i], 0, 0)),
            pl.BlockSpec((1, 1, D_MODEL), lambda i, be, nb, tk: (be[i], 0, 0)),
        ],
        out_specs=pl.BlockSpec((MOE_BLOCK, D_MODEL), lambda i, be, nb, tk: (i, 0)),
        scratch_shapes=[pltpu.VMEM((2, MOE_BLOCK, D_MODEL), f32),
                        pltpu.VMEM((D_MODEL, 2 * D_FF), bf16),
                        pltpu.VMEM((D_FF, D_MODEL), bf16),
                        pltpu.SemaphoreType.DMA((2,))],
    )
    return pl.pallas_call(
        _moe_body,
        grid_spec=grid_spec,
        out_shape=jax.ShapeDtypeStruct((rows, D_MODEL), f32),
        compiler_params=_cparams(("arbitrary",)),
        name="moe",
    )(block_e, n_used, buf_tok, hf_pad, buf_g, w_gu, b_gu.reshape(N_EXPERTS, 1, 2 * D_FF), w_dn,
      b_dn.reshape(N_EXPERTS, 1, D_MODEL))


def _combine_body(pos_ref, ys_hbm, x_ref, g_ref, o_ref, buf, sem):
    i = pl.program_id(0)
    n = pl.num_programs(0)
    tc = x_ref.shape[0]
    slot = i % 2

    def start(step, sl):
        for kq in range(TOP_K):
            _row_gather_start(pos_ref, (kq * n + step) * tc, ys_hbm, buf.at[sl, kq], sem.at[sl], tc)

    @pl.when(i == 0)
    def _():
        start(0, 0)

    @pl.when(i + 1 < n)
    def _():
        start(i + 1, 1 - slot)

    for kq in range(TOP_K):
        pltpu.make_async_copy(ys_hbm.at[pl.ds(0, tc), :], buf.at[slot, kq], sem.at[slot]).wait()
    acc = x_ref[...] + ((buf[slot, 0] + buf[slot, 1]) + (buf[slot, 2] + buf[slot, 3]))
    o_ref[...] = _rms(acc, g_ref[...])


def _combine(pos_km, ys, x2, g_final, tc):
    t = x2.shape[0]
    grid_spec = pltpu.PrefetchScalarGridSpec(
        num_scalar_prefetch=1,
        grid=(t // tc,),
        in_specs=[pl.BlockSpec(memory_space=pl.ANY),
                  pl.BlockSpec((tc, D_MODEL), lambda i, p: (i, 0)),
                  pl.BlockSpec((1, D_MODEL), lambda i, p: (0, 0))],
        out_specs=pl.BlockSpec((tc, D_MODEL), lambda i, p: (i, 0)),
        scratch_shapes=[pltpu.VMEM((2, TOP_K, tc, D_MODEL), f32), pltpu.SemaphoreType.DMA((2,))],
    )
    return pl.pallas_call(
        _combine_body,
        grid_spec=grid_spec,
        out_shape=jax.ShapeDtypeStruct((t, D_MODEL), f32),
        compiler_params=_cparams(("arbitrary",)),
        name="combine",
    )(pos_km, ys, x2, g_final)


def _routing(top_idx, gates, t):
    n_assign = t * TOP_K
    e_flat = top_idx.reshape(n_assign)
    onehot = (e_flat[:, None] == jnp.arange(N_EXPERTS, dtype=i32)[None, :]).astype(i32)
    csum = jnp.cumsum(onehot, axis=0)
    rank = jnp.sum(csum * onehot, axis=1) - 1
    counts = csum[-1]
    padded = (counts + MOE_BLOCK - 1) // MOE_BLOCK * MOE_BLOCK
    pends = jnp.cumsum(padded)
    pstarts = pends - padded
    dest = (pstarts[e_flat] + rank).astype(i32)
    n_blocks = (n_assign + N_EXPERTS * (MOE_BLOCK - 1) + MOE_BLOCK - 1) // MOE_BLOCK
    rows = n_blocks * MOE_BLOCK
    tok_flat = jnp.arange(n_assign, dtype=i32) // TOP_K
    buf_tok = jnp.zeros((rows,), i32).at[dest].set(tok_flat)
    buf_g = jnp.zeros((rows,), f32).at[dest].set(gates.reshape(n_assign))
    block_e = jnp.minimum(jnp.searchsorted(pends, jnp.arange(n_blocks, dtype=i32) * MOE_BLOCK, side='right'),
                          N_EXPERTS - 1).astype(i32)
    n_used = (pends[-1] // MOE_BLOCK).astype(i32).reshape(1)
    pos_km = dest.reshape(t, TOP_K).T.reshape(n_assign)
    return block_e, n_used, buf_tok, buf_g.reshape(rows, 1), pos_km, n_blocks


def _block_diag(w):
    n, bi, bj = w.shape
    eye = jnp.eye(n, dtype=w.dtype)
    return jnp.einsum('nij,nm->nimj', w, eye).reshape(n * bi, n * bj)


def _layer(x2d, mem2d, bsz, s, mlen, p):
    t = bsz * s
    row = lambda a: a.reshape(1, -1)
    ones_blk = _block_diag(jnp.ones((RWKV_W // RWKV_HEAD, RWKV_HEAD, RWKV_HEAD), f32))

    xl, gl, ur = _inproj(x2d, row(p['norm_mix_g']), p['w_in'].astype(bf16), tm=min(512, t))

    wg = jnp.concatenate([_block_diag(p['lru_wx']), _block_diag(p['lru_wa'])], axis=1).astype(bf16)
    bg = jnp.concatenate([p['lru_bx'], p['lru_ba']]).reshape(1, -1)
    y_lru = _lru(xl, gl, p['conv_w'], row(p['conv_b']), wg, bg, row(p['lru_lambda']), bsz, s, ts=min(256, s))

    zl = jnp.zeros((DECAY_LORA, RWKV_W), f32)
    wl = jnp.concatenate([jnp.concatenate([p['rwkv_w_up'], zl], axis=1),
                          jnp.concatenate([zl, p['rwkv_a_up']], axis=1)], axis=0).astype(bf16)
    at, bt, kt, rt, v, cl, bon, g = _rwkv_prep(
        ur, row(p['rwkv_mu']), row(p['rwkv_w0']), row(p['rwkv_a0']), wl, p['rwkv_g_up'].astype(bf16),
        row(p['rwkv_k_k']), row(p['rwkv_k_a']), row(p['rwkv_r_k']), ones_blk, bsz, s, ts=min(256, s))
    gm, sp, rc, yp = _rwkv_chunk(at, bt, kt, rt, v, cl, ts=min(256, s))
    y_scan = _rwkv_state(gm, sp, rc, yp, bsz, s, nck=min(8, s // RWKV_CHUNK))

    x1 = _outproj(x2d, y_lru, y_scan, bon, v, g, row(p['rwkv_lnx_g']), row(p['rwkv_lnx_b']), ones_blk,
                  p['w_out'].astype(bf16), tm=min(512, t))

    kmem, vmem = _memkv(mem2d, row(p['norm_mem_g']), p['xa_wk'].astype(bf16), p['xa_wv'].astype(bf16),
                        tm=min(512, bsz * mlen))
    wr_pad = jnp.zeros((D_MODEL, LANES), f32).at[:, :N_EXPERTS].set(p['w_router'])
    br_pad = jnp.zeros((1, LANES), f32).at[0, :N_EXPERTS].set(p['b_router'])
    x2, hf, idx_pad, gate_pad = _xattn(x1, kmem, vmem, row(p['norm_xa_g']), p['xa_wq'].astype(bf16),
                                       p['xa_wo'].astype(bf16), row(p['norm_ffn_g']), wr_pad, br_pad,
                                       bsz, s, mlen, tm=min(512, s))

    block_e, n_used, buf_tok, buf_g, pos_km, n_blocks = _routing(idx_pad[:, :TOP_K], gate_pad[:, :TOP_K], t)
    ys = _moe(block_e, n_used, buf_tok, hf, buf_g, p['w_gu'], p['b_gu'], p['w_dn'], p['b_dn'], n_blocks)
    return _combine(pos_km, ys, x2, row(p['final_norm_g']), tc=min(128, t))


def kernel(x, mem, norm_mix_g, w_in, conv_w, conv_b, lru_wx, lru_bx, lru_wa, lru_ba, lru_lambda, rwkv_mu, rwkv_w0, rwkv_w_up, rwkv_a0, rwkv_a_up, rwkv_g_up, rwkv_k_k, rwkv_k_a, rwkv_r_k, rwkv_lnx_g, rwkv_lnx_b, w_out, norm_xa_g, norm_mem_g, xa_wq, xa_wk, xa_wv, xa_wo, norm_ffn_g, w_router, b_router, w_gu, b_gu, w_dn, b_dn, final_norm_g):
    bsz, s, d = x.shape
    mlen = mem.shape[1]
    assert d == D_MODEL and w_in.shape[0] == 1
    p = dict(norm_mix_g=norm_mix_g[0], w_in=w_in[0], conv_w=conv_w[0], conv_b=conv_b[0], lru_wx=lru_wx[0],
             lru_bx=lru_bx[0], lru_wa=lru_wa[0], lru_ba=lru_ba[0], lru_lambda=lru_lambda[0],
             rwkv_mu=rwkv_mu[0], rwkv_w0=rwkv_w0[0], rwkv_w_up=rwkv_w_up[0], rwkv_a0=rwkv_a0[0],
             rwkv_a_up=rwkv_a_up[0], rwkv_g_up=rwkv_g_up[0], rwkv_k_k=rwkv_k_k[0], rwkv_k_a=rwkv_k_a[0],
             rwkv_r_k=rwkv_r_k[0].reshape(-1), rwkv_lnx_g=rwkv_lnx_g[0], rwkv_lnx_b=rwkv_lnx_b[0],
             w_out=w_out[0], norm_xa_g=norm_xa_g[0], norm_mem_g=norm_mem_g[0], xa_wq=xa_wq[0],
             xa_wk=xa_wk[0], xa_wv=xa_wv[0], xa_wo=xa_wo[0], norm_ffn_g=norm_ffn_g[0],
             w_router=w_router[0], b_router=b_router[0], w_gu=w_gu[0], b_gu=b_gu[0], w_dn=w_dn[0],
             b_dn=b_dn[0], final_norm_g=final_norm_g)
    out = _layer(x.reshape(bsz * s, d), mem.reshape(bsz * mlen, d), bsz, s, mlen, p)
    return out.reshape(bsz, s, d)
```

```python
import functools

import jax
import jax.numpy as jnp
from jax import lax
from jax.experimental import pallas as pl
from jax.experimental.pallas import tpu as pltpu

f32 = jnp.float32
bf16 = jnp.bfloat16
i32 = jnp.int32
HI = lax.Precision.HIGHEST

D_MODEL = 1024
LRU_W = 512
RWKV_W = 512
LRU_BLOCKS = 8
LRU_BLOCK = 64
CONV_W = 4
LRU_C = 8.0
RWKV_HEAD = 64
DECAY_LORA = 64
AAA_LORA = 64
GATE_LORA = 128
RWKV_IN = 3 * RWKV_W + DECAY_LORA + AAA_LORA + GATE_LORA
XA_HEADS = 4
XA_HEAD = D_MODEL // XA_HEADS
N_EXPERTS = 32
TOP_K = 4
D_FF = D_MODEL
SWIGLU_LIMIT = 7.0
SWIGLU_ALPHA = 1.702
EPS = 1e-6
GN_EPS = 64e-5

LANES = 128
SUBLANES = 8
RWKV_CHUNK = 64
PAIR = 2 * RWKV_HEAD
N_PAIRS = RWKV_W // PAIR
MOE_BLOCK = 256
ROW_TILE = D_MODEL // LANES
VMEM_LIMIT = 52 * 1024 * 1024


def _cparams(sem):
    return pltpu.CompilerParams(dimension_semantics=sem, vmem_limit_bytes=VMEM_LIMIT)


def _rms(x, g):
    return x * lax.rsqrt(jnp.mean(x * x, axis=-1, keepdims=True) + EPS) * g


def _full(shape):
    n = len(shape)
    return pl.BlockSpec(shape, lambda *a: (0,) * n)


def _shift_rows(x, prev8, d):
    xr = pltpu.roll(x, d, 0)
    tr = pltpu.roll(prev8, d, 0)
    row = lax.broadcasted_iota(i32, prev8.shape, 0)
    head = jnp.where(row < d, tr, xr[:SUBLANES])
    return jnp.concatenate([head, xr[SUBLANES:]], axis=0)


def _store_row_tiles(ref, val, off=0):
    n = val.shape[0]
    for j in range(ROW_TILE):
        ref[pl.ds(off + j, n, stride=ROW_TILE), :] = val[:, j * LANES:(j + 1) * LANES]


def _load_row_tiles(ref, n, off=0):
    return jnp.concatenate([ref[pl.ds(off + j, n, stride=ROW_TILE), :] for j in range(ROW_TILE)], axis=1)


def _inproj_body(x_ref, g_ref, w_ref, xl_ref, gl_ref, ur_ref):
    h = _rms(x_ref[...], g_ref[...])
    u = jnp.dot(h.astype(bf16), w_ref[...], preferred_element_type=f32)
    xl_ref[...] = u[:, :LRU_W]
    gl_ref[...] = u[:, LRU_W:2 * LRU_W]
    ur_ref[...] = u[:, 2 * LRU_W:]


def _inproj(x2d, g, w_in_bf, tm):
    t = x2d.shape[0]
    return pl.pallas_call(
        _inproj_body,
        grid=(t // tm,),
        in_specs=[pl.BlockSpec((tm, D_MODEL), lambda i: (i, 0)), _full((1, D_MODEL)),
                  _full(w_in_bf.shape)],
        out_specs=[pl.BlockSpec((tm, LRU_W), lambda i: (i, 0)),
                   pl.BlockSpec((tm, LRU_W), lambda i: (i, 0)),
                   pl.BlockSpec((tm, RWKV_IN), lambda i: (i, 0))],
        out_shape=[jax.ShapeDtypeStruct((t, LRU_W), f32), jax.ShapeDtypeStruct((t, LRU_W), f32),
                   jax.ShapeDtypeStruct((t, RWKV_IN), f32)],
        compiler_params=_cparams(("parallel",)),
        name="inproj",
    )(x2d, g, w_in_bf)


def _lru_body(xl_ref, gl_ref, cw_ref, cb_ref, wg_ref, bg_ref, lam_ref, o_ref, tail_ref, h_ref):
    ts = xl_ref.shape[0]

    @pl.when(pl.program_id(1) == 0)
    def _():
        tail_ref[...] = jnp.zeros_like(tail_ref)
        h_ref[...] = jnp.zeros_like(h_ref)

    x = xl_ref[...]
    tail = tail_ref[...]
    cw = cw_ref[...]
    xc = cb_ref[...] + cw[CONV_W - 1:CONV_W] * x
    for d in range(1, CONV_W):
        xc = xc + cw[CONV_W - 1 - d:CONV_W - d] * _shift_rows(x, tail, d)
    tail_ref[...] = x[ts - SUBLANES:]

    gates = jax.nn.sigmoid(jnp.dot(xc.astype(bf16), wg_ref[...], preferred_element_type=f32) + bg_ref[...])
    gx = gates[:, :LRU_W]
    ga = gates[:, LRU_W:]
    log_a = -LRU_C * ga * jax.nn.softplus(-lam_ref[...])
    a = jnp.exp(log_a)
    b = jnp.sqrt(-jnp.tanh(log_a) * (a * a + 1.0)) * gx * xc

    row = lax.broadcasted_iota(i32, (ts, LRU_W), 0)
    d = 1
    while d < ts:
        keep = row >= d
        a_s = jnp.where(keep, pltpu.roll(a, d, 0), 1.0)
        b_s = jnp.where(keep, pltpu.roll(b, d, 0), 0.0)
        b = a * b_s + b
        a = a * a_s
        d *= 2
    h = b + a * h_ref[SUBLANES - 1:SUBLANES, :]
    h_ref[...] = h[ts - SUBLANES:]
    o_ref[...] = (h * jax.nn.gelu(gl_ref[...])).astype(o_ref.dtype)


def _lru(xl, gl, cw, cb, wg_bf, bg, lam, bsz, s, ts):
    nt = s // ts
    blk = pl.BlockSpec((ts, LRU_W), lambda b, i: (b * nt + i, 0))
    return pl.pallas_call(
        _lru_body,
        grid=(bsz, nt),
        in_specs=[blk, blk, _full(cw.shape), _full(cb.shape), _full(wg_bf.shape), _full(bg.shape),
                  _full(lam.shape)],
        out_specs=blk,
        out_shape=jax.ShapeDtypeStruct((bsz * s, LRU_W), bf16),
        scratch_shapes=[pltpu.VMEM((SUBLANES, LRU_W), f32), pltpu.VMEM((SUBLANES, LRU_W), f32)],
        compiler_params=_cparams(("parallel", "arbitrary")),
        name="lru",
    )(xl, gl, cw, cb, wg_bf, bg, lam)


def _rwkv_prep_body(ur_ref, mu_ref, w0_ref, a0_ref, wl_ref, gup_ref, kk_ref, ka_ref, rk_ref, ones_ref,
                    at_ref, bt_ref, kt_ref, rt_ref, v_ref, cl_ref, bon_ref, g_ref, prev_ref):
    ts = ur_ref.shape[0]

    @pl.when(pl.program_id(1) == 0)
    def _():
        prev_ref[...] = jnp.zeros_like(prev_ref)

    u0 = ur_ref[...]
    us = _shift_rows(u0, prev_ref[...], 1)
    prev_ref[...] = u0[ts - SUBLANES:]
    u = u0 + (us - u0) * mu_ref[...]
    c1, c2, c3 = RWKV_W, 2 * RWKV_W, 3 * RWKV_W
    r = u[:, :c1]
    k = u[:, c1:c2]
    v = u[:, c2:c3]
    lora = u[:, c3:c3 + LANES]
    dg = u[:, c3 + LANES:]
    lane = lax.broadcasted_iota(i32, lora.shape, 1)
    lora = jnp.where(lane < DECAY_LORA, jnp.tanh(lora), lora)
    proj = jnp.dot(lora.astype(bf16), wl_ref[...], preferred_element_type=f32)
    w = -jax.nn.softplus(-(w0_ref[...] + proj[:, :RWKV_W])) - 0.5
    lw = -jnp.exp(w)
    a = jax.nn.sigmoid(a0_ref[...] + proj[:, RWKV_W:])
    g_ref[...] = jnp.dot(jax.nn.sigmoid(dg).astype(bf16), gup_ref[...], preferred_element_type=f32)

    ones = ones_ref[...]
    kk = k * kk_ref[...]
    ss = jnp.dot(kk * kk, ones, precision=HI, preferred_element_type=f32)
    kk = kk / jnp.maximum(jnp.sqrt(ss), 1e-12)
    k2 = k * (1.0 + (a - 1.0) * ka_ref[...])
    bon_ref[...] = jnp.dot(r * k2 * rk_ref[...], ones, precision=HI, preferred_element_type=f32)

    ri = lax.broadcasted_iota(i32, (ts, ts), 0)
    ci = lax.broadcasted_iota(i32, (ts, ts), 1)
    tri = jnp.where((ci <= ri) & (ri // RWKV_CHUNK == ci // RWKV_CHUNK), 1.0, 0.0).astype(f32)
    cl = jnp.dot(tri, lw, precision=HI, preferred_element_type=f32)
    e_neg = jnp.exp(-cl)
    at_ref[...] = -kk * jnp.exp(cl - lw)
    bt_ref[...] = kk * a * e_neg
    kt_ref[...] = k2 * e_neg
    rt_ref[...] = r * jnp.exp(cl)
    v_ref[...] = v
    cl_ref[...] = cl


def _rwkv_prep(ur, mu, w0, a0, wl_bf, gup_bf, k_k, k_a, r_k, ones_blk, bsz, s, ts):
    nt = s // ts
    t = bsz * s
    oblk = pl.BlockSpec((ts, RWKV_W), lambda b, i: (b * nt + i, 0))
    osh = jax.ShapeDtypeStruct((t, RWKV_W), f32)
    return pl.pallas_call(
        _rwkv_prep_body,
        grid=(bsz, nt),
        in_specs=[pl.BlockSpec((ts, RWKV_IN), lambda b, i: (b * nt + i, 0)), _full(mu.shape),
                  _full(w0.shape), _full(a0.shape), _full(wl_bf.shape), _full(gup_bf.shape),
                  _full(k_k.shape), _full(k_a.shape), _full(r_k.shape), _full(ones_blk.shape)],
        out_specs=[oblk] * 8,
        out_shape=[osh] * 8,
        scratch_shapes=[pltpu.VMEM((SUBLANES, RWKV_IN), f32)],
        compiler_params=_cparams(("parallel", "arbitrary")),
        name="rwkv_prep",
    )(ur, mu, w0, a0, wl_bf, gup_bf, k_k, k_a, r_k, ones_blk)


def _mm_hi(a, b):
    return jnp.dot(a, b, precision=HI, preferred_element_type=f32)


def _mm(a, b):
    return jnp.dot(a.astype(bf16), b.astype(bf16), preferred_element_type=f32)


def _mm_nt(a, b):
    return lax.dot_general(a.astype(bf16), b.astype(bf16), (((1,), (1,)), ((), ())), preferred_element_type=f32)


def _mm_tn(a, b):
    return lax.dot_general(a.astype(bf16), b.astype(bf16), (((0,), (0,)), ((), ())), preferred_element_type=f32)


def _stack_heads(x, m0):
    return jnp.concatenate([jnp.where(m0, x, 0.0), jnp.where(m0, 0.0, x)], axis=0)


def _unstack(x):
    c = x.shape[0] // 2
    return x[:c] + x[c:]


def _chunk_maps(chunks):
    c = RWKV_CHUNK
    n = 2 * c
    lane = lax.broadcasted_iota(i32, (c, PAIR), 1)
    m0 = lane < RWKV_HEAD
    ri = lax.broadcasted_iota(i32, (n, n), 0)
    ci = lax.broadcasted_iota(i32, (n, n), 1)
    strict = ci < ri
    incl2 = jnp.concatenate([ci <= ri, ci <= ri], axis=1)
    diag = ri == ci
    eye = jnp.where(diag, 1.0, 0.0).astype(f32)
    each = lambda f, *ls: [f(*xs) for xs in zip(*ls)]

    ats, bts, kts, rts, vs, cls = [list(x) for x in zip(*chunks)]
    pcs = each(lambda cl: jnp.exp(cl[c - 1:c, :]), cls)
    a_s = each(lambda x: _stack_heads(x, m0).astype(bf16), ats)
    r_s = each(lambda x: _stack_heads(x, m0).astype(bf16), rts)
    b_f = each(lambda x: _stack_heads(x, m0), bts)
    k_f = each(lambda x: _stack_heads(x, m0), kts)
    v_s = each(lambda x: _stack_heads(x, m0).astype(bf16), vs)

    aa = each(lambda a, r, b, k: _mm_nt(jnp.concatenate([a, r], axis=0),
                                        jnp.concatenate([b.astype(bf16), k.astype(bf16)], axis=0)),
              a_s, r_s, b_f, k_f)
    l_ab = each(lambda x: jnp.where(strict, x[:n, :n], 0.0), aa)
    a_ak = each(lambda x: jnp.where(strict, x[:n, n:], 0.0), aa)
    a_r = each(lambda x: jnp.where(incl2, x[n:, :], 0.0).astype(bf16), aa)

    tinv = each(lambda x: eye + x, l_ab)
    lp = l_ab
    p = 2
    while p < c:
        lp = each(lambda x: _mm(x, x), lp)
        tinv = each(lambda t, x: t + _mm(t, x), tinv, lp)
        p *= 2

    w_pre = each(_mm, a_ak, v_s)
    za = each(lambda t, w, a: _mm(t, jnp.concatenate([w.astype(bf16), a], axis=1)), tinv, w_pre, a_s)
    zp_s = each(lambda x: x[:, :PAIR].astype(bf16), za)
    ac_s = each(lambda x: x[:, PAIR:].astype(bf16), za)
    yp = each(lambda ar, z, v: _unstack(_mm(ar, jnp.concatenate([z, v], axis=0))), a_r, zp_s, v_s)
    rc = each(lambda r, ar, ac: r + _unstack(_mm(ar[:, :n], ac)), rts, a_r, ac_s)
    sm = each(lambda b, pc, z, ac: _mm_tn(b * pc, jnp.concatenate([z, ac], axis=1)), b_f, pcs, zp_s, ac_s)
    sp = each(lambda k, pc, v, m: _mm_tn(k * pc, v) + m[:, :PAIR], k_f, pcs, v_s, sm)
    g = each(lambda pc, m: jnp.where(diag, jnp.broadcast_to(pc, (n, n)), 0.0) + m[:, PAIR:], pcs, sm)
    return list(zip(g, sp, rc, yp))


def _rwkv_chunk_body(at_ref, bt_ref, kt_ref, rt_ref, v_ref, cl_ref, g_ref, sp_ref, rc_ref, yp_ref):
    ts = at_ref.shape[0]
    c = RWKV_CHUNK
    sls = [slice(j * c, (j + 1) * c) for j in range(ts // c)]
    outs = _chunk_maps([(at_ref[sl, :], bt_ref[sl, :], kt_ref[sl, :], rt_ref[sl, :], v_ref[sl, :], cl_ref[sl, :])
                        for sl in sls])
    for j, (sl, (g, sp, rc, yp)) in enumerate(zip(sls, outs)):
        g_ref[j, 0] = g
        sp_ref[j, 0] = sp
        rc_ref[sl, :] = rc
        yp_ref[sl, :] = yp


def _rwkv_chunk(at, bt, kt, rt, v, cl, ts):
    t = at.shape[0]
    nck = ts // RWKV_CHUNK
    iblk = pl.BlockSpec((ts, PAIR), lambda i, p: (i, p))
    mblk = pl.BlockSpec((nck, 1, PAIR, PAIR), lambda i, p: (i, p, 0, 0))
    msh = jax.ShapeDtypeStruct((t // RWKV_CHUNK, N_PAIRS, PAIR, PAIR), f32)
    osh = jax.ShapeDtypeStruct((t, RWKV_W), f32)
    return pl.pallas_call(
        _rwkv_chunk_body,
        grid=(t // ts, N_PAIRS),
        in_specs=[iblk] * 6,
        out_specs=[mblk, mblk, iblk, iblk],
        out_shape=[msh, msh, osh, osh],
        compiler_params=_cparams(("parallel", "parallel")),
        name="rwkv_chunk",
    )(at, bt, kt, rt, v, cl)


def _rwkv_state_body(g_ref, sp_ref, rc_ref, yp_ref, y_ref, s_ref):
    @pl.when(pl.program_id(1) == 0)
    def _():
        s_ref[...] = jnp.zeros_like(s_ref)

    c = RWKV_CHUNK
    nck = g_ref.shape[0]
    for j in range(nck):
        sl = slice(j * c, (j + 1) * c)
        for p in range(N_PAIRS):
            ls = slice(p * PAIR, (p + 1) * PAIR)
            s = s_ref[p]
            y_ref[sl, ls] = yp_ref[sl, ls] + _mm_hi(rc_ref[sl, ls], s)
            s_ref[p] = _mm_hi(g_ref[j, p], s) + sp_ref[j, p]


def _rwkv_state(g, sp, rc, yp, bsz, s, nck):
    t = bsz * s
    ns = s // (nck * RWKV_CHUNK)
    rows = nck * RWKV_CHUNK
    mblk = pl.BlockSpec((nck, N_PAIRS, PAIR, PAIR), lambda b, i: (b * ns + i, 0, 0, 0))
    rblk = pl.BlockSpec((rows, RWKV_W), lambda b, i: (b * ns + i, 0))
    return pl.pallas_call(
        _rwkv_state_body,
        grid=(bsz, ns),
        in_specs=[mblk, mblk, rblk, rblk],
        out_specs=rblk,
        out_shape=jax.ShapeDtypeStruct((t, RWKV_W), f32),
        scratch_shapes=[pltpu.VMEM((N_PAIRS, PAIR, PAIR), f32)],
        compiler_params=_cparams(("parallel", "arbitrary")),
        name="rwkv_state",
    )(g, sp, rc, yp)


def _outproj_body(x_ref, yl_ref, ys_ref, bon_ref, v_ref, g_ref, lg_ref, lb_ref, ones_ref, w_ref, o_ref):
    y = ys_ref[...]
    ones = ones_ref[...]
    inv_n = 1.0 / RWKV_HEAD
    mean = jnp.dot(y, ones, precision=HI, preferred_element_type=f32) * inv_n
    yc = y - mean
    var = jnp.dot(yc * yc, ones, precision=HI, preferred_element_type=f32) * inv_n
    yn = yc * lax.rsqrt(var + GN_EPS) * lg_ref[...] + lb_ref[...]
    yr = (yn + bon_ref[...] * v_ref[...]) * g_ref[...]
    cat = jnp.concatenate([yl_ref[...], yr.astype(bf16)], axis=1)
    o_ref[...] = x_ref[...] + jnp.dot(cat, w_ref[...], preferred_element_type=f32)


def _outproj(x2d, y_lru, y_scan, bon, v, g, lnx_g, lnx_b, ones_blk, w_out_bf, tm):
    t = x2d.shape[0]
    xb = pl.BlockSpec((tm, D_MODEL), lambda i: (i, 0))
    hb = pl.BlockSpec((tm, RWKV_W), lambda i: (i, 0))
    return pl.pallas_call(
        _outproj_body,
        grid=(t // tm,),
        in_specs=[xb, hb, hb, hb, hb, hb, _full(lnx_g.shape), _full(lnx_b.shape), _full(ones_blk.shape),
                  _full(w_out_bf.shape)],
        out_specs=xb,
        out_shape=jax.ShapeDtypeStruct((t, D_MODEL), f32),
        compiler_params=_cparams(("parallel",)),
        name="outproj",
    )(x2d, y_lru, y_scan, bon, v, g, lnx_g, lnx_b, ones_blk, w_out_bf)


def _memkv_body(m_ref, g_ref, wk_ref, wv_ref, k_ref, v_ref):
    h = _rms(m_ref[...], g_ref[...]).astype(bf16)
    k_ref[...] = jnp.dot(h, wk_ref[...], preferred_element_type=f32).astype(bf16)
    v_ref[...] = jnp.dot(h, wv_ref[...], preferred_element_type=f32).astype(bf16)


def _memkv(mem2d, g, wk_bf, wv_bf, tm):
    t = mem2d.shape[0]
    blk = pl.BlockSpec((tm, D_MODEL), lambda i: (i, 0))
    sh = jax.ShapeDtypeStruct((t, D_MODEL), bf16)
    return pl.pallas_call(
        _memkv_body,
        grid=(t // tm,),
        in_specs=[blk, _full(g.shape), _full(wk_bf.shape), _full(wv_bf.shape)],
        out_specs=[blk, blk],
        out_shape=[sh, sh],
        compiler_params=_cparams(("parallel",)),
        name="memkv",
    )(mem2d, g, wk_bf, wv_bf)


def _xattn_body(x_ref, k_ref, v_ref, gx_ref, wq_ref, wo_ref, gf_ref, wr_ref, br_ref,
                x2_ref, hf_ref, idx_ref, gate_ref, cnt_ref, base_ref):
    x = x_ref[...]
    h = _rms(x, gx_ref[...]).astype(bf16)
    q = jnp.dot(h, wq_ref[...], preferred_element_type=f32).astype(bf16)
    k = k_ref[...]
    v = v_ref[...]
    outs = []
    for hd in range(XA_HEADS):
        sl = slice(hd * XA_HEAD, (hd + 1) * XA_HEAD)
        sc = lax.dot_general(q[:, sl], k[:, sl], (((1,), (1,)), ((), ())),
                             preferred_element_type=f32) * (XA_HEAD ** -0.5)
        sc = sc - jnp.max(sc, axis=-1, keepdims=True)
        e = jnp.exp(sc)
        p = e / jnp.sum(e, axis=-1, keepdims=True)
        outs.append(jnp.dot(p.astype(bf16), v[:, sl], preferred_element_type=f32).astype(bf16))
    o = jnp.concatenate(outs, axis=1)
    x2 = x + jnp.dot(o, wo_ref[...], preferred_element_type=f32)
    x2_ref[...] = x2

    hf = _rms(x2, gf_ref[...])
    _store_row_tiles(hf_ref, hf)
    logits = jnp.dot(hf, wr_ref[...], precision=HI, preferred_element_type=f32) + br_ref[...]
    lane = lax.broadcasted_iota(i32, logits.shape, 1)
    neg = jnp.float32(-jnp.inf)
    cur = jnp.where(lane < N_EXPERTS, logits, neg)
    vals = []
    idxs = []
    for _ in range(TOP_K):
        m = jnp.max(cur, axis=-1, keepdims=True)
        am = jnp.min(jnp.where(cur == m, lane, LANES), axis=-1, keepdims=True)
        vals.append(m)
        idxs.append(am)
        cur = jnp.where(lane == am, neg, cur)
    es = [jnp.exp(vk - vals[0]) for vk in vals]
    den = es[0] + es[1] + es[2] + es[3]

    @pl.when(pl.program_id(0) == 0)
    def _():
        base_ref[...] = jnp.zeros_like(base_ref)

    tm = x.shape[0]
    onehot = [jnp.where(lane == am, 1.0, 0.0) for am in idxs]
    cnt = (onehot[0] + onehot[1]) + (onehot[2] + onehot[3])
    ri = lax.broadcasted_iota(i32, (tm, tm), 0)
    ci = lax.broadcasted_iota(i32, (tm, tm), 1)
    before = jnp.where(ci < ri, 1.0, 0.0).astype(bf16)
    base = base_ref[0:1, :]
    prior = jnp.dot(before, cnt.astype(bf16), preferred_element_type=f32) + base
    base_ref[...] = jnp.broadcast_to(base + jnp.sum(cnt, axis=0, keepdims=True), base_ref.shape)
    cnt_ref[...] = base_ref[...]

    idx_out = jnp.zeros(logits.shape, i32)
    gate_out = jnp.zeros(logits.shape, f32)
    for kq in range(TOP_K):
        rank = jnp.sum(prior * onehot[kq], axis=-1, keepdims=True).astype(i32)
        idx_out = jnp.where(lane == kq, idxs[kq], idx_out)
        idx_out = jnp.where(lane == TOP_K + kq, rank, idx_out)
        gate_out = jnp.where(lane == kq, es[kq] / den, gate_out)
    idx_ref[...] = idx_out
    gate_ref[...] = gate_out


def _xattn(x1, kmem, vmem, g_xa, wq_bf, wo_bf, g_ffn, wr_pad, br_pad, bsz, s, mlen, tm):
    t = bsz * s
    nt = s // tm
    xb = pl.BlockSpec((tm, D_MODEL), lambda i: (i, 0))
    mb = pl.BlockSpec((mlen, D_MODEL), lambda i: (i // nt, 0))
    lb = pl.BlockSpec((tm, LANES), lambda i: (i, 0))
    return pl.pallas_call(
        _xattn_body,
        grid=(t // tm,),
        in_specs=[xb, mb, mb, _full(g_xa.shape), _full(wq_bf.shape), _full(wo_bf.shape),
                  _full(g_ffn.shape), _full(wr_pad.shape), _full(br_pad.shape)],
        out_specs=[xb, pl.BlockSpec((tm * ROW_TILE, LANES), lambda i: (i, 0)), lb, lb, _full((SUBLANES, LANES))],
        out_shape=[jax.ShapeDtypeStruct((t, D_MODEL), f32), jax.ShapeDtypeStruct((t * ROW_TILE, LANES), f32),
                   jax.ShapeDtypeStruct((t, LANES), i32), jax.ShapeDtypeStruct((t, LANES), f32),
                   jax.ShapeDtypeStruct((SUBLANES, LANES), f32)],
        scratch_shapes=[pltpu.VMEM((SUBLANES, LANES), f32)],
        compiler_params=_cparams(("arbitrary",)),
        name="xattn",
    )(x1, kmem, vmem, g_xa, wq_bf, wo_bf, g_ffn, wr_pad, br_pad)


DMA_UNROLL = 8


def _tile_at(ref, row):
    return ref.at[pl.ds(pl.multiple_of(row * ROW_TILE, ROW_TILE), ROW_TILE), :]


def _dispatch_body(dest_ref, pend_ref, hf_ref, xs_hbm, zeros_ref, sem, zsem):
    i = pl.program_id(0)
    tm = hf_ref.shape[0] // ROW_TILE
    zrows = MOE_BLOCK * ROW_TILE

    def zero_copy(e):
        start = pl.multiple_of((pend_ref[e] - MOE_BLOCK) * ROW_TILE, ROW_TILE)
        return pltpu.make_async_copy(zeros_ref, xs_hbm.at[pl.ds(start, zrows), :], zsem)

    def nonempty(e):
        return pend_ref[e] > (pend_ref[e - 1] if e else 0)

    @pl.when(i == 0)
    def _():
        zeros_ref[...] = jnp.zeros_like(zeros_ref)
        for e in range(N_EXPERTS):
            @pl.when(nonempty(e))
            def _():
                zero_copy(e).start()
        for e in range(N_EXPERTS):
            @pl.when(nonempty(e))
            def _():
                zero_copy(e).wait()

        def tail_copy(b):
            return pltpu.make_async_copy(zeros_ref, xs_hbm.at[pl.ds(pl.multiple_of(b * zrows, zrows), zrows), :], zsem)

        def tail_start(b, carry):
            tail_copy(b).start()
            return carry

        def tail_wait(b, carry):
            tail_copy(b).wait()
            return carry
        n_used = pend_ref[N_EXPERTS - 1] // MOE_BLOCK
        n_all = xs_hbm.shape[0] // zrows
        lax.fori_loop(n_used, n_all, tail_start, 0)
        lax.fori_loop(n_used, n_all, tail_wait, 0)

    def body(q, carry):
        for u in range(DMA_UNROLL):
            r = q * DMA_UNROLL + u
            src = _tile_at(hf_ref, r)
            for kq in range(TOP_K):
                pltpu.make_async_copy(src, _tile_at(xs_hbm, dest_ref[(i * tm + r) * TOP_K + kq]), sem).start()
        return carry
    lax.fori_loop(0, tm // DMA_UNROLL, body, 0)
    for kq in range(TOP_K):
        pltpu.make_async_copy(hf_ref, xs_hbm.at[pl.ds(0, tm * ROW_TILE), :], sem).wait()


def _dispatch(dest_flat, pends, hf_tiles, n_blocks, tm):
    t = hf_tiles.shape[0] // ROW_TILE
    rows = n_blocks * MOE_BLOCK
    grid_spec = pltpu.PrefetchScalarGridSpec(
        num_scalar_prefetch=2,
        grid=(t // tm,),
        in_specs=[pl.BlockSpec((tm * ROW_TILE, LANES), lambda i, d, pe: (i, 0))],
        out_specs=pl.BlockSpec(memory_space=pl.ANY),
        scratch_shapes=[pltpu.VMEM((MOE_BLOCK * ROW_TILE, LANES), f32), pltpu.SemaphoreType.DMA(()),
                        pltpu.SemaphoreType.DMA(())],
    )
    return pl.pallas_call(
        _dispatch_body,
        grid_spec=grid_spec,
        out_shape=jax.ShapeDtypeStruct((rows * ROW_TILE, LANES), f32),
        compiler_params=_cparams(("arbitrary",)),
        name="dispatch",
    )(dest_flat, pends, hf_tiles)


def _moe_body(be_ref, nb_ref, x_ref, wgu_ref, bgu_ref, wdn_ref, bdn_ref, o_ref, wgu_bf, wdn_bf):
    i = pl.program_id(0)

    @pl.when(i >= nb_ref[0])
    def _():
        o_ref[...] = jnp.zeros_like(o_ref)

    @pl.when(i < nb_ref[0])
    def _():
        prev_e = be_ref[jnp.maximum(i - 1, 0)]

        @pl.when((i == 0) | (be_ref[i] != prev_e))
        def _():
            wgu_bf[...] = wgu_ref[0].astype(bf16)
            wdn_bf[...] = wdn_ref[0].astype(bf16)

        xb = _load_row_tiles(x_ref, MOE_BLOCK).astype(bf16)
        gu = jnp.dot(xb, wgu_bf[...], preferred_element_type=f32) + bgu_ref[0]
        gate = jnp.minimum(gu[:, :D_FF], SWIGLU_LIMIT)
        up = jnp.clip(gu[:, D_FF:], -SWIGLU_LIMIT, SWIGLU_LIMIT)
        act = (up + 1.0) * (gate * jax.nn.sigmoid(SWIGLU_ALPHA * gate))
        y = jnp.dot(act.astype(bf16), wdn_bf[...], preferred_element_type=f32) + bdn_ref[0]
        _store_row_tiles(o_ref, y)


def _moe(block_e, n_used, xs_tiles, w_gu, b_gu, w_dn, b_dn, n_blocks):
    rows = n_blocks * MOE_BLOCK
    xblk = pl.BlockSpec((MOE_BLOCK * ROW_TILE, LANES), lambda i, be, nb: (i, 0))
    grid_spec = pltpu.PrefetchScalarGridSpec(
        num_scalar_prefetch=2,
        grid=(n_blocks,),
        in_specs=[
            pl.BlockSpec((MOE_BLOCK * ROW_TILE, LANES), lambda i, be, nb: (jnp.minimum(i, nb[0] - 1), 0)),
            pl.BlockSpec((1, D_MODEL, 2 * D_FF), lambda i, be, nb: (be[i], 0, 0)),
            pl.BlockSpec((1, 1, 2 * D_FF), lambda i, be, nb: (be[i], 0, 0)),
            pl.BlockSpec((1, D_FF, D_MODEL), lambda i, be, nb: (be[i], 0, 0)),
            pl.BlockSpec((1, 1, D_MODEL), lambda i, be, nb: (be[i], 0, 0)),
        ],
        out_specs=xblk,
        scratch_shapes=[pltpu.VMEM((D_MODEL, 2 * D_FF), bf16), pltpu.VMEM((D_FF, D_MODEL), bf16)],
    )
    return pl.pallas_call(
        _moe_body,
        grid_spec=grid_spec,
        out_shape=jax.ShapeDtypeStruct((rows * ROW_TILE, LANES), f32),
        compiler_params=_cparams(("arbitrary",)),
        name="moe",
    )(block_e, n_used, xs_tiles, w_gu, b_gu.reshape(N_EXPERTS, 1, 2 * D_FF), w_dn,
      b_dn.reshape(N_EXPERTS, 1, D_MODEL))


def _combine_body(pos_ref, ys_hbm, x_ref, gate_ref, g_ref, o_ref, buf, sem):
    i = pl.program_id(0)
    n = pl.num_programs(0)
    tc = x_ref.shape[0]
    slot = i % 2
    slot_rows = TOP_K * tc * ROW_TILE

    def start(step, sl):
        def body(q, carry):
            for u in range(DMA_UNROLL):
                r = q * DMA_UNROLL + u
                for kq in range(TOP_K):
                    dst = _tile_at(buf, (sl * TOP_K + kq) * tc + r)
                    pltpu.make_async_copy(_tile_at(ys_hbm, pos_ref[(step * tc + r) * TOP_K + kq]), dst,
                                          sem.at[sl]).start()
            return carry
        lax.fori_loop(0, tc // DMA_UNROLL, body, 0)

    @pl.when(i == 0)
    def _():
        start(0, 0)

    @pl.when(i + 1 < n)
    def _():
        start(i + 1, 1 - slot)

    off = pl.multiple_of(slot * slot_rows, slot_rows)
    pltpu.make_async_copy(ys_hbm.at[pl.ds(0, slot_rows), :], buf.at[pl.ds(off, slot_rows), :], sem.at[slot]).wait()
    gates = gate_ref[...]
    acc = x_ref[...]
    for kq in range(TOP_K):
        rows = _load_row_tiles(buf, tc, off + kq * tc * ROW_TILE)
        acc = acc + gates[:, kq:kq + 1] * rows
    o_ref[...] = _rms(acc, g_ref[...])


def _combine(pos_flat, ys_tiles, x2, gate_pad, g_final, tc):
    t = x2.shape[0]
    grid_spec = pltpu.PrefetchScalarGridSpec(
        num_scalar_prefetch=1,
        grid=(t // tc,),
        in_specs=[pl.BlockSpec(memory_space=pl.ANY),
                  pl.BlockSpec((tc, D_MODEL), lambda i, p: (i, 0)),
                  pl.BlockSpec((tc, LANES), lambda i, p: (i, 0)),
                  pl.BlockSpec((1, D_MODEL), lambda i, p: (0, 0))],
        out_specs=pl.BlockSpec((tc, D_MODEL), lambda i, p: (i, 0)),
        scratch_shapes=[pltpu.VMEM((2 * TOP_K * tc * ROW_TILE, LANES), f32), pltpu.SemaphoreType.DMA((2,))],
    )
    return pl.pallas_call(
        _combine_body,
        grid_spec=grid_spec,
        out_shape=jax.ShapeDtypeStruct((t, D_MODEL), f32),
        compiler_params=_cparams(("arbitrary",)),
        name="combine",
    )(pos_flat, ys_tiles, x2, gate_pad, g_final)


def _routing(top_idx, rank, counts, t):
    n_assign = t * TOP_K
    experts = jnp.arange(N_EXPERTS, dtype=i32)
    padded = (counts + MOE_BLOCK - 1) // MOE_BLOCK * MOE_BLOCK
    pends = jnp.cumsum(padded).astype(i32)
    pstarts = pends - padded
    start_of = jnp.sum(jnp.where(top_idx[:, :, None] == experts, pstarts, 0), axis=-1)
    dest = (start_of + rank).astype(i32).reshape(n_assign)
    n_blocks = (n_assign + N_EXPERTS * (MOE_BLOCK - 1) + MOE_BLOCK - 1) // MOE_BLOCK
    block_start = jnp.arange(n_blocks, dtype=i32) * MOE_BLOCK
    block_e = jnp.minimum(jnp.sum((block_start[:, None] >= pends[None, :]).astype(i32), axis=1), N_EXPERTS - 1)
    n_used = (pends[-1:] // MOE_BLOCK).astype(i32)
    return block_e.astype(i32), n_used, dest, pends, n_blocks


def _block_diag(w):
    n, bi, bj = w.shape
    eye = jnp.eye(n, dtype=w.dtype)
    return jnp.einsum('nij,nm->nimj', w, eye).reshape(n * bi, n * bj)


def _layer(x2d, mem2d, bsz, s, mlen, p):
    t = bsz * s
    row = lambda a: a.reshape(1, -1)
    ones_blk = _block_diag(jnp.ones((RWKV_W // RWKV_HEAD, RWKV_HEAD, RWKV_HEAD), f32))

    xl, gl, ur = _inproj(x2d, row(p['norm_mix_g']), p['w_in'].astype(bf16), tm=min(512, t))

    wg = jnp.concatenate([_block_diag(p['lru_wx']), _block_diag(p['lru_wa'])], axis=1).astype(bf16)
    bg = jnp.concatenate([p['lru_bx'], p['lru_ba']]).reshape(1, -1)
    y_lru = _lru(xl, gl, p['conv_w'], row(p['conv_b']), wg, bg, row(p['lru_lambda']), bsz, s, ts=min(256, s))

    zl = jnp.zeros((DECAY_LORA, RWKV_W), f32)
    wl = jnp.concatenate([jnp.concatenate([p['rwkv_w_up'], zl], axis=1),
                          jnp.concatenate([zl, p['rwkv_a_up']], axis=1)], axis=0).astype(bf16)
    at, bt, kt, rt, v, cl, bon, g = _rwkv_prep(
        ur, row(p['rwkv_mu']), row(p['rwkv_w0']), row(p['rwkv_a0']), wl, p['rwkv_g_up'].astype(bf16),
        row(p['rwkv_k_k']), row(p['rwkv_k_a']), row(p['rwkv_r_k']), ones_blk, bsz, s, ts=min(256, s))
    gm, sp, rc, yp = _rwkv_chunk(at, bt, kt, rt, v, cl, ts=min(512, s))
    y_scan = _rwkv_state(gm, sp, rc, yp, bsz, s, nck=min(8, s // RWKV_CHUNK))

    x1 = _outproj(x2d, y_lru, y_scan, bon, v, g, row(p['rwkv_lnx_g']), row(p['rwkv_lnx_b']), ones_blk,
                  p['w_out'].astype(bf16), tm=min(512, t))

    kmem, vmem = _memkv(mem2d, row(p['norm_mem_g']), p['xa_wk'].astype(bf16), p['xa_wv'].astype(bf16),
                        tm=min(512, bsz * mlen))
    wr_pad = jnp.zeros((D_MODEL, LANES), f32).at[:, :N_EXPERTS].set(p['w_router'])
    br_pad = jnp.zeros((1, LANES), f32).at[0, :N_EXPERTS].set(p['b_router'])
    x2, hf, idx_pad, gate_pad, cnt_pad = _xattn(x1, kmem, vmem, row(p['norm_xa_g']), p['xa_wq'].astype(bf16),
                                       p['xa_wo'].astype(bf16), row(p['norm_ffn_g']), wr_pad, br_pad,
                                       bsz, s, mlen, tm=min(512, s))

    counts = cnt_pad[0, :N_EXPERTS].astype(i32)
    block_e, n_used, dest, pends, n_blocks = _routing(idx_pad[:, :TOP_K], idx_pad[:, TOP_K:2 * TOP_K], counts, t)
    xs = _dispatch(dest, pends, hf, n_blocks, tm=min(256, t))
    ys = _moe(block_e, n_used, xs, p['w_gu'], p['b_gu'], p['w_dn'], p['b_dn'], n_blocks)
    return _combine(dest, ys, x2, gate_pad, row(p['final_norm_g']), tc=min(256, t))


def kernel(x, mem, norm_mix_g, w_in, conv_w, conv_b, lru_wx, lru_bx, lru_wa, lru_ba, lru_lambda, rwkv_mu, rwkv_w0, rwkv_w_up, rwkv_a0, rwkv_a_up, rwkv_g_up, rwkv_k_k, rwkv_k_a, rwkv_r_k, rwkv_lnx_g, rwkv_lnx_b, w_out, norm_xa_g, norm_mem_g, xa_wq, xa_wk, xa_wv, xa_wo, norm_ffn_g, w_router, b_router, w_gu, b_gu, w_dn, b_dn, final_norm_g):
    bsz, s, d = x.shape
    mlen = mem.shape[1]
    assert d == D_MODEL and w_in.shape[0] == 1
    p = dict(norm_mix_g=norm_mix_g[0], w_in=w_in[0], conv_w=conv_w[0], conv_b=conv_b[0], lru_wx=lru_wx[0],
             lru_bx=lru_bx[0], lru_wa=lru_wa[0], lru_ba=lru_ba[0], lru_lambda=lru_lambda[0],
             rwkv_mu=rwkv_mu[0], rwkv_w0=rwkv_w0[0], rwkv_w_up=rwkv_w_up[0], rwkv_a0=rwkv_a0[0],
             rwkv_a_up=rwkv_a_up[0], rwkv_g_up=rwkv_g_up[0], rwkv_k_k=rwkv_k_k[0], rwkv_k_a=rwkv_k_a[0],
             rwkv_r_k=rwkv_r_k[0].reshape(-1), rwkv_lnx_g=rwkv_lnx_g[0], rwkv_lnx_b=rwkv_lnx_b[0],
             w_out=w_out[0], norm_xa_g=norm_xa_g[0], norm_mem_g=norm_mem_g[0], xa_wq=xa_wq[0],
             xa_wk=xa_wk[0], xa_wv=xa_wv[0], xa_wo=xa_wo[0], norm_ffn_g=norm_ffn_g[0],
             w_router=w_router[0], b_router=b_router[0], w_gu=w_gu[0], b_gu=b_gu[0], w_dn=w_dn[0],
             b_dn=b_dn[0], final_norm_g=final_norm_g)
    out = _layer(x.reshape(bsz * s, d), mem.reshape(bsz * mlen, d), bsz, s, mlen, p)
    return out.reshape(bsz, s, d)
```

```python
import functools

import jax
import jax.numpy as jnp
from jax import lax
from jax.experimental import pallas as pl
from jax.experimental.pallas import tpu as pltpu

f32 = jnp.float32
bf16 = jnp.bfloat16
i32 = jnp.int32

D_MODEL = 1024
LRU_W = 512
RWKV_W = 512
LRU_BLOCKS = 8
LRU_BLOCK = 64
CONV_W = 4
LRU_C = 8.0
RWKV_HEAD = 64
DECAY_LORA = 64
AAA_LORA = 64
GATE_LORA = 128
RWKV_IN = 3 * RWKV_W + DECAY_LORA + AAA_LORA + GATE_LORA
XA_HEADS = 4
XA_HEAD = D_MODEL // XA_HEADS
N_EXPERTS = 32
TOP_K = 4
D_FF = D_MODEL
SWIGLU_LIMIT = 7.0
SWIGLU_ALPHA = 1.702
EPS = 1e-6
GN_EPS = 64e-5

LANES = 128
SUBLANES = 8
RWKV_CHUNK = 64
PAIR = 2 * RWKV_HEAD
N_PAIRS = RWKV_W // PAIR
MOE_BLOCK = 256
ROW_TILE = D_MODEL // LANES
VMEM_LIMIT = 52 * 1024 * 1024


def _cparams(sem):
    return pltpu.CompilerParams(dimension_semantics=sem, vmem_limit_bytes=VMEM_LIMIT)


def _rms(x, g):
    return x * lax.rsqrt(jnp.mean(x * x, axis=-1, keepdims=True) + EPS) * g


def _full(shape):
    n = len(shape)
    return pl.BlockSpec(shape, lambda *a: (0,) * n)


def _shift_rows(x, prev8, d):
    xr = pltpu.roll(x, d, 0)
    tr = pltpu.roll(prev8, d, 0)
    row = lax.broadcasted_iota(i32, prev8.shape, 0)
    head = jnp.where(row < d, tr, xr[:SUBLANES])
    return jnp.concatenate([head, xr[SUBLANES:]], axis=0)


def _bf16_parts(x, n):
    parts = []
    for _ in range(n):
        piece = x.astype(bf16)
        parts.append(piece)
        x = x - piece.astype(f32)
    return parts


def _sum_dot(x, mask_bf, n_parts, mask_left=False):
    acc = None
    for piece in _bf16_parts(x, n_parts):
        d = (jnp.dot(mask_bf, piece, preferred_element_type=f32) if mask_left
             else jnp.dot(piece, mask_bf, preferred_element_type=f32))
        acc = d if acc is None else acc + d
    return acc


def _store_row_tiles(ref, val, off=0):
    n = val.shape[0]
    for j in range(ROW_TILE):
        ref[pl.ds(off + j, n, stride=ROW_TILE), :] = val[:, j * LANES:(j + 1) * LANES]


def _load_row_tiles(ref, n, off=0):
    return jnp.concatenate([ref[pl.ds(off + j, n, stride=ROW_TILE), :] for j in range(ROW_TILE)], axis=1)


def _inproj_body(x_ref, g_ref, w_ref, xl_ref, gl_ref, ur_ref):
    h = _rms(x_ref[...], g_ref[...])
    u = jnp.dot(h.astype(bf16), w_ref[...], preferred_element_type=f32)
    xl_ref[...] = u[:, :LRU_W]
    gl_ref[...] = u[:, LRU_W:2 * LRU_W]
    ur_ref[...] = u[:, 2 * LRU_W:]


def _inproj(x2d, g, w_in_bf, tm):
    t = x2d.shape[0]
    return pl.pallas_call(
        _inproj_body,
        grid=(t // tm,),
        in_specs=[pl.BlockSpec((tm, D_MODEL), lambda i: (i, 0)), _full((1, D_MODEL)),
                  _full(w_in_bf.shape)],
        out_specs=[pl.BlockSpec((tm, LRU_W), lambda i: (i, 0)),
                   pl.BlockSpec((tm, LRU_W), lambda i: (i, 0)),
                   pl.BlockSpec((tm, RWKV_IN), lambda i: (i, 0))],
        out_shape=[jax.ShapeDtypeStruct((t, LRU_W), f32), jax.ShapeDtypeStruct((t, LRU_W), f32),
                   jax.ShapeDtypeStruct((t, RWKV_IN), f32)],
        compiler_params=_cparams(("parallel",)),
        name="inproj",
    )(x2d, g, w_in_bf)


def _lru_body(xl_ref, gl_ref, cw_ref, cb_ref, wg_ref, bg_ref, lam_ref, o_ref, tail_ref, h_ref):
    ts = xl_ref.shape[0]

    @pl.when(pl.program_id(1) == 0)
    def _():
        tail_ref[...] = jnp.zeros_like(tail_ref)
        h_ref[...] = jnp.zeros_like(h_ref)

    x = xl_ref[...]
    tail = tail_ref[...]
    cw = cw_ref[...]
    xc = cb_ref[...] + cw[CONV_W - 1:CONV_W] * x
    for d in range(1, CONV_W):
        xc = xc + cw[CONV_W - 1 - d:CONV_W - d] * _shift_rows(x, tail, d)
    tail_ref[...] = x[ts - SUBLANES:]

    gates = jax.nn.sigmoid(jnp.dot(xc.astype(bf16), wg_ref[...], preferred_element_type=f32) + bg_ref[...])
    gx = gates[:, :LRU_W]
    ga = gates[:, LRU_W:]
    log_a = -LRU_C * ga * jax.nn.softplus(-lam_ref[...])
    a = jnp.exp(log_a)
    b = jnp.sqrt(-jnp.tanh(log_a) * (a * a + 1.0)) * gx * xc

    row = lax.broadcasted_iota(i32, (ts, LRU_W), 0)
    d = 1
    while d < ts:
        keep = row >= d
        a_s = jnp.where(keep, pltpu.roll(a, d, 0), 1.0)
        b_s = jnp.where(keep, pltpu.roll(b, d, 0), 0.0)
        b = a * b_s + b
        a = a * a_s
        d *= 2
    h = b + a * h_ref[SUBLANES - 1:SUBLANES, :]
    h_ref[...] = h[ts - SUBLANES:]
    o_ref[...] = (h * jax.nn.gelu(gl_ref[...])).astype(o_ref.dtype)


def _lru(xl, gl, cw, cb, wg_bf, bg, lam, bsz, s, ts):
    nt = s // ts
    blk = pl.BlockSpec((ts, LRU_W), lambda b, i: (b * nt + i, 0))
    return pl.pallas_call(
        _lru_body,
        grid=(bsz, nt),
        in_specs=[blk, blk, _full(cw.shape), _full(cb.shape), _full(wg_bf.shape), _full(bg.shape),
                  _full(lam.shape)],
        out_specs=blk,
        out_shape=jax.ShapeDtypeStruct((bsz * s, LRU_W), bf16),
        scratch_shapes=[pltpu.VMEM((SUBLANES, LRU_W), f32), pltpu.VMEM((SUBLANES, LRU_W), f32)],
        compiler_params=_cparams(("parallel", "arbitrary")),
        name="lru",
    )(xl, gl, cw, cb, wg_bf, bg, lam)


def _rwkv_prep_body(ur_ref, mu_ref, w0_ref, a0_ref, wl_ref, gup_ref, kk_ref, ka_ref, rk_ref, ones_ref,
                    at_ref, bt_ref, kt_ref, rt_ref, v_ref, cl_ref, bon_ref, g_ref, prev_ref):
    ts = ur_ref.shape[0]

    @pl.when(pl.program_id(1) == 0)
    def _():
        prev_ref[...] = jnp.zeros_like(prev_ref)

    u0 = ur_ref[...]
    us = _shift_rows(u0, prev_ref[...], 1)
    prev_ref[...] = u0[ts - SUBLANES:]
    u = u0 + (us - u0) * mu_ref[...]
    c1, c2, c3 = RWKV_W, 2 * RWKV_W, 3 * RWKV_W
    r = u[:, :c1]
    k = u[:, c1:c2]
    v = u[:, c2:c3]
    lora = u[:, c3:c3 + LANES]
    dg = u[:, c3 + LANES:]
    lane = lax.broadcasted_iota(i32, lora.shape, 1)
    lora = jnp.where(lane < DECAY_LORA, jnp.tanh(lora), lora)
    proj = jnp.dot(lora.astype(bf16), wl_ref[...], preferred_element_type=f32)
    w = -jax.nn.softplus(-(w0_ref[...] + proj[:, :RWKV_W])) - 0.5
    lw = -jnp.exp(w)
    a = jax.nn.sigmoid(a0_ref[...] + proj[:, RWKV_W:])
    g_ref[...] = jnp.dot(jax.nn.sigmoid(dg).astype(bf16), gup_ref[...], preferred_element_type=f32)

    ones = ones_ref[...]
    kk = k * kk_ref[...]
    ss = _sum_dot(kk * kk, ones, 2)
    kk = kk / jnp.maximum(jnp.sqrt(ss), 1e-12)
    k2 = k * (1.0 + (a - 1.0) * ka_ref[...])
    bon_ref[...] = _sum_dot(r * k2 * rk_ref[...], ones, 2)

    ri = lax.broadcasted_iota(i32, (ts, ts), 0)
    ci = lax.broadcasted_iota(i32, (ts, ts), 1)
    tri = jnp.where((ci <= ri) & (ri // RWKV_CHUNK == ci // RWKV_CHUNK), 1.0, 0.0).astype(bf16)
    cl = _sum_dot(lw, tri, 3, mask_left=True)
    e_neg = jnp.exp(-cl)
    at_ref[...] = -kk * jnp.exp(cl - lw)
    bt_ref[...] = kk * a * e_neg
    kt_ref[...] = k2 * e_neg
    rt_ref[...] = r * jnp.exp(cl)
    v_ref[...] = v
    cl_ref[...] = cl


def _rwkv_prep(ur, mu, w0, a0, wl_bf, gup_bf, k_k, k_a, r_k, ones_blk, bsz, s, ts):
    nt = s // ts
    t = bsz * s
    oblk = pl.BlockSpec((ts, RWKV_W), lambda b, i: (b * nt + i, 0))
    osh = jax.ShapeDtypeStruct((t, RWKV_W), f32)
    return pl.pallas_call(
        _rwkv_prep_body,
        grid=(bsz, nt),
        in_specs=[pl.BlockSpec((ts, RWKV_IN), lambda b, i: (b * nt + i, 0)), _full(mu.shape),
                  _full(w0.shape), _full(a0.shape), _full(wl_bf.shape), _full(gup_bf.shape),
                  _full(k_k.shape), _full(k_a.shape), _full(r_k.shape), _full(ones_blk.shape)],
        out_specs=[oblk] * 8,
        out_shape=[osh] * 8,
        scratch_shapes=[pltpu.VMEM((SUBLANES, RWKV_IN), f32)],
        compiler_params=_cparams(("parallel", "arbitrary")),
        name="rwkv_prep",
    )(ur, mu, w0, a0, wl_bf, gup_bf, k_k, k_a, r_k, ones_blk)


def _mm_x3(a, b_parts):
    ah, al = _bf16_parts(a, 2)
    bh, bl = b_parts
    d = lambda x, y: jnp.dot(x, y, preferred_element_type=f32)
    return d(ah, bh) + (d(ah, bl) + d(al, bh))


def _mm(a, b):
    return jnp.dot(a.astype(bf16), b.astype(bf16), preferred_element_type=f32)


def _mm_nt(a, b):
    return lax.dot_general(a.astype(bf16), b.astype(bf16), (((1,), (1,)), ((), ())), preferred_element_type=f32)


def _mm_tn(a, b):
    return lax.dot_general(a.astype(bf16), b.astype(bf16), (((0,), (0,)), ((), ())), preferred_element_type=f32)


def _stack_heads(x, m0):
    return jnp.concatenate([jnp.where(m0, x, 0.0), jnp.where(m0, 0.0, x)], axis=0)


def _unstack(x):
    c = x.shape[0] // 2
    return x[:c] + x[c:]


def _chunk_maps(chunks):
    c = RWKV_CHUNK
    n = 2 * c
    lane = lax.broadcasted_iota(i32, (c, PAIR), 1)
    m0 = lane < RWKV_HEAD
    ri = lax.broadcasted_iota(i32, (n, n), 0)
    ci = lax.broadcasted_iota(i32, (n, n), 1)
    strict = ci < ri
    incl2 = jnp.concatenate([ci <= ri, ci <= ri], axis=1)
    diag = ri == ci
    eye = jnp.where(diag, 1.0, 0.0).astype(f32)
    each = lambda f, *ls: [f(*xs) for xs in zip(*ls)]

    ats, bts, kts, rts, vs, cls = [list(x) for x in zip(*chunks)]
    pcs = each(lambda cl: jnp.exp(cl[c - 1:c, :]), cls)
    a_s = each(lambda x: _stack_heads(x, m0).astype(bf16), ats)
    r_s = each(lambda x: _stack_heads(x, m0).astype(bf16), rts)
    b_f = each(lambda x: _stack_heads(x, m0), bts)
    k_f = each(lambda x: _stack_heads(x, m0), kts)
    v_s = each(lambda x: _stack_heads(x, m0).astype(bf16), vs)

    aa = each(lambda a, r, b, k: _mm_nt(jnp.concatenate([a, r], axis=0),
                                        jnp.concatenate([b.astype(bf16), k.astype(bf16)], axis=0)),
              a_s, r_s, b_f, k_f)
    l_ab = each(lambda x: jnp.where(strict, x[:n, :n], 0.0), aa)
    a_ak = each(lambda x: jnp.where(strict, x[:n, n:], 0.0), aa)
    a_r = each(lambda x: jnp.where(incl2, x[n:, :], 0.0).astype(bf16), aa)

    tinv = each(lambda x: eye + x, l_ab)
    lp = l_ab
    p = 2
    while p < c:
        lp = each(lambda x: _mm(x, x), lp)
        tinv = each(lambda t, x: t + _mm(t, x), tinv, lp)
        p *= 2

    w_pre = each(_mm, a_ak, v_s)
    za = each(lambda t, w, a: _mm(t, jnp.concatenate([w.astype(bf16), a], axis=1)), tinv, w_pre, a_s)
    zp_s = each(lambda x: x[:, :PAIR].astype(bf16), za)
    ac_s = each(lambda x: x[:, PAIR:].astype(bf16), za)
    yp = each(lambda ar, z, v: _unstack(_mm(ar, jnp.concatenate([z, v], axis=0))), a_r, zp_s, v_s)
    rc = each(lambda r, ar, ac: r + _unstack(_mm(ar[:, :n], ac)), rts, a_r, ac_s)
    sm = each(lambda b, pc, z, ac: _mm_tn(b * pc, jnp.concatenate([z, ac], axis=1)), b_f, pcs, zp_s, ac_s)
    sp = each(lambda k, pc, v, m: _mm_tn(k * pc, v) + m[:, :PAIR], k_f, pcs, v_s, sm)
    g = each(lambda pc, m: jnp.where(diag, jnp.broadcast_to(pc, (n, n)), 0.0) + m[:, PAIR:], pcs, sm)
    return list(zip(g, sp, rc, yp))


def _rwkv_chunk_body(at_ref, bt_ref, kt_ref, rt_ref, v_ref, cl_ref, g_ref, sp_ref, rc_ref, yp_ref):
    ts = at_ref.shape[0]
    c = RWKV_CHUNK
    sls = [slice(j * c, (j + 1) * c) for j in range(ts // c)]
    outs = _chunk_maps([(at_ref[sl, :], bt_ref[sl, :], kt_ref[sl, :], rt_ref[sl, :], v_ref[sl, :], cl_ref[sl, :])
                        for sl in sls])
    for j, (sl, (g, sp, rc, yp)) in enumerate(zip(sls, outs)):
        g_ref[j, 0] = g
        sp_ref[j, 0] = sp
        rc_ref[sl, :] = rc
        yp_ref[sl, :] = yp


def _rwkv_chunk(at, bt, kt, rt, v, cl, ts):
    t = at.shape[0]
    nck = ts // RWKV_CHUNK
    iblk = pl.BlockSpec((ts, PAIR), lambda i, p: (i, p))
    mblk = pl.BlockSpec((nck, 1, PAIR, PAIR), lambda i, p: (i, p, 0, 0))
    msh = jax.ShapeDtypeStruct((t // RWKV_CHUNK, N_PAIRS, PAIR, PAIR), f32)
    osh = jax.ShapeDtypeStruct((t, RWKV_W), f32)
    return pl.pallas_call(
        _rwkv_chunk_body,
        grid=(t // ts, N_PAIRS),
        in_specs=[iblk] * 6,
        out_specs=[mblk, mblk, iblk, iblk],
        out_shape=[msh, msh, osh, osh],
        compiler_params=_cparams(("parallel", "parallel")),
        name="rwkv_chunk",
    )(at, bt, kt, rt, v, cl)


def _rwkv_state_body(g_ref, sp_ref, rc_ref, yp_ref, y_ref, s_ref):
    @pl.when(pl.program_id(0) == 0)
    def _():
        s_ref[...] = jnp.zeros_like(s_ref)

    c = RWKV_CHUNK
    bsz, nck = g_ref.shape[0], g_ref.shape[1]
    chains = [(b, p) for b in range(bsz) for p in range(N_PAIRS)]
    states = [s_ref[b, p] for b, p in chains]
    for j in range(nck):
        sl = slice(j * c, (j + 1) * c)
        parts = [_bf16_parts(s, 2) for s in states]
        for (b, p), sp2 in zip(chains, parts):
            ls = slice(p * PAIR, (p + 1) * PAIR)
            y_ref[b, sl, ls] = yp_ref[b, sl, ls] + _mm_x3(rc_ref[b, sl, ls], sp2)
        states = [_mm_x3(g_ref[b, j, p], sp2) + sp_ref[b, j, p] for (b, p), sp2 in zip(chains, parts)]
    for (b, p), s in zip(chains, states):
        s_ref[b, p] = s


def _rwkv_state(g, sp, rc, yp, bsz, s, nck):
    nc = s // RWKV_CHUNK
    rows = nck * RWKV_CHUNK
    mblk = pl.BlockSpec((bsz, nck, N_PAIRS, PAIR, PAIR), lambda i: (0, i, 0, 0, 0))
    rblk = pl.BlockSpec((bsz, rows, RWKV_W), lambda i: (0, i, 0))
    m5 = lambda a: a.reshape(bsz, nc, N_PAIRS, PAIR, PAIR)
    r3 = lambda a: a.reshape(bsz, s, RWKV_W)
    y = pl.pallas_call(
        _rwkv_state_body,
        grid=(nc // nck,),
        in_specs=[mblk, mblk, rblk, rblk],
        out_specs=rblk,
        out_shape=jax.ShapeDtypeStruct((bsz, s, RWKV_W), f32),
        scratch_shapes=[pltpu.VMEM((bsz, N_PAIRS, PAIR, PAIR), f32)],
        compiler_params=_cparams(("arbitrary",)),
        name="rwkv_state",
    )(m5(g), m5(sp), r3(rc), r3(yp))
    return y.reshape(bsz * s, RWKV_W)


def _outproj_body(x_ref, yl_ref, ys_ref, bon_ref, v_ref, g_ref, lg_ref, lb_ref, ones_ref, w_ref, o_ref):
    y = ys_ref[...]
    ones = ones_ref[...]
    inv_n = 1.0 / RWKV_HEAD
    mean = _sum_dot(y, ones, 2) * inv_n
    yc = y - mean
    var = _sum_dot(yc * yc, ones, 2) * inv_n
    yn = yc * lax.rsqrt(var + GN_EPS) * lg_ref[...] + lb_ref[...]
    yr = (yn + bon_ref[...] * v_ref[...]) * g_ref[...]
    cat = jnp.concatenate([yl_ref[...], yr.astype(bf16)], axis=1)
    o_ref[...] = x_ref[...] + jnp.dot(cat, w_ref[...], preferred_element_type=f32)


def _outproj(x2d, y_lru, y_scan, bon, v, g, lnx_g, lnx_b, ones_blk, w_out_bf, tm):
    t = x2d.shape[0]
    xb = pl.BlockSpec((tm, D_MODEL), lambda i: (i, 0))
    hb = pl.BlockSpec((tm, RWKV_W), lambda i: (i, 0))
    return pl.pallas_call(
        _outproj_body,
        grid=(t // tm,),
        in_specs=[xb, hb, hb, hb, hb, hb, _full(lnx_g.shape), _full(lnx_b.shape), _full(ones_blk.shape),
                  _full(w_out_bf.shape)],
        out_specs=xb,
        out_shape=jax.ShapeDtypeStruct((t, D_MODEL), f32),
        compiler_params=_cparams(("parallel",)),
        name="outproj",
    )(x2d, y_lru, y_scan, bon, v, g, lnx_g, lnx_b, ones_blk, w_out_bf)


def _memkv_body(m_ref, g_ref, wk_ref, wv_ref, k_ref, v_ref):
    h = _rms(m_ref[...], g_ref[...]).astype(bf16)
    k_ref[...] = jnp.dot(h, wk_ref[...], preferred_element_type=f32).astype(bf16)
    v_ref[...] = jnp.dot(h, wv_ref[...], preferred_element_type=f32).astype(bf16)


def _memkv(mem2d, g, wk_bf, wv_bf, tm):
    t = mem2d.shape[0]
    blk = pl.BlockSpec((tm, D_MODEL), lambda i: (i, 0))
    sh = jax.ShapeDtypeStruct((t, D_MODEL), bf16)
    return pl.pallas_call(
        _memkv_body,
        grid=(t // tm,),
        in_specs=[blk, _full(g.shape), _full(wk_bf.shape), _full(wv_bf.shape)],
        out_specs=[blk, blk],
        out_shape=[sh, sh],
        compiler_params=_cparams(("parallel",)),
        name="memkv",
    )(mem2d, g, wk_bf, wv_bf)


def _xattn_body(x_ref, k_ref, v_ref, gx_ref, wq_ref, wo_ref, gf_ref, wr_ref, br_ref, upper_ref,
                x2_ref, hf_ref, idx_ref, gate_ref, cnt_ref, base_ref):
    x = x_ref[...]
    h = _rms(x, gx_ref[...]).astype(bf16)
    q = jnp.dot(h, wq_ref[...], preferred_element_type=f32).astype(bf16)
    k = k_ref[...]
    v = v_ref[...]
    outs = []
    for hd in range(XA_HEADS):
        sl = slice(hd * XA_HEAD, (hd + 1) * XA_HEAD)
        sc = lax.dot_general(q[:, sl], k[:, sl], (((1,), (1,)), ((), ())),
                             preferred_element_type=f32) * (XA_HEAD ** -0.5)
        sc = sc - jnp.max(sc, axis=-1, keepdims=True)
        e = jnp.exp(sc)
        p = e / jnp.sum(e, axis=-1, keepdims=True)
        outs.append(jnp.dot(p.astype(bf16), v[:, sl], preferred_element_type=f32).astype(bf16))
    o = jnp.concatenate(outs, axis=1)
    x2 = x + jnp.dot(o, wo_ref[...], preferred_element_type=f32)
    x2_ref[...] = x2

    hf = _rms(x2, gf_ref[...])
    _store_row_tiles(hf_ref, hf)
    logits = lax.dot_general(wr_ref[...], hf.astype(bf16), (((1,), (1,)), ((), ())),
                             preferred_element_type=f32) + br_ref[...]
    erow = lax.broadcasted_iota(i32, logits.shape, 0)
    neg = jnp.float32(-jnp.inf)
    cur = logits
    vals = []
    idxs = []
    for _ in range(TOP_K):
        m = jnp.max(cur, axis=0, keepdims=True)
        am = jnp.min(jnp.where(cur == m, erow, N_EXPERTS), axis=0, keepdims=True)
        vals.append(m)
        idxs.append(am)
        cur = jnp.where(erow == am, neg, cur)
    es = [jnp.exp(vk - vals[0]) for vk in vals]
    den = es[0] + es[1] + es[2] + es[3]

    @pl.when(pl.program_id(0) == 0)
    def _():
        base_ref[...] = jnp.zeros_like(base_ref)

    onehot = [jnp.where(erow == am, 1.0, 0.0) for am in idxs]
    cnt = (onehot[0] + onehot[1]) + (onehot[2] + onehot[3])
    base = base_ref[:, 0:1]
    prior = jnp.dot(cnt.astype(bf16), upper_ref[...], preferred_element_type=f32) + base
    base_ref[...] = jnp.broadcast_to(base + jnp.sum(cnt, axis=1, keepdims=True), base_ref.shape)
    cnt_ref[...] = base_ref[...]

    orow = lax.broadcasted_iota(i32, idx_ref.shape, 0)
    idx_out = jnp.zeros(idx_ref.shape, i32)
    gate_out = jnp.zeros(gate_ref.shape, f32)
    for kq in range(TOP_K):
        rank = jnp.sum(prior * onehot[kq], axis=0, keepdims=True).astype(i32)
        idx_out = jnp.where(orow == kq, idxs[kq], idx_out)
        idx_out = jnp.where(orow == TOP_K + kq, rank, idx_out)
        gate_out = jnp.where(orow == kq, es[kq] / den, gate_out)
    idx_ref[...] = idx_out
    gate_ref[...] = gate_out


def _xattn(x1, kmem, vmem, g_xa, wq_bf, wo_bf, g_ffn, wr_t, br_col, bsz, s, mlen, tm):
    t = bsz * s
    nt = s // tm
    xb = pl.BlockSpec((tm, D_MODEL), lambda i: (i, 0))
    mb = pl.BlockSpec((mlen, D_MODEL), lambda i: (i // nt, 0))
    lb = pl.BlockSpec((2 * TOP_K, tm), lambda i: (0, i))
    upper = jnp.triu(jnp.ones((tm, tm), bf16), k=1)
    return pl.pallas_call(
        _xattn_body,
        grid=(t // tm,),
        in_specs=[xb, mb, mb, _full(g_xa.shape), _full(wq_bf.shape), _full(wo_bf.shape),
                  _full(g_ffn.shape), _full(wr_t.shape), _full(br_col.shape), _full(upper.shape)],
        out_specs=[xb, pl.BlockSpec((tm * ROW_TILE, LANES), lambda i: (i, 0)), lb, lb, _full((N_EXPERTS, LANES))],
        out_shape=[jax.ShapeDtypeStruct((t, D_MODEL), f32), jax.ShapeDtypeStruct((t * ROW_TILE, LANES), f32),
                   jax.ShapeDtypeStruct((2 * TOP_K, t), i32), jax.ShapeDtypeStruct((2 * TOP_K, t), f32),
                   jax.ShapeDtypeStruct((N_EXPERTS, LANES), f32)],
        scratch_shapes=[pltpu.VMEM((N_EXPERTS, LANES), f32)],
        compiler_params=_cparams(("arbitrary",)),
        name="xattn",
    )(x1, kmem, vmem, g_xa, wq_bf, wo_bf, g_ffn, wr_t, br_col, upper)


DMA_UNROLL = 8


def _tile_at(ref, row):
    return ref.at[pl.ds(pl.multiple_of(row * ROW_TILE, ROW_TILE), ROW_TILE), :]


def _dispatch_body(dest_ref, pend_ref, hf_ref, xs_hbm, zeros_ref, sem, zsem):
    i = pl.program_id(0)
    tm = hf_ref.shape[0] // ROW_TILE
    zrows = MOE_BLOCK * ROW_TILE

    def zero_copy(e):
        start = pl.multiple_of((pend_ref[e] - MOE_BLOCK) * ROW_TILE, ROW_TILE)
        return pltpu.make_async_copy(zeros_ref, xs_hbm.at[pl.ds(start, zrows), :], zsem)

    def nonempty(e):
        return pend_ref[e] > (pend_ref[e - 1] if e else 0)

    @pl.when(i == 0)
    def _():
        zeros_ref[...] = jnp.zeros_like(zeros_ref)
        for e in range(N_EXPERTS):
            @pl.when(nonempty(e))
            def _():
                zero_copy(e).start()
        for e in range(N_EXPERTS):
            @pl.when(nonempty(e))
            def _():
                zero_copy(e).wait()

        def tail_copy(b):
            return pltpu.make_async_copy(zeros_ref, xs_hbm.at[pl.ds(pl.multiple_of(b * zrows, zrows), zrows), :], zsem)

        def tail_start(b, carry):
            tail_copy(b).start()
            return carry

        def tail_wait(b, carry):
            tail_copy(b).wait()
            return carry
        n_used = pend_ref[N_EXPERTS - 1] // MOE_BLOCK
        n_all = xs_hbm.shape[0] // zrows
        lax.fori_loop(n_used, n_all, tail_start, 0)
        lax.fori_loop(n_used, n_all, tail_wait, 0)

    def body(q, carry):
        for u in range(DMA_UNROLL):
            r = q * DMA_UNROLL + u
            src = _tile_at(hf_ref, r)
            for kq in range(TOP_K):
                pltpu.make_async_copy(src, _tile_at(xs_hbm, dest_ref[(i * tm + r) * TOP_K + kq]),
                                      sem).start(priority=kq % 2)
        return carry
    lax.fori_loop(0, tm // DMA_UNROLL, body, 0)
    for kq in range(TOP_K):
        pltpu.make_async_copy(hf_ref, xs_hbm.at[pl.ds(0, tm * ROW_TILE), :], sem).wait()


def _dispatch(dest_flat, pends, hf_tiles, n_blocks, tm):
    t = hf_tiles.shape[0] // ROW_TILE
    rows = n_blocks * MOE_BLOCK
    grid_spec = pltpu.PrefetchScalarGridSpec(
        num_scalar_prefetch=2,
        grid=(t // tm,),
        in_specs=[pl.BlockSpec((tm * ROW_TILE, LANES), lambda i, d, pe: (i, 0))],
        out_specs=pl.BlockSpec(memory_space=pl.ANY),
        scratch_shapes=[pltpu.VMEM((MOE_BLOCK * ROW_TILE, LANES), f32), pltpu.SemaphoreType.DMA(()),
                        pltpu.SemaphoreType.DMA(())],
    )
    return pl.pallas_call(
        _dispatch_body,
        grid_spec=grid_spec,
        out_shape=jax.ShapeDtypeStruct((rows * ROW_TILE, LANES), f32),
        compiler_params=_cparams(("arbitrary",)),
        name="dispatch",
    )(dest_flat, pends, hf_tiles)


def _moe_body(be_ref, nb_ref, x_ref, wgu_ref, bgu_ref, wdn_ref, bdn_ref, o_ref, wgu_bf, wdn_bf):
    i = pl.program_id(0)

    @pl.when(i >= nb_ref[0])
    def _():
        o_ref[...] = jnp.zeros_like(o_ref)

    @pl.when(i < nb_ref[0])
    def _():
        prev_e = be_ref[jnp.maximum(i - 1, 0)]

        @pl.when((i == 0) | (be_ref[i] != prev_e))
        def _():
            wgu_bf[...] = wgu_ref[0].astype(bf16)
            wdn_bf[...] = wdn_ref[0].astype(bf16)

        xb = _load_row_tiles(x_ref, MOE_BLOCK).astype(bf16)
        gu = jnp.dot(xb, wgu_bf[...], preferred_element_type=f32) + bgu_ref[0]
        gate = jnp.minimum(gu[:, :D_FF], SWIGLU_LIMIT)
        up = jnp.clip(gu[:, D_FF:], -SWIGLU_LIMIT, SWIGLU_LIMIT)
        act = (up + 1.0) * (gate * jax.nn.sigmoid(SWIGLU_ALPHA * gate))
        y = jnp.dot(act.astype(bf16), wdn_bf[...], preferred_element_type=f32) + bdn_ref[0]
        _store_row_tiles(o_ref, y)


def _moe(block_e, n_used, xs_tiles, w_gu, b_gu, w_dn, b_dn, n_blocks):
    rows = n_blocks * MOE_BLOCK
    xblk = pl.BlockSpec((MOE_BLOCK * ROW_TILE, LANES), lambda i, be, nb: (i, 0))
    grid_spec = pltpu.PrefetchScalarGridSpec(
        num_scalar_prefetch=2,
        grid=(n_blocks,),
        in_specs=[
            pl.BlockSpec((MOE_BLOCK * ROW_TILE, LANES), lambda i, be, nb: (jnp.minimum(i, nb[0] - 1), 0)),
            pl.BlockSpec((1, D_MODEL, 2 * D_FF), lambda i, be, nb: (be[i], 0, 0)),
            pl.BlockSpec((1, 1, 2 * D_FF), lambda i, be, nb: (be[i], 0, 0)),
            pl.BlockSpec((1, D_FF, D_MODEL), lambda i, be, nb: (be[i], 0, 0)),
            pl.BlockSpec((1, 1, D_MODEL), lambda i, be, nb: (be[i], 0, 0)),
        ],
        out_specs=xblk,
        scratch_shapes=[pltpu.VMEM((D_MODEL, 2 * D_FF), bf16), pltpu.VMEM((D_FF, D_MODEL), bf16)],
    )
    return pl.pallas_call(
        _moe_body,
        grid_spec=grid_spec,
        out_shape=jax.ShapeDtypeStruct((rows * ROW_TILE, LANES), f32),
        compiler_params=_cparams(("arbitrary",)),
        name="moe",
    )(block_e, n_used, xs_tiles, w_gu, b_gu.reshape(N_EXPERTS, 1, 2 * D_FF), w_dn,
      b_dn.reshape(N_EXPERTS, 1, D_MODEL))


def _combine_body(pos_ref, ys_hbm, x_ref, gate_ref, g_ref, o_ref, buf, sem):
    i = pl.program_id(0)
    n = pl.num_programs(0)
    tc = x_ref.shape[0]
    slot = i % 2
    slot_rows = TOP_K * tc * ROW_TILE

    def start(step, sl):
        def body(q, carry):
            for u in range(DMA_UNROLL):
                r = q * DMA_UNROLL + u
                for kq in range(TOP_K):
                    dst = _tile_at(buf, (sl * TOP_K + kq) * tc + r)
                    pltpu.make_async_copy(_tile_at(ys_hbm, pos_ref[(step * tc + r) * TOP_K + kq]), dst,
                                          sem.at[sl]).start(priority=kq % 2)
            return carry
        lax.fori_loop(0, tc // DMA_UNROLL, body, 0)

    @pl.when(i == 0)
    def _():
        start(0, 0)

    @pl.when(i + 1 < n)
    def _():
        start(i + 1, 1 - slot)

    off = pl.multiple_of(slot * slot_rows, slot_rows)
    pltpu.make_async_copy(ys_hbm.at[pl.ds(0, slot_rows), :], buf.at[pl.ds(off, slot_rows), :], sem.at[slot]).wait()
    gates = gate_ref[...]
    acc = x_ref[...]
    for kq in range(TOP_K):
        rows = _load_row_tiles(buf, tc, off + kq * tc * ROW_TILE)
        acc = acc + gates[:, kq:kq + 1] * rows
    o_ref[...] = _rms(acc, g_ref[...])


def _combine(pos_flat, ys_tiles, x2, gate_pad, g_final, tc):
    t = x2.shape[0]
    grid_spec = pltpu.PrefetchScalarGridSpec(
        num_scalar_prefetch=1,
        grid=(t // tc,),
        in_specs=[pl.BlockSpec(memory_space=pl.ANY),
                  pl.BlockSpec((tc, D_MODEL), lambda i, p: (i, 0)),
                  pl.BlockSpec((tc, LANES), lambda i, p: (i, 0)),
                  pl.BlockSpec((1, D_MODEL), lambda i, p: (0, 0))],
        out_specs=pl.BlockSpec((tc, D_MODEL), lambda i, p: (i, 0)),
        scratch_shapes=[pltpu.VMEM((2 * TOP_K * tc * ROW_TILE, LANES), f32), pltpu.SemaphoreType.DMA((2,))],
    )
    return pl.pallas_call(
        _combine_body,
        grid_spec=grid_spec,
        out_shape=jax.ShapeDtypeStruct((t, D_MODEL), f32),
        compiler_params=_cparams(("arbitrary",)),
        name="combine",
    )(pos_flat, ys_tiles, x2, gate_pad, g_final)


def _routing(top_idx, rank, counts, t):
    n_assign = t * TOP_K
    experts = jnp.arange(N_EXPERTS, dtype=i32)
    padded = (counts + MOE_BLOCK - 1) // MOE_BLOCK * MOE_BLOCK
    pends = jnp.cumsum(padded).astype(i32)
    pstarts = pends - padded
    start_of = jnp.sum(jnp.where(top_idx[:, :, None] == experts, pstarts, 0), axis=-1)
    dest = (start_of + rank).astype(i32).reshape(n_assign)
    n_blocks = (n_assign + N_EXPERTS * (MOE_BLOCK - 1) + MOE_BLOCK - 1) // MOE_BLOCK
    block_start = jnp.arange(n_blocks, dtype=i32) * MOE_BLOCK
    block_e = jnp.minimum(jnp.sum((block_start[:, None] >= pends[None, :]).astype(i32), axis=1), N_EXPERTS - 1)
    n_used = (pends[-1:] // MOE_BLOCK).astype(i32)
    return block_e.astype(i32), n_used, dest, pends, n_blocks


def _block_diag(w):
    n, bi, bj = w.shape
    eye = jnp.eye(n, dtype=w.dtype)
    return jnp.einsum('nij,nm->nimj', w, eye).reshape(n * bi, n * bj)


def _layer(x2d, mem2d, bsz, s, mlen, p):
    t = bsz * s
    row = lambda a: a.reshape(1, -1)
    ones_blk = _block_diag(jnp.ones((RWKV_W // RWKV_HEAD, RWKV_HEAD, RWKV_HEAD), bf16))

    xl, gl, ur = _inproj(x2d, row(p['norm_mix_g']), p['w_in'].astype(bf16), tm=min(512, t))

    wg = jnp.concatenate([_block_diag(p['lru_wx']), _block_diag(p['lru_wa'])], axis=1).astype(bf16)
    bg = jnp.concatenate([p['lru_bx'], p['lru_ba']]).reshape(1, -1)
    y_lru = _lru(xl, gl, p['conv_w'], row(p['conv_b']), wg, bg, row(p['lru_lambda']), bsz, s, ts=min(256, s))

    zl = jnp.zeros((DECAY_LORA, RWKV_W), f32)
    wl = jnp.concatenate([jnp.concatenate([p['rwkv_w_up'], zl], axis=1),
                          jnp.concatenate([zl, p['rwkv_a_up']], axis=1)], axis=0).astype(bf16)
    at, bt, kt, rt, v, cl, bon, g = _rwkv_prep(
        ur, row(p['rwkv_mu']), row(p['rwkv_w0']), row(p['rwkv_a0']), wl, p['rwkv_g_up'].astype(bf16),
        row(p['rwkv_k_k']), row(p['rwkv_k_a']), row(p['rwkv_r_k']), ones_blk, bsz, s, ts=min(256, s))
    gm, sp, rc, yp = _rwkv_chunk(at, bt, kt, rt, v, cl, ts=min(512, s))
    y_scan = _rwkv_state(gm, sp, rc, yp, bsz, s, nck=min(4, s // RWKV_CHUNK))

    x1 = _outproj(x2d, y_lru, y_scan, bon, v, g, row(p['rwkv_lnx_g']), row(p['rwkv_lnx_b']), ones_blk,
                  p['w_out'].astype(bf16), tm=min(512, t))

    kmem, vmem = _memkv(mem2d, row(p['norm_mem_g']), p['xa_wk'].astype(bf16), p['xa_wv'].astype(bf16),
                        tm=min(512, bsz * mlen))
    x2, hf, route, gates, cnt_pad = _xattn(x1, kmem, vmem, row(p['norm_xa_g']), p['xa_wq'].astype(bf16),
                                           p['xa_wo'].astype(bf16), row(p['norm_ffn_g']),
                                           p['w_router'].T.astype(bf16), p['b_router'].reshape(-1, 1),
                                           bsz, s, mlen, tm=min(512, s))

    counts = cnt_pad[:, 0].astype(i32)
    block_e, n_used, dest, pends, n_blocks = _routing(route[:TOP_K].T, route[TOP_K:].T, counts, t)
    gate_pad = jnp.pad(gates[:TOP_K].T, ((0, 0), (0, LANES - TOP_K)))
    xs = _dispatch(dest, pends, hf, n_blocks, tm=min(256, t))
    ys = _moe(block_e, n_used, xs, p['w_gu'], p['b_gu'], p['w_dn'], p['b_dn'], n_blocks)
    return _combine(dest, ys, x2, gate_pad, row(p['final_norm_g']), tc=min(256, t))


def kernel(x, mem, norm_mix_g, w_in, conv_w, conv_b, lru_wx, lru_bx, lru_wa, lru_ba, lru_lambda, rwkv_mu, rwkv_w0, rwkv_w_up, rwkv_a0, rwkv_a_up, rwkv_g_up, rwkv_k_k, rwkv_k_a, rwkv_r_k, rwkv_lnx_g, rwkv_lnx_b, w_out, norm_xa_g, norm_mem_g, xa_wq, xa_wk, xa_wv, xa_wo, norm_ffn_g, w_router, b_router, w_gu, b_gu, w_dn, b_dn, final_norm_g):
    bsz, s, d = x.shape
    mlen = mem.shape[1]
    assert d == D_MODEL and w_in.shape[0] == 1
    p = dict(norm_mix_g=norm_mix_g[0], w_in=w_in[0], conv_w=conv_w[0], conv_b=conv_b[0], lru_wx=lru_wx[0],
             lru_bx=lru_bx[0], lru_wa=lru_wa[0], lru_ba=lru_ba[0], lru_lambda=lru_lambda[0],
             rwkv_mu=rwkv_mu[0], rwkv_w0=rwkv_w0[0], rwkv_w_up=rwkv_w_up[0], rwkv_a0=rwkv_a0[0],
             rwkv_a_up=rwkv_a_up[0], rwkv_g_up=rwkv_g_up[0], rwkv_k_k=rwkv_k_k[0], rwkv_k_a=rwkv_k_a[0],
             rwkv_r_k=rwkv_r_k[0].reshape(-1), rwkv_lnx_g=rwkv_lnx_g[0], rwkv_lnx_b=rwkv_lnx_b[0],
             w_out=w_out[0], norm_xa_g=norm_xa_g[0], norm_mem_g=norm_mem_g[0], xa_wq=xa_wq[0],
             xa_wk=xa_wk[0], xa_wv=xa_wv[0], xa_wo=xa_wo[0], norm_ffn_g=norm_ffn_g[0],
             w_router=w_router[0], b_router=b_router[0], w_gu=w_gu[0], b_gu=b_gu[0], w_dn=w_dn[0],
             b_dn=b_dn[0], final_norm_g=final_norm_g)
    out = _layer(x.reshape(bsz * s, d), mem.reshape(bsz * mlen, d), bsz, s, mlen, p)
    return out.reshape(bsz, s, d)
```

```python
import functools

import jax
import jax.numpy as jnp
from jax import lax
from jax.experimental import pallas as pl
from jax.experimental.pallas import tpu as pltpu

f32 = jnp.float32
bf16 = jnp.bfloat16
i32 = jnp.int32

D_MODEL = 1024
LRU_W = 512
RWKV_W = 512
LRU_BLOCKS = 8
LRU_BLOCK = 64
CONV_W = 4
LRU_C = 8.0
RWKV_HEAD = 64
DECAY_LORA = 64
AAA_LORA = 64
GATE_LORA = 128
RWKV_IN = 3 * RWKV_W + DECAY_LORA + AAA_LORA + GATE_LORA
XA_HEADS = 4
XA_HEAD = D_MODEL // XA_HEADS
N_EXPERTS = 32
TOP_K = 4
D_FF = D_MODEL
SWIGLU_LIMIT = 7.0
SWIGLU_ALPHA = 1.702
EPS = 1e-6
GN_EPS = 64e-5

LANES = 128
SUBLANES = 8
RWKV_CHUNK = 64
PAIR = 2 * RWKV_HEAD
N_PAIRS = RWKV_W // PAIR
MOE_BLOCK = 256
ROW_TILE = D_MODEL // LANES
VMEM_LIMIT = 52 * 1024 * 1024


def _cparams(sem):
    return pltpu.CompilerParams(dimension_semantics=sem, vmem_limit_bytes=VMEM_LIMIT)


def _rms(x, g):
    return x * lax.rsqrt(jnp.mean(x * x, axis=-1, keepdims=True) + EPS) * g


def _full(shape):
    n = len(shape)
    return pl.BlockSpec(shape, lambda *a: (0,) * n)


def _shift_rows(x, prev8, d):
    xr = pltpu.roll(x, d, 0)
    tr = pltpu.roll(prev8, d, 0)
    row = lax.broadcasted_iota(i32, prev8.shape, 0)
    head = jnp.where(row < d, tr, xr[:SUBLANES])
    return jnp.concatenate([head, xr[SUBLANES:]], axis=0)


def _bf16_parts(x, n):
    parts = []
    for _ in range(n):
        piece = x.astype(bf16)
        parts.append(piece)
        x = x - piece.astype(f32)
    return parts


def _sum_dot(x, mask_bf, n_parts, mask_left=False):
    acc = None
    for piece in _bf16_parts(x, n_parts):
        d = (jnp.dot(mask_bf, piece, preferred_element_type=f32) if mask_left
             else jnp.dot(piece, mask_bf, preferred_element_type=f32))
        acc = d if acc is None else acc + d
    return acc


def _store_row_tiles(ref, val, off=0):
    n = val.shape[0]
    for j in range(ROW_TILE):
        ref[pl.ds(off + j, n, stride=ROW_TILE), :] = val[:, j * LANES:(j + 1) * LANES]


def _load_row_tiles(ref, n, off=0):
    return jnp.concatenate([ref[pl.ds(off + j, n, stride=ROW_TILE), :] for j in range(ROW_TILE)], axis=1)


def _inproj_body(x_ref, g_ref, w_ref, xl_ref, gl_ref, ur_ref):
    h = _rms(x_ref[...], g_ref[...])
    u = jnp.dot(h.astype(bf16), w_ref[...], preferred_element_type=f32)
    xl_ref[...] = u[:, :LRU_W]
    gl_ref[...] = u[:, LRU_W:2 * LRU_W]
    ur_ref[...] = u[:, 2 * LRU_W:]


def _inproj(x2d, g, w_in_bf, tm):
    t = x2d.shape[0]
    return pl.pallas_call(
        _inproj_body,
        grid=(t // tm,),
        in_specs=[pl.BlockSpec((tm, D_MODEL), lambda i: (i, 0)), _full((1, D_MODEL)),
                  _full(w_in_bf.shape)],
        out_specs=[pl.BlockSpec((tm, LRU_W), lambda i: (i, 0)),
                   pl.BlockSpec((tm, LRU_W), lambda i: (i, 0)),
                   pl.BlockSpec((tm, RWKV_IN), lambda i: (i, 0))],
        out_shape=[jax.ShapeDtypeStruct((t, LRU_W), f32), jax.ShapeDtypeStruct((t, LRU_W), f32),
                   jax.ShapeDtypeStruct((t, RWKV_IN), f32)],
        compiler_params=_cparams(("parallel",)),
        name="inproj",
    )(x2d, g, w_in_bf)


def _lru_body(xl_ref, gl_ref, cw_ref, cb_ref, wg_ref, bg_ref, lam_ref, o_ref, tail_ref, h_ref):
    ts = xl_ref.shape[0]

    @pl.when(pl.program_id(1) == 0)
    def _():
        tail_ref[...] = jnp.zeros_like(tail_ref)
        h_ref[...] = jnp.zeros_like(h_ref)

    x = xl_ref[...]
    tail = tail_ref[...]
    cw = cw_ref[...]
    xc = cb_ref[...] + cw[CONV_W - 1:CONV_W] * x
    for d in range(1, CONV_W):
        xc = xc + cw[CONV_W - 1 - d:CONV_W - d] * _shift_rows(x, tail, d)
    tail_ref[...] = x[ts - SUBLANES:]

    gates = jax.nn.sigmoid(jnp.dot(xc.astype(bf16), wg_ref[...], preferred_element_type=f32) + bg_ref[...])
    gx = gates[:, :LRU_W]
    ga = gates[:, LRU_W:]
    log_a = -LRU_C * ga * jax.nn.softplus(-lam_ref[...])
    a = jnp.exp(log_a)
    b = jnp.sqrt(-jnp.tanh(log_a) * (a * a + 1.0)) * gx * xc

    row = lax.broadcasted_iota(i32, (ts, LRU_W), 0)
    d = 1
    while d < ts:
        keep = row >= d
        a_s = jnp.where(keep, pltpu.roll(a, d, 0), 1.0)
        b_s = jnp.where(keep, pltpu.roll(b, d, 0), 0.0)
        b = a * b_s + b
        a = a * a_s
        d *= 2
    h = b + a * h_ref[SUBLANES - 1:SUBLANES, :]
    h_ref[...] = h[ts - SUBLANES:]
    o_ref[...] = (h * jax.nn.gelu(gl_ref[...])).astype(o_ref.dtype)


def _lru(xl, gl, cw, cb, wg_bf, bg, lam, bsz, s, ts):
    nt = s // ts
    blk = pl.BlockSpec((ts, LRU_W), lambda b, i: (b * nt + i, 0))
    return pl.pallas_call(
        _lru_body,
        grid=(bsz, nt),
        in_specs=[blk, blk, _full(cw.shape), _full(cb.shape), _full(wg_bf.shape), _full(bg.shape),
                  _full(lam.shape)],
        out_specs=blk,
        out_shape=jax.ShapeDtypeStruct((bsz * s, LRU_W), bf16),
        scratch_shapes=[pltpu.VMEM((SUBLANES, LRU_W), f32), pltpu.VMEM((SUBLANES, LRU_W), f32)],
        compiler_params=_cparams(("parallel", "arbitrary")),
        name="lru",
    )(xl, gl, cw, cb, wg_bf, bg, lam)


def _rwkv_prep_body(ur_ref, mu_ref, w0_ref, a0_ref, wl_ref, gup_ref, kk_ref, ka_ref, rk_ref, ones_ref,
                    at_ref, bt_ref, kt_ref, rt_ref, v_ref, cl_ref, bon_ref, g_ref, prev_ref):
    ts = ur_ref.shape[0]

    @pl.when(pl.program_id(1) == 0)
    def _():
        prev_ref[...] = jnp.zeros_like(prev_ref)

    u0 = ur_ref[...]
    us = _shift_rows(u0, prev_ref[...], 1)
    prev_ref[...] = u0[ts - SUBLANES:]
    u = u0 + (us - u0) * mu_ref[...]
    c1, c2, c3 = RWKV_W, 2 * RWKV_W, 3 * RWKV_W
    r = u[:, :c1]
    k = u[:, c1:c2]
    v = u[:, c2:c3]
    lora = u[:, c3:c3 + LANES]
    dg = u[:, c3 + LANES:]
    lane = lax.broadcasted_iota(i32, lora.shape, 1)
    lora = jnp.where(lane < DECAY_LORA, jnp.tanh(lora), lora)
    proj = jnp.dot(lora.astype(bf16), wl_ref[...], preferred_element_type=f32)
    w = -jax.nn.softplus(-(w0_ref[...] + proj[:, :RWKV_W])) - 0.5
    lw = -jnp.exp(w)
    a = jax.nn.sigmoid(a0_ref[...] + proj[:, RWKV_W:])
    g_ref[...] = jnp.dot(jax.nn.sigmoid(dg).astype(bf16), gup_ref[...], preferred_element_type=f32)

    ones = ones_ref[...]
    kk = k * kk_ref[...]
    ss = _sum_dot(kk * kk, ones, 2)
    kk = kk / jnp.maximum(jnp.sqrt(ss), 1e-12)
    k2 = k * (1.0 + (a - 1.0) * ka_ref[...])
    bon_ref[...] = _sum_dot(r * k2 * rk_ref[...], ones, 2)

    ri = lax.broadcasted_iota(i32, (ts, ts), 0)
    ci = lax.broadcasted_iota(i32, (ts, ts), 1)
    tri = jnp.where((ci <= ri) & (ri // RWKV_CHUNK == ci // RWKV_CHUNK), 1.0, 0.0).astype(bf16)
    cl = _sum_dot(lw, tri, 3, mask_left=True)
    e_neg = jnp.exp(-cl)
    at_ref[...] = -kk * jnp.exp(cl - lw)
    bt_ref[...] = kk * a * e_neg
    kt_ref[...] = k2 * e_neg
    rt_ref[...] = r * jnp.exp(cl)
    v_ref[...] = v
    cl_ref[...] = cl


def _rwkv_prep(ur, mu, w0, a0, wl_bf, gup_bf, k_k, k_a, r_k, ones_blk, bsz, s, ts):
    nt = s // ts
    t = bsz * s
    oblk = pl.BlockSpec((ts, RWKV_W), lambda b, i: (b * nt + i, 0))
    osh = jax.ShapeDtypeStruct((t, RWKV_W), f32)
    return pl.pallas_call(
        _rwkv_prep_body,
        grid=(bsz, nt),
        in_specs=[pl.BlockSpec((ts, RWKV_IN), lambda b, i: (b * nt + i, 0)), _full(mu.shape),
                  _full(w0.shape), _full(a0.shape), _full(wl_bf.shape), _full(gup_bf.shape),
                  _full(k_k.shape), _full(k_a.shape), _full(r_k.shape), _full(ones_blk.shape)],
        out_specs=[oblk] * 8,
        out_shape=[osh] * 8,
        scratch_shapes=[pltpu.VMEM((SUBLANES, RWKV_IN), f32)],
        compiler_params=_cparams(("parallel", "arbitrary")),
        name="rwkv_prep",
    )(ur, mu, w0, a0, wl_bf, gup_bf, k_k, k_a, r_k, ones_blk)


def _mm_x3(a, b_parts):
    ah, al = _bf16_parts(a, 2)
    bh, bl = b_parts
    d = lambda x, y: jnp.dot(x, y, preferred_element_type=f32)
    return d(ah, bh) + (d(ah, bl) + d(al, bh))


def _mm(a, b):
    return jnp.dot(a.astype(bf16), b.astype(bf16), preferred_element_type=f32)


def _mm_nt(a, b):
    return lax.dot_general(a.astype(bf16), b.astype(bf16), (((1,), (1,)), ((), ())), preferred_element_type=f32)


def _mm_tn(a, b):
    return lax.dot_general(a.astype(bf16), b.astype(bf16), (((0,), (0,)), ((), ())), preferred_element_type=f32)


def _bd(x):
    m0 = lax.broadcasted_iota(i32, x.shape, 1) < RWKV_HEAD
    zero = jnp.zeros_like(x)
    return jnp.concatenate([jnp.where(m0, x, zero), jnp.where(m0, zero, x)], axis=0)


def _side_by_side(d):
    h = d.shape[0] // 2
    m0 = lax.broadcasted_iota(i32, (h, d.shape[1]), 1) < RWKV_HEAD
    return jnp.where(m0, d[:h], d[h:])


def _chunk_maps(chunks):
    c = RWKV_CHUNK
    ri = lax.broadcasted_iota(i32, (c, PAIR), 0)
    ji = lax.broadcasted_iota(i32, (c, PAIR), 1) % RWKV_HEAD
    strict = ji < ri
    incl = ji <= ri
    diag = ji == ri
    eye = jnp.where(diag, 1.0, 0.0).astype(f32)
    each = lambda f, *ls: [f(*xs) for xs in zip(*ls)]
    cat0 = lambda *xs: jnp.concatenate(xs, axis=0)
    cat1 = lambda *xs: jnp.concatenate(xs, axis=1)
    tb = lambda x: x.astype(bf16)

    ats, bts, kts, rts, vs, cls = [list(x) for x in zip(*chunks)]
    pcs = each(lambda cl: jnp.exp(cl[c - 1:c, :]), cls)
    bd_a = each(lambda x: tb(_bd(x)), ats)
    bd_v = each(lambda x: tb(_bd(x)), vs)

    aa = each(lambda a, r, b, k: _mm_nt(cat0(tb(a), tb(r)), cat0(tb(_bd(b)), tb(_bd(k)))), ats, rts, bts, kts)
    l_ab = each(lambda x: tb(jnp.where(strict, x[:c, :PAIR], 0.0)), aa)
    a_k = each(lambda x: tb(cat0(jnp.where(strict, x[:c, PAIR:], 0.0), jnp.where(incl, x[c:, PAIR:], 0.0))), aa)
    a_rb = each(lambda x: tb(jnp.where(incl, x[c:, :PAIR], 0.0)), aa)

    tinv = each(lambda x: eye + x, l_ab)
    lp = each(lambda x: tb(_mm(x, _bd(x))), l_ab)
    p = 2
    while 2 * p < c:
        x2 = each(lambda t, x: _mm(cat0(tb(t), x), _bd(x)), tinv, lp)
        tinv = each(lambda t, x: t + x[:c], tinv, x2)
        lp = each(lambda x: tb(x[c:]), x2)
        p *= 2
    tinv = each(lambda t, x: t + _mm(t, _bd(x)), tinv, lp)

    wy = each(_mm, a_k, bd_v)
    za = each(lambda t, w, a: _mm(t, cat1(tb(_bd(w[:c])), a)), tinv, wy, bd_a)
    zp = each(lambda x: x[:, :PAIR], za)
    ac = each(lambda x: x[:, PAIR:], za)
    y1 = each(lambda ar, z, a: _mm(ar, cat1(tb(_bd(z)), tb(_bd(a)))), a_rb, zp, ac)
    yp = each(lambda w, y: w[c:] + y[:, :PAIR], wy, y1)
    rc = each(lambda r, y: r + y[:, PAIR:], rts, y1)
    sm = each(lambda b, pc, z, a: _mm_tn(b * pc, cat1(z, a)), bts, pcs, zp, ac)
    kv = each(lambda k, pc, v: _mm_tn(k * pc, v), kts, pcs, vs)
    sp = each(lambda x, m: _side_by_side(x) + _side_by_side(m[:, :PAIR]), kv, sm)
    g = each(lambda pc, m: jnp.where(diag, jnp.broadcast_to(pc, (c, PAIR)), 0.0) + _side_by_side(m[:, PAIR:]), pcs, sm)
    return list(zip(g, sp, rc, yp))


def _rwkv_chunk_body(at_ref, bt_ref, kt_ref, rt_ref, v_ref, cl_ref, g_ref, sp_ref, rc_ref, yp_ref):
    ts = at_ref.shape[0]
    c = RWKV_CHUNK
    sls = [slice(j * c, (j + 1) * c) for j in range(ts // c)]
    outs = _chunk_maps([(at_ref[sl, :], bt_ref[sl, :], kt_ref[sl, :], rt_ref[sl, :], v_ref[sl, :], cl_ref[sl, :])
                        for sl in sls])
    for sl, (g, sp, rc, yp) in zip(sls, outs):
        g_ref[sl, :] = g
        sp_ref[sl, :] = sp
        rc_ref[sl, :] = rc
        yp_ref[sl, :] = yp


def _rwkv_chunk(at, bt, kt, rt, v, cl, ts):
    t = at.shape[0]
    iblk = pl.BlockSpec((ts, PAIR), lambda i, p: (i, p))
    osh = jax.ShapeDtypeStruct((t, RWKV_W), f32)
    return pl.pallas_call(
        _rwkv_chunk_body,
        grid=(t // ts, N_PAIRS),
        in_specs=[iblk] * 6,
        out_specs=[iblk] * 4,
        out_shape=[osh] * 4,
        compiler_params=_cparams(("parallel", "parallel")),
        name="rwkv_chunk",
    )(at, bt, kt, rt, v, cl)


def _rwkv_state_body(g_ref, sp_ref, rc_ref, yp_ref, y_ref, s_ref):
    @pl.when(pl.program_id(0) == 0)
    def _():
        s_ref[...] = jnp.zeros_like(s_ref)

    c = RWKV_CHUNK
    bsz = g_ref.shape[0]
    nck = g_ref.shape[1] // c
    chains = [(b, slice(p * PAIR, (p + 1) * PAIR)) for b in range(bsz) for p in range(N_PAIRS)]
    states = [s_ref[b, :, ls] for b, ls in chains]
    for j in range(nck):
        sl = slice(j * c, (j + 1) * c)
        prods = [_mm_x3(jnp.concatenate([rc_ref[b, sl, ls], g_ref[b, sl, ls]], axis=0), _bf16_parts(_bd(s), 2))
                 for (b, ls), s in zip(chains, states)]
        for (b, ls), pr in zip(chains, prods):
            y_ref[b, sl, ls] = yp_ref[b, sl, ls] + pr[:c]
        states = [pr[c:] + sp_ref[b, sl, ls] for (b, ls), pr in zip(chains, prods)]
    for (b, ls), s in zip(chains, states):
        s_ref[b, :, ls] = s


def _rwkv_state(g, sp, rc, yp, bsz, s, nck):
    rows = nck * RWKV_CHUNK
    rblk = pl.BlockSpec((bsz, rows, RWKV_W), lambda i: (0, i, 0))
    r3 = lambda a: a.reshape(bsz, s, RWKV_W)
    y = pl.pallas_call(
        _rwkv_state_body,
        grid=(s // rows,),
        in_specs=[rblk] * 4,
        out_specs=rblk,
        out_shape=jax.ShapeDtypeStruct((bsz, s, RWKV_W), f32),
        scratch_shapes=[pltpu.VMEM((bsz, RWKV_CHUNK, RWKV_W), f32)],
        compiler_params=_cparams(("arbitrary",)),
        name="rwkv_state",
    )(r3(g), r3(sp), r3(rc), r3(yp))
    return y.reshape(bsz * s, RWKV_W)


def _outproj_body(x_ref, yl_ref, ys_ref, bon_ref, v_ref, g_ref, lg_ref, lb_ref, ones_ref, w_ref, o_ref):
    y = ys_ref[...]
    ones = ones_ref[...]
    inv_n = 1.0 / RWKV_HEAD
    mean = _sum_dot(y, ones, 2) * inv_n
    yc = y - mean
    var = _sum_dot(yc * yc, ones, 2) * inv_n
    yn = yc * lax.rsqrt(var + GN_EPS) * lg_ref[...] + lb_ref[...]
    yr = (yn + bon_ref[...] * v_ref[...]) * g_ref[...]
    cat = jnp.concatenate([yl_ref[...], yr.astype(bf16)], axis=1)
    o_ref[...] = x_ref[...] + jnp.dot(cat, w_ref[...], preferred_element_type=f32)


def _outproj(x2d, y_lru, y_scan, bon, v, g, lnx_g, lnx_b, ones_blk, w_out_bf, tm):
    t = x2d.shape[0]
    xb = pl.BlockSpec((tm, D_MODEL), lambda i: (i, 0))
    hb = pl.BlockSpec((tm, RWKV_W), lambda i: (i, 0))
    return pl.pallas_call(
        _outproj_body,
        grid=(t // tm,),
        in_specs=[xb, hb, hb, hb, hb, hb, _full(lnx_g.shape), _full(lnx_b.shape), _full(ones_blk.shape),
                  _full(w_out_bf.shape)],
        out_specs=xb,
        out_shape=jax.ShapeDtypeStruct((t, D_MODEL), f32),
        compiler_params=_cparams(("parallel",)),
        name="outproj",
    )(x2d, y_lru, y_scan, bon, v, g, lnx_g, lnx_b, ones_blk, w_out_bf)


def _memkv_body(m_ref, g_ref, wk_ref, wv_ref, k_ref, v_ref):
    h = _rms(m_ref[...], g_ref[...]).astype(bf16)
    k_ref[...] = jnp.dot(h, wk_ref[...], preferred_element_type=f32).astype(bf16)
    v_ref[...] = jnp.dot(h, wv_ref[...], preferred_element_type=f32).astype(bf16)


def _memkv(mem2d, g, wk_bf, wv_bf, tm):
    t = mem2d.shape[0]
    blk = pl.BlockSpec((tm, D_MODEL), lambda i: (i, 0))
    sh = jax.ShapeDtypeStruct((t, D_MODEL), bf16)
    return pl.pallas_call(
        _memkv_body,
        grid=(t // tm,),
        in_specs=[blk, _full(g.shape), _full(wk_bf.shape), _full(wv_bf.shape)],
        out_specs=[blk, blk],
        out_shape=[sh, sh],
        compiler_params=_cparams(("parallel",)),
        name="memkv",
    )(mem2d, g, wk_bf, wv_bf)


def _xattn_body(x_ref, k_ref, v_ref, gx_ref, wq_ref, wo_ref, gf_ref, wr_ref, br_ref, upper_ref,
                x2_ref, hf_ref, idx_ref, gate_ref, cnt_ref, base_ref):
    x = x_ref[...]
    h = _rms(x, gx_ref[...]).astype(bf16)
    q = jnp.dot(h, wq_ref[...], preferred_element_type=f32).astype(bf16)
    k = k_ref[...]
    v = v_ref[...]
    outs = []
    for hd in range(XA_HEADS):
        sl = slice(hd * XA_HEAD, (hd + 1) * XA_HEAD)
        sc = lax.dot_general(q[:, sl], k[:, sl], (((1,), (1,)), ((), ())),
                             preferred_element_type=f32) * (XA_HEAD ** -0.5)
        sc = sc - jnp.max(sc, axis=-1, keepdims=True)
        e = jnp.exp(sc)
        p = e / jnp.sum(e, axis=-1, keepdims=True)
        outs.append(jnp.dot(p.astype(bf16), v[:, sl], preferred_element_type=f32).astype(bf16))
    o = jnp.concatenate(outs, axis=1)
    x2 = x + jnp.dot(o, wo_ref[...], preferred_element_type=f32)
    x2_ref[...] = x2

    hf = _rms(x2, gf_ref[...])
    _store_row_tiles(hf_ref, hf)
    logits = lax.dot_general(wr_ref[...], hf.astype(bf16), (((1,), (1,)), ((), ())),
                             preferred_element_type=f32) + br_ref[...]
    erow = lax.broadcasted_iota(i32, logits.shape, 0)
    neg = jnp.float32(-jnp.inf)
    cur = logits
    vals = []
    idxs = []
    for _ in range(TOP_K):
        m = jnp.max(cur, axis=0, keepdims=True)
        am = jnp.min(jnp.where(cur == m, erow, N_EXPERTS), axis=0, keepdims=True)
        vals.append(m)
        idxs.append(am)
        cur = jnp.where(erow == am, neg, cur)
    es = [jnp.exp(vk - vals[0]) for vk in vals]
    den = es[0] + es[1] + es[2] + es[3]

    @pl.when(pl.program_id(0) == 0)
    def _():
        base_ref[...] = jnp.zeros_like(base_ref)

    onehot = [jnp.where(erow == am, 1.0, 0.0) for am in idxs]
    cnt = (onehot[0] + onehot[1]) + (onehot[2] + onehot[3])
    base = base_ref[:, 0:1]
    prior = jnp.dot(cnt.astype(bf16), upper_ref[...], preferred_element_type=f32) + base
    base_ref[...] = jnp.broadcast_to(base + jnp.sum(cnt, axis=1, keepdims=True), base_ref.shape)
    cnt_ref[...] = base_ref[...]

    orow = lax.broadcasted_iota(i32, idx_ref.shape, 0)
    idx_out = jnp.zeros(idx_ref.shape, i32)
    gate_out = jnp.zeros(gate_ref.shape, f32)
    for kq in range(TOP_K):
        rank = jnp.sum(prior * onehot[kq], axis=0, keepdims=True).astype(i32)
        idx_out = jnp.where(orow == kq, idxs[kq], idx_out)
        idx_out = jnp.where(orow == TOP_K + kq, rank, idx_out)
        gate_out = jnp.where(orow == kq, es[kq] / den, gate_out)
    idx_ref[...] = idx_out
    gate_ref[...] = gate_out


def _xattn(x1, kmem, vmem, g_xa, wq_bf, wo_bf, g_ffn, wr_t, br_col, bsz, s, mlen, tm):
    t = bsz * s
    nt = s // tm
    xb = pl.BlockSpec((tm, D_MODEL), lambda i: (i, 0))
    mb = pl.BlockSpec((mlen, D_MODEL), lambda i: (i // nt, 0))
    lb = pl.BlockSpec((2 * TOP_K, tm), lambda i: (0, i))
    upper = jnp.triu(jnp.ones((tm, tm), bf16), k=1)
    return pl.pallas_call(
        _xattn_body,
        grid=(t // tm,),
        in_specs=[xb, mb, mb, _full(g_xa.shape), _full(wq_bf.shape), _full(wo_bf.shape),
                  _full(g_ffn.shape), _full(wr_t.shape), _full(br_col.shape), _full(upper.shape)],
        out_specs=[xb, pl.BlockSpec((tm * ROW_TILE, LANES), lambda i: (i, 0)), lb, lb, _full((N_EXPERTS, LANES))],
        out_shape=[jax.ShapeDtypeStruct((t, D_MODEL), f32), jax.ShapeDtypeStruct((t * ROW_TILE, LANES), f32),
                   jax.ShapeDtypeStruct((2 * TOP_K, t), i32), jax.ShapeDtypeStruct((2 * TOP_K, t), f32),
                   jax.ShapeDtypeStruct((N_EXPERTS, LANES), f32)],
        scratch_shapes=[pltpu.VMEM((N_EXPERTS, LANES), f32)],
        compiler_params=_cparams(("arbitrary",)),
        name="xattn",
    )(x1, kmem, vmem, g_xa, wq_bf, wo_bf, g_ffn, wr_t, br_col, upper)


DMA_UNROLL = 8


def _tile_at(ref, row):
    return ref.at[pl.ds(pl.multiple_of(row * ROW_TILE, ROW_TILE), ROW_TILE), :]


def _dispatch_body(dest_ref, pend_ref, hf_ref, xs_hbm, zeros_ref, sem, zsem):
    i = pl.program_id(0)
    tm = hf_ref.shape[0] // ROW_TILE
    zrows = MOE_BLOCK * ROW_TILE

    def zero_copy(e):
        start = pl.multiple_of((pend_ref[e] - MOE_BLOCK) * ROW_TILE, ROW_TILE)
        return pltpu.make_async_copy(zeros_ref, xs_hbm.at[pl.ds(start, zrows), :], zsem)

    def nonempty(e):
        return pend_ref[e] > (pend_ref[e - 1] if e else 0)

    @pl.when(i == 0)
    def _():
        zeros_ref[...] = jnp.zeros_like(zeros_ref)
        for e in range(N_EXPERTS):
            @pl.when(nonempty(e))
            def _():
                zero_copy(e).start()
        for e in range(N_EXPERTS):
            @pl.when(nonempty(e))
            def _():
                zero_copy(e).wait()

        def tail_copy(b):
            return pltpu.make_async_copy(zeros_ref, xs_hbm.at[pl.ds(pl.multiple_of(b * zrows, zrows), zrows), :], zsem)

        def tail_start(b, carry):
            tail_copy(b).start()
            return carry

        def tail_wait(b, carry):
            tail_copy(b).wait()
            return carry
        n_used = pend_ref[N_EXPERTS - 1] // MOE_BLOCK
        n_all = xs_hbm.shape[0] // zrows
        lax.fori_loop(n_used, n_all, tail_start, 0)
        lax.fori_loop(n_used, n_all, tail_wait, 0)

    def body(q, carry):
        for u in range(DMA_UNROLL):
            r = q * DMA_UNROLL + u
            src = _tile_at(hf_ref, r)
            for kq in range(TOP_K):
                pltpu.make_async_copy(src, _tile_at(xs_hbm, dest_ref[(i * tm + r) * TOP_K + kq]),
                                      sem).start(priority=kq % 2)
        return carry
    lax.fori_loop(0, tm // DMA_UNROLL, body, 0)
    for kq in range(TOP_K):
        pltpu.make_async_copy(hf_ref, xs_hbm.at[pl.ds(0, tm * ROW_TILE), :], sem).wait()


def _dispatch(dest_flat, pends, hf_tiles, n_blocks, tm):
    t = hf_tiles.shape[0] // ROW_TILE
    rows = n_blocks * MOE_BLOCK
    grid_spec = pltpu.PrefetchScalarGridSpec(
        num_scalar_prefetch=2,
        grid=(t // tm,),
        in_specs=[pl.BlockSpec((tm * ROW_TILE, LANES), lambda i, d, pe: (i, 0))],
        out_specs=pl.BlockSpec(memory_space=pl.ANY),
        scratch_shapes=[pltpu.VMEM((MOE_BLOCK * ROW_TILE, LANES), f32), pltpu.SemaphoreType.DMA(()),
                        pltpu.SemaphoreType.DMA(())],
    )
    return pl.pallas_call(
        _dispatch_body,
        grid_spec=grid_spec,
        out_shape=jax.ShapeDtypeStruct((rows * ROW_TILE, LANES), f32),
        compiler_params=_cparams(("arbitrary",)),
        name="dispatch",
    )(dest_flat, pends, hf_tiles)


def _moe_body(be_ref, nb_ref, x_ref, wgu_ref, bgu_ref, wdn_ref, bdn_ref, o_ref, wgu_bf, wdn_bf):
    i = pl.program_id(0)

    @pl.when(i >= nb_ref[0])
    def _():
        o_ref[...] = jnp.zeros_like(o_ref)

    @pl.when(i < nb_ref[0])
    def _():
        prev_e = be_ref[jnp.maximum(i - 1, 0)]

        @pl.when((i == 0) | (be_ref[i] != prev_e))
        def _():
            wgu_bf[...] = wgu_ref[0].astype(bf16)
            wdn_bf[...] = wdn_ref[0].astype(bf16)

        xb = _load_row_tiles(x_ref, MOE_BLOCK).astype(bf16)
        gu = jnp.dot(xb, wgu_bf[...], preferred_element_type=f32) + bgu_ref[0]
        gate = jnp.minimum(gu[:, :D_FF], SWIGLU_LIMIT)
        up = jnp.clip(gu[:, D_FF:], -SWIGLU_LIMIT, SWIGLU_LIMIT)
        act = (up + 1.0) * (gate * jax.nn.sigmoid(SWIGLU_ALPHA * gate))
        y = jnp.dot(act.astype(bf16), wdn_bf[...], preferred_element_type=f32) + bdn_ref[0]
        _store_row_tiles(o_ref, y)


def _moe(block_e, n_used, xs_tiles, w_gu, b_gu, w_dn, b_dn, n_blocks):
    rows = n_blocks * MOE_BLOCK
    xblk = pl.BlockSpec((MOE_BLOCK * ROW_TILE, LANES), lambda i, be, nb: (i, 0))
    grid_spec = pltpu.PrefetchScalarGridSpec(
        num_scalar_prefetch=2,
        grid=(n_blocks,),
        in_specs=[
            pl.BlockSpec((MOE_BLOCK * ROW_TILE, LANES), lambda i, be, nb: (jnp.maximum(jnp.minimum(i, nb[0] - 1), 0), 0)),
            pl.BlockSpec((1, D_MODEL, 2 * D_FF), lambda i, be, nb: (be[i], 0, 0)),
            pl.BlockSpec((1, 1, 2 * D_FF), lambda i, be, nb: (be[i], 0, 0)),
            pl.BlockSpec((1, D_FF, D_MODEL), lambda i, be, nb: (be[i], 0, 0)),
            pl.BlockSpec((1, 1, D_MODEL), lambda i, be, nb: (be[i], 0, 0)),
        ],
        out_specs=xblk,
        scratch_shapes=[pltpu.VMEM((D_MODEL, 2 * D_FF), bf16), pltpu.VMEM((D_FF, D_MODEL), bf16)],
    )
    return pl.pallas_call(
        _moe_body,
        grid_spec=grid_spec,
        out_shape=jax.ShapeDtypeStruct((rows * ROW_TILE, LANES), f32),
        compiler_params=_cparams(("arbitrary",)),
        name="moe",
    )(block_e, n_used, xs_tiles, w_gu, b_gu.reshape(N_EXPERTS, 1, 2 * D_FF), w_dn,
      b_dn.reshape(N_EXPERTS, 1, D_MODEL))


def _combine_body(pos_ref, ys_hbm, x_ref, gate_ref, g_ref, o_ref, buf, sem):
    i = pl.program_id(0)
    n = pl.num_programs(0)
    tc = x_ref.shape[0]
    slot = i % 2
    slot_rows = TOP_K * tc * ROW_TILE

    def start(step, sl):
        def body(q, carry):
            for u in range(DMA_UNROLL):
                r = q * DMA_UNROLL + u
                for kq in range(TOP_K):
                    dst = _tile_at(buf, (sl * TOP_K + kq) * tc + r)
                    pltpu.make_async_copy(_tile_at(ys_hbm, pos_ref[(step * tc + r) * TOP_K + kq]), dst,
                                          sem.at[sl]).start(priority=kq % 2)
            return carry
        lax.fori_loop(0, tc // DMA_UNROLL, body, 0)

    @pl.when(i == 0)
    def _():
        start(0, 0)

    @pl.when(i + 1 < n)
    def _():
        start(i + 1, 1 - slot)

    off = pl.multiple_of(slot * slot_rows, slot_rows)
    pltpu.make_async_copy(ys_hbm.at[pl.ds(0, slot_rows), :], buf.at[pl.ds(off, slot_rows), :], sem.at[slot]).wait()
    gates = gate_ref[...]
    acc = x_ref[...]
    for kq in range(TOP_K):
        rows = _load_row_tiles(buf, tc, off + kq * tc * ROW_TILE)
        acc = acc + gates[:, kq:kq + 1] * rows
    o_ref[...] = _rms(acc, g_ref[...])


def _combine(pos_flat, ys_tiles, x2, gate_pad, g_final, tc):
    t = x2.shape[0]
    grid_spec = pltpu.PrefetchScalarGridSpec(
        num_scalar_prefetch=1,
        grid=(t // tc,),
        in_specs=[pl.BlockSpec(memory_space=pl.ANY),
                  pl.BlockSpec((tc, D_MODEL), lambda i, p: (i, 0)),
                  pl.BlockSpec((tc, LANES), lambda i, p: (i, 0)),
                  pl.BlockSpec((1, D_MODEL), lambda i, p: (0, 0))],
        out_specs=pl.BlockSpec((tc, D_MODEL), lambda i, p: (i, 0)),
        scratch_shapes=[pltpu.VMEM((2 * TOP_K * tc * ROW_TILE, LANES), f32), pltpu.SemaphoreType.DMA((2,))],
    )
    return pl.pallas_call(
        _combine_body,
        grid_spec=grid_spec,
        out_shape=jax.ShapeDtypeStruct((t, D_MODEL), f32),
        compiler_params=_cparams(("arbitrary",)),
        name="combine",
    )(pos_flat, ys_tiles, x2, gate_pad, g_final)


def _routing(top_idx, rank, counts, t):
    n_assign = t * TOP_K
    experts = jnp.arange(N_EXPERTS, dtype=i32)
    padded = (counts + MOE_BLOCK - 1) // MOE_BLOCK * MOE_BLOCK
    pends = jnp.cumsum(padded).astype(i32)
    pstarts = pends - padded
    start_of = jnp.sum(jnp.where(top_idx[:, :, None] == experts, pstarts, 0), axis=-1)
    dest = (start_of + rank).astype(i32).reshape(n_assign)
    n_blocks = (n_assign + N_EXPERTS * (MOE_BLOCK - 1) + MOE_BLOCK - 1) // MOE_BLOCK
    block_start = jnp.arange(n_blocks, dtype=i32) * MOE_BLOCK
    block_e = jnp.minimum(jnp.sum((block_start[:, None] >= pends[None, :]).astype(i32), axis=1), N_EXPERTS - 1)
    n_used = (pends[-1:] // MOE_BLOCK).astype(i32)
    return block_e.astype(i32), n_used, dest, pends, n_blocks


def _block_diag(w):
    n, bi, bj = w.shape
    eye = jnp.eye(n, dtype=w.dtype)
    return jnp.einsum('nij,nm->nimj', w, eye).reshape(n * bi, n * bj)


def _layer(x2d, mem2d, bsz, s, mlen, p):
    t = bsz * s
    row = lambda a: a.reshape(1, -1)
    ones_blk = _block_diag(jnp.ones((RWKV_W // RWKV_HEAD, RWKV_HEAD, RWKV_HEAD), bf16))

    xl, gl, ur = _inproj(x2d, row(p['norm_mix_g']), p['w_in'].astype(bf16), tm=min(512, t))

    wg = jnp.concatenate([_block_diag(p['lru_wx']), _block_diag(p['lru_wa'])], axis=1).astype(bf16)
    bg = jnp.concatenate([p['lru_bx'], p['lru_ba']]).reshape(1, -1)
    y_lru = _lru(xl, gl, p['conv_w'], row(p['conv_b']), wg, bg, row(p['lru_lambda']), bsz, s, ts=min(256, s))

    zl = jnp.zeros((DECAY_LORA, RWKV_W), f32)
    wl = jnp.concatenate([jnp.concatenate([p['rwkv_w_up'], zl], axis=1),
                          jnp.concatenate([zl, p['rwkv_a_up']], axis=1)], axis=0).astype(bf16)
    at, bt, kt, rt, v, cl, bon, g = _rwkv_prep(
        ur, row(p['rwkv_mu']), row(p['rwkv_w0']), row(p['rwkv_a0']), wl, p['rwkv_g_up'].astype(bf16),
        row(p['rwkv_k_k']), row(p['rwkv_k_a']), row(p['rwkv_r_k']), ones_blk, bsz, s, ts=min(256, s))
    gm, sp, rc, yp = _rwkv_chunk(at, bt, kt, rt, v, cl, ts=min(1024, s))
    y_scan = _rwkv_state(gm, sp, rc, yp, bsz, s, nck=min(4, s // RWKV_CHUNK))

    x1 = _outproj(x2d, y_lru, y_scan, bon, v, g, row(p['rwkv_lnx_g']), row(p['rwkv_lnx_b']), ones_blk,
                  p['w_out'].astype(bf16), tm=min(512, t))

    kmem, vmem = _memkv(mem2d, row(p['norm_mem_g']), p['xa_wk'].astype(bf16), p['xa_wv'].astype(bf16),
                        tm=min(512, bsz * mlen))
    x2, hf, route, gates, cnt_pad = _xattn(x1, kmem, vmem, row(p['norm_xa_g']), p['xa_wq'].astype(bf16),
                                           p['xa_wo'].astype(bf16), row(p['norm_ffn_g']),
                                           p['w_router'].T.astype(bf16), p['b_router'].reshape(-1, 1),
                                           bsz, s, mlen, tm=min(512, s))

    counts = cnt_pad[:, 0].astype(i32)
    block_e, n_used, dest, pends, n_blocks = _routing(route[:TOP_K].T, route[TOP_K:].T, counts, t)
    gate_pad = jnp.pad(gates[:TOP_K].T, ((0, 0), (0, LANES - TOP_K)))
    xs = _dispatch(dest, pends, hf, n_blocks, tm=min(256, t))
    ys = _moe(block_e, n_used, xs, p['w_gu'], p['b_gu'], p['w_dn'], p['b_dn'], n_blocks)
    return _combine(dest, ys, x2, gate_pad, row(p['final_norm_g']), tc=min(256, t))


def kernel(x, mem, norm_mix_g, w_in, conv_w, conv_b, lru_wx, lru_bx, lru_wa, lru_ba, lru_lambda, rwkv_mu, rwkv_w0, rwkv_w_up, rwkv_a0, rwkv_a_up, rwkv_g_up, rwkv_k_k, rwkv_k_a, rwkv_r_k, rwkv_lnx_g, rwkv_lnx_b, w_out, norm_xa_g, norm_mem_g, xa_wq, xa_wk, xa_wv, xa_wo, norm_ffn_g, w_router, b_router, w_gu, b_gu, w_dn, b_dn, final_norm_g):
    bsz, s, d = x.shape
    mlen = mem.shape[1]
    assert d == D_MODEL and w_in.shape[0] == 1
    p = dict(norm_mix_g=norm_mix_g[0], w_in=w_in[0], conv_w=conv_w[0], conv_b=conv_b[0], lru_wx=lru_wx[0],
             lru_bx=lru_bx[0], lru_wa=lru_wa[0], lru_ba=lru_ba[0], lru_lambda=lru_lambda[0],
             rwkv_mu=rwkv_mu[0], rwkv_w0=rwkv_w0[0], rwkv_w_up=rwkv_w_up[0], rwkv_a0=rwkv_a0[0],
             rwkv_a_up=rwkv_a_up[0], rwkv_g_up=rwkv_g_up[0], rwkv_k_k=rwkv_k_k[0], rwkv_k_a=rwkv_k_a[0],
             rwkv_r_k=rwkv_r_k[0].reshape(-1), rwkv_lnx_g=rwkv_lnx_g[0], rwkv_lnx_b=rwkv_lnx_b[0],
             w_out=w_out[0], norm_xa_g=norm_xa_g[0], norm_mem_g=norm_mem_g[0], xa_wq=xa_wq[0],
             xa_wk=xa_wk[0], xa_wv=xa_wv[0], xa_wo=xa_wo[0], norm_ffn_g=norm_ffn_g[0],
             w_router=w_router[0], b_router=b_router[0], w_gu=w_gu[0], b_gu=b_gu[0], w_dn=w_dn[0],
             b_dn=b_dn[0], final_norm_g=final_norm_g)
    out = _layer(x.reshape(bsz * s, d), mem.reshape(bsz * mlen, d), bsz, s, mlen, p)
    return out.reshape(bsz, s, d)
```

```python
import functools

import jax
import jax.numpy as jnp
from jax import lax
from jax.experimental import pallas as pl
from jax.experimental.pallas import tpu as pltpu

f32 = jnp.float32
bf16 = jnp.bfloat16
i32 = jnp.int32

D_MODEL = 1024
LRU_W = 512
RWKV_W = 512
LRU_BLOCKS = 8
LRU_BLOCK = 64
CONV_W = 4
LRU_C = 8.0
RWKV_HEAD = 64
DECAY_LORA = 64
AAA_LORA = 64
GATE_LORA = 128
RWKV_IN = 3 * RWKV_W + DECAY_LORA + AAA_LORA + GATE_LORA
XA_HEADS = 4
XA_HEAD = D_MODEL // XA_HEADS
N_EXPERTS = 32
TOP_K = 4
D_FF = D_MODEL
SWIGLU_LIMIT = 7.0
SWIGLU_ALPHA = 1.702
EPS = 1e-6
GN_EPS = 64e-5

LANES = 128
SUBLANES = 8
RWKV_CHUNK = 64
PAIR = 2 * RWKV_HEAD
N_PAIRS = RWKV_W // PAIR
MOE_BLOCK = 256
ROW_TILE = D_MODEL // LANES
VMEM_LIMIT = 52 * 1024 * 1024


def _cparams(sem):
    return pltpu.CompilerParams(dimension_semantics=sem, vmem_limit_bytes=VMEM_LIMIT)


def _rms(x, g):
    return x * lax.rsqrt(jnp.mean(x * x, axis=-1, keepdims=True) + EPS) * g


def _full(shape):
    n = len(shape)
    return pl.BlockSpec(shape, lambda *a: (0,) * n)


def _shift_rows(x, prev8, d):
    xr = pltpu.roll(x, d, 0)
    tr = pltpu.roll(prev8, d, 0)
    row = lax.broadcasted_iota(i32, prev8.shape, 0)
    head = jnp.where(row < d, tr, xr[:SUBLANES])
    return jnp.concatenate([head, xr[SUBLANES:]], axis=0)


def _bf16_parts(x, n):
    parts = []
    for _ in range(n):
        piece = x.astype(bf16)
        parts.append(piece)
        x = x - piece.astype(f32)
    return parts


def _sum_dot(x, mask_bf, n_parts, mask_left=False):
    acc = None
    for piece in _bf16_parts(x, n_parts):
        d = (jnp.dot(mask_bf, piece, preferred_element_type=f32) if mask_left
             else jnp.dot(piece, mask_bf, preferred_element_type=f32))
        acc = d if acc is None else acc + d
    return acc


def _store_row_tiles(ref, val, off=0):
    n = val.shape[0]
    for j in range(ROW_TILE):
        ref[pl.ds(off + j, n, stride=ROW_TILE), :] = val[:, j * LANES:(j + 1) * LANES]


def _load_row_tiles(ref, n, off=0):
    return jnp.concatenate([ref[pl.ds(off + j, n, stride=ROW_TILE), :] for j in range(ROW_TILE)], axis=1)


def _inproj_body(x_ref, g_ref, w_ref, xl_ref, gl_ref, ur_ref):
    h = _rms(x_ref[...], g_ref[...])
    u = jnp.dot(h.astype(bf16), w_ref[...], preferred_element_type=f32)
    xl_ref[...] = u[:, :LRU_W]
    gl_ref[...] = u[:, LRU_W:2 * LRU_W]
    ur_ref[...] = u[:, 2 * LRU_W:]


def _inproj(x2d, g, w_in_bf, tm):
    t = x2d.shape[0]
    return pl.pallas_call(
        _inproj_body,
        grid=(t // tm,),
        in_specs=[pl.BlockSpec((tm, D_MODEL), lambda i: (i, 0)), _full((1, D_MODEL)),
                  _full(w_in_bf.shape)],
        out_specs=[pl.BlockSpec((tm, LRU_W), lambda i: (i, 0)),
                   pl.BlockSpec((tm, LRU_W), lambda i: (i, 0)),
                   pl.BlockSpec((tm, RWKV_IN), lambda i: (i, 0))],
        out_shape=[jax.ShapeDtypeStruct((t, LRU_W), f32), jax.ShapeDtypeStruct((t, LRU_W), f32),
                   jax.ShapeDtypeStruct((t, RWKV_IN), f32)],
        compiler_params=_cparams(("parallel",)),
        name="inproj",
    )(x2d, g, w_in_bf)


def _lru_body(xl_ref, gl_ref, cw_ref, cb_ref, wg_ref, bg_ref, lam_ref, o_ref, tail_ref, h_ref):
    ts = xl_ref.shape[0]

    @pl.when(pl.program_id(1) == 0)
    def _():
        tail_ref[...] = jnp.zeros_like(tail_ref)
        h_ref[...] = jnp.zeros_like(h_ref)

    x = xl_ref[...]
    tail = tail_ref[...]
    cw = cw_ref[...]
    xc = cb_ref[...] + cw[CONV_W - 1:CONV_W] * x
    for d in range(1, CONV_W):
        xc = xc + cw[CONV_W - 1 - d:CONV_W - d] * _shift_rows(x, tail, d)
    tail_ref[...] = x[ts - SUBLANES:]

    gates = jax.nn.sigmoid(jnp.dot(xc.astype(bf16), wg_ref[...], preferred_element_type=f32) + bg_ref[...])
    gx = gates[:, :LRU_W]
    ga = gates[:, LRU_W:]
    log_a = -LRU_C * ga * jax.nn.softplus(-lam_ref[...])
    a = jnp.exp(log_a)
    b = jnp.sqrt(-jnp.tanh(log_a) * (a * a + 1.0)) * gx * xc

    row = lax.broadcasted_iota(i32, (ts, LRU_W), 0)
    d = 1
    while d < ts:
        keep = row >= d
        a_s = jnp.where(keep, pltpu.roll(a, d, 0), 1.0)
        b_s = jnp.where(keep, pltpu.roll(b, d, 0), 0.0)
        b = a * b_s + b
        a = a * a_s
        d *= 2
    h = b + a * h_ref[SUBLANES - 1:SUBLANES, :]
    h_ref[...] = h[ts - SUBLANES:]
    o_ref[...] = (h * jax.nn.gelu(gl_ref[...])).astype(o_ref.dtype)


def _lru(xl, gl, cw, cb, wg_bf, bg, lam, bsz, s, ts):
    nt = s // ts
    blk = pl.BlockSpec((ts, LRU_W), lambda b, i: (b * nt + i, 0))
    return pl.pallas_call(
        _lru_body,
        grid=(bsz, nt),
        in_specs=[blk, blk, _full(cw.shape), _full(cb.shape), _full(wg_bf.shape), _full(bg.shape),
                  _full(lam.shape)],
        out_specs=blk,
        out_shape=jax.ShapeDtypeStruct((bsz * s, LRU_W), bf16),
        scratch_shapes=[pltpu.VMEM((SUBLANES, LRU_W), f32), pltpu.VMEM((SUBLANES, LRU_W), f32)],
        compiler_params=_cparams(("parallel", "arbitrary")),
        name="lru",
    )(xl, gl, cw, cb, wg_bf, bg, lam)


def _rwkv_prep_body(ur_ref, mu_ref, w0_ref, a0_ref, wl_ref, gup_ref, kk_ref, ka_ref, rk_ref, ones_ref,
                    at_ref, bt_ref, kt_ref, rt_ref, v_ref, cl_ref, bon_ref, g_ref, prev_ref):
    ts = ur_ref.shape[0]

    @pl.when(pl.program_id(1) == 0)
    def _():
        prev_ref[...] = jnp.zeros_like(prev_ref)

    u0 = ur_ref[...]
    us = _shift_rows(u0, prev_ref[...], 1)
    prev_ref[...] = u0[ts - SUBLANES:]
    u = u0 + (us - u0) * mu_ref[...]
    c1, c2, c3 = RWKV_W, 2 * RWKV_W, 3 * RWKV_W
    r = u[:, :c1]
    k = u[:, c1:c2]
    v = u[:, c2:c3]
    lora = u[:, c3:c3 + LANES]
    dg = u[:, c3 + LANES:]
    lane = lax.broadcasted_iota(i32, lora.shape, 1)
    lora = jnp.where(lane < DECAY_LORA, jnp.tanh(lora), lora)
    proj = jnp.dot(lora.astype(bf16), wl_ref[...], preferred_element_type=f32)
    w = -jax.nn.softplus(-(w0_ref[...] + proj[:, :RWKV_W])) - 0.5
    lw = -jnp.exp(w)
    a = jax.nn.sigmoid(a0_ref[...] + proj[:, RWKV_W:])
    g_ref[...] = jnp.dot(jax.nn.sigmoid(dg).astype(bf16), gup_ref[...], preferred_element_type=f32)

    ones = ones_ref[...]
    kk = k * kk_ref[...]
    ss = _sum_dot(kk * kk, ones, 2)
    kk = kk / jnp.maximum(jnp.sqrt(ss), 1e-12)
    k2 = k * (1.0 + (a - 1.0) * ka_ref[...])
    bon_ref[...] = _sum_dot(r * k2 * rk_ref[...], ones, 2)

    ri = lax.broadcasted_iota(i32, (ts, ts), 0)
    ci = lax.broadcasted_iota(i32, (ts, ts), 1)
    tri = jnp.where((ci <= ri) & (ri // RWKV_CHUNK == ci // RWKV_CHUNK), 1.0, 0.0).astype(bf16)
    cl = _sum_dot(lw, tri, 3, mask_left=True)
    e_neg = jnp.exp(-cl)
    at_ref[...] = -kk * jnp.exp(cl - lw)
    bt_ref[...] = kk * a * e_neg
    kt_ref[...] = k2 * e_neg
    rt_ref[...] = r * jnp.exp(cl)
    v_ref[...] = v
    cl_ref[...] = cl


def _rwkv_prep(ur, mu, w0, a0, wl_bf, gup_bf, k_k, k_a, r_k, ones_blk, bsz, s, ts):
    nt = s // ts
    t = bsz * s
    oblk = pl.BlockSpec((ts, RWKV_W), lambda b, i: (b * nt + i, 0))
    osh = jax.ShapeDtypeStruct((t, RWKV_W), f32)
    return pl.pallas_call(
        _rwkv_prep_body,
        grid=(bsz, nt),
        in_specs=[pl.BlockSpec((ts, RWKV_IN), lambda b, i: (b * nt + i, 0)), _full(mu.shape),
                  _full(w0.shape), _full(a0.shape), _full(wl_bf.shape), _full(gup_bf.shape),
                  _full(k_k.shape), _full(k_a.shape), _full(r_k.shape), _full(ones_blk.shape)],
        out_specs=[oblk] * 8,
        out_shape=[osh] * 8,
        scratch_shapes=[pltpu.VMEM((SUBLANES, RWKV_IN), f32)],
        compiler_params=_cparams(("parallel", "arbitrary")),
        name="rwkv_prep",
    )(ur, mu, w0, a0, wl_bf, gup_bf, k_k, k_a, r_k, ones_blk)


def _mm_x3(a, b_parts):
    ah, al = _bf16_parts(a, 2)
    bh, bl = b_parts
    d = lambda x, y: jnp.dot(x, y, preferred_element_type=f32)
    return d(ah, bh) + (d(ah, bl) + d(al, bh))


def _mm(a, b):
    return jnp.dot(a.astype(bf16), b.astype(bf16), preferred_element_type=f32)


def _mm_nt(a, b):
    return lax.dot_general(a.astype(bf16), b.astype(bf16), (((1,), (1,)), ((), ())), preferred_element_type=f32)


def _mm_tn(a, b):
    return lax.dot_general(a.astype(bf16), b.astype(bf16), (((0,), (0,)), ((), ())), preferred_element_type=f32)


def _bd(x):
    m0 = lax.broadcasted_iota(i32, x.shape, 1) < RWKV_HEAD
    zero = jnp.zeros_like(x)
    return jnp.concatenate([jnp.where(m0, x, zero), jnp.where(m0, zero, x)], axis=0)


def _side_by_side(d):
    h = d.shape[0] // 2
    m0 = lax.broadcasted_iota(i32, (h, d.shape[1]), 1) < RWKV_HEAD
    return jnp.where(m0, d[:h], d[h:])


def _chunk_maps(chunks):
    c = RWKV_CHUNK
    ri = lax.broadcasted_iota(i32, (c, PAIR), 0)
    ji = lax.broadcasted_iota(i32, (c, PAIR), 1) % RWKV_HEAD
    strict = ji < ri
    incl = ji <= ri
    diag = ji == ri
    eye = jnp.where(diag, 1.0, 0.0).astype(f32)
    each = lambda f, *ls: [f(*xs) for xs in zip(*ls)]
    cat0 = lambda *xs: jnp.concatenate(xs, axis=0)
    cat1 = lambda *xs: jnp.concatenate(xs, axis=1)
    tb = lambda x: x.astype(bf16)

    ats, bts, kts, rts, vs, cls = [list(x) for x in zip(*chunks)]
    pcs = each(lambda cl: jnp.exp(cl[c - 1:c, :]), cls)
    bd_a = each(lambda x: tb(_bd(x)), ats)
    bd_v = each(lambda x: tb(_bd(x)), vs)

    aa = each(lambda a, r, b, k: _mm_nt(cat0(tb(a), tb(r)), cat0(tb(_bd(b)), tb(_bd(k)))), ats, rts, bts, kts)
    l_ab = each(lambda x: tb(jnp.where(strict, x[:c, :PAIR], 0.0)), aa)
    a_k = each(lambda x: tb(cat0(jnp.where(strict, x[:c, PAIR:], 0.0), jnp.where(incl, x[c:, PAIR:], 0.0))), aa)
    a_rb = each(lambda x: tb(jnp.where(incl, x[c:, :PAIR], 0.0)), aa)

    tinv = each(lambda x: eye + x, l_ab)
    lp = each(lambda x: tb(_mm(x, _bd(x))), l_ab)
    p = 2
    while 2 * p < c:
        x2 = each(lambda t, x: _mm(cat0(tb(t), x), _bd(x)), tinv, lp)
        tinv = each(lambda t, x: t + x[:c], tinv, x2)
        lp = each(lambda x: tb(x[c:]), x2)
        p *= 2
    tinv = each(lambda t, x: t + _mm(t, _bd(x)), tinv, lp)

    wy = each(_mm, a_k, bd_v)
    za = each(lambda t, w, a: _mm(t, cat1(tb(_bd(w[:c])), a)), tinv, wy, bd_a)
    zp = each(lambda x: x[:, :PAIR], za)
    ac = each(lambda x: x[:, PAIR:], za)
    y1 = each(lambda ar, z, a: _mm(ar, cat1(tb(_bd(z)), tb(_bd(a)))), a_rb, zp, ac)
    yp = each(lambda w, y: w[c:] + y[:, :PAIR], wy, y1)
    rc = each(lambda r, y: r + y[:, PAIR:], rts, y1)
    sm = each(lambda b, pc, z, a: _mm_tn(b * pc, cat1(z, a)), bts, pcs, zp, ac)
    kv = each(lambda k, pc, v: _mm_tn(k * pc, v), kts, pcs, vs)
    sp = each(lambda x, m: _side_by_side(x) + _side_by_side(m[:, :PAIR]), kv, sm)
    g = each(lambda pc, m: jnp.where(diag, jnp.broadcast_to(pc, (c, PAIR)), 0.0) + _side_by_side(m[:, PAIR:]), pcs, sm)
    return list(zip(g, sp, rc, yp))


def _rwkv_chunk_body(at_ref, bt_ref, kt_ref, rt_ref, v_ref, cl_ref, g_ref, sp_ref, rc_ref, yp_ref):
    ts = at_ref.shape[0]
    c = RWKV_CHUNK
    sls = [slice(j * c, (j + 1) * c) for j in range(ts // c)]
    outs = _chunk_maps([(at_ref[sl, :], bt_ref[sl, :], kt_ref[sl, :], rt_ref[sl, :], v_ref[sl, :], cl_ref[sl, :])
                        for sl in sls])
    for sl, (g, sp, rc, yp) in zip(sls, outs):
        g_ref[sl, :] = g
        sp_ref[sl, :] = sp
        rc_ref[sl, :] = rc
        yp_ref[sl, :] = yp


def _rwkv_chunk(at, bt, kt, rt, v, cl, ts):
    t = at.shape[0]
    iblk = pl.BlockSpec((ts, PAIR), lambda i, p: (i, p))
    osh = jax.ShapeDtypeStruct((t, RWKV_W), f32)
    return pl.pallas_call(
        _rwkv_chunk_body,
        grid=(t // ts, N_PAIRS),
        in_specs=[iblk] * 6,
        out_specs=[iblk] * 4,
        out_shape=[osh] * 4,
        compiler_params=_cparams(("parallel", "parallel")),
        name="rwkv_chunk",
    )(at, bt, kt, rt, v, cl)


def _rwkv_state_body(g_ref, sp_ref, rc_ref, yp_ref, y_ref, s_ref):
    @pl.when(pl.program_id(0) == 0)
    def _():
        s_ref[...] = jnp.zeros_like(s_ref)

    c = RWKV_CHUNK
    bsz = g_ref.shape[0]
    nck = g_ref.shape[1] // c
    chains = [(b, slice(p * PAIR, (p + 1) * PAIR)) for b in range(bsz) for p in range(N_PAIRS)]
    states = [s_ref[b, :, ls] for b, ls in chains]
    for j in range(nck):
        sl = slice(j * c, (j + 1) * c)
        prods = [_mm_x3(jnp.concatenate([rc_ref[b, sl, ls], g_ref[b, sl, ls]], axis=0), _bf16_parts(_bd(s), 2))
                 for (b, ls), s in zip(chains, states)]
        for (b, ls), pr in zip(chains, prods):
            y_ref[b, sl, ls] = yp_ref[b, sl, ls] + pr[:c]
        states = [pr[c:] + sp_ref[b, sl, ls] for (b, ls), pr in zip(chains, prods)]
    for (b, ls), s in zip(chains, states):
        s_ref[b, :, ls] = s


def _rwkv_state(g, sp, rc, yp, bsz, s, nck):
    rows = nck * RWKV_CHUNK
    rblk = pl.BlockSpec((bsz, rows, RWKV_W), lambda i: (0, i, 0))
    r3 = lambda a: a.reshape(bsz, s, RWKV_W)
    y = pl.pallas_call(
        _rwkv_state_body,
        grid=(s // rows,),
        in_specs=[rblk] * 4,
        out_specs=rblk,
        out_shape=jax.ShapeDtypeStruct((bsz, s, RWKV_W), f32),
        scratch_shapes=[pltpu.VMEM((bsz, RWKV_CHUNK, RWKV_W), f32)],
        compiler_params=_cparams(("arbitrary",)),
        name="rwkv_state",
    )(r3(g), r3(sp), r3(rc), r3(yp))
    return y.reshape(bsz * s, RWKV_W)


def _outproj_body(x_ref, yl_ref, ys_ref, bon_ref, v_ref, g_ref, lg_ref, lb_ref, ones_ref, w_ref, o_ref):
    y = ys_ref[...]
    ones = ones_ref[...]
    inv_n = 1.0 / RWKV_HEAD
    mean = _sum_dot(y, ones, 2) * inv_n
    yc = y - mean
    var = _sum_dot(yc * yc, ones, 2) * inv_n
    yn = yc * lax.rsqrt(var + GN_EPS) * lg_ref[...] + lb_ref[...]
    yr = (yn + bon_ref[...] * v_ref[...]) * g_ref[...]
    cat = jnp.concatenate([yl_ref[...], yr.astype(bf16)], axis=1)
    o_ref[...] = x_ref[...] + jnp.dot(cat, w_ref[...], preferred_element_type=f32)


def _outproj(x2d, y_lru, y_scan, bon, v, g, lnx_g, lnx_b, ones_blk, w_out_bf, tm):
    t = x2d.shape[0]
    xb = pl.BlockSpec((tm, D_MODEL), lambda i: (i, 0))
    hb = pl.BlockSpec((tm, RWKV_W), lambda i: (i, 0))
    return pl.pallas_call(
        _outproj_body,
        grid=(t // tm,),
        in_specs=[xb, hb, hb, hb, hb, hb, _full(lnx_g.shape), _full(lnx_b.shape), _full(ones_blk.shape),
                  _full(w_out_bf.shape)],
        out_specs=xb,
        out_shape=jax.ShapeDtypeStruct((t, D_MODEL), f32),
        compiler_params=_cparams(("parallel",)),
        name="outproj",
    )(x2d, y_lru, y_scan, bon, v, g, lnx_g, lnx_b, ones_blk, w_out_bf)


def _memkv_body(m_ref, g_ref, wk_ref, wv_ref, k_ref, v_ref):
    h = _rms(m_ref[...], g_ref[...]).astype(bf16)
    k_ref[...] = jnp.dot(h, wk_ref[...], preferred_element_type=f32).astype(bf16)
    v_ref[...] = jnp.dot(h, wv_ref[...], preferred_element_type=f32).astype(bf16)


def _memkv(mem2d, g, wk_bf, wv_bf, tm):
    t = mem2d.shape[0]
    blk = pl.BlockSpec((tm, D_MODEL), lambda i: (i, 0))
    sh = jax.ShapeDtypeStruct((t, D_MODEL), bf16)
    return pl.pallas_call(
        _memkv_body,
        grid=(t // tm,),
        in_specs=[blk, _full(g.shape), _full(wk_bf.shape), _full(wv_bf.shape)],
        out_specs=[blk, blk],
        out_shape=[sh, sh],
        compiler_params=_cparams(("parallel",)),
        name="memkv",
    )(mem2d, g, wk_bf, wv_bf)


def _xattn_body(x_ref, k_ref, v_ref, gx_ref, wq_ref, wo_ref, gf_ref, wr_ref, br_ref, upper_ref,
                x2_ref, hf_ref, idx_ref, gate_ref, cnt_ref, base_ref):
    x = x_ref[...]
    h = _rms(x, gx_ref[...]).astype(bf16)
    q = jnp.dot(h, wq_ref[...], preferred_element_type=f32).astype(bf16)
    k = k_ref[...]
    v = v_ref[...]
    outs = []
    for hd in range(XA_HEADS):
        sl = slice(hd * XA_HEAD, (hd + 1) * XA_HEAD)
        sc = lax.dot_general(q[:, sl], k[:, sl], (((1,), (1,)), ((), ())),
                             preferred_element_type=f32) * (XA_HEAD ** -0.5)
        sc = sc - jnp.max(sc, axis=-1, keepdims=True)
        e = jnp.exp(sc)
        p = e / jnp.sum(e, axis=-1, keepdims=True)
        outs.append(jnp.dot(p.astype(bf16), v[:, sl], preferred_element_type=f32).astype(bf16))
    o = jnp.concatenate(outs, axis=1)
    x2 = x + jnp.dot(o, wo_ref[...], preferred_element_type=f32)
    x2_ref[...] = x2

    hf = _rms(x2, gf_ref[...])
    _store_row_tiles(hf_ref, hf)
    logits = lax.dot_general(wr_ref[...], hf.astype(bf16), (((1,), (1,)), ((), ())),
                             preferred_element_type=f32) + br_ref[...]
    erow = lax.broadcasted_iota(i32, logits.shape, 0)
    neg = jnp.float32(-jnp.inf)
    cur = logits
    vals = []
    idxs = []
    for _ in range(TOP_K):
        m = jnp.max(cur, axis=0, keepdims=True)
        am = jnp.min(jnp.where(cur == m, erow, N_EXPERTS), axis=0, keepdims=True)
        vals.append(m)
        idxs.append(am)
        cur = jnp.where(erow == am, neg, cur)
    es = [jnp.exp(vk - vals[0]) for vk in vals]
    den = es[0] + es[1] + es[2] + es[3]

    @pl.when(pl.program_id(0) == 0)
    def _():
        base_ref[...] = jnp.zeros_like(base_ref)

    onehot = [jnp.where(erow == am, 1.0, 0.0) for am in idxs]
    cnt = (onehot[0] + onehot[1]) + (onehot[2] + onehot[3])
    base = base_ref[:, 0:1]
    prior = jnp.dot(cnt.astype(bf16), upper_ref[...], preferred_element_type=f32) + base
    base_ref[...] = jnp.broadcast_to(base + jnp.sum(cnt, axis=1, keepdims=True), base_ref.shape)
    cnt_ref[...] = base_ref[...]

    orow = lax.broadcasted_iota(i32, idx_ref.shape, 0)
    idx_out = jnp.zeros(idx_ref.shape, i32)
    gate_out = jnp.zeros(gate_ref.shape, f32)
    for kq in range(TOP_K):
        rank = jnp.sum(prior * onehot[kq], axis=0, keepdims=True).astype(i32)
        idx_out = jnp.where(orow == kq, idxs[kq], idx_out)
        idx_out = jnp.where(orow == TOP_K + kq, rank, idx_out)
        gate_out = jnp.where(orow == kq, es[kq] / den, gate_out)
    idx_ref[...] = idx_out
    gate_ref[...] = gate_out


def _xattn(x1, kmem, vmem, g_xa, wq_bf, wo_bf, g_ffn, wr_t, br_col, bsz, s, mlen, tm):
    t = bsz * s
    nt = s // tm
    xb = pl.BlockSpec((tm, D_MODEL), lambda i: (i, 0))
    mb = pl.BlockSpec((mlen, D_MODEL), lambda i: (i // nt, 0))
    lb = pl.BlockSpec((2 * TOP_K, tm), lambda i: (0, i))
    upper = jnp.triu(jnp.ones((tm, tm), bf16), k=1)
    return pl.pallas_call(
        _xattn_body,
        grid=(t // tm,),
        in_specs=[xb, mb, mb, _full(g_xa.shape), _full(wq_bf.shape), _full(wo_bf.shape),
                  _full(g_ffn.shape), _full(wr_t.shape), _full(br_col.shape), _full(upper.shape)],
        out_specs=[xb, pl.BlockSpec((tm * ROW_TILE, LANES), lambda i: (i, 0)), lb, lb, _full((N_EXPERTS, LANES))],
        out_shape=[jax.ShapeDtypeStruct((t, D_MODEL), f32), jax.ShapeDtypeStruct((t * ROW_TILE, LANES), f32),
                   jax.ShapeDtypeStruct((2 * TOP_K, t), i32), jax.ShapeDtypeStruct((2 * TOP_K, t), f32),
                   jax.ShapeDtypeStruct((N_EXPERTS, LANES), f32)],
        scratch_shapes=[pltpu.VMEM((N_EXPERTS, LANES), f32)],
        compiler_params=_cparams(("arbitrary",)),
        name="xattn",
    )(x1, kmem, vmem, g_xa, wq_bf, wo_bf, g_ffn, wr_t, br_col, upper)


DMA_UNROLL = 8


def _tile_at(ref, row):
    return ref.at[pl.ds(pl.multiple_of(row * ROW_TILE, ROW_TILE), ROW_TILE), :]


def _dispatch_body(dest_ref, pend_ref, hf_ref, xs_hbm, zeros_ref, sem, zsem):
    i = pl.program_id(0)
    tm = hf_ref.shape[0] // ROW_TILE
    zrows = MOE_BLOCK * ROW_TILE

    def zero_copy(e):
        start = pl.multiple_of((pend_ref[e] - MOE_BLOCK) * ROW_TILE, ROW_TILE)
        return pltpu.make_async_copy(zeros_ref, xs_hbm.at[pl.ds(start, zrows), :], zsem)

    def nonempty(e):
        return pend_ref[e] > (pend_ref[e - 1] if e else 0)

    @pl.when(i == 0)
    def _():
        zeros_ref[...] = jnp.zeros_like(zeros_ref)
        for e in range(N_EXPERTS):
            @pl.when(nonempty(e))
            def _():
                zero_copy(e).start()
        for e in range(N_EXPERTS):
            @pl.when(nonempty(e))
            def _():
                zero_copy(e).wait()

        def tail_copy(b):
            return pltpu.make_async_copy(zeros_ref, xs_hbm.at[pl.ds(pl.multiple_of(b * zrows, zrows), zrows), :], zsem)

        def tail_start(b, carry):
            tail_copy(b).start()
            return carry

        def tail_wait(b, carry):
            tail_copy(b).wait()
            return carry
        n_used = pend_ref[N_EXPERTS - 1] // MOE_BLOCK
        n_all = xs_hbm.shape[0] // zrows
        lax.fori_loop(n_used, n_all, tail_start, 0)
        lax.fori_loop(n_used, n_all, tail_wait, 0)

    def body(q, carry):
        for u in range(DMA_UNROLL):
            r = q * DMA_UNROLL + u
            src = _tile_at(hf_ref, r)
            for kq in range(TOP_K):
                pltpu.make_async_copy(src, _tile_at(xs_hbm, dest_ref[(i * tm + r) * TOP_K + kq]),
                                      sem).start(priority=kq % 2)
        return carry
    lax.fori_loop(0, tm // DMA_UNROLL, body, 0)
    for kq in range(TOP_K):
        pltpu.make_async_copy(hf_ref, xs_hbm.at[pl.ds(0, tm * ROW_TILE), :], sem).wait()


def _dispatch(dest_flat, pends, hf_tiles, n_blocks, tm):
    t = hf_tiles.shape[0] // ROW_TILE
    rows = n_blocks * MOE_BLOCK
    grid_spec = pltpu.PrefetchScalarGridSpec(
        num_scalar_prefetch=2,
        grid=(t // tm,),
        in_specs=[pl.BlockSpec((tm * ROW_TILE, LANES), lambda i, d, pe: (i, 0))],
        out_specs=pl.BlockSpec(memory_space=pl.ANY),
        scratch_shapes=[pltpu.VMEM((MOE_BLOCK * ROW_TILE, LANES), f32), pltpu.SemaphoreType.DMA(()),
                        pltpu.SemaphoreType.DMA(())],
    )
    return pl.pallas_call(
        _dispatch_body,
        grid_spec=grid_spec,
        out_shape=jax.ShapeDtypeStruct((rows * ROW_TILE, LANES), f32),
        compiler_params=_cparams(("arbitrary",)),
        name="dispatch",
    )(dest_flat, pends, hf_tiles)


def _moe_body(pend_ref, xs_hbm, wgu_ref, bgu_ref, wdn_ref, bdn_ref, ys_hbm, xbuf, ybuf, wgu_bf, wdn_bf,
              sem_in, sem_out):
    e = pl.program_id(0)
    blk = MOE_BLOCK * ROW_TILE
    end_blk = pend_ref[e] // MOE_BLOCK
    start_blk = jnp.where(e == 0, 0, pend_ref[jnp.maximum(e - 1, 0)] // MOE_BLOCK)
    nb = end_blk - start_blk

    def rows_of(j):
        return pl.ds(pl.multiple_of((start_blk + j) * blk, blk), blk)

    def x_copy(j, slot):
        return pltpu.make_async_copy(xs_hbm.at[rows_of(j), :], xbuf.at[slot], sem_in.at[slot])

    def y_copy(j, slot):
        return pltpu.make_async_copy(ybuf.at[slot], ys_hbm.at[rows_of(j), :], sem_out.at[slot])

    @pl.when(nb > 0)
    def _():
        x_copy(0, 0).start()
        wgu_bf[...] = wgu_ref[0].astype(bf16)
        wdn_bf[...] = wdn_ref[0].astype(bf16)

    def block(j, carry):
        slot = j % 2

        @pl.when(j + 1 < nb)
        def _():
            x_copy(j + 1, 1 - slot).start()

        x_copy(j, slot).wait()

        @pl.when(j >= 2)
        def _():
            y_copy(j - 2, slot).wait()

        xb = _load_row_tiles(xbuf.at[slot], MOE_BLOCK).astype(bf16)
        gu = jnp.dot(xb, wgu_bf[...], preferred_element_type=f32) + bgu_ref[0]
        gate = jnp.minimum(gu[:, :D_FF], SWIGLU_LIMIT)
        up = jnp.clip(gu[:, D_FF:], -SWIGLU_LIMIT, SWIGLU_LIMIT)
        act = (up + 1.0) * (gate * jax.nn.sigmoid(SWIGLU_ALPHA * gate))
        y = jnp.dot(act.astype(bf16), wdn_bf[...], preferred_element_type=f32) + bdn_ref[0]
        _store_row_tiles(ybuf.at[slot], y)
        y_copy(j, slot).start()
        return carry

    lax.fori_loop(0, nb, block, 0)

    @pl.when(nb >= 2)
    def _():
        y_copy(nb - 2, nb % 2).wait()

    @pl.when(nb >= 1)
    def _():
        y_copy(nb - 1, (nb - 1) % 2).wait()

    @pl.when(e == N_EXPERTS - 1)
    def _():
        n_all = ys_hbm.shape[0] // blk
        ybuf[0] = jnp.zeros(ybuf.shape[1:], ybuf.dtype)

        def tail_copy(b):
            return pltpu.make_async_copy(ybuf.at[0], ys_hbm.at[pl.ds(pl.multiple_of(b * blk, blk), blk), :],
                                         sem_out.at[0])

        def tail_start(b, carry):
            tail_copy(b).start()
            return carry

        def tail_wait(b, carry):
            tail_copy(b).wait()
            return carry
        lax.fori_loop(end_blk, n_all, tail_start, 0)
        lax.fori_loop(end_blk, n_all, tail_wait, 0)


def _moe(pends, xs_tiles, w_gu, b_gu, w_dn, b_dn, n_blocks):
    rows = n_blocks * MOE_BLOCK
    grid_spec = pltpu.PrefetchScalarGridSpec(
        num_scalar_prefetch=1,
        grid=(N_EXPERTS,),
        in_specs=[
            pl.BlockSpec(memory_space=pl.ANY),
            pl.BlockSpec((1, D_MODEL, 2 * D_FF), lambda e, pe: (e, 0, 0)),
            pl.BlockSpec((1, 1, 2 * D_FF), lambda e, pe: (e, 0, 0)),
            pl.BlockSpec((1, D_FF, D_MODEL), lambda e, pe: (e, 0, 0)),
            pl.BlockSpec((1, 1, D_MODEL), lambda e, pe: (e, 0, 0)),
        ],
        out_specs=pl.BlockSpec(memory_space=pl.ANY),
        scratch_shapes=[pltpu.VMEM((2, MOE_BLOCK * ROW_TILE, LANES), f32),
                        pltpu.VMEM((2, MOE_BLOCK * ROW_TILE, LANES), f32),
                        pltpu.VMEM((D_MODEL, 2 * D_FF), bf16), pltpu.VMEM((D_FF, D_MODEL), bf16),
                        pltpu.SemaphoreType.DMA((2,)), pltpu.SemaphoreType.DMA((2,))],
    )
    return pl.pallas_call(
        _moe_body,
        grid_spec=grid_spec,
        out_shape=jax.ShapeDtypeStruct((rows * ROW_TILE, LANES), f32),
        compiler_params=_cparams(("arbitrary",)),
        name="moe",
    )(pends, xs_tiles, w_gu, b_gu.reshape(N_EXPERTS, 1, 2 * D_FF), w_dn, b_dn.reshape(N_EXPERTS, 1, D_MODEL))


def _combine_body(pos_ref, ys_hbm, x_ref, gate_ref, g_ref, o_ref, buf, sem):
    i = pl.program_id(0)
    n = pl.num_programs(0)
    tc = x_ref.shape[0]
    slot = i % 2
    slot_rows = TOP_K * tc * ROW_TILE

    def start(step, sl):
        def body(q, carry):
            for u in range(DMA_UNROLL):
                r = q * DMA_UNROLL + u
                for kq in range(TOP_K):
                    dst = _tile_at(buf, (sl * TOP_K + kq) * tc + r)
                    pltpu.make_async_copy(_tile_at(ys_hbm, pos_ref[(step * tc + r) * TOP_K + kq]), dst,
                                          sem.at[sl]).start(priority=kq % 2)
            return carry
        lax.fori_loop(0, tc // DMA_UNROLL, body, 0)

    @pl.when(i == 0)
    def _():
        start(0, 0)

    @pl.when(i + 1 < n)
    def _():
        start(i + 1, 1 - slot)

    off = pl.multiple_of(slot * slot_rows, slot_rows)
    pltpu.make_async_copy(ys_hbm.at[pl.ds(0, slot_rows), :], buf.at[pl.ds(off, slot_rows), :], sem.at[slot]).wait()
    gates = gate_ref[...]
    acc = x_ref[...]
    for kq in range(TOP_K):
        rows = _load_row_tiles(buf, tc, off + kq * tc * ROW_TILE)
        acc = acc + gates[:, kq:kq + 1] * rows
    o_ref[...] = _rms(acc, g_ref[...])


def _combine(pos_flat, ys_tiles, x2, gate_pad, g_final, tc):
    t = x2.shape[0]
    grid_spec = pltpu.PrefetchScalarGridSpec(
        num_scalar_prefetch=1,
        grid=(t // tc,),
        in_specs=[pl.BlockSpec(memory_space=pl.ANY),
                  pl.BlockSpec((tc, D_MODEL), lambda i, p: (i, 0)),
                  pl.BlockSpec((tc, LANES), lambda i, p: (i, 0)),
                  pl.BlockSpec((1, D_MODEL), lambda i, p: (0, 0))],
        out_specs=pl.BlockSpec((tc, D_MODEL), lambda i, p: (i, 0)),
        scratch_shapes=[pltpu.VMEM((2 * TOP_K * tc * ROW_TILE, LANES), f32), pltpu.SemaphoreType.DMA((2,))],
    )
    return pl.pallas_call(
        _combine_body,
        grid_spec=grid_spec,
        out_shape=jax.ShapeDtypeStruct((t, D_MODEL), f32),
        compiler_params=_cparams(("arbitrary",)),
        name="combine",
    )(pos_flat, ys_tiles, x2, gate_pad, g_final)


def _routing(top_idx, rank, counts, t):
    n_assign = t * TOP_K
    experts = jnp.arange(N_EXPERTS, dtype=i32)
    padded = (counts + MOE_BLOCK - 1) // MOE_BLOCK * MOE_BLOCK
    pends = jnp.cumsum(padded).astype(i32)
    pstarts = pends - padded
    start_of = jnp.sum(jnp.where(top_idx[:, :, None] == experts, pstarts, 0), axis=-1)
    dest = (start_of + rank).astype(i32).reshape(n_assign)
    n_blocks = (n_assign + N_EXPERTS * (MOE_BLOCK - 1) + MOE_BLOCK - 1) // MOE_BLOCK
    return dest, pends, n_blocks


def _block_diag(w):
    n, bi, bj = w.shape
    eye = jnp.eye(n, dtype=w.dtype)
    return jnp.einsum('nij,nm->nimj', w, eye).reshape(n * bi, n * bj)


def _layer(x2d, mem2d, bsz, s, mlen, p):
    t = bsz * s
    row = lambda a: a.reshape(1, -1)
    ones_blk = _block_diag(jnp.ones((RWKV_W // RWKV_HEAD, RWKV_HEAD, RWKV_HEAD), bf16))

    xl, gl, ur = _inproj(x2d, row(p['norm_mix_g']), p['w_in'].astype(bf16), tm=min(512, t))

    wg = jnp.concatenate([_block_diag(p['lru_wx']), _block_diag(p['lru_wa'])], axis=1).astype(bf16)
    bg = jnp.concatenate([p['lru_bx'], p['lru_ba']]).reshape(1, -1)
    y_lru = _lru(xl, gl, p['conv_w'], row(p['conv_b']), wg, bg, row(p['lru_lambda']), bsz, s, ts=min(256, s))

    zl = jnp.zeros((DECAY_LORA, RWKV_W), f32)
    wl = jnp.concatenate([jnp.concatenate([p['rwkv_w_up'], zl], axis=1),
                          jnp.concatenate([zl, p['rwkv_a_up']], axis=1)], axis=0).astype(bf16)
    at, bt, kt, rt, v, cl, bon, g = _rwkv_prep(
        ur, row(p['rwkv_mu']), row(p['rwkv_w0']), row(p['rwkv_a0']), wl, p['rwkv_g_up'].astype(bf16),
        row(p['rwkv_k_k']), row(p['rwkv_k_a']), row(p['rwkv_r_k']), ones_blk, bsz, s, ts=min(256, s))
    gm, sp, rc, yp = _rwkv_chunk(at, bt, kt, rt, v, cl, ts=min(1024, s))
    y_scan = _rwkv_state(gm, sp, rc, yp, bsz, s, nck=min(4, s // RWKV_CHUNK))

    x1 = _outproj(x2d, y_lru, y_scan, bon, v, g, row(p['rwkv_lnx_g']), row(p['rwkv_lnx_b']), ones_blk,
                  p['w_out'].astype(bf16), tm=min(512, t))

    kmem, vmem = _memkv(mem2d, row(p['norm_mem_g']), p['xa_wk'].astype(bf16), p['xa_wv'].astype(bf16),
                        tm=min(512, bsz * mlen))
    x2, hf, route, gates, cnt_pad = _xattn(x1, kmem, vmem, row(p['norm_xa_g']), p['xa_wq'].astype(bf16),
                                           p['xa_wo'].astype(bf16), row(p['norm_ffn_g']),
                                           p['w_router'].T.astype(bf16), p['b_router'].reshape(-1, 1),
                                           bsz, s, mlen, tm=min(512, s))

    counts = cnt_pad[:, 0].astype(i32)
    dest, pends, n_blocks = _routing(route[:TOP_K].T, route[TOP_K:].T, counts, t)
    gate_pad = jnp.pad(gates[:TOP_K].T, ((0, 0), (0, LANES - TOP_K)))
    xs = _dispatch(dest, pends, hf, n_blocks, tm=min(256, t))
    ys = _moe(pends, xs, p['w_gu'], p['b_gu'], p['w_dn'], p['b_dn'], n_blocks)
    return _combine(dest, ys, x2, gate_pad, row(p['final_norm_g']), tc=min(256, t))


def kernel(x, mem, norm_mix_g, w_in, conv_w, conv_b, lru_wx, lru_bx, lru_wa, lru_ba, lru_lambda, rwkv_mu, rwkv_w0, rwkv_w_up, rwkv_a0, rwkv_a_up, rwkv_g_up, rwkv_k_k, rwkv_k_a, rwkv_r_k, rwkv_lnx_g, rwkv_lnx_b, w_out, norm_xa_g, norm_mem_g, xa_wq, xa_wk, xa_wv, xa_wo, norm_ffn_g, w_router, b_router, w_gu, b_gu, w_dn, b_dn, final_norm_g):
    bsz, s, d = x.shape
    mlen = mem.shape[1]
    assert d == D_MODEL and w_in.shape[0] == 1
    p = dict(norm_mix_g=norm_mix_g[0], w_in=w_in[0], conv_w=conv_w[0], conv_b=conv_b[0], lru_wx=lru_wx[0],
             lru_bx=lru_bx[0], lru_wa=lru_wa[0], lru_ba=lru_ba[0], lru_lambda=lru_lambda[0],
             rwkv_mu=rwkv_mu[0], rwkv_w0=rwkv_w0[0], rwkv_w_up=rwkv_w_up[0], rwkv_a0=rwkv_a0[0],
             rwkv_a_up=rwkv_a_up[0], rwkv_g_up=rwkv_g_up[0], rwkv_k_k=rwkv_k_k[0], rwkv_k_a=rwkv_k_a[0],
             rwkv_r_k=rwkv_r_k[0].reshape(-1), rwkv_lnx_g=rwkv_lnx_g[0], rwkv_lnx_b=rwkv_lnx_b[0],
             w_out=w_out[0], norm_xa_g=norm_xa_g[0], norm_mem_g=norm_mem_g[0], xa_wq=xa_wq[0],
             xa_wk=xa_wk[0], xa_wv=xa_wv[0], xa_wo=xa_wo[0], norm_ffn_g=norm_ffn_g[0],
             w_router=w_router[0], b_router=b_router[0], w_gu=w_gu[0], b_gu=b_gu[0], w_dn=w_dn[0],
             b_dn=b_dn[0], final_norm_g=final_norm_g)
    out = _layer(x.reshape(bsz * s, d), mem.reshape(bsz * mlen, d), bsz, s, mlen, p)
    return out.reshape(bsz, s, d)
```

```python
import functools

import jax
import jax.numpy as jnp
from jax import lax
from jax.experimental import pallas as pl
from jax.experimental.pallas import tpu as pltpu

f32 = jnp.float32
bf16 = jnp.bfloat16
i32 = jnp.int32

D_MODEL = 1024
LRU_W = 512
RWKV_W = 512
LRU_BLOCKS = 8
LRU_BLOCK = 64
CONV_W = 4
LRU_C = 8.0
RWKV_HEAD = 64
DECAY_LORA = 64
AAA_LORA = 64
GATE_LORA = 128
RWKV_IN = 3 * RWKV_W + DECAY_LORA + AAA_LORA + GATE_LORA
XA_HEADS = 4
XA_HEAD = D_MODEL // XA_HEADS
N_EXPERTS = 32
TOP_K = 4
D_FF = D_MODEL
SWIGLU_LIMIT = 7.0
SWIGLU_ALPHA = 1.702
EPS = 1e-6
GN_EPS = 64e-5

LANES = 128
SUBLANES = 8
RWKV_CHUNK = 64
PAIR = 2 * RWKV_HEAD
N_PAIRS = RWKV_W // PAIR
MOE_BLOCK = 256
ROW_TILE = D_MODEL // LANES
VMEM_LIMIT = 52 * 1024 * 1024


def _cparams(sem):
    return pltpu.CompilerParams(dimension_semantics=sem, vmem_limit_bytes=VMEM_LIMIT)


def _rms(x, g):
    return x * lax.rsqrt(jnp.mean(x * x, axis=-1, keepdims=True) + EPS) * g


def _full(shape):
    n = len(shape)
    return pl.BlockSpec(shape, lambda *a: (0,) * n)


def _shift_rows(x, prev8, d):
    xr = pltpu.roll(x, d, 0)
    tr = pltpu.roll(prev8, d, 0)
    row = lax.broadcasted_iota(i32, prev8.shape, 0)
    head = jnp.where(row < d, tr, xr[:SUBLANES])
    return jnp.concatenate([head, xr[SUBLANES:]], axis=0)


def _bf16_parts(x, n):
    parts = []
    for _ in range(n):
        piece = x.astype(bf16)
        parts.append(piece)
        x = x - piece.astype(f32)
    return parts


def _sum_dot(x, mask_bf, n_parts, mask_left=False):
    acc = None
    for piece in _bf16_parts(x, n_parts):
        d = (jnp.dot(mask_bf, piece, preferred_element_type=f32) if mask_left
             else jnp.dot(piece, mask_bf, preferred_element_type=f32))
        acc = d if acc is None else acc + d
    return acc


def _store_row_tiles(ref, val, off=0):
    n = val.shape[0]
    for j in range(ROW_TILE):
        ref[pl.ds(off + j, n, stride=ROW_TILE), :] = val[:, j * LANES:(j + 1) * LANES]


def _load_row_tiles(ref, n, off=0):
    return jnp.concatenate([ref[pl.ds(off + j, n, stride=ROW_TILE), :] for j in range(ROW_TILE)], axis=1)


def _inproj_body(x_ref, g_ref, w_ref, xl_ref, gl_ref, ur_ref):
    h = _rms(x_ref[...], g_ref[...])
    u = jnp.dot(h.astype(bf16), w_ref[...], preferred_element_type=f32)
    xl_ref[...] = u[:, :LRU_W]
    gl_ref[...] = u[:, LRU_W:2 * LRU_W]
    ur_ref[...] = u[:, 2 * LRU_W:]


def _inproj(x2d, g, w_in_bf, tm):
    t = x2d.shape[0]
    return pl.pallas_call(
        _inproj_body,
        grid=(t // tm,),
        in_specs=[pl.BlockSpec((tm, D_MODEL), lambda i: (i, 0)), _full((1, D_MODEL)),
                  _full(w_in_bf.shape)],
        out_specs=[pl.BlockSpec((tm, LRU_W), lambda i: (i, 0)),
                   pl.BlockSpec((tm, LRU_W), lambda i: (i, 0)),
                   pl.BlockSpec((tm, RWKV_IN), lambda i: (i, 0))],
        out_shape=[jax.ShapeDtypeStruct((t, LRU_W), f32), jax.ShapeDtypeStruct((t, LRU_W), f32),
                   jax.ShapeDtypeStruct((t, RWKV_IN), f32)],
        compiler_params=_cparams(("parallel",)),
        name="inproj",
    )(x2d, g, w_in_bf)


def _lru_body(xl_ref, gl_ref, cw_ref, cb_ref, wg_ref, bg_ref, lam_ref, o_ref, tail_ref, h_ref):
    ts = xl_ref.shape[0]

    @pl.when(pl.program_id(1) == 0)
    def _():
        tail_ref[...] = jnp.zeros_like(tail_ref)
        h_ref[...] = jnp.zeros_like(h_ref)

    x = xl_ref[...]
    tail = tail_ref[...]
    cw = cw_ref[...]
    xc = cb_ref[...] + cw[CONV_W - 1:CONV_W] * x
    for d in range(1, CONV_W):
        xc = xc + cw[CONV_W - 1 - d:CONV_W - d] * _shift_rows(x, tail, d)
    tail_ref[...] = x[ts - SUBLANES:]

    gates = jax.nn.sigmoid(jnp.dot(xc.astype(bf16), wg_ref[...], preferred_element_type=f32) + bg_ref[...])
    gx = gates[:, :LRU_W]
    ga = gates[:, LRU_W:]
    log_a = -LRU_C * ga * jax.nn.softplus(-lam_ref[...])
    a = jnp.exp(log_a)
    b = jnp.sqrt(-jnp.tanh(log_a) * (a * a + 1.0)) * gx * xc

    row = lax.broadcasted_iota(i32, (ts, LRU_W), 0)
    d = 1
    while d < ts:
        keep = row >= d
        a_s = jnp.where(keep, pltpu.roll(a, d, 0), 1.0)
        b_s = jnp.where(keep, pltpu.roll(b, d, 0), 0.0)
        b = a * b_s + b
        a = a * a_s
        d *= 2
    h = b + a * h_ref[SUBLANES - 1:SUBLANES, :]
    h_ref[...] = h[ts - SUBLANES:]
    o_ref[...] = (h * jax.nn.gelu(gl_ref[...])).astype(o_ref.dtype)


def _lru(xl, gl, cw, cb, wg_bf, bg, lam, bsz, s, ts):
    nt = s // ts
    blk = pl.BlockSpec((ts, LRU_W), lambda b, i: (b * nt + i, 0))
    return pl.pallas_call(
        _lru_body,
        grid=(bsz, nt),
        in_specs=[blk, blk, _full(cw.shape), _full(cb.shape), _full(wg_bf.shape), _full(bg.shape),
                  _full(lam.shape)],
        out_specs=blk,
        out_shape=jax.ShapeDtypeStruct((bsz * s, LRU_W), bf16),
        scratch_shapes=[pltpu.VMEM((SUBLANES, LRU_W), f32), pltpu.VMEM((SUBLANES, LRU_W), f32)],
        compiler_params=_cparams(("parallel", "arbitrary")),
        name="lru",
    )(xl, gl, cw, cb, wg_bf, bg, lam)


def _rwkv_prep_body(ur_ref, mu_ref, w0_ref, a0_ref, wl_ref, gup_ref, kk_ref, ka_ref, rk_ref, ones_ref,
                    at_ref, bt_ref, kt_ref, rt_ref, v_ref, cl_ref, bon_ref, g_ref, prev_ref):
    ts = ur_ref.shape[0]

    @pl.when(pl.program_id(1) == 0)
    def _():
        prev_ref[...] = jnp.zeros_like(prev_ref)

    u0 = ur_ref[...]
    us = _shift_rows(u0, prev_ref[...], 1)
    prev_ref[...] = u0[ts - SUBLANES:]
    u = u0 + (us - u0) * mu_ref[...]
    c1, c2, c3 = RWKV_W, 2 * RWKV_W, 3 * RWKV_W
    r = u[:, :c1]
    k = u[:, c1:c2]
    v = u[:, c2:c3]
    lora = u[:, c3:c3 + LANES]
    dg = u[:, c3 + LANES:]
    lane = lax.broadcasted_iota(i32, lora.shape, 1)
    lora = jnp.where(lane < DECAY_LORA, jnp.tanh(lora), lora)
    proj = jnp.dot(lora.astype(bf16), wl_ref[...], preferred_element_type=f32)
    w = -jax.nn.softplus(-(w0_ref[...] + proj[:, :RWKV_W])) - 0.5
    lw = -jnp.exp(w)
    a = jax.nn.sigmoid(a0_ref[...] + proj[:, RWKV_W:])
    g_ref[...] = jnp.dot(jax.nn.sigmoid(dg).astype(bf16), gup_ref[...], preferred_element_type=f32)

    ones = ones_ref[...]
    kk = k * kk_ref[...]
    ss = _sum_dot(kk * kk, ones, 2)
    kk = kk / jnp.maximum(jnp.sqrt(ss), 1e-12)
    k2 = k * (1.0 + (a - 1.0) * ka_ref[...])
    bon_ref[...] = _sum_dot(r * k2 * rk_ref[...], ones, 2)

    ri = lax.broadcasted_iota(i32, (ts, ts), 0)
    ci = lax.broadcasted_iota(i32, (ts, ts), 1)
    tri = jnp.where((ci <= ri) & (ri // RWKV_CHUNK == ci // RWKV_CHUNK), 1.0, 0.0).astype(bf16)
    cl = _sum_dot(lw, tri, 3, mask_left=True)
    e_neg = jnp.exp(-cl)
    at_ref[...] = -kk * jnp.exp(cl - lw)
    bt_ref[...] = kk * a * e_neg
    kt_ref[...] = k2 * e_neg
    rt_ref[...] = r * jnp.exp(cl)
    v_ref[...] = v
    cl_ref[...] = cl


def _rwkv_prep(ur, mu, w0, a0, wl_bf, gup_bf, k_k, k_a, r_k, ones_blk, bsz, s, ts):
    nt = s // ts
    t = bsz * s
    oblk = pl.BlockSpec((ts, RWKV_W), lambda b, i: (b * nt + i, 0))
    osh = jax.ShapeDtypeStruct((t, RWKV_W), f32)
    return pl.pallas_call(
        _rwkv_prep_body,
        grid=(bsz, nt),
        in_specs=[pl.BlockSpec((ts, RWKV_IN), lambda b, i: (b * nt + i, 0)), _full(mu.shape),
                  _full(w0.shape), _full(a0.shape), _full(wl_bf.shape), _full(gup_bf.shape),
                  _full(k_k.shape), _full(k_a.shape), _full(r_k.shape), _full(ones_blk.shape)],
        out_specs=[oblk] * 8,
        out_shape=[osh] * 8,
        scratch_shapes=[pltpu.VMEM((SUBLANES, RWKV_IN), f32)],
        compiler_params=_cparams(("parallel", "arbitrary")),
        name="rwkv_prep",
    )(ur, mu, w0, a0, wl_bf, gup_bf, k_k, k_a, r_k, ones_blk)


def _mm_x3(a, b_parts):
    ah, al = _bf16_parts(a, 2)
    bh, bl = b_parts
    d = lambda x, y: jnp.dot(x, y, preferred_element_type=f32)
    return d(ah, bh) + (d(ah, bl) + d(al, bh))


def _mm(a, b):
    return jnp.dot(a.astype(bf16), b.astype(bf16), preferred_element_type=f32)


def _mm_nt(a, b):
    return lax.dot_general(a.astype(bf16), b.astype(bf16), (((1,), (1,)), ((), ())), preferred_element_type=f32)


def _mm_tn(a, b):
    return lax.dot_general(a.astype(bf16), b.astype(bf16), (((0,), (0,)), ((), ())), preferred_element_type=f32)


def _bd(x):
    m0 = lax.broadcasted_iota(i32, x.shape, 1) < RWKV_HEAD
    zero = jnp.zeros_like(x)
    return jnp.concatenate([jnp.where(m0, x, zero), jnp.where(m0, zero, x)], axis=0)


def _side_by_side(d):
    h = d.shape[0] // 2
    m0 = lax.broadcasted_iota(i32, (h, d.shape[1]), 1) < RWKV_HEAD
    return jnp.where(m0, d[:h], d[h:])


def _chunk_maps(chunks):
    c = RWKV_CHUNK
    ri = lax.broadcasted_iota(i32, (c, PAIR), 0)
    ji = lax.broadcasted_iota(i32, (c, PAIR), 1) % RWKV_HEAD
    strict = ji < ri
    incl = ji <= ri
    diag = ji == ri
    eye = jnp.where(diag, 1.0, 0.0).astype(f32)
    each = lambda f, *ls: [f(*xs) for xs in zip(*ls)]
    cat0 = lambda *xs: jnp.concatenate(xs, axis=0)
    cat1 = lambda *xs: jnp.concatenate(xs, axis=1)
    tb = lambda x: x.astype(bf16)

    ats, bts, kts, rts, vs, cls = [list(x) for x in zip(*chunks)]
    pcs = each(lambda cl: jnp.exp(cl[c - 1:c, :]), cls)
    bd_a = each(lambda x: tb(_bd(x)), ats)
    bd_v = each(lambda x: tb(_bd(x)), vs)

    aa = each(lambda a, r, b, k: _mm_nt(cat0(tb(a), tb(r)), cat0(tb(_bd(b)), tb(_bd(k)))), ats, rts, bts, kts)
    l_ab = each(lambda x: tb(jnp.where(strict, x[:c, :PAIR], 0.0)), aa)
    a_k = each(lambda x: tb(cat0(jnp.where(strict, x[:c, PAIR:], 0.0), jnp.where(incl, x[c:, PAIR:], 0.0))), aa)
    a_rb = each(lambda x: tb(jnp.where(incl, x[c:, :PAIR], 0.0)), aa)

    tinv = each(lambda x: eye + x, l_ab)
    lp = each(lambda x: tb(_mm(x, _bd(x))), l_ab)
    p = 2
    while 2 * p < c:
        x2 = each(lambda t, x: _mm(cat0(tb(t), x), _bd(x)), tinv, lp)
        tinv = each(lambda t, x: t + x[:c], tinv, x2)
        lp = each(lambda x: tb(x[c:]), x2)
        p *= 2
    tinv = each(lambda t, x: t + _mm(t, _bd(x)), tinv, lp)

    wy = each(_mm, a_k, bd_v)
    za = each(lambda t, w, a: _mm(t, cat1(tb(_bd(w[:c])), a)), tinv, wy, bd_a)
    zp = each(lambda x: x[:, :PAIR], za)
    ac = each(lambda x: x[:, PAIR:], za)
    y1 = each(lambda ar, z, a: _mm(ar, cat1(tb(_bd(z)), tb(_bd(a)))), a_rb, zp, ac)
    yp = each(lambda w, y: w[c:] + y[:, :PAIR], wy, y1)
    rc = each(lambda r, y: r + y[:, PAIR:], rts, y1)
    sm = each(lambda b, pc, z, a: _mm_tn(b * pc, cat1(z, a)), bts, pcs, zp, ac)
    kv = each(lambda k, pc, v: _mm_tn(k * pc, v), kts, pcs, vs)
    sp = each(lambda x, m: _side_by_side(x) + _side_by_side(m[:, :PAIR]), kv, sm)
    g = each(lambda pc, m: jnp.where(diag, jnp.broadcast_to(pc, (c, PAIR)), 0.0) + _side_by_side(m[:, PAIR:]), pcs, sm)
    return list(zip(g, sp, rc, yp))


def _rwkv_chunk_body(at_ref, bt_ref, kt_ref, rt_ref, v_ref, cl_ref, g_ref, sp_ref, rc_ref, yp_ref):
    ts = at_ref.shape[0]
    c = RWKV_CHUNK
    sls = [slice(j * c, (j + 1) * c) for j in range(ts // c)]
    outs = _chunk_maps([(at_ref[sl, :], bt_ref[sl, :], kt_ref[sl, :], rt_ref[sl, :], v_ref[sl, :], cl_ref[sl, :])
                        for sl in sls])
    for sl, (g, sp, rc, yp) in zip(sls, outs):
        g_ref[sl, :] = g
        sp_ref[sl, :] = sp
        rc_ref[sl, :] = rc
        yp_ref[sl, :] = yp


def _rwkv_chunk(at, bt, kt, rt, v, cl, ts):
    t = at.shape[0]
    iblk = pl.BlockSpec((ts, PAIR), lambda i, p: (i, p))
    osh = jax.ShapeDtypeStruct((t, RWKV_W), f32)
    return pl.pallas_call(
        _rwkv_chunk_body,
        grid=(t // ts, N_PAIRS),
        in_specs=[iblk] * 6,
        out_specs=[iblk] * 4,
        out_shape=[osh] * 4,
        compiler_params=_cparams(("parallel", "parallel")),
        name="rwkv_chunk",
    )(at, bt, kt, rt, v, cl)


def _rwkv_state_body(g_ref, sp_ref, rc_ref, yp_ref, y_ref, s_ref):
    @pl.when(pl.program_id(0) == 0)
    def _():
        s_ref[...] = jnp.zeros_like(s_ref)

    c = RWKV_CHUNK
    bsz = g_ref.shape[0]
    nck = g_ref.shape[1] // c
    chains = [(b, slice(p * PAIR, (p + 1) * PAIR)) for b in range(bsz) for p in range(N_PAIRS)]
    states = [s_ref[b, :, ls] for b, ls in chains]
    for j in range(nck):
        sl = slice(j * c, (j + 1) * c)
        prods = [_mm_x3(jnp.concatenate([rc_ref[b, sl, ls], g_ref[b, sl, ls]], axis=0), _bf16_parts(_bd(s), 2))
                 for (b, ls), s in zip(chains, states)]
        for (b, ls), pr in zip(chains, prods):
            y_ref[b, sl, ls] = yp_ref[b, sl, ls] + pr[:c]
        states = [pr[c:] + sp_ref[b, sl, ls] for (b, ls), pr in zip(chains, prods)]
    for (b, ls), s in zip(chains, states):
        s_ref[b, :, ls] = s


def _rwkv_state(g, sp, rc, yp, bsz, s, nck):
    rows = nck * RWKV_CHUNK
    rblk = pl.BlockSpec((bsz, rows, RWKV_W), lambda i: (0, i, 0))
    r3 = lambda a: a.reshape(bsz, s, RWKV_W)
    y = pl.pallas_call(
        _rwkv_state_body,
        grid=(s // rows,),
        in_specs=[rblk] * 4,
        out_specs=rblk,
        out_shape=jax.ShapeDtypeStruct((bsz, s, RWKV_W), f32),
        scratch_shapes=[pltpu.VMEM((bsz, RWKV_CHUNK, RWKV_W), f32)],
        compiler_params=_cparams(("arbitrary",)),
        name="rwkv_state",
    )(r3(g), r3(sp), r3(rc), r3(yp))
    return y.reshape(bsz * s, RWKV_W)


def _outproj_body(x_ref, yl_ref, ys_ref, bon_ref, v_ref, g_ref, lg_ref, lb_ref, ones_ref, w_ref, o_ref):
    y = ys_ref[...]
    ones = ones_ref[...]
    inv_n = 1.0 / RWKV_HEAD
    mean = _sum_dot(y, ones, 2) * inv_n
    yc = y - mean
    var = _sum_dot(yc * yc, ones, 2) * inv_n
    yn = yc * lax.rsqrt(var + GN_EPS) * lg_ref[...] + lb_ref[...]
    yr = (yn + bon_ref[...] * v_ref[...]) * g_ref[...]
    cat = jnp.concatenate([yl_ref[...], yr.astype(bf16)], axis=1)
    o_ref[...] = x_ref[...] + jnp.dot(cat, w_ref[...], preferred_element_type=f32)


def _outproj(x2d, y_lru, y_scan, bon, v, g, lnx_g, lnx_b, ones_blk, w_out_bf, tm):
    t = x2d.shape[0]
    xb = pl.BlockSpec((tm, D_MODEL), lambda i: (i, 0))
    hb = pl.BlockSpec((tm, RWKV_W), lambda i: (i, 0))
    return pl.pallas_call(
        _outproj_body,
        grid=(t // tm,),
        in_specs=[xb, hb, hb, hb, hb, hb, _full(lnx_g.shape), _full(lnx_b.shape), _full(ones_blk.shape),
                  _full(w_out_bf.shape)],
        out_specs=xb,
        out_shape=jax.ShapeDtypeStruct((t, D_MODEL), f32),
        compiler_params=_cparams(("parallel",)),
        name="outproj",
    )(x2d, y_lru, y_scan, bon, v, g, lnx_g, lnx_b, ones_blk, w_out_bf)


def _memkv_body(m_ref, g_ref, wk_ref, wv_ref, k_ref, v_ref):
    h = _rms(m_ref[...], g_ref[...]).astype(bf16)
    k_ref[...] = jnp.dot(h, wk_ref[...], preferred_element_type=f32).astype(bf16)
    v_ref[...] = jnp.dot(h, wv_ref[...], preferred_element_type=f32).astype(bf16)


def _memkv(mem2d, g, wk_bf, wv_bf, tm):
    t = mem2d.shape[0]
    blk = pl.BlockSpec((tm, D_MODEL), lambda i: (i, 0))
    sh = jax.ShapeDtypeStruct((t, D_MODEL), bf16)
    return pl.pallas_call(
        _memkv_body,
        grid=(t // tm,),
        in_specs=[blk, _full(g.shape), _full(wk_bf.shape), _full(wv_bf.shape)],
        out_specs=[blk, blk],
        out_shape=[sh, sh],
        compiler_params=_cparams(("parallel",)),
        name="memkv",
    )(mem2d, g, wk_bf, wv_bf)


def _xattn_body(x_ref, k_ref, v_ref, gx_ref, wq_ref, wo_ref, gf_ref, wr_ref, br_ref, upper_ref,
                x2_ref, hf_ref, idx_ref, gate_ref, cnt_ref, base_ref):
    x = x_ref[...]
    h = _rms(x, gx_ref[...]).astype(bf16)
    q = jnp.dot(h, wq_ref[...], preferred_element_type=f32).astype(bf16)
    k = k_ref[...]
    v = v_ref[...]
    outs = []
    for hd in range(XA_HEADS):
        sl = slice(hd * XA_HEAD, (hd + 1) * XA_HEAD)
        sc = lax.dot_general(q[:, sl], k[:, sl], (((1,), (1,)), ((), ())),
                             preferred_element_type=f32) * (XA_HEAD ** -0.5)
        sc = sc - jnp.max(sc, axis=-1, keepdims=True)
        e = jnp.exp(sc)
        p = e / jnp.sum(e, axis=-1, keepdims=True)
        outs.append(jnp.dot(p.astype(bf16), v[:, sl], preferred_element_type=f32).astype(bf16))
    o = jnp.concatenate(outs, axis=1)
    x2 = x + jnp.dot(o, wo_ref[...], preferred_element_type=f32)
    x2_ref[...] = x2

    hf = _rms(x2, gf_ref[...])
    _store_row_tiles(hf_ref, hf)
    logits = lax.dot_general(wr_ref[...], hf.astype(bf16), (((1,), (1,)), ((), ())),
                             preferred_element_type=f32) + br_ref[...]
    erow = lax.broadcasted_iota(i32, logits.shape, 0)
    neg = jnp.float32(-jnp.inf)
    cur = logits
    vals = []
    idxs = []
    for _ in range(TOP_K):
        m = jnp.max(cur, axis=0, keepdims=True)
        am = jnp.min(jnp.where(cur == m, erow, N_EXPERTS), axis=0, keepdims=True)
        vals.append(m)
        idxs.append(am)
        cur = jnp.where(erow == am, neg, cur)
    es = [jnp.exp(vk - vals[0]) for vk in vals]
    den = es[0] + es[1] + es[2] + es[3]

    @pl.when(pl.program_id(0) == 0)
    def _():
        base_ref[...] = jnp.zeros_like(base_ref)

    onehot = [jnp.where(erow == am, 1.0, 0.0) for am in idxs]
    cnt = (onehot[0] + onehot[1]) + (onehot[2] + onehot[3])
    base = base_ref[:, 0:1]
    prior = jnp.dot(cnt.astype(bf16), upper_ref[...], preferred_element_type=f32) + base
    base_ref[...] = jnp.broadcast_to(base + jnp.sum(cnt, axis=1, keepdims=True), base_ref.shape)
    cnt_ref[...] = base_ref[...]

    orow = lax.broadcasted_iota(i32, idx_ref.shape, 0)
    idx_out = jnp.zeros(idx_ref.shape, i32)
    gate_out = jnp.zeros(gate_ref.shape, f32)
    for kq in range(TOP_K):
        rank = jnp.sum(prior * onehot[kq], axis=0, keepdims=True).astype(i32)
        idx_out = jnp.where(orow == kq, idxs[kq], idx_out)
        idx_out = jnp.where(orow == TOP_K + kq, rank, idx_out)
        gate_out = jnp.where(orow == kq, es[kq] / den, gate_out)
    idx_ref[...] = idx_out
    gate_ref[...] = gate_out


def _xattn(x1, kmem, vmem, g_xa, wq_bf, wo_bf, g_ffn, wr_t, br_col, bsz, s, mlen, tm):
    t = bsz * s
    nt = s // tm
    xb = pl.BlockSpec((tm, D_MODEL), lambda i: (i, 0))
    mb = pl.BlockSpec((mlen, D_MODEL), lambda i: (i // nt, 0))
    lb = pl.BlockSpec((2 * TOP_K, tm), lambda i: (0, i))
    upper = jnp.triu(jnp.ones((tm, tm), bf16), k=1)
    return pl.pallas_call(
        _xattn_body,
        grid=(t // tm,),
        in_specs=[xb, mb, mb, _full(g_xa.shape), _full(wq_bf.shape), _full(wo_bf.shape),
                  _full(g_ffn.shape), _full(wr_t.shape), _full(br_col.shape), _full(upper.shape)],
        out_specs=[xb, pl.BlockSpec((tm * ROW_TILE, LANES), lambda i: (i, 0)), lb, lb, _full((N_EXPERTS, LANES))],
        out_shape=[jax.ShapeDtypeStruct((t, D_MODEL), f32), jax.ShapeDtypeStruct((t * ROW_TILE, LANES), f32),
                   jax.ShapeDtypeStruct((2 * TOP_K, t), i32), jax.ShapeDtypeStruct((2 * TOP_K, t), f32),
                   jax.ShapeDtypeStruct((N_EXPERTS, LANES), f32)],
        scratch_shapes=[pltpu.VMEM((N_EXPERTS, LANES), f32)],
        compiler_params=_cparams(("arbitrary",)),
        name="xattn",
    )(x1, kmem, vmem, g_xa, wq_bf, wo_bf, g_ffn, wr_t, br_col, upper)


DMA_UNROLL = 8


def _tile_at(ref, row):
    return ref.at[pl.ds(pl.multiple_of(row * ROW_TILE, ROW_TILE), ROW_TILE), :]


def _dispatch_body(dest_ref, pend_ref, hf_ref, wgu_ref, wdn_ref, xs_hbm, wgu_o, wdn_o, zeros_ref, sem, zsem):
    i = pl.program_id(0)
    tm = hf_ref.shape[0] // ROW_TILE
    zrows = MOE_BLOCK * ROW_TILE

    def zero_copy(e):
        start = pl.multiple_of((pend_ref[e] - MOE_BLOCK) * ROW_TILE, ROW_TILE)
        return pltpu.make_async_copy(zeros_ref, xs_hbm.at[pl.ds(start, zrows), :], zsem)

    def nonempty(e):
        return pend_ref[e] > (pend_ref[e - 1] if e else 0)

    @pl.when(i == 0)
    def _():
        zeros_ref[...] = jnp.zeros_like(zeros_ref)
        for e in range(N_EXPERTS):
            @pl.when(nonempty(e))
            def _():
                zero_copy(e).start()
        for e in range(N_EXPERTS):
            @pl.when(nonempty(e))
            def _():
                zero_copy(e).wait()

        def tail_copy(b):
            return pltpu.make_async_copy(zeros_ref, xs_hbm.at[pl.ds(pl.multiple_of(b * zrows, zrows), zrows), :], zsem)

        def tail_start(b, carry):
            tail_copy(b).start()
            return carry

        def tail_wait(b, carry):
            tail_copy(b).wait()
            return carry
        n_used = pend_ref[N_EXPERTS - 1] // MOE_BLOCK
        n_all = xs_hbm.shape[0] // zrows
        lax.fori_loop(n_used, n_all, tail_start, 0)
        lax.fori_loop(n_used, n_all, tail_wait, 0)

    def body(q, carry):
        for u in range(DMA_UNROLL):
            r = q * DMA_UNROLL + u
            src = _tile_at(hf_ref, r)
            for kq in range(TOP_K):
                pltpu.make_async_copy(src, _tile_at(xs_hbm, dest_ref[(i * tm + r) * TOP_K + kq]),
                                      sem).start(priority=kq % 2)
        return carry
    lax.fori_loop(0, tm // DMA_UNROLL, body, 0)
    wgu_o[...] = wgu_ref[...].astype(bf16)
    wdn_o[...] = wdn_ref[...].astype(bf16)
    for kq in range(TOP_K):
        pltpu.make_async_copy(hf_ref, xs_hbm.at[pl.ds(0, tm * ROW_TILE), :], sem).wait()


def _dispatch(dest_flat, pends, hf_tiles, w_gu, w_dn, n_blocks, tm):
    t = hf_tiles.shape[0] // ROW_TILE
    rows = n_blocks * MOE_BLOCK
    steps = t // tm
    wrows = N_EXPERTS * D_MODEL // steps
    wgu2 = w_gu.reshape(N_EXPERTS * D_MODEL, 2 * D_FF)
    wdn2 = w_dn.reshape(N_EXPERTS * D_FF, D_MODEL)
    gu_blk = pl.BlockSpec((wrows, 2 * D_FF), lambda i, d, pe: (i, 0))
    dn_blk = pl.BlockSpec((wrows, D_MODEL), lambda i, d, pe: (i, 0))
    grid_spec = pltpu.PrefetchScalarGridSpec(
        num_scalar_prefetch=2,
        grid=(steps,),
        in_specs=[pl.BlockSpec((tm * ROW_TILE, LANES), lambda i, d, pe: (i, 0)), gu_blk, dn_blk],
        out_specs=[pl.BlockSpec(memory_space=pl.ANY), gu_blk, dn_blk],
        scratch_shapes=[pltpu.VMEM((MOE_BLOCK * ROW_TILE, LANES), f32), pltpu.SemaphoreType.DMA(()),
                        pltpu.SemaphoreType.DMA(())],
    )
    xs, wgu_bf, wdn_bf = pl.pallas_call(
        _dispatch_body,
        grid_spec=grid_spec,
        out_shape=[jax.ShapeDtypeStruct((rows * ROW_TILE, LANES), f32), jax.ShapeDtypeStruct(wgu2.shape, bf16),
                   jax.ShapeDtypeStruct(wdn2.shape, bf16)],
        compiler_params=_cparams(("arbitrary",)),
        name="dispatch",
    )(dest_flat, pends, hf_tiles, wgu2, wdn2)
    return xs, wgu_bf.reshape(w_gu.shape), wdn_bf.reshape(w_dn.shape)


def _moe_body(pend_ref, xs_hbm, wgu_ref, bgu_ref, wdn_ref, bdn_ref, ys_hbm, xbuf, ybuf, sem_in, sem_out):
    e = pl.program_id(0)
    blk = MOE_BLOCK * ROW_TILE
    end_blk = pend_ref[e] // MOE_BLOCK
    start_blk = jnp.where(e == 0, 0, pend_ref[jnp.maximum(e - 1, 0)] // MOE_BLOCK)
    nb = end_blk - start_blk

    def rows_of(j):
        return pl.ds(pl.multiple_of((start_blk + j) * blk, blk), blk)

    def x_copy(j, slot):
        return pltpu.make_async_copy(xs_hbm.at[rows_of(j), :], xbuf.at[slot], sem_in.at[slot])

    def y_copy(j, slot):
        return pltpu.make_async_copy(ybuf.at[slot], ys_hbm.at[rows_of(j), :], sem_out.at[slot])

    @pl.when(nb > 0)
    def _():
        x_copy(0, 0).start(priority=1)

    def block(j, carry):
        slot = j % 2

        @pl.when(j + 1 < nb)
        def _():
            x_copy(j + 1, 1 - slot).start(priority=1)

        x_copy(j, slot).wait()

        @pl.when(j >= 2)
        def _():
            y_copy(j - 2, slot).wait()

        xb = _load_row_tiles(xbuf.at[slot], MOE_BLOCK).astype(bf16)
        gu = jnp.dot(xb, wgu_ref[0], preferred_element_type=f32) + bgu_ref[0]
        gate = jnp.minimum(gu[:, :D_FF], SWIGLU_LIMIT)
        up = jnp.clip(gu[:, D_FF:], -SWIGLU_LIMIT, SWIGLU_LIMIT)
        act = (up + 1.0) * (gate * jax.nn.sigmoid(SWIGLU_ALPHA * gate))
        y = jnp.dot(act.astype(bf16), wdn_ref[0], preferred_element_type=f32) + bdn_ref[0]
        _store_row_tiles(ybuf.at[slot], y)
        y_copy(j, slot).start(priority=1)
        return carry

    lax.fori_loop(0, nb, block, 0)

    @pl.when(nb >= 2)
    def _():
        y_copy(nb - 2, nb % 2).wait()

    @pl.when(nb >= 1)
    def _():
        y_copy(nb - 1, (nb - 1) % 2).wait()

    @pl.when(e == N_EXPERTS - 1)
    def _():
        n_all = ys_hbm.shape[0] // blk
        ybuf[0] = jnp.zeros(ybuf.shape[1:], ybuf.dtype)

        def tail_copy(b):
            return pltpu.make_async_copy(ybuf.at[0], ys_hbm.at[pl.ds(pl.multiple_of(b * blk, blk), blk), :],
                                         sem_out.at[0])

        def tail_start(b, carry):
            tail_copy(b).start()
            return carry

        def tail_wait(b, carry):
            tail_copy(b).wait()
            return carry
        lax.fori_loop(end_blk, n_all, tail_start, 0)
        lax.fori_loop(end_blk, n_all, tail_wait, 0)


def _moe(pends, xs_tiles, w_gu, b_gu, w_dn, b_dn, n_blocks):
    rows = n_blocks * MOE_BLOCK
    grid_spec = pltpu.PrefetchScalarGridSpec(
        num_scalar_prefetch=1,
        grid=(N_EXPERTS,),
        in_specs=[
            pl.BlockSpec(memory_space=pl.ANY),
            pl.BlockSpec((1, D_MODEL, 2 * D_FF), lambda e, pe: (e, 0, 0)),
            pl.BlockSpec((1, 1, 2 * D_FF), lambda e, pe: (e, 0, 0)),
            pl.BlockSpec((1, D_FF, D_MODEL), lambda e, pe: (e, 0, 0)),
            pl.BlockSpec((1, 1, D_MODEL), lambda e, pe: (e, 0, 0)),
        ],
        out_specs=pl.BlockSpec(memory_space=pl.ANY),
        scratch_shapes=[pltpu.VMEM((2, MOE_BLOCK * ROW_TILE, LANES), f32),
                        pltpu.VMEM((2, MOE_BLOCK * ROW_TILE, LANES), f32),
                        pltpu.SemaphoreType.DMA((2,)), pltpu.SemaphoreType.DMA((2,))],
    )
    return pl.pallas_call(
        _moe_body,
        grid_spec=grid_spec,
        out_shape=jax.ShapeDtypeStruct((rows * ROW_TILE, LANES), f32),
        compiler_params=_cparams(("arbitrary",)),
        name="moe",
    )(pends, xs_tiles, w_gu, b_gu.reshape(N_EXPERTS, 1, 2 * D_FF), w_dn, b_dn.reshape(N_EXPERTS, 1, D_MODEL))


def _combine_body(pos_ref, ys_hbm, x_ref, gate_ref, g_ref, o_ref, buf, sem):
    i = pl.program_id(0)
    n = pl.num_programs(0)
    tc = x_ref.shape[0]
    slot = i % 2
    slot_rows = TOP_K * tc * ROW_TILE

    def start(step, sl):
        def body(q, carry):
            for u in range(DMA_UNROLL):
                r = q * DMA_UNROLL + u
                for kq in range(TOP_K):
                    dst = _tile_at(buf, (sl * TOP_K + kq) * tc + r)
                    pltpu.make_async_copy(_tile_at(ys_hbm, pos_ref[(step * tc + r) * TOP_K + kq]), dst,
                                          sem.at[sl]).start(priority=kq % 2)
            return carry
        lax.fori_loop(0, tc // DMA_UNROLL, body, 0)

    @pl.when(i == 0)
    def _():
        start(0, 0)

    @pl.when(i + 1 < n)
    def _():
        start(i + 1, 1 - slot)

    off = pl.multiple_of(slot * slot_rows, slot_rows)
    pltpu.make_async_copy(ys_hbm.at[pl.ds(0, slot_rows), :], buf.at[pl.ds(off, slot_rows), :], sem.at[slot]).wait()
    gates = gate_ref[...]
    acc = x_ref[...]
    for kq in range(TOP_K):
        rows = _load_row_tiles(buf, tc, off + kq * tc * ROW_TILE)
        acc = acc + gates[:, kq:kq + 1] * rows
    o_ref[...] = _rms(acc, g_ref[...])


def _combine(pos_flat, ys_tiles, x2, gate_pad, g_final, tc):
    t = x2.shape[0]
    grid_spec = pltpu.PrefetchScalarGridSpec(
        num_scalar_prefetch=1,
        grid=(t // tc,),
        in_specs=[pl.BlockSpec(memory_space=pl.ANY),
                  pl.BlockSpec((tc, D_MODEL), lambda i, p: (i, 0)),
                  pl.BlockSpec((tc, LANES), lambda i, p: (i, 0)),
                  pl.BlockSpec((1, D_MODEL), lambda i, p: (0, 0))],
        out_specs=pl.BlockSpec((tc, D_MODEL), lambda i, p: (i, 0)),
        scratch_shapes=[pltpu.VMEM((2 * TOP_K * tc * ROW_TILE, LANES), f32), pltpu.SemaphoreType.DMA((2,))],
    )
    return pl.pallas_call(
        _combine_body,
        grid_spec=grid_spec,
        out_shape=jax.ShapeDtypeStruct((t, D_MODEL), f32),
        compiler_params=_cparams(("arbitrary",)),
        name="combine",
    )(pos_flat, ys_tiles, x2, gate_pad, g_final)


def _routing(top_idx, rank, counts, t):
    n_assign = t * TOP_K
    experts = jnp.arange(N_EXPERTS, dtype=i32)
    padded = (counts + MOE_BLOCK - 1) // MOE_BLOCK * MOE_BLOCK
    pends = jnp.cumsum(padded).astype(i32)
    pstarts = pends - padded
    start_of = jnp.sum(jnp.where(top_idx[:, :, None] == experts, pstarts, 0), axis=-1)
    dest = (start_of + rank).astype(i32).reshape(n_assign)
    n_blocks = (n_assign + N_EXPERTS * (MOE_BLOCK - 1) + MOE_BLOCK - 1) // MOE_BLOCK
    return dest, pends, n_blocks


def _block_diag(w):
    n, bi, bj = w.shape
    eye = jnp.eye(n, dtype=w.dtype)
    return jnp.einsum('nij,nm->nimj', w, eye).reshape(n * bi, n * bj)


def _layer(x2d, mem2d, bsz, s, mlen, p):
    t = bsz * s
    row = lambda a: a.reshape(1, -1)
    ones_blk = _block_diag(jnp.ones((RWKV_W // RWKV_HEAD, RWKV_HEAD, RWKV_HEAD), bf16))

    xl, gl, ur = _inproj(x2d, row(p['norm_mix_g']), p['w_in'].astype(bf16), tm=min(512, t))

    wg = jnp.concatenate([_block_diag(p['lru_wx']), _block_diag(p['lru_wa'])], axis=1).astype(bf16)
    bg = jnp.concatenate([p['lru_bx'], p['lru_ba']]).reshape(1, -1)
    y_lru = _lru(xl, gl, p['conv_w'], row(p['conv_b']), wg, bg, row(p['lru_lambda']), bsz, s, ts=min(256, s))

    zl = jnp.zeros((DECAY_LORA, RWKV_W), f32)
    wl = jnp.concatenate([jnp.concatenate([p['rwkv_w_up'], zl], axis=1),
                          jnp.concatenate([zl, p['rwkv_a_up']], axis=1)], axis=0).astype(bf16)
    at, bt, kt, rt, v, cl, bon, g = _rwkv_prep(
        ur, row(p['rwkv_mu']), row(p['rwkv_w0']), row(p['rwkv_a0']), wl, p['rwkv_g_up'].astype(bf16),
        row(p['rwkv_k_k']), row(p['rwkv_k_a']), row(p['rwkv_r_k']), ones_blk, bsz, s, ts=min(256, s))
    gm, sp, rc, yp = _rwkv_chunk(at, bt, kt, rt, v, cl, ts=min(1024, s))
    y_scan = _rwkv_state(gm, sp, rc, yp, bsz, s, nck=min(4, s // RWKV_CHUNK))

    x1 = _outproj(x2d, y_lru, y_scan, bon, v, g, row(p['rwkv_lnx_g']), row(p['rwkv_lnx_b']), ones_blk,
                  p['w_out'].astype(bf16), tm=min(512, t))

    kmem, vmem = _memkv(mem2d, row(p['norm_mem_g']), p['xa_wk'].astype(bf16), p['xa_wv'].astype(bf16),
                        tm=min(512, bsz * mlen))
    x2, hf, route, gates, cnt_pad = _xattn(x1, kmem, vmem, row(p['norm_xa_g']), p['xa_wq'].astype(bf16),
                                           p['xa_wo'].astype(bf16), row(p['norm_ffn_g']),
                                           p['w_router'].T.astype(bf16), p['b_router'].reshape(-1, 1),
                                           bsz, s, mlen, tm=min(512, s))

    counts = cnt_pad[:, 0].astype(i32)
    dest, pends, n_blocks = _routing(route[:TOP_K].T, route[TOP_K:].T, counts, t)
    gate_pad = jnp.pad(gates[:TOP_K].T, ((0, 0), (0, LANES - TOP_K)))
    xs, wgu_bf, wdn_bf = _dispatch(dest, pends, hf, p['w_gu'], p['w_dn'], n_blocks, tm=min(256, t))
    ys = _moe(pends, xs, wgu_bf, p['b_gu'], wdn_bf, p['b_dn'], n_blocks)
    return _combine(dest, ys, x2, gate_pad, row(p['final_norm_g']), tc=min(256, t))


def kernel(x, mem, norm_mix_g, w_in, conv_w, conv_b, lru_wx, lru_bx, lru_wa, lru_ba, lru_lambda, rwkv_mu, rwkv_w0, rwkv_w_up, rwkv_a0, rwkv_a_up, rwkv_g_up, rwkv_k_k, rwkv_k_a, rwkv_r_k, rwkv_lnx_g, rwkv_lnx_b, w_out, norm_xa_g, norm_mem_g, xa_wq, xa_wk, xa_wv, xa_wo, norm_ffn_g, w_router, b_router, w_gu, b_gu, w_dn, b_dn, final_norm_g):
    bsz, s, d = x.shape
    mlen = mem.shape[1]
    assert d == D_MODEL and w_in.shape[0] == 1
    p = dict(norm_mix_g=norm_mix_g[0], w_in=w_in[0], conv_w=conv_w[0], conv_b=conv_b[0], lru_wx=lru_wx[0],
             lru_bx=lru_bx[0], lru_wa=lru_wa[0], lru_ba=lru_ba[0], lru_lambda=lru_lambda[0],
             rwkv_mu=rwkv_mu[0], rwkv_w0=rwkv_w0[0], rwkv_w_up=rwkv_w_up[0], rwkv_a0=rwkv_a0[0],
             rwkv_a_up=rwkv_a_up[0], rwkv_g_up=rwkv_g_up[0], rwkv_k_k=rwkv_k_k[0], rwkv_k_a=rwkv_k_a[0],
             rwkv_r_k=rwkv_r_k[0].reshape(-1), rwkv_lnx_g=rwkv_lnx_g[0], rwkv_lnx_b=rwkv_lnx_b[0],
             w_out=w_out[0], norm_xa_g=norm_xa_g[0], norm_mem_g=norm_mem_g[0], xa_wq=xa_wq[0],
             xa_wk=xa_wk[0], xa_wv=xa_wv[0], xa_wo=xa_wo[0], norm_ffn_g=norm_ffn_g[0],
             w_router=w_router[0], b_router=b_router[0], w_gu=w_gu[0], b_gu=b_gu[0], w_dn=w_dn[0],
             b_dn=b_dn[0], final_norm_g=final_norm_g)
    out = _layer(x.reshape(bsz * s, d), mem.reshape(bsz * mlen, d), bsz, s, mlen, p)
    return out.reshape(bsz, s, d)
```

```python
import functools

import jax
import jax.numpy as jnp
from jax import lax
from jax.experimental import pallas as pl
from jax.experimental.pallas import tpu as pltpu

f32 = jnp.float32
bf16 = jnp.bfloat16
i32 = jnp.int32

D_MODEL = 1024
LRU_W = 512
RWKV_W = 512
LRU_BLOCKS = 8
LRU_BLOCK = 64
CONV_W = 4
LRU_C = 8.0
RWKV_HEAD = 64
DECAY_LORA = 64
AAA_LORA = 64
GATE_LORA = 128
RWKV_IN = 3 * RWKV_W + DECAY_LORA + AAA_LORA + GATE_LORA
XA_HEADS = 4
XA_HEAD = D_MODEL // XA_HEADS
N_EXPERTS = 32
TOP_K = 4
D_FF = D_MODEL
SWIGLU_LIMIT = 7.0
SWIGLU_ALPHA = 1.702
EPS = 1e-6
GN_EPS = 64e-5

LANES = 128
SUBLANES = 8
RWKV_CHUNK = 64
PAIR = 2 * RWKV_HEAD
N_PAIRS = RWKV_W // PAIR
MOE_BLOCK = 256
ROW_TILE = D_MODEL // LANES
VMEM_LIMIT = 52 * 1024 * 1024


def _cparams(sem):
    return pltpu.CompilerParams(dimension_semantics=sem, vmem_limit_bytes=VMEM_LIMIT)


def _rms(x, g):
    return x * lax.rsqrt(jnp.mean(x * x, axis=-1, keepdims=True) + EPS) * g


def _full(shape):
    n = len(shape)
    return pl.BlockSpec(shape, lambda *a: (0,) * n)


def _shift_rows(x, prev8, d):
    xr = pltpu.roll(x, d, 0)
    tr = pltpu.roll(prev8, d, 0)
    row = lax.broadcasted_iota(i32, prev8.shape, 0)
    head = jnp.where(row < d, tr, xr[:SUBLANES])
    return jnp.concatenate([head, xr[SUBLANES:]], axis=0)


def _bf16_parts(x, n):
    parts = []
    for _ in range(n):
        piece = x.astype(bf16)
        parts.append(piece)
        x = x - piece.astype(f32)
    return parts


def _sum_dot(x, mask_bf, n_parts, mask_left=False):
    acc = None
    for piece in _bf16_parts(x, n_parts):
        d = (jnp.dot(mask_bf, piece, preferred_element_type=f32) if mask_left
             else jnp.dot(piece, mask_bf, preferred_element_type=f32))
        acc = d if acc is None else acc + d
    return acc


def _store_row_tiles(ref, val, off=0):
    n = val.shape[0]
    for j in range(ROW_TILE):
        ref[pl.ds(off + j, n, stride=ROW_TILE), :] = val[:, j * LANES:(j + 1) * LANES]


def _load_row_tiles(ref, n, off=0):
    return jnp.concatenate([ref[pl.ds(off + j, n, stride=ROW_TILE), :] for j in range(ROW_TILE)], axis=1)


def _inproj_body(x_ref, g_ref, w_ref, xl_ref, gl_ref, ur_ref):
    h = _rms(x_ref[...], g_ref[...])
    u = jnp.dot(h.astype(bf16), w_ref[...], preferred_element_type=f32)
    xl_ref[...] = u[:, :LRU_W]
    gl_ref[...] = u[:, LRU_W:2 * LRU_W]
    ur_ref[...] = u[:, 2 * LRU_W:]


def _inproj(x2d, g, w_in_bf, tm):
    t = x2d.shape[0]
    return pl.pallas_call(
        _inproj_body,
        grid=(t // tm,),
        in_specs=[pl.BlockSpec((tm, D_MODEL), lambda i: (i, 0)), _full((1, D_MODEL)),
                  _full(w_in_bf.shape)],
        out_specs=[pl.BlockSpec((tm, LRU_W), lambda i: (i, 0)),
                   pl.BlockSpec((tm, LRU_W), lambda i: (i, 0)),
                   pl.BlockSpec((tm, RWKV_IN), lambda i: (i, 0))],
        out_shape=[jax.ShapeDtypeStruct((t, LRU_W), f32), jax.ShapeDtypeStruct((t, LRU_W), f32),
                   jax.ShapeDtypeStruct((t, RWKV_IN), f32)],
        compiler_params=_cparams(("parallel",)),
        name="inproj",
    )(x2d, g, w_in_bf)


def _lru_body(xl_ref, gl_ref, cw_ref, cb_ref, wg_ref, bg_ref, lam_ref, o_ref, tail_ref, h_ref):
    ts = xl_ref.shape[0]

    @pl.when(pl.program_id(1) == 0)
    def _():
        tail_ref[...] = jnp.zeros_like(tail_ref)
        h_ref[...] = jnp.zeros_like(h_ref)

    x = xl_ref[...]
    tail = tail_ref[...]
    cw = cw_ref[...]
    xc = cb_ref[...] + cw[CONV_W - 1:CONV_W] * x
    for d in range(1, CONV_W):
        xc = xc + cw[CONV_W - 1 - d:CONV_W - d] * _shift_rows(x, tail, d)
    tail_ref[...] = x[ts - SUBLANES:]

    gates = jax.nn.sigmoid(jnp.dot(xc.astype(bf16), wg_ref[...], preferred_element_type=f32) + bg_ref[...])
    gx = gates[:, :LRU_W]
    ga = gates[:, LRU_W:]
    log_a = -LRU_C * ga * jax.nn.softplus(-lam_ref[...])
    a = jnp.exp(log_a)
    b = jnp.sqrt(-jnp.tanh(log_a) * (a * a + 1.0)) * gx * xc

    row = lax.broadcasted_iota(i32, (ts, LRU_W), 0)
    d = 1
    while d < ts:
        keep = row >= d
        a_s = jnp.where(keep, pltpu.roll(a, d, 0), 1.0)
        b_s = jnp.where(keep, pltpu.roll(b, d, 0), 0.0)
        b = a * b_s + b
        a = a * a_s
        d *= 2
    h = b + a * h_ref[SUBLANES - 1:SUBLANES, :]
    h_ref[...] = h[ts - SUBLANES:]
    o_ref[...] = (h * jax.nn.gelu(gl_ref[...])).astype(o_ref.dtype)


def _lru(xl, gl, cw, cb, wg_bf, bg, lam, bsz, s, ts):
    nt = s // ts
    blk = pl.BlockSpec((ts, LRU_W), lambda b, i: (b * nt + i, 0))
    return pl.pallas_call(
        _lru_body,
        grid=(bsz, nt),
        in_specs=[blk, blk, _full(cw.shape), _full(cb.shape), _full(wg_bf.shape), _full(bg.shape),
                  _full(lam.shape)],
        out_specs=blk,
        out_shape=jax.ShapeDtypeStruct((bsz * s, LRU_W), bf16),
        scratch_shapes=[pltpu.VMEM((SUBLANES, LRU_W), f32), pltpu.VMEM((SUBLANES, LRU_W), f32)],
        compiler_params=_cparams(("parallel", "arbitrary")),
        name="lru",
    )(xl, gl, cw, cb, wg_bf, bg, lam)


def _mm_x3(a, b_parts):
    ah, al = _bf16_parts(a, 2)
    bh, bl = b_parts
    d = lambda x, y: jnp.dot(x, y, preferred_element_type=f32)
    return d(ah, bh) + (d(ah, bl) + d(al, bh))


def _mm(a, b):
    return jnp.dot(a.astype(bf16), b.astype(bf16), preferred_element_type=f32)


def _mm_nt(a, b):
    return lax.dot_general(a.astype(bf16), b.astype(bf16), (((1,), (1,)), ((), ())), preferred_element_type=f32)


def _mm_tn(a, b):
    return lax.dot_general(a.astype(bf16), b.astype(bf16), (((0,), (0,)), ((), ())), preferred_element_type=f32)


def _bd(x):
    m0 = lax.broadcasted_iota(i32, x.shape, 1) < RWKV_HEAD
    zero = jnp.zeros_like(x)
    return jnp.concatenate([jnp.where(m0, x, zero), jnp.where(m0, zero, x)], axis=0)


def _side_by_side(d):
    h = d.shape[0] // 2
    m0 = lax.broadcasted_iota(i32, (h, d.shape[1]), 1) < RWKV_HEAD
    return jnp.where(m0, d[:h], d[h:])


def _chunk_maps(chunks):
    c = RWKV_CHUNK
    ri = lax.broadcasted_iota(i32, (c, PAIR), 0)
    ji = lax.broadcasted_iota(i32, (c, PAIR), 1) % RWKV_HEAD
    strict = ji < ri
    incl = ji <= ri
    diag = ji == ri
    eye = jnp.where(diag, 1.0, 0.0).astype(f32)
    each = lambda f, *ls: [f(*xs) for xs in zip(*ls)]
    cat0 = lambda *xs: jnp.concatenate(xs, axis=0)
    cat1 = lambda *xs: jnp.concatenate(xs, axis=1)
    tb = lambda x: x.astype(bf16)

    ats, bts, kts, rts, vs, cls = [list(x) for x in zip(*chunks)]
    pcs = each(lambda cl: jnp.exp(cl[c - 1:c, :]), cls)
    bd_a = each(lambda x: tb(_bd(x)), ats)
    bd_v = each(lambda x: tb(_bd(x)), vs)

    aa = each(lambda a, r, b, k: _mm_nt(cat0(tb(a), tb(r)), cat0(tb(_bd(b)), tb(_bd(k)))), ats, rts, bts, kts)
    l_ab = each(lambda x: tb(jnp.where(strict, x[:c, :PAIR], 0.0)), aa)
    a_k = each(lambda x: tb(cat0(jnp.where(strict, x[:c, PAIR:], 0.0), jnp.where(incl, x[c:, PAIR:], 0.0))), aa)
    a_rb = each(lambda x: tb(jnp.where(incl, x[c:, :PAIR], 0.0)), aa)

    tinv = each(lambda x: eye + x, l_ab)
    lp = each(lambda x: tb(_mm(x, _bd(x))), l_ab)
    p = 2
    while 2 * p < c:
        x2 = each(lambda t, x: _mm(cat0(tb(t), x), _bd(x)), tinv, lp)
        tinv = each(lambda t, x: t + x[:c], tinv, x2)
        lp = each(lambda x: tb(x[c:]), x2)
        p *= 2
    tinv = each(lambda t, x: t + _mm(t, _bd(x)), tinv, lp)

    wy = each(_mm, a_k, bd_v)
    za = each(lambda t, w, a: _mm(t, cat1(tb(_bd(w[:c])), a)), tinv, wy, bd_a)
    zp = each(lambda x: x[:, :PAIR], za)
    ac = each(lambda x: x[:, PAIR:], za)
    y1 = each(lambda ar, z, a: _mm(ar, cat1(tb(_bd(z)), tb(_bd(a)))), a_rb, zp, ac)
    yp = each(lambda w, y: w[c:] + y[:, :PAIR], wy, y1)
    rc = each(lambda r, y: r + y[:, PAIR:], rts, y1)
    sm = each(lambda b, pc, z, a: _mm_tn(b * pc, cat1(z, a)), bts, pcs, zp, ac)
    kv = each(lambda k, pc, v: _mm_tn(k * pc, v), kts, pcs, vs)
    sp = each(lambda x, m: _side_by_side(x) + _side_by_side(m[:, :PAIR]), kv, sm)
    g = each(lambda pc, m: jnp.where(diag, jnp.broadcast_to(pc, (c, PAIR)), 0.0) + _side_by_side(m[:, PAIR:]), pcs, sm)
    return list(zip(g, sp, rc, yp))


N_UCOLS = 5


def _rwkv_chunk_body(ur_r, ur_k, ur_v, ur_lo, ur_dg, mu_ref, par_ref, wproj_ref, gup_ref, ones_ref, tri_ref,
                     g_ref, sp_ref, rc_ref, yp_ref, bon_ref, v_ref, gate_ref, prev_ref):
    ts = ur_r.shape[0]
    c = RWKV_CHUNK

    @pl.when(pl.program_id(2) == 0)
    def _():
        prev_ref[...] = jnp.zeros_like(prev_ref)

    mixed = []
    for j, ref in enumerate((ur_r, ur_k, ur_v, ur_lo, ur_dg)):
        u0 = ref[...]
        ls = slice(j * LANES, (j + 1) * LANES)
        us = _shift_rows(u0, prev_ref[:, ls], 1)
        prev_ref[:, ls] = u0[ts - SUBLANES:]
        mixed.append(u0 + (us - u0) * mu_ref[0, j:j + 1, :])
    r, k, v, lora, dg = mixed
    w0, a0, k_k, k_a, r_k = [par_ref[0, j:j + 1, :] for j in range(5)]

    lane = lax.broadcasted_iota(i32, lora.shape, 1)
    lora = jnp.where(lane < DECAY_LORA, jnp.tanh(lora), lora)
    proj = jnp.dot(lora.astype(bf16), wproj_ref[0], preferred_element_type=f32)
    w = -jax.nn.softplus(-(w0 + proj[:, :PAIR])) - 0.5
    lw = -jnp.exp(w)
    a = jax.nn.sigmoid(a0 + proj[:, PAIR:])
    gate_ref[...] = jnp.dot(jax.nn.sigmoid(dg).astype(bf16), gup_ref[0], preferred_element_type=f32)

    ones = ones_ref[...]
    kk = k * k_k
    ss = _sum_dot(kk * kk, ones, 2)
    kk = kk / jnp.maximum(jnp.sqrt(ss), 1e-12)
    k2 = k * (1.0 + (a - 1.0) * k_a)
    bon_ref[...] = _sum_dot(r * k2 * r_k, ones, 2)
    v_ref[...] = v

    tri = tri_ref[...]
    grp = tri.shape[0]
    cl = jnp.concatenate([_sum_dot(lw[q * grp:(q + 1) * grp], tri, 3, mask_left=True) for q in range(ts // grp)],
                         axis=0)
    e_neg = jnp.exp(-cl)
    at = -kk * jnp.exp(cl - lw)
    bt = kk * a * e_neg
    kt = k2 * e_neg
    rt = r * jnp.exp(cl)

    sls = [slice(j * c, (j + 1) * c) for j in range(ts // c)]
    outs = _chunk_maps([(at[sl], bt[sl], kt[sl], rt[sl], v[sl], cl[sl]) for sl in sls])
    for sl, (g, sp, rc, yp) in zip(sls, outs):
        g_ref[sl, :] = g
        sp_ref[sl, :] = sp
        rc_ref[sl, :] = rc
        yp_ref[sl, :] = yp


def _rwkv_chunk(ur, mu5, par5, wproj, gup, bsz, s, ts):
    t = bsz * s
    nt = s // ts
    col = lambda blk: pl.BlockSpec((ts, PAIR), lambda p, b, i, blk=blk: (b * nt + i, blk(p)))
    nrw = RWKV_W // PAIR
    ucols = [col(lambda p: p), col(lambda p: nrw + p), col(lambda p: 2 * nrw + p), col(lambda p: 3 * nrw),
             col(lambda p: 3 * nrw + 1)]
    per_pair = lambda a: pl.BlockSpec((1,) + a.shape[1:], lambda p, b, i: (p, 0, 0))
    ones_pair = _block_diag(jnp.ones((2, RWKV_HEAD, RWKV_HEAD), bf16))
    tri = _block_diag(jnp.tril(jnp.ones((2, RWKV_CHUNK, RWKV_CHUNK), bf16)))
    oblk = pl.BlockSpec((ts, PAIR), lambda p, b, i: (b * nt + i, p))
    osh = jax.ShapeDtypeStruct((t, RWKV_W), f32)
    return pl.pallas_call(
        _rwkv_chunk_body,
        grid=(N_PAIRS, bsz, nt),
        in_specs=ucols + [per_pair(mu5), per_pair(par5), per_pair(wproj), per_pair(gup), _full(ones_pair.shape),
                          _full(tri.shape)],
        out_specs=[oblk] * 7,
        out_shape=[osh] * 7,
        scratch_shapes=[pltpu.VMEM((SUBLANES, N_UCOLS * LANES), f32)],
        compiler_params=_cparams(("parallel", "parallel", "arbitrary")),
        name="rwkv_chunk",
    )(ur, ur, ur, ur, ur, mu5, par5, wproj, gup, ones_pair, tri)


def _rwkv_state_body(g_ref, sp_ref, rc_ref, yp_ref, y_ref, s_ref):
    @pl.when(pl.program_id(0) == 0)
    def _():
        s_ref[...] = jnp.zeros_like(s_ref)

    c = RWKV_CHUNK
    bsz = g_ref.shape[0]
    nck = g_ref.shape[1] // c
    chains = [(b, slice(p * PAIR, (p + 1) * PAIR)) for b in range(bsz) for p in range(N_PAIRS)]
    states = [s_ref[b, :, ls] for b, ls in chains]
    for j in range(nck):
        sl = slice(j * c, (j + 1) * c)
        prods = [_mm_x3(jnp.concatenate([rc_ref[b, sl, ls], g_ref[b, sl, ls]], axis=0), _bf16_parts(_bd(s), 2))
                 for (b, ls), s in zip(chains, states)]
        for (b, ls), pr in zip(chains, prods):
            y_ref[b, sl, ls] = yp_ref[b, sl, ls] + pr[:c]
        states = [pr[c:] + sp_ref[b, sl, ls] for (b, ls), pr in zip(chains, prods)]
    for (b, ls), s in zip(chains, states):
        s_ref[b, :, ls] = s


def _rwkv_state(g, sp, rc, yp, bsz, s, nck):
    rows = nck * RWKV_CHUNK
    rblk = pl.BlockSpec((bsz, rows, RWKV_W), lambda i: (0, i, 0))
    r3 = lambda a: a.reshape(bsz, s, RWKV_W)
    y = pl.pallas_call(
        _rwkv_state_body,
        grid=(s // rows,),
        in_specs=[rblk] * 4,
        out_specs=rblk,
        out_shape=jax.ShapeDtypeStruct((bsz, s, RWKV_W), f32),
        scratch_shapes=[pltpu.VMEM((bsz, RWKV_CHUNK, RWKV_W), f32)],
        compiler_params=_cparams(("arbitrary",)),
        name="rwkv_state",
    )(r3(g), r3(sp), r3(rc), r3(yp))
    return y.reshape(bsz * s, RWKV_W)


def _outproj_body(x_ref, yl_ref, ys_ref, bon_ref, v_ref, g_ref, lg_ref, lb_ref, ones_ref, w_ref, o_ref):
    y = ys_ref[...]
    ones = ones_ref[...]
    inv_n = 1.0 / RWKV_HEAD
    mean = _sum_dot(y, ones, 2) * inv_n
    yc = y - mean
    var = _sum_dot(yc * yc, ones, 2) * inv_n
    yn = yc * lax.rsqrt(var + GN_EPS) * lg_ref[...] + lb_ref[...]
    yr = (yn + bon_ref[...] * v_ref[...]) * g_ref[...]
    cat = jnp.concatenate([yl_ref[...], yr.astype(bf16)], axis=1)
    o_ref[...] = x_ref[...] + jnp.dot(cat, w_ref[...], preferred_element_type=f32)


def _outproj(x2d, y_lru, y_scan, bon, v, g, lnx_g, lnx_b, ones_blk, w_out_bf, tm):
    t = x2d.shape[0]
    xb = pl.BlockSpec((tm, D_MODEL), lambda i: (i, 0))
    hb = pl.BlockSpec((tm, RWKV_W), lambda i: (i, 0))
    return pl.pallas_call(
        _outproj_body,
        grid=(t // tm,),
        in_specs=[xb, hb, hb, hb, hb, hb, _full(lnx_g.shape), _full(lnx_b.shape), _full(ones_blk.shape),
                  _full(w_out_bf.shape)],
        out_specs=xb,
        out_shape=jax.ShapeDtypeStruct((t, D_MODEL), f32),
        compiler_params=_cparams(("parallel",)),
        name="outproj",
    )(x2d, y_lru, y_scan, bon, v, g, lnx_g, lnx_b, ones_blk, w_out_bf)


def _memkv_body(m_ref, g_ref, wk_ref, wv_ref, k_ref, v_ref):
    h = _rms(m_ref[...], g_ref[...]).astype(bf16)
    k_ref[...] = jnp.dot(h, wk_ref[...], preferred_element_type=f32).astype(bf16)
    v_ref[...] = jnp.dot(h, wv_ref[...], preferred_element_type=f32).astype(bf16)


def _memkv(mem2d, g, wk_bf, wv_bf, tm):
    t = mem2d.shape[0]
    blk = pl.BlockSpec((tm, D_MODEL), lambda i: (i, 0))
    sh = jax.ShapeDtypeStruct((t, D_MODEL), bf16)
    return pl.pallas_call(
        _memkv_body,
        grid=(t // tm,),
        in_specs=[blk, _full(g.shape), _full(wk_bf.shape), _full(wv_bf.shape)],
        out_specs=[blk, blk],
        out_shape=[sh, sh],
        compiler_params=_cparams(("parallel",)),
        name="memkv",
    )(mem2d, g, wk_bf, wv_bf)


def _xattn_body(x_ref, k_ref, v_ref, gx_ref, wq_ref, wo_ref, gf_ref, wr_ref, br_ref, upper_ref,
                x2_ref, hf_ref, idx_ref, gate_ref, cnt_ref, base_ref):
    x = x_ref[...]
    h = _rms(x, gx_ref[...]).astype(bf16)
    q = jnp.dot(h, wq_ref[...], preferred_element_type=f32).astype(bf16)
    k = k_ref[...]
    v = v_ref[...]
    outs = []
    for hd in range(XA_HEADS):
        sl = slice(hd * XA_HEAD, (hd + 1) * XA_HEAD)
        sc = lax.dot_general(q[:, sl], k[:, sl], (((1,), (1,)), ((), ())),
                             preferred_element_type=f32) * (XA_HEAD ** -0.5)
        sc = sc - jnp.max(sc, axis=-1, keepdims=True)
        e = jnp.exp(sc)
        p = e / jnp.sum(e, axis=-1, keepdims=True)
        outs.append(jnp.dot(p.astype(bf16), v[:, sl], preferred_element_type=f32).astype(bf16))
    o = jnp.concatenate(outs, axis=1)
    x2 = x + jnp.dot(o, wo_ref[...], preferred_element_type=f32)
    x2_ref[...] = x2

    hf = _rms(x2, gf_ref[...])
    _store_row_tiles(hf_ref, hf)
    logits = lax.dot_general(wr_ref[...], hf.astype(bf16), (((1,), (1,)), ((), ())),
                             preferred_element_type=f32) + br_ref[...]
    erow = lax.broadcasted_iota(i32, logits.shape, 0)
    neg = jnp.float32(-jnp.inf)
    cur = logits
    vals = []
    idxs = []
    for _ in range(TOP_K):
        m = jnp.max(cur, axis=0, keepdims=True)
        am = jnp.min(jnp.where(cur == m, erow, N_EXPERTS), axis=0, keepdims=True)
        vals.append(m)
        idxs.append(am)
        cur = jnp.where(erow == am, neg, cur)
    es = [jnp.exp(vk - vals[0]) for vk in vals]
    den = es[0] + es[1] + es[2] + es[3]

    @pl.when(pl.program_id(0) == 0)
    def _():
        base_ref[...] = jnp.zeros_like(base_ref)

    onehot = [jnp.where(erow == am, 1.0, 0.0) for am in idxs]
    cnt = (onehot[0] + onehot[1]) + (onehot[2] + onehot[3])
    base = base_ref[:, 0:1]
    prior = jnp.dot(cnt.astype(bf16), upper_ref[...], preferred_element_type=f32) + base
    base_ref[...] = jnp.broadcast_to(base + jnp.sum(cnt, axis=1, keepdims=True), base_ref.shape)
    cnt_ref[...] = base_ref[...]

    orow = lax.broadcasted_iota(i32, idx_ref.shape, 0)
    idx_out = jnp.zeros(idx_ref.shape, i32)
    gate_out = jnp.zeros(gate_ref.shape, f32)
    for kq in range(TOP_K):
        rank = jnp.sum(prior * onehot[kq], axis=0, keepdims=True).astype(i32)
        idx_out = jnp.where(orow == kq, idxs[kq], idx_out)
        idx_out = jnp.where(orow == TOP_K + kq, rank, idx_out)
        gate_out = jnp.where(orow == kq, es[kq] / den, gate_out)
    idx_ref[...] = idx_out
    gate_ref[...] = gate_out


def _xattn(x1, kmem, vmem, g_xa, wq_bf, wo_bf, g_ffn, wr_t, br_col, bsz, s, mlen, tm):
    t = bsz * s
    nt = s // tm
    xb = pl.BlockSpec((tm, D_MODEL), lambda i: (i, 0))
    mb = pl.BlockSpec((mlen, D_MODEL), lambda i: (i // nt, 0))
    lb = pl.BlockSpec((2 * TOP_K, tm), lambda i: (0, i))
    upper = jnp.triu(jnp.ones((tm, tm), bf16), k=1)
    return pl.pallas_call(
        _xattn_body,
        grid=(t // tm,),
        in_specs=[xb, mb, mb, _full(g_xa.shape), _full(wq_bf.shape), _full(wo_bf.shape),
                  _full(g_ffn.shape), _full(wr_t.shape), _full(br_col.shape), _full(upper.shape)],
        out_specs=[xb, pl.BlockSpec((tm * ROW_TILE, LANES), lambda i: (i, 0)), lb, lb, _full((N_EXPERTS, LANES))],
        out_shape=[jax.ShapeDtypeStruct((t, D_MODEL), f32), jax.ShapeDtypeStruct((t * ROW_TILE, LANES), f32),
                   jax.ShapeDtypeStruct((2 * TOP_K, t), i32), jax.ShapeDtypeStruct((2 * TOP_K, t), f32),
                   jax.ShapeDtypeStruct((N_EXPERTS, LANES), f32)],
        scratch_shapes=[pltpu.VMEM((N_EXPERTS, LANES), f32)],
        compiler_params=_cparams(("arbitrary",)),
        name="xattn",
    )(x1, kmem, vmem, g_xa, wq_bf, wo_bf, g_ffn, wr_t, br_col, upper)


DMA_UNROLL = 8


def _tile_at(ref, row):
    return ref.at[pl.ds(pl.multiple_of(row * ROW_TILE, ROW_TILE), ROW_TILE), :]


def _dispatch_body(dest_ref, pend_ref, hf_ref, xs_hbm, zeros_ref, sem, zsem):
    i = pl.program_id(0)
    tm = hf_ref.shape[0] // ROW_TILE
    zrows = MOE_BLOCK * ROW_TILE

    def zero_copy(e):
        start = pl.multiple_of((pend_ref[e] - MOE_BLOCK) * ROW_TILE, ROW_TILE)
        return pltpu.make_async_copy(zeros_ref, xs_hbm.at[pl.ds(start, zrows), :], zsem)

    def nonempty(e):
        return pend_ref[e] > (pend_ref[e - 1] if e else 0)

    @pl.when(i == 0)
    def _():
        zeros_ref[...] = jnp.zeros_like(zeros_ref)
        for e in range(N_EXPERTS):
            @pl.when(nonempty(e))
            def _():
                zero_copy(e).start()
        for e in range(N_EXPERTS):
            @pl.when(nonempty(e))
            def _():
                zero_copy(e).wait()

        def tail_copy(b):
            return pltpu.make_async_copy(zeros_ref, xs_hbm.at[pl.ds(pl.multiple_of(b * zrows, zrows), zrows), :], zsem)

        def tail_start(b, carry):
            tail_copy(b).start()
            return carry

        def tail_wait(b, carry):
            tail_copy(b).wait()
            return carry
        n_used = pend_ref[N_EXPERTS - 1] // MOE_BLOCK
        n_all = xs_hbm.shape[0] // zrows
        lax.fori_loop(n_used, n_all, tail_start, 0)
        lax.fori_loop(n_used, n_all, tail_wait, 0)

    def body(q, carry):
        for u in range(DMA_UNROLL):
            r = q * DMA_UNROLL + u
            src = _tile_at(hf_ref, r)
            for kq in range(TOP_K):
                pltpu.make_async_copy(src, _tile_at(xs_hbm, dest_ref[(i * tm + r) * TOP_K + kq]),
                                      sem).start(priority=kq % 2)
        return carry
    lax.fori_loop(0, tm // DMA_UNROLL, body, 0)
    for kq in range(TOP_K):
        pltpu.make_async_copy(hf_ref, xs_hbm.at[pl.ds(0, tm * ROW_TILE), :], sem).wait()


def _dispatch(dest_flat, pends, hf_tiles, n_blocks, tm):
    t = hf_tiles.shape[0] // ROW_TILE
    rows = n_blocks * MOE_BLOCK
    grid_spec = pltpu.PrefetchScalarGridSpec(
        num_scalar_prefetch=2,
        grid=(t // tm,),
        in_specs=[pl.BlockSpec((tm * ROW_TILE, LANES), lambda i, d, pe: (i, 0))],
        out_specs=pl.BlockSpec(memory_space=pl.ANY),
        scratch_shapes=[pltpu.VMEM((MOE_BLOCK * ROW_TILE, LANES), f32), pltpu.SemaphoreType.DMA(()),
                        pltpu.SemaphoreType.DMA(())],
    )
    return pl.pallas_call(
        _dispatch_body,
        grid_spec=grid_spec,
        out_shape=jax.ShapeDtypeStruct((rows * ROW_TILE, LANES), f32),
        compiler_params=_cparams(("arbitrary",)),
        name="dispatch",
    )(dest_flat, pends, hf_tiles)


def _moe_body(pend_ref, xs_hbm, wgu_ref, bgu_ref, wdn_ref, bdn_ref, ys_hbm, xbuf, ybuf, wgu_bf, wdn_bf,
              sem_in, sem_out):
    e = pl.program_id(0)
    blk = MOE_BLOCK * ROW_TILE
    end_blk = pend_ref[e] // MOE_BLOCK
    start_blk = jnp.where(e == 0, 0, pend_ref[jnp.maximum(e - 1, 0)] // MOE_BLOCK)
    nb = end_blk - start_blk

    def rows_of(j):
        return pl.ds(pl.multiple_of((start_blk + j) * blk, blk), blk)

    def x_copy(j, slot):
        return pltpu.make_async_copy(xs_hbm.at[rows_of(j), :], xbuf.at[slot], sem_in.at[slot])

    def y_copy(j, slot):
        return pltpu.make_async_copy(ybuf.at[slot], ys_hbm.at[rows_of(j), :], sem_out.at[slot])

    @pl.when(nb > 0)
    def _():
        x_copy(0, 0).start(priority=1)
        wgu_bf[...] = wgu_ref[0].astype(bf16)
        wdn_bf[...] = wdn_ref[0].astype(bf16)

    def block(j, carry):
        slot = j % 2

        @pl.when(j + 1 < nb)
        def _():
            x_copy(j + 1, 1 - slot).start(priority=1)

        x_copy(j, slot).wait()

        @pl.when(j >= 2)
        def _():
            y_copy(j - 2, slot).wait()

        xb = _load_row_tiles(xbuf.at[slot], MOE_BLOCK).astype(bf16)
        gu = jnp.dot(xb, wgu_bf[...], preferred_element_type=f32) + bgu_ref[0]
        gate = jnp.minimum(gu[:, :D_FF], SWIGLU_LIMIT)
        up = jnp.clip(gu[:, D_FF:], -SWIGLU_LIMIT, SWIGLU_LIMIT)
        act = (up + 1.0) * (gate * jax.nn.sigmoid(SWIGLU_ALPHA * gate))
        y = jnp.dot(act.astype(bf16), wdn_bf[...], preferred_element_type=f32) + bdn_ref[0]
        _store_row_tiles(ybuf.at[slot], y)
        y_copy(j, slot).start(priority=1)
        return carry

    lax.fori_loop(0, nb, block, 0)

    @pl.when(nb >= 2)
    def _():
        y_copy(nb - 2, nb % 2).wait()

    @pl.when(nb >= 1)
    def _():
        y_copy(nb - 1, (nb - 1) % 2).wait()

    @pl.when(e == N_EXPERTS - 1)
    def _():
        n_all = ys_hbm.shape[0] // blk
        ybuf[0] = jnp.zeros(ybuf.shape[1:], ybuf.dtype)

        def tail_copy(b):
            return pltpu.make_async_copy(ybuf.at[0], ys_hbm.at[pl.ds(pl.multiple_of(b * blk, blk), blk), :],
                                         sem_out.at[0])

        def tail_start(b, carry):
            tail_copy(b).start()
            return carry

        def tail_wait(b, carry):
            tail_copy(b).wait()
            return carry
        lax.fori_loop(end_blk, n_all, tail_start, 0)
        lax.fori_loop(end_blk, n_all, tail_wait, 0)


def _moe(pends, xs_tiles, w_gu, b_gu, w_dn, b_dn, n_blocks):
    rows = n_blocks * MOE_BLOCK
    grid_spec = pltpu.PrefetchScalarGridSpec(
        num_scalar_prefetch=1,
        grid=(N_EXPERTS,),
        in_specs=[
            pl.BlockSpec(memory_space=pl.ANY),
            pl.BlockSpec((1, D_MODEL, 2 * D_FF), lambda e, pe: (e, 0, 0)),
            pl.BlockSpec((1, 1, 2 * D_FF), lambda e, pe: (e, 0, 0)),
            pl.BlockSpec((1, D_FF, D_MODEL), lambda e, pe: (e, 0, 0)),
            pl.BlockSpec((1, 1, D_MODEL), lambda e, pe: (e, 0, 0)),
        ],
        out_specs=pl.BlockSpec(memory_space=pl.ANY),
        scratch_shapes=[pltpu.VMEM((2, MOE_BLOCK * ROW_TILE, LANES), f32),
                        pltpu.VMEM((2, MOE_BLOCK * ROW_TILE, LANES), f32),
                        pltpu.VMEM((D_MODEL, 2 * D_FF), bf16), pltpu.VMEM((D_FF, D_MODEL), bf16),
                        pltpu.SemaphoreType.DMA((2,)), pltpu.SemaphoreType.DMA((2,))],
    )
    return pl.pallas_call(
        _moe_body,
        grid_spec=grid_spec,
        out_shape=jax.ShapeDtypeStruct((rows * ROW_TILE, LANES), f32),
        compiler_params=_cparams(("arbitrary",)),
        name="moe",
    )(pends, xs_tiles, w_gu, b_gu.reshape(N_EXPERTS, 1, 2 * D_FF), w_dn, b_dn.reshape(N_EXPERTS, 1, D_MODEL))


def _combine_body(pos_ref, ys_hbm, x_ref, gate_ref, g_ref, o_ref, buf, sem):
    i = pl.program_id(0)
    n = pl.num_programs(0)
    tc = x_ref.shape[0]
    slot = i % 2
    slot_rows = TOP_K * tc * ROW_TILE

    def start(step, sl):
        def body(q, carry):
            for u in range(DMA_UNROLL):
                r = q * DMA_UNROLL + u
                for kq in range(TOP_K):
                    dst = _tile_at(buf, (sl * TOP_K + kq) * tc + r)
                    pltpu.make_async_copy(_tile_at(ys_hbm, pos_ref[(step * tc + r) * TOP_K + kq]), dst,
                                          sem.at[sl]).start(priority=kq % 2)
            return carry
        lax.fori_loop(0, tc // DMA_UNROLL, body, 0)

    @pl.when(i == 0)
    def _():
        start(0, 0)

    @pl.when(i + 1 < n)
    def _():
        start(i + 1, 1 - slot)

    off = pl.multiple_of(slot * slot_rows, slot_rows)
    pltpu.make_async_copy(ys_hbm.at[pl.ds(0, slot_rows), :], buf.at[pl.ds(off, slot_rows), :], sem.at[slot]).wait()
    gates = gate_ref[...]
    acc = x_ref[...]
    for kq in range(TOP_K):
        rows = _load_row_tiles(buf, tc, off + kq * tc * ROW_TILE)
        acc = acc + gates[:, kq:kq + 1] * rows
    o_ref[...] = _rms(acc, g_ref[...])


def _combine(pos_flat, ys_tiles, x2, gate_pad, g_final, tc):
    t = x2.shape[0]
    grid_spec = pltpu.PrefetchScalarGridSpec(
        num_scalar_prefetch=1,
        grid=(t // tc,),
        in_specs=[pl.BlockSpec(memory_space=pl.ANY),
                  pl.BlockSpec((tc, D_MODEL), lambda i, p: (i, 0)),
                  pl.BlockSpec((tc, LANES), lambda i, p: (i, 0)),
                  pl.BlockSpec((1, D_MODEL), lambda i, p: (0, 0))],
        out_specs=pl.BlockSpec((tc, D_MODEL), lambda i, p: (i, 0)),
        scratch_shapes=[pltpu.VMEM((2 * TOP_K * tc * ROW_TILE, LANES), f32), pltpu.SemaphoreType.DMA((2,))],
    )
    return pl.pallas_call(
        _combine_body,
        grid_spec=grid_spec,
        out_shape=jax.ShapeDtypeStruct((t, D_MODEL), f32),
        compiler_params=_cparams(("arbitrary",)),
        name="combine",
    )(pos_flat, ys_tiles, x2, gate_pad, g_final)


def _routing(top_idx, rank, counts, t):
    n_assign = t * TOP_K
    experts = jnp.arange(N_EXPERTS, dtype=i32)
    padded = (counts + MOE_BLOCK - 1) // MOE_BLOCK * MOE_BLOCK
    pends = jnp.cumsum(padded).astype(i32)
    pstarts = pends - padded
    start_of = jnp.sum(jnp.where(top_idx[:, :, None] == experts, pstarts, 0), axis=-1)
    dest = (start_of + rank).astype(i32).reshape(n_assign)
    n_blocks = (n_assign + N_EXPERTS * (MOE_BLOCK - 1) + MOE_BLOCK - 1) // MOE_BLOCK
    return dest, pends, n_blocks


def _block_diag(w):
    n, bi, bj = w.shape
    eye = jnp.eye(n, dtype=w.dtype)
    return jnp.einsum('nij,nm->nimj', w, eye).reshape(n * bi, n * bj)


def _layer(x2d, mem2d, bsz, s, mlen, p):
    t = bsz * s
    row = lambda a: a.reshape(1, -1)
    ones_blk = _block_diag(jnp.ones((RWKV_W // RWKV_HEAD, RWKV_HEAD, RWKV_HEAD), bf16))

    xl, gl, ur = _inproj(x2d, row(p['norm_mix_g']), p['w_in'].astype(bf16), tm=min(512, t))

    wg = jnp.concatenate([_block_diag(p['lru_wx']), _block_diag(p['lru_wa'])], axis=1).astype(bf16)
    bg = jnp.concatenate([p['lru_bx'], p['lru_ba']]).reshape(1, -1)
    y_lru = _lru(xl, gl, p['conv_w'], row(p['conv_b']), wg, bg, row(p['lru_lambda']), bsz, s, ts=min(256, s))

    pairs = lambda a: a.reshape(N_PAIRS, PAIR)
    rows8 = lambda rows: jnp.pad(jnp.stack(rows, axis=1), ((0, 0), (0, SUBLANES - len(rows)), (0, 0)))
    mu = p['rwkv_mu']
    shared = lambda a: jnp.broadcast_to(a, (N_PAIRS, PAIR))
    mu5 = rows8([pairs(mu[:RWKV_W]), pairs(mu[RWKV_W:2 * RWKV_W]), pairs(mu[2 * RWKV_W:3 * RWKV_W]),
                 shared(mu[3 * RWKV_W:3 * RWKV_W + PAIR]), shared(mu[3 * RWKV_W + PAIR:])])
    par5 = rows8([pairs(p['rwkv_w0']), pairs(p['rwkv_a0']), pairs(p['rwkv_k_k']), pairs(p['rwkv_k_a']),
                  pairs(p['rwkv_r_k'])])
    zl = jnp.zeros((DECAY_LORA, RWKV_W), f32)
    w_dec = jnp.concatenate([p['rwkv_w_up'], zl], axis=0).reshape(PAIR, N_PAIRS, PAIR)
    w_icl = jnp.concatenate([zl, p['rwkv_a_up']], axis=0).reshape(PAIR, N_PAIRS, PAIR)
    wproj = jnp.concatenate([w_dec, w_icl], axis=2).transpose(1, 0, 2).astype(bf16)
    gup = p['rwkv_g_up'].reshape(GATE_LORA, N_PAIRS, PAIR).transpose(1, 0, 2).astype(bf16)
    gm, sp, rc, yp, bon, v, g = _rwkv_chunk(ur, mu5, par5, wproj, gup, bsz, s, ts=min(1024, s))
    y_scan = _rwkv_state(gm, sp, rc, yp, bsz, s, nck=min(4, s // RWKV_CHUNK))

    x1 = _outproj(x2d, y_lru, y_scan, bon, v, g, row(p['rwkv_lnx_g']), row(p['rwkv_lnx_b']), ones_blk,
                  p['w_out'].astype(bf16), tm=min(512, t))

    kmem, vmem = _memkv(mem2d, row(p['norm_mem_g']), p['xa_wk'].astype(bf16), p['xa_wv'].astype(bf16),
                        tm=min(512, bsz * mlen))
    x2, hf, route, gates, cnt_pad = _xattn(x1, kmem, vmem, row(p['norm_xa_g']), p['xa_wq'].astype(bf16),
                                           p['xa_wo'].astype(bf16), row(p['norm_ffn_g']),
                                           p['w_router'].T.astype(bf16), p['b_router'].reshape(-1, 1),
                                           bsz, s, mlen, tm=min(512, s))

    counts = cnt_pad[:, 0].astype(i32)
    dest, pends, n_blocks = _routing(route[:TOP_K].T, route[TOP_K:].T, counts, t)
    gate_pad = jnp.pad(gates[:TOP_K].T, ((0, 0), (0, LANES - TOP_K)))
    xs = _dispatch(dest, pends, hf, n_blocks, tm=min(256, t))
    ys = _moe(pends, xs, p['w_gu'], p['b_gu'], p['w_dn'], p['b_dn'], n_blocks)
    return _combine(dest, ys, x2, gate_pad, row(p['final_norm_g']), tc=min(256, t))


def kernel(x, mem, norm_mix_g, w_in, conv_w, conv_b, lru_wx, lru_bx, lru_wa, lru_ba, lru_lambda, rwkv_mu, rwkv_w0, rwkv_w_up, rwkv_a0, rwkv_a_up, rwkv_g_up, rwkv_k_k, rwkv_k_a, rwkv_r_k, rwkv_lnx_g, rwkv_lnx_b, w_out, norm_xa_g, norm_mem_g, xa_wq, xa_wk, xa_wv, xa_wo, norm_ffn_g, w_router, b_router, w_gu, b_gu, w_dn, b_dn, final_norm_g):
    bsz, s, d = x.shape
    mlen = mem.shape[1]
    assert d == D_MODEL and w_in.shape[0] == 1
    p = dict(norm_mix_g=norm_mix_g[0], w_in=w_in[0], conv_w=conv_w[0], conv_b=conv_b[0], lru_wx=lru_wx[0],
             lru_bx=lru_bx[0], lru_wa=lru_wa[0], lru_ba=lru_ba[0], lru_lambda=lru_lambda[0],
             rwkv_mu=rwkv_mu[0], rwkv_w0=rwkv_w0[0], rwkv_w_up=rwkv_w_up[0], rwkv_a0=rwkv_a0[0],
             rwkv_a_up=rwkv_a_up[0], rwkv_g_up=rwkv_g_up[0], rwkv_k_k=rwkv_k_k[0], rwkv_k_a=rwkv_k_a[0],
             rwkv_r_k=rwkv_r_k[0].reshape(-1), rwkv_lnx_g=rwkv_lnx_g[0], rwkv_lnx_b=rwkv_lnx_b[0],
             w_out=w_out[0], norm_xa_g=norm_xa_g[0], norm_mem_g=norm_mem_g[0], xa_wq=xa_wq[0],
             xa_wk=xa_wk[0], xa_wv=xa_wv[0], xa_wo=xa_wo[0], norm_ffn_g=norm_ffn_g[0],
             w_router=w_router[0], b_router=b_router[0], w_gu=w_gu[0], b_gu=b_gu[0], w_dn=w_dn[0],
             b_dn=b_dn[0], final_norm_g=final_norm_g)
    out = _layer(x.reshape(bsz * s, d), mem.reshape(bsz * mlen, d), bsz, s, mlen, p)
    return out.reshape(bsz, s, d)
```

```python
import functools

import jax
import jax.numpy as jnp
from jax import lax
from jax.experimental import pallas as pl
from jax.experimental.pallas import tpu as pltpu

f32 = jnp.float32
bf16 = jnp.bfloat16
i32 = jnp.int32

D_MODEL = 1024
LRU_W = 512
RWKV_W = 512
LRU_BLOCKS = 8
LRU_BLOCK = 64
CONV_W = 4
LRU_C = 8.0
RWKV_HEAD = 64
DECAY_LORA = 64
AAA_LORA = 64
GATE_LORA = 128
RWKV_IN = 3 * RWKV_W + DECAY_LORA + AAA_LORA + GATE_LORA
XA_HEADS = 4
XA_HEAD = D_MODEL // XA_HEADS
N_EXPERTS = 32
TOP_K = 4
D_FF = D_MODEL
SWIGLU_LIMIT = 7.0
SWIGLU_ALPHA = 1.702
EPS = 1e-6
GN_EPS = 64e-5

LANES = 128
SUBLANES = 8
RWKV_CHUNK = 64
PAIR = 2 * RWKV_HEAD
N_PAIRS = RWKV_W // PAIR
MOE_BLOCK = 256
ROW_TILE = D_MODEL // LANES
X_TILE = ROW_TILE // 2
u32 = jnp.uint32
VMEM_LIMIT = 52 * 1024 * 1024


def _cparams(sem):
    return pltpu.CompilerParams(dimension_semantics=sem, vmem_limit_bytes=VMEM_LIMIT)


def _rms(x, g):
    return x * lax.rsqrt(jnp.mean(x * x, axis=-1, keepdims=True) + EPS) * g


def _full(shape):
    n = len(shape)
    return pl.BlockSpec(shape, lambda *a: (0,) * n)


def _shift_rows(x, prev8, d):
    xr = pltpu.roll(x, d, 0)
    tr = pltpu.roll(prev8, d, 0)
    row = lax.broadcasted_iota(i32, prev8.shape, 0)
    head = jnp.where(row < d, tr, xr[:SUBLANES])
    return jnp.concatenate([head, xr[SUBLANES:]], axis=0)


def _bf16_parts(x, n):
    parts = []
    for _ in range(n):
        piece = x.astype(bf16)
        parts.append(piece)
        x = x - piece.astype(f32)
    return parts


def _sum_dot(x, mask_bf, n_parts, mask_left=False):
    acc = None
    for piece in _bf16_parts(x, n_parts):
        d = (jnp.dot(mask_bf, piece, preferred_element_type=f32) if mask_left
             else jnp.dot(piece, mask_bf, preferred_element_type=f32))
        acc = d if acc is None else acc + d
    return acc


def _store_packed_rows(ref, val):
    n = val.shape[0]
    half = D_MODEL // 2
    lo = lax.bitcast_convert_type(val[:, :half].astype(bf16).astype(f32), u32) >> 16
    hi = lax.bitcast_convert_type(val[:, half:].astype(bf16).astype(f32), u32) & jnp.uint32(0xFFFF0000)
    words = hi | lo
    for j in range(X_TILE):
        ref[pl.ds(j, n, stride=X_TILE), :] = words[:, j * LANES:(j + 1) * LANES]


def _load_packed_rows(ref, n):
    ws = [ref[pl.ds(j, n, stride=X_TILE), :] for j in range(X_TILE)]
    lo = [lax.bitcast_convert_type(w << 16, f32).astype(bf16) for w in ws]
    hi = [lax.bitcast_convert_type(w & jnp.uint32(0xFFFF0000), f32).astype(bf16) for w in ws]
    return jnp.concatenate(lo + hi, axis=1)


def _store_row_tiles(ref, val, off=0):
    n = val.shape[0]
    for j in range(ROW_TILE):
        ref[pl.ds(off + j, n, stride=ROW_TILE), :] = val[:, j * LANES:(j + 1) * LANES]


def _load_row_tiles(ref, n, off=0):
    return jnp.concatenate([ref[pl.ds(off + j, n, stride=ROW_TILE), :] for j in range(ROW_TILE)], axis=1)


def _inproj_body(x_ref, g_ref, w_ref, xl_ref, gl_ref, ur_ref):
    h = _rms(x_ref[...], g_ref[...])
    u = jnp.dot(h.astype(bf16), w_ref[...], preferred_element_type=f32)
    xl_ref[...] = u[:, :LRU_W]
    gl_ref[...] = u[:, LRU_W:2 * LRU_W]
    ur_ref[...] = u[:, 2 * LRU_W:]


def _inproj(x2d, g, w_in_bf, tm):
    t = x2d.shape[0]
    return pl.pallas_call(
        _inproj_body,
        grid=(t // tm,),
        in_specs=[pl.BlockSpec((tm, D_MODEL), lambda i: (i, 0)), _full((1, D_MODEL)),
                  _full(w_in_bf.shape)],
        out_specs=[pl.BlockSpec((tm, LRU_W), lambda i: (i, 0)),
                   pl.BlockSpec((tm, LRU_W), lambda i: (i, 0)),
                   pl.BlockSpec((tm, RWKV_IN), lambda i: (i, 0))],
        out_shape=[jax.ShapeDtypeStruct((t, LRU_W), f32), jax.ShapeDtypeStruct((t, LRU_W), f32),
                   jax.ShapeDtypeStruct((t, RWKV_IN), f32)],
        compiler_params=_cparams(("parallel",)),
        name="inproj",
    )(x2d, g, w_in_bf)


def _lru_body(xl_ref, gl_ref, cw_ref, cb_ref, wg_ref, bg_ref, lam_ref, o_ref, tail_ref, h_ref):
    ts = xl_ref.shape[0]

    @pl.when(pl.program_id(1) == 0)
    def _():
        tail_ref[...] = jnp.zeros_like(tail_ref)
        h_ref[...] = jnp.zeros_like(h_ref)

    x = xl_ref[...]
    tail = tail_ref[...]
    cw = cw_ref[...]
    xc = cb_ref[...] + cw[CONV_W - 1:CONV_W] * x
    for d in range(1, CONV_W):
        xc = xc + cw[CONV_W - 1 - d:CONV_W - d] * _shift_rows(x, tail, d)
    tail_ref[...] = x[ts - SUBLANES:]

    gates = jax.nn.sigmoid(jnp.dot(xc.astype(bf16), wg_ref[...], preferred_element_type=f32) + bg_ref[...])
    gx = gates[:, :LRU_W]
    ga = gates[:, LRU_W:]
    log_a = -LRU_C * ga * jax.nn.softplus(-lam_ref[...])
    a = jnp.exp(log_a)
    b = jnp.sqrt(-jnp.tanh(log_a) * (a * a + 1.0)) * gx * xc

    row = lax.broadcasted_iota(i32, (ts, LRU_W), 0)
    d = 1
    while d < ts:
        keep = row >= d
        a_s = jnp.where(keep, pltpu.roll(a, d, 0), 1.0)
        b_s = jnp.where(keep, pltpu.roll(b, d, 0), 0.0)
        b = a * b_s + b
        a = a * a_s
        d *= 2
    h = b + a * h_ref[SUBLANES - 1:SUBLANES, :]
    h_ref[...] = h[ts - SUBLANES:]
    o_ref[...] = (h * jax.nn.gelu(gl_ref[...])).astype(o_ref.dtype)


def _lru(xl, gl, cw, cb, wg_bf, bg, lam, bsz, s, ts):
    nt = s // ts
    blk = pl.BlockSpec((ts, LRU_W), lambda b, i: (b * nt + i, 0))
    return pl.pallas_call(
        _lru_body,
        grid=(bsz, nt),
        in_specs=[blk, blk, _full(cw.shape), _full(cb.shape), _full(wg_bf.shape), _full(bg.shape),
                  _full(lam.shape)],
        out_specs=blk,
        out_shape=jax.ShapeDtypeStruct((bsz * s, LRU_W), bf16),
        scratch_shapes=[pltpu.VMEM((SUBLANES, LRU_W), f32), pltpu.VMEM((SUBLANES, LRU_W), f32)],
        compiler_params=_cparams(("parallel", "arbitrary")),
        name="lru",
    )(xl, gl, cw, cb, wg_bf, bg, lam)


def _mm_x3(a, b_parts):
    ah, al = _bf16_parts(a, 2)
    bh, bl = b_parts
    d = lambda x, y: jnp.dot(x, y, preferred_element_type=f32)
    return d(ah, bh) + (d(ah, bl) + d(al, bh))


def _mm(a, b):
    return jnp.dot(a.astype(bf16), b.astype(bf16), preferred_element_type=f32)


def _mm_nt(a, b):
    return lax.dot_general(a.astype(bf16), b.astype(bf16), (((1,), (1,)), ((), ())), preferred_element_type=f32)


def _mm_tn(a, b):
    return lax.dot_general(a.astype(bf16), b.astype(bf16), (((0,), (0,)), ((), ())), preferred_element_type=f32)


def _bd(x):
    m0 = lax.broadcasted_iota(i32, x.shape, 1) < RWKV_HEAD
    zero = jnp.zeros_like(x)
    return jnp.concatenate([jnp.where(m0, x, zero), jnp.where(m0, zero, x)], axis=0)


def _side_by_side(d):
    h = d.shape[0] // 2
    m0 = lax.broadcasted_iota(i32, (h, d.shape[1]), 1) < RWKV_HEAD
    return jnp.where(m0, d[:h], d[h:])


def _chunk_maps(chunks):
    c = RWKV_CHUNK
    ri = lax.broadcasted_iota(i32, (c, PAIR), 0)
    ji = lax.broadcasted_iota(i32, (c, PAIR), 1) % RWKV_HEAD
    strict = ji < ri
    incl = ji <= ri
    diag = ji == ri
    eye = jnp.where(diag, 1.0, 0.0).astype(f32)
    each = lambda f, *ls: [f(*xs) for xs in zip(*ls)]
    cat0 = lambda *xs: jnp.concatenate(xs, axis=0)
    cat1 = lambda *xs: jnp.concatenate(xs, axis=1)
    tb = lambda x: x.astype(bf16)

    ats, bts, kts, rts, vs, cls = [list(x) for x in zip(*chunks)]
    pcs = each(lambda cl: jnp.exp(cl[c - 1:c, :]), cls)
    bd_a = each(lambda x: tb(_bd(x)), ats)
    bd_v = each(lambda x: tb(_bd(x)), vs)

    aa = each(lambda a, r, b, k: _mm_nt(cat0(tb(a), tb(r)), cat0(tb(_bd(b)), tb(_bd(k)))), ats, rts, bts, kts)
    l_ab = each(lambda x: tb(jnp.where(strict, x[:c, :PAIR], 0.0)), aa)
    a_k = each(lambda x: tb(cat0(jnp.where(strict, x[:c, PAIR:], 0.0), jnp.where(incl, x[c:, PAIR:], 0.0))), aa)
    a_rb = each(lambda x: tb(jnp.where(incl, x[c:, :PAIR], 0.0)), aa)

    tinv = each(lambda x: eye + x, l_ab)
    lp = each(lambda x: tb(_mm(x, _bd(x))), l_ab)
    p = 2
    while 2 * p < c:
        x2 = each(lambda t, x: _mm(cat0(tb(t), x), _bd(x)), tinv, lp)
        tinv = each(lambda t, x: t + x[:c], tinv, x2)
        lp = each(lambda x: tb(x[c:]), x2)
        p *= 2
    tinv = each(lambda t, x: t + _mm(t, _bd(x)), tinv, lp)

    wy = each(_mm, a_k, bd_v)
    za = each(lambda t, w, a: _mm(t, cat1(tb(_bd(w[:c])), a)), tinv, wy, bd_a)
    zp = each(lambda x: x[:, :PAIR], za)
    ac = each(lambda x: x[:, PAIR:], za)
    y1 = each(lambda ar, z, a: _mm(ar, cat1(tb(_bd(z)), tb(_bd(a)))), a_rb, zp, ac)
    yp = each(lambda w, y: w[c:] + y[:, :PAIR], wy, y1)
    rc = each(lambda r, y: r + y[:, PAIR:], rts, y1)
    sm = each(lambda b, pc, z, a: _mm_tn(b * pc, cat1(z, a)), bts, pcs, zp, ac)
    kv = each(lambda k, pc, v: _mm_tn(k * pc, v), kts, pcs, vs)
    sp = each(lambda x, m: _side_by_side(x) + _side_by_side(m[:, :PAIR]), kv, sm)
    g = each(lambda pc, m: jnp.where(diag, jnp.broadcast_to(pc, (c, PAIR)), 0.0) + _side_by_side(m[:, PAIR:]), pcs, sm)
    return list(zip(g, sp, rc, yp))


N_UCOLS = 5


def _rwkv_chunk_body(ur_r, ur_k, ur_v, ur_lo, ur_dg, mu_ref, par_ref, wproj_ref, gup_ref, ones_ref, tri_ref,
                     g_ref, sp_ref, rc_ref, yp_ref, bon_ref, v_ref, gate_ref, prev_ref):
    ts = ur_r.shape[0]
    c = RWKV_CHUNK

    @pl.when(pl.program_id(2) == 0)
    def _():
        prev_ref[...] = jnp.zeros_like(prev_ref)

    mixed = []
    for j, ref in enumerate((ur_r, ur_k, ur_v, ur_lo, ur_dg)):
        u0 = ref[...]
        ls = slice(j * LANES, (j + 1) * LANES)
        us = _shift_rows(u0, prev_ref[:, ls], 1)
        prev_ref[:, ls] = u0[ts - SUBLANES:]
        mixed.append(u0 + (us - u0) * mu_ref[0, j:j + 1, :])
    r, k, v, lora, dg = mixed
    w0, a0, k_k, k_a, r_k = [par_ref[0, j:j + 1, :] for j in range(5)]

    lane = lax.broadcasted_iota(i32, lora.shape, 1)
    lora = jnp.where(lane < DECAY_LORA, jnp.tanh(lora), lora)
    proj = jnp.dot(lora.astype(bf16), wproj_ref[0], preferred_element_type=f32)
    w = -jax.nn.softplus(-(w0 + proj[:, :PAIR])) - 0.5
    lw = -jnp.exp(w)
    a = jax.nn.sigmoid(a0 + proj[:, PAIR:])
    gate_ref[...] = jnp.dot(jax.nn.sigmoid(dg).astype(bf16), gup_ref[0], preferred_element_type=f32)

    ones = ones_ref[...]
    kk = k * k_k
    ss = _sum_dot(kk * kk, ones, 2)
    kk = kk / jnp.maximum(jnp.sqrt(ss), 1e-12)
    k2 = k * (1.0 + (a - 1.0) * k_a)
    bon_ref[...] = _sum_dot(r * k2 * r_k, ones, 2)
    v_ref[...] = v

    tri = tri_ref[...]
    grp = tri.shape[0]
    cl = jnp.concatenate([_sum_dot(lw[q * grp:(q + 1) * grp], tri, 3, mask_left=True) for q in range(ts // grp)],
                         axis=0)
    e_neg = jnp.exp(-cl)
    at = -kk * jnp.exp(cl - lw)
    bt = kk * a * e_neg
    kt = k2 * e_neg
    rt = r * jnp.exp(cl)

    sls = [slice(j * c, (j + 1) * c) for j in range(ts // c)]
    outs = _chunk_maps([(at[sl], bt[sl], kt[sl], rt[sl], v[sl], cl[sl]) for sl in sls])
    for sl, (g, sp, rc, yp) in zip(sls, outs):
        g_ref[sl, :] = g
        sp_ref[sl, :] = sp
        rc_ref[sl, :] = rc
        yp_ref[sl, :] = yp


def _rwkv_chunk(ur, mu5, par5, wproj, gup, bsz, s, ts):
    t = bsz * s
    nt = s // ts
    col = lambda blk: pl.BlockSpec((ts, PAIR), lambda p, b, i, blk=blk: (b * nt + i, blk(p)))
    nrw = RWKV_W // PAIR
    ucols = [col(lambda p: p), col(lambda p: nrw + p), col(lambda p: 2 * nrw + p), col(lambda p: 3 * nrw),
             col(lambda p: 3 * nrw + 1)]
    per_pair = lambda a: pl.BlockSpec((1,) + a.shape[1:], lambda p, b, i: (p, 0, 0))
    ones_pair = _block_diag(jnp.ones((2, RWKV_HEAD, RWKV_HEAD), bf16))
    tri = _block_diag(jnp.tril(jnp.ones((2, RWKV_CHUNK, RWKV_CHUNK), bf16)))
    oblk = pl.BlockSpec((ts, PAIR), lambda p, b, i: (b * nt + i, p))
    osh = jax.ShapeDtypeStruct((t, RWKV_W), f32)
    return pl.pallas_call(
        _rwkv_chunk_body,
        grid=(N_PAIRS, bsz, nt),
        in_specs=ucols + [per_pair(mu5), per_pair(par5), per_pair(wproj), per_pair(gup), _full(ones_pair.shape),
                          _full(tri.shape)],
        out_specs=[oblk] * 7,
        out_shape=[osh] * 7,
        scratch_shapes=[pltpu.VMEM((SUBLANES, N_UCOLS * LANES), f32)],
        compiler_params=_cparams(("parallel", "parallel", "arbitrary")),
        name="rwkv_chunk",
    )(ur, ur, ur, ur, ur, mu5, par5, wproj, gup, ones_pair, tri)


def _rwkv_state_body(g_ref, sp_ref, rc_ref, yp_ref, y_ref, s_ref):
    @pl.when(pl.program_id(0) == 0)
    def _():
        s_ref[...] = jnp.zeros_like(s_ref)

    c = RWKV_CHUNK
    bsz = g_ref.shape[0]
    nck = g_ref.shape[1] // c
    chains = [(b, slice(p * PAIR, (p + 1) * PAIR)) for b in range(bsz) for p in range(N_PAIRS)]
    states = [s_ref[b, :, ls] for b, ls in chains]
    for j in range(nck):
        sl = slice(j * c, (j + 1) * c)
        prods = [_mm_x3(jnp.concatenate([rc_ref[b, sl, ls], g_ref[b, sl, ls]], axis=0), _bf16_parts(_bd(s), 2))
                 for (b, ls), s in zip(chains, states)]
        for (b, ls), pr in zip(chains, prods):
            y_ref[b, sl, ls] = yp_ref[b, sl, ls] + pr[:c]
        states = [pr[c:] + sp_ref[b, sl, ls] for (b, ls), pr in zip(chains, prods)]
    for (b, ls), s in zip(chains, states):
        s_ref[b, :, ls] = s


def _rwkv_state(g, sp, rc, yp, bsz, s, nck):
    rows = nck * RWKV_CHUNK
    rblk = pl.BlockSpec((bsz, rows, RWKV_W), lambda i: (0, i, 0))
    r3 = lambda a: a.reshape(bsz, s, RWKV_W)
    y = pl.pallas_call(
        _rwkv_state_body,
        grid=(s // rows,),
        in_specs=[rblk] * 4,
        out_specs=rblk,
        out_shape=jax.ShapeDtypeStruct((bsz, s, RWKV_W), f32),
        scratch_shapes=[pltpu.VMEM((bsz, RWKV_CHUNK, RWKV_W), f32)],
        compiler_params=_cparams(("arbitrary",)),
        name="rwkv_state",
    )(r3(g), r3(sp), r3(rc), r3(yp))
    return y.reshape(bsz * s, RWKV_W)


def _outproj_body(x_ref, yl_ref, ys_ref, bon_ref, v_ref, g_ref, lg_ref, lb_ref, ones_ref, w_ref, o_ref):
    y = ys_ref[...]
    ones = ones_ref[...]
    inv_n = 1.0 / RWKV_HEAD
    mean = _sum_dot(y, ones, 2) * inv_n
    yc = y - mean
    var = _sum_dot(yc * yc, ones, 2) * inv_n
    yn = yc * lax.rsqrt(var + GN_EPS) * lg_ref[...] + lb_ref[...]
    yr = (yn + bon_ref[...] * v_ref[...]) * g_ref[...]
    cat = jnp.concatenate([yl_ref[...], yr.astype(bf16)], axis=1)
    o_ref[...] = x_ref[...] + jnp.dot(cat, w_ref[...], preferred_element_type=f32)


def _outproj(x2d, y_lru, y_scan, bon, v, g, lnx_g, lnx_b, ones_blk, w_out_bf, tm):
    t = x2d.shape[0]
    xb = pl.BlockSpec((tm, D_MODEL), lambda i: (i, 0))
    hb = pl.BlockSpec((tm, RWKV_W), lambda i: (i, 0))
    return pl.pallas_call(
        _outproj_body,
        grid=(t // tm,),
        in_specs=[xb, hb, hb, hb, hb, hb, _full(lnx_g.shape), _full(lnx_b.shape), _full(ones_blk.shape),
                  _full(w_out_bf.shape)],
        out_specs=xb,
        out_shape=jax.ShapeDtypeStruct((t, D_MODEL), f32),
        compiler_params=_cparams(("parallel",)),
        name="outproj",
    )(x2d, y_lru, y_scan, bon, v, g, lnx_g, lnx_b, ones_blk, w_out_bf)


def _memkv_body(m_ref, g_ref, wk_ref, wv_ref, k_ref, v_ref):
    h = _rms(m_ref[...], g_ref[...]).astype(bf16)
    k_ref[...] = jnp.dot(h, wk_ref[...], preferred_element_type=f32).astype(bf16)
    v_ref[...] = jnp.dot(h, wv_ref[...], preferred_element_type=f32).astype(bf16)


def _memkv(mem2d, g, wk_bf, wv_bf, tm):
    t = mem2d.shape[0]
    blk = pl.BlockSpec((tm, D_MODEL), lambda i: (i, 0))
    sh = jax.ShapeDtypeStruct((t, D_MODEL), bf16)
    return pl.pallas_call(
        _memkv_body,
        grid=(t // tm,),
        in_specs=[blk, _full(g.shape), _full(wk_bf.shape), _full(wv_bf.shape)],
        out_specs=[blk, blk],
        out_shape=[sh, sh],
        compiler_params=_cparams(("parallel",)),
        name="memkv",
    )(mem2d, g, wk_bf, wv_bf)


def _xattn_body(x_ref, k_ref, v_ref, gx_ref, wq_ref, wo_ref, gf_ref, wr_ref, br_ref, upper_ref,
                x2_ref, hf_ref, idx_ref, gate_ref, cnt_ref, base_ref):
    x = x_ref[...]
    h = _rms(x, gx_ref[...]).astype(bf16)
    q = jnp.dot(h, wq_ref[...], preferred_element_type=f32).astype(bf16)
    k = k_ref[...]
    v = v_ref[...]
    outs = []
    for hd in range(XA_HEADS):
        sl = slice(hd * XA_HEAD, (hd + 1) * XA_HEAD)
        sc = lax.dot_general(q[:, sl], k[:, sl], (((1,), (1,)), ((), ())),
                             preferred_element_type=f32) * (XA_HEAD ** -0.5)
        sc = sc - jnp.max(sc, axis=-1, keepdims=True)
        e = jnp.exp(sc)
        p = e / jnp.sum(e, axis=-1, keepdims=True)
        outs.append(jnp.dot(p.astype(bf16), v[:, sl], preferred_element_type=f32).astype(bf16))
    o = jnp.concatenate(outs, axis=1)
    x2 = x + jnp.dot(o, wo_ref[...], preferred_element_type=f32)
    x2_ref[...] = x2

    hf = _rms(x2, gf_ref[...])
    _store_packed_rows(hf_ref, hf)
    logits = lax.dot_general(wr_ref[...], hf.astype(bf16), (((1,), (1,)), ((), ())),
                             preferred_element_type=f32) + br_ref[...]
    erow = lax.broadcasted_iota(i32, logits.shape, 0)
    neg = jnp.float32(-jnp.inf)
    cur = logits
    vals = []
    idxs = []
    for _ in range(TOP_K):
        m = jnp.max(cur, axis=0, keepdims=True)
        am = jnp.min(jnp.where(cur == m, erow, N_EXPERTS), axis=0, keepdims=True)
        vals.append(m)
        idxs.append(am)
        cur = jnp.where(erow == am, neg, cur)
    es = [jnp.exp(vk - vals[0]) for vk in vals]
    den = es[0] + es[1] + es[2] + es[3]

    @pl.when(pl.program_id(0) == 0)
    def _():
        base_ref[...] = jnp.zeros_like(base_ref)

    onehot = [jnp.where(erow == am, 1.0, 0.0) for am in idxs]
    cnt = (onehot[0] + onehot[1]) + (onehot[2] + onehot[3])
    base = base_ref[:, 0:1]
    prior = jnp.dot(cnt.astype(bf16), upper_ref[...], preferred_element_type=f32) + base
    base_ref[...] = jnp.broadcast_to(base + jnp.sum(cnt, axis=1, keepdims=True), base_ref.shape)
    cnt_ref[...] = base_ref[...]

    orow = lax.broadcasted_iota(i32, idx_ref.shape, 0)
    idx_out = jnp.zeros(idx_ref.shape, i32)
    gate_out = jnp.zeros(gate_ref.shape, f32)
    for kq in range(TOP_K):
        rank = jnp.sum(prior * onehot[kq], axis=0, keepdims=True).astype(i32)
        idx_out = jnp.where(orow == kq, idxs[kq], idx_out)
        idx_out = jnp.where(orow == TOP_K + kq, rank, idx_out)
        gate_out = jnp.where(orow == kq, es[kq] / den, gate_out)
    idx_ref[...] = idx_out
    gate_ref[...] = gate_out


def _xattn(x1, kmem, vmem, g_xa, wq_bf, wo_bf, g_ffn, wr_t, br_col, bsz, s, mlen, tm):
    t = bsz * s
    nt = s // tm
    xb = pl.BlockSpec((tm, D_MODEL), lambda i: (i, 0))
    mb = pl.BlockSpec((mlen, D_MODEL), lambda i: (i // nt, 0))
    lb = pl.BlockSpec((2 * TOP_K, tm), lambda i: (0, i))
    upper = jnp.triu(jnp.ones((tm, tm), bf16), k=1)
    return pl.pallas_call(
        _xattn_body,
        grid=(t // tm,),
        in_specs=[xb, mb, mb, _full(g_xa.shape), _full(wq_bf.shape), _full(wo_bf.shape),
                  _full(g_ffn.shape), _full(wr_t.shape), _full(br_col.shape), _full(upper.shape)],
        out_specs=[xb, pl.BlockSpec((tm * X_TILE, LANES), lambda i: (i, 0)), lb, lb, _full((N_EXPERTS, LANES))],
        out_shape=[jax.ShapeDtypeStruct((t, D_MODEL), f32), jax.ShapeDtypeStruct((t * X_TILE, LANES), u32),
                   jax.ShapeDtypeStruct((2 * TOP_K, t), i32), jax.ShapeDtypeStruct((2 * TOP_K, t), f32),
                   jax.ShapeDtypeStruct((N_EXPERTS, LANES), f32)],
        scratch_shapes=[pltpu.VMEM((N_EXPERTS, LANES), f32)],
        compiler_params=_cparams(("arbitrary",)),
        name="xattn",
    )(x1, kmem, vmem, g_xa, wq_bf, wo_bf, g_ffn, wr_t, br_col, upper)


DMA_UNROLL = 8


def _tile_at(ref, row, tile=ROW_TILE):
    return ref.at[pl.ds(pl.multiple_of(row * tile, tile), tile), :]


def _dispatch_body(cnt_ref, gst_ref, pend_ref, hf_ref, idx_ref, upper_ref, xs_hbm, cbuf, zeros_ref, sem, zsem):
    i = pl.program_id(0)
    tm = hf_ref.shape[0] // X_TILE
    zrows = MOE_BLOCK * X_TILE

    def zero_copy(e):
        start = pl.multiple_of((pend_ref[e] - MOE_BLOCK) * X_TILE, X_TILE)
        return pltpu.make_async_copy(zeros_ref, xs_hbm.at[pl.ds(start, zrows), :], zsem)

    def nonempty(e):
        return pend_ref[e] > (pend_ref[e - 1] if e else 0)

    @pl.when(i == 0)
    def _():
        zeros_ref[...] = jnp.zeros_like(zeros_ref)
        for e in range(N_EXPERTS):
            @pl.when(nonempty(e))
            def _():
                zero_copy(e).start()
        for e in range(N_EXPERTS):
            @pl.when(nonempty(e))
            def _():
                zero_copy(e).wait()

        def tail_copy(b):
            return pltpu.make_async_copy(zeros_ref, xs_hbm.at[pl.ds(pl.multiple_of(b * zrows, zrows), zrows), :], zsem)

        def tail_start(b, carry):
            tail_copy(b).start()
            return carry

        def tail_wait(b, carry):
            tail_copy(b).wait()
            return carry
        n_used = pend_ref[N_EXPERTS - 1] // MOE_BLOCK
        n_all = xs_hbm.shape[0] // zrows
        lax.fori_loop(n_used, n_all, tail_start, 0)
        lax.fori_loop(n_used, n_all, tail_wait, 0)

    n_rows = TOP_K * tm
    slot = i % 2
    whole = pltpu.make_async_copy(cbuf.at[slot], xs_hbm.at[pl.ds(0, n_rows * X_TILE), :], sem.at[slot])

    @pl.when(i >= 2)
    def _():
        whole.wait()

    x = _load_packed_rows(hf_ref, tm)
    idx = idx_ref[...]
    erow = lax.broadcasted_iota(i32, (N_EXPERTS, tm), 0)
    onehot = [jnp.where(erow == idx[kq:kq + 1, :], 1.0, 0.0) for kq in range(TOP_K)]
    cnt = (onehot[0] + onehot[1]) + (onehot[2] + onehot[3])
    prior = jnp.dot(cnt.astype(bf16), upper_ref[...], preferred_element_type=f32)
    tot = jnp.broadcast_to(jnp.sum(cnt, axis=1, keepdims=True), (N_EXPERTS, LANES))
    er = lax.broadcasted_iota(i32, (N_EXPERTS, N_EXPERTS), 0)
    ec = lax.broadcasted_iota(i32, (N_EXPERTS, N_EXPERTS), 1)
    before_e = jnp.where(ec < er, 1.0, 0.0).astype(bf16)
    off = jnp.dot(before_e, tot.astype(bf16), preferred_element_type=f32)[:, 0:1]
    place = prior + off
    qrow = lax.broadcasted_iota(i32, (n_rows, tm), 0)
    sel = None
    for kq in range(TOP_K):
        q = jnp.sum(place * onehot[kq], axis=0, keepdims=True).astype(i32)
        hit = qrow == q
        sel = hit if sel is None else (sel | hit)
    xc = jnp.dot(jnp.where(sel, 1.0, 0.0).astype(bf16), x, preferred_element_type=f32)
    _store_packed_rows(cbuf.at[slot], xc)

    q0 = 0
    for e in range(N_EXPERTS):
        n = cnt_ref[i * N_EXPERTS + e]
        g = gst_ref[i * N_EXPERTS + e]
        bit = tm
        while bit >= 1:
            done = (n // (2 * bit)) * (2 * bit)

            @pl.when((n // bit) % 2 == 1)
            def _(bit=bit, done=done, q0=q0, g=g):
                src = cbuf.at[slot, pl.ds(pl.multiple_of((q0 + done) * X_TILE, X_TILE), bit * X_TILE), :]
                dst = xs_hbm.at[pl.ds(pl.multiple_of((g + done) * X_TILE, X_TILE), bit * X_TILE), :]
                pltpu.make_async_copy(src, dst, sem.at[slot]).start()
            bit //= 2
        q0 = q0 + n

    @pl.when(i == pl.num_programs(0) - 1)
    def _():
        whole.wait()

        @pl.when(i >= 1)
        def _():
            pltpu.make_async_copy(cbuf.at[1 - slot], xs_hbm.at[pl.ds(0, n_rows * X_TILE), :], sem.at[1 - slot]).wait()


def _dispatch(tile_cnt, tile_start, pends, hf_tiles, route, n_blocks, tm):
    t = hf_tiles.shape[0] // X_TILE
    rows = n_blocks * MOE_BLOCK
    upper = jnp.triu(jnp.ones((tm, tm), bf16), k=1)
    grid_spec = pltpu.PrefetchScalarGridSpec(
        num_scalar_prefetch=3,
        grid=(t // tm,),
        in_specs=[pl.BlockSpec((tm * X_TILE, LANES), lambda i, c, g, pe: (i, 0)),
                  pl.BlockSpec((2 * TOP_K, tm), lambda i, c, g, pe: (0, i)),
                  pl.BlockSpec((tm, tm), lambda i, c, g, pe: (0, 0))],
        out_specs=pl.BlockSpec(memory_space=pl.ANY),
        scratch_shapes=[pltpu.VMEM((2, TOP_K * tm * X_TILE, LANES), u32),
                        pltpu.VMEM((MOE_BLOCK * X_TILE, LANES), u32), pltpu.SemaphoreType.DMA((2,)),
                        pltpu.SemaphoreType.DMA(())],
    )
    return pl.pallas_call(
        _dispatch_body,
        grid_spec=grid_spec,
        out_shape=jax.ShapeDtypeStruct((rows * X_TILE, LANES), u32),
        compiler_params=_cparams(("arbitrary",)),
        name="dispatch",
    )(tile_cnt, tile_start, pends, hf_tiles, route, upper)


def _moe_body(pend_ref, xs_hbm, wgu_ref, bgu_ref, wdn_ref, bdn_ref, ys_hbm, xbuf, ybuf, wgu_bf, wdn_bf,
              sem_in, sem_out):
    e = pl.program_id(0)
    blk = MOE_BLOCK * ROW_TILE
    end_blk = pend_ref[e] // MOE_BLOCK
    start_blk = jnp.where(e == 0, 0, pend_ref[jnp.maximum(e - 1, 0)] // MOE_BLOCK)
    nb = end_blk - start_blk

    def rows_of(j, n=blk):
        return pl.ds(pl.multiple_of((start_blk + j) * n, n), n)

    def x_copy(j, slot):
        return pltpu.make_async_copy(xs_hbm.at[rows_of(j, MOE_BLOCK * X_TILE), :], xbuf.at[slot], sem_in.at[slot])

    def y_copy(j, slot):
        return pltpu.make_async_copy(ybuf.at[slot], ys_hbm.at[rows_of(j), :], sem_out.at[slot])

    @pl.when(nb > 0)
    def _():
        x_copy(0, 0).start(priority=1)
        wgu_bf[...] = wgu_ref[0].astype(bf16)
        wdn_bf[...] = wdn_ref[0].astype(bf16)

    def block(j, carry):
        slot = j % 2

        @pl.when(j + 1 < nb)
        def _():
            x_copy(j + 1, 1 - slot).start(priority=1)

        x_copy(j, slot).wait()

        @pl.when(j >= 2)
        def _():
            y_copy(j - 2, slot).wait()

        xb = _load_packed_rows(xbuf.at[slot], MOE_BLOCK)
        gu = jnp.dot(xb, wgu_bf[...], preferred_element_type=f32) + bgu_ref[0]
        gate = jnp.minimum(gu[:, :D_FF], SWIGLU_LIMIT)
        up = jnp.clip(gu[:, D_FF:], -SWIGLU_LIMIT, SWIGLU_LIMIT)
        act = (up + 1.0) * (gate * jax.nn.sigmoid(SWIGLU_ALPHA * gate))
        y = jnp.dot(act.astype(bf16), wdn_bf[...], preferred_element_type=f32) + bdn_ref[0]
        _store_row_tiles(ybuf.at[slot], y)
        y_copy(j, slot).start(priority=1)
        return carry

    lax.fori_loop(0, nb, block, 0)

    @pl.when(nb >= 2)
    def _():
        y_copy(nb - 2, nb % 2).wait()

    @pl.when(nb >= 1)
    def _():
        y_copy(nb - 1, (nb - 1) % 2).wait()

    @pl.when(e == N_EXPERTS - 1)
    def _():
        n_all = ys_hbm.shape[0] // blk
        ybuf[0] = jnp.zeros(ybuf.shape[1:], ybuf.dtype)

        def tail_copy(b):
            return pltpu.make_async_copy(ybuf.at[0], ys_hbm.at[pl.ds(pl.multiple_of(b * blk, blk), blk), :],
                                         sem_out.at[0])

        def tail_start(b, carry):
            tail_copy(b).start()
            return carry

        def tail_wait(b, carry):
            tail_copy(b).wait()
            return carry
        lax.fori_loop(end_blk, n_all, tail_start, 0)
        lax.fori_loop(end_blk, n_all, tail_wait, 0)


def _moe(pends, xs_tiles, w_gu, b_gu, w_dn, b_dn, n_blocks):
    rows = n_blocks * MOE_BLOCK
    grid_spec = pltpu.PrefetchScalarGridSpec(
        num_scalar_prefetch=1,
        grid=(N_EXPERTS,),
        in_specs=[
            pl.BlockSpec(memory_space=pl.ANY),
            pl.BlockSpec((1, D_MODEL, 2 * D_FF), lambda e, pe: (e, 0, 0)),
            pl.BlockSpec((1, 1, 2 * D_FF), lambda e, pe: (e, 0, 0)),
            pl.BlockSpec((1, D_FF, D_MODEL), lambda e, pe: (e, 0, 0)),
            pl.BlockSpec((1, 1, D_MODEL), lambda e, pe: (e, 0, 0)),
        ],
        out_specs=pl.BlockSpec(memory_space=pl.ANY),
        scratch_shapes=[pltpu.VMEM((2, MOE_BLOCK * X_TILE, LANES), u32),
                        pltpu.VMEM((2, MOE_BLOCK * ROW_TILE, LANES), f32),
                        pltpu.VMEM((D_MODEL, 2 * D_FF), bf16), pltpu.VMEM((D_FF, D_MODEL), bf16),
                        pltpu.SemaphoreType.DMA((2,)), pltpu.SemaphoreType.DMA((2,))],
    )
    return pl.pallas_call(
        _moe_body,
        grid_spec=grid_spec,
        out_shape=jax.ShapeDtypeStruct((rows * ROW_TILE, LANES), f32),
        compiler_params=_cparams(("arbitrary",)),
        name="moe",
    )(pends, xs_tiles, w_gu, b_gu.reshape(N_EXPERTS, 1, 2 * D_FF), w_dn, b_dn.reshape(N_EXPERTS, 1, D_MODEL))


def _combine_body(pos_ref, ys_hbm, x_ref, gate_ref, g_ref, o_ref, buf, sem):
    i = pl.program_id(0)
    n = pl.num_programs(0)
    tc = x_ref.shape[0]
    slot = i % 2
    slot_rows = TOP_K * tc * ROW_TILE

    def start(step, sl):
        def body(q, carry):
            for u in range(DMA_UNROLL):
                r = q * DMA_UNROLL + u
                for kq in range(TOP_K):
                    dst = _tile_at(buf, (sl * TOP_K + kq) * tc + r)
                    pltpu.make_async_copy(_tile_at(ys_hbm, pos_ref[(step * tc + r) * TOP_K + kq]), dst,
                                          sem.at[sl]).start(priority=kq % 2)
            return carry
        lax.fori_loop(0, tc // DMA_UNROLL, body, 0)

    @pl.when(i == 0)
    def _():
        start(0, 0)

    @pl.when(i + 1 < n)
    def _():
        start(i + 1, 1 - slot)

    off = pl.multiple_of(slot * slot_rows, slot_rows)
    pltpu.make_async_copy(ys_hbm.at[pl.ds(0, slot_rows), :], buf.at[pl.ds(off, slot_rows), :], sem.at[slot]).wait()
    gates = gate_ref[...]
    acc = x_ref[...]
    for kq in range(TOP_K):
        rows = _load_row_tiles(buf, tc, off + kq * tc * ROW_TILE)
        acc = acc + gates[:, kq:kq + 1] * rows
    o_ref[...] = _rms(acc, g_ref[...])


def _combine(pos_flat, ys_tiles, x2, gate_pad, g_final, tc):
    t = x2.shape[0]
    grid_spec = pltpu.PrefetchScalarGridSpec(
        num_scalar_prefetch=1,
        grid=(t // tc,),
        in_specs=[pl.BlockSpec(memory_space=pl.ANY),
                  pl.BlockSpec((tc, D_MODEL), lambda i, p: (i, 0)),
                  pl.BlockSpec((tc, LANES), lambda i, p: (i, 0)),
                  pl.BlockSpec((1, D_MODEL), lambda i, p: (0, 0))],
        out_specs=pl.BlockSpec((tc, D_MODEL), lambda i, p: (i, 0)),
        scratch_shapes=[pltpu.VMEM((2 * TOP_K * tc * ROW_TILE, LANES), f32), pltpu.SemaphoreType.DMA((2,))],
    )
    return pl.pallas_call(
        _combine_body,
        grid_spec=grid_spec,
        out_shape=jax.ShapeDtypeStruct((t, D_MODEL), f32),
        compiler_params=_cparams(("arbitrary",)),
        name="combine",
    )(pos_flat, ys_tiles, x2, gate_pad, g_final)


def _routing(top_idx, rank, counts, t, tile):
    n_assign = t * TOP_K
    experts = jnp.arange(N_EXPERTS, dtype=i32)
    padded = (counts + MOE_BLOCK - 1) // MOE_BLOCK * MOE_BLOCK
    pends = jnp.cumsum(padded).astype(i32)
    pstarts = pends - padded
    start_of = jnp.sum(jnp.where(top_idx[:, :, None] == experts, pstarts, 0), axis=-1)
    dest = (start_of + rank).astype(i32).reshape(n_assign)
    n_blocks = (n_assign + N_EXPERTS * (MOE_BLOCK - 1) + MOE_BLOCK - 1) // MOE_BLOCK
    chosen = jnp.any(top_idx.reshape(t // tile, tile, TOP_K, 1) == experts, axis=2)
    tile_cnt = jnp.sum(chosen.astype(i32), axis=1)
    tile_start = pstarts[None, :] + jnp.cumsum(tile_cnt, axis=0) - tile_cnt
    return dest, pends, n_blocks, tile_cnt.reshape(-1).astype(i32), tile_start.reshape(-1).astype(i32)


def _block_diag(w):
    n, bi, bj = w.shape
    eye = jnp.eye(n, dtype=w.dtype)
    return jnp.einsum('nij,nm->nimj', w, eye).reshape(n * bi, n * bj)


def _layer(x2d, mem2d, bsz, s, mlen, p):
    t = bsz * s
    row = lambda a: a.reshape(1, -1)
    ones_blk = _block_diag(jnp.ones((RWKV_W // RWKV_HEAD, RWKV_HEAD, RWKV_HEAD), bf16))

    xl, gl, ur = _inproj(x2d, row(p['norm_mix_g']), p['w_in'].astype(bf16), tm=min(512, t))

    wg = jnp.concatenate([_block_diag(p['lru_wx']), _block_diag(p['lru_wa'])], axis=1).astype(bf16)
    bg = jnp.concatenate([p['lru_bx'], p['lru_ba']]).reshape(1, -1)
    y_lru = _lru(xl, gl, p['conv_w'], row(p['conv_b']), wg, bg, row(p['lru_lambda']), bsz, s, ts=min(256, s))

    pairs = lambda a: a.reshape(N_PAIRS, PAIR)
    rows8 = lambda rows: jnp.pad(jnp.stack(rows, axis=1), ((0, 0), (0, SUBLANES - len(rows)), (0, 0)))
    mu = p['rwkv_mu']
    shared = lambda a: jnp.broadcast_to(a, (N_PAIRS, PAIR))
    mu5 = rows8([pairs(mu[:RWKV_W]), pairs(mu[RWKV_W:2 * RWKV_W]), pairs(mu[2 * RWKV_W:3 * RWKV_W]),
                 shared(mu[3 * RWKV_W:3 * RWKV_W + PAIR]), shared(mu[3 * RWKV_W + PAIR:])])
    par5 = rows8([pairs(p['rwkv_w0']), pairs(p['rwkv_a0']), pairs(p['rwkv_k_k']), pairs(p['rwkv_k_a']),
                  pairs(p['rwkv_r_k'])])
    zl = jnp.zeros((DECAY_LORA, RWKV_W), f32)
    w_dec = jnp.concatenate([p['rwkv_w_up'], zl], axis=0).reshape(PAIR, N_PAIRS, PAIR)
    w_icl = jnp.concatenate([zl, p['rwkv_a_up']], axis=0).reshape(PAIR, N_PAIRS, PAIR)
    wproj = jnp.concatenate([w_dec, w_icl], axis=2).transpose(1, 0, 2).astype(bf16)
    gup = p['rwkv_g_up'].reshape(GATE_LORA, N_PAIRS, PAIR).transpose(1, 0, 2).astype(bf16)
    gm, sp, rc, yp, bon, v, g = _rwkv_chunk(ur, mu5, par5, wproj, gup, bsz, s, ts=min(1024, s))
    y_scan = _rwkv_state(gm, sp, rc, yp, bsz, s, nck=min(4, s // RWKV_CHUNK))

    x1 = _outproj(x2d, y_lru, y_scan, bon, v, g, row(p['rwkv_lnx_g']), row(p['rwkv_lnx_b']), ones_blk,
                  p['w_out'].astype(bf16), tm=min(512, t))

    kmem, vmem = _memkv(mem2d, row(p['norm_mem_g']), p['xa_wk'].astype(bf16), p['xa_wv'].astype(bf16),
                        tm=min(512, bsz * mlen))
    x2, hf, route, gates, cnt_pad = _xattn(x1, kmem, vmem, row(p['norm_xa_g']), p['xa_wq'].astype(bf16),
                                           p['xa_wo'].astype(bf16), row(p['norm_ffn_g']),
                                           p['w_router'].T.astype(bf16), p['b_router'].reshape(-1, 1),
                                           bsz, s, mlen, tm=min(512, s))

    counts = cnt_pad[:, 0].astype(i32)
    tile = min(256, t)
    dest, pends, n_blocks, tile_cnt, tile_start = _routing(route[:TOP_K].T, route[TOP_K:].T, counts, t, tile)
    gate_pad = jnp.pad(gates[:TOP_K].T, ((0, 0), (0, LANES - TOP_K)))
    xs = _dispatch(tile_cnt, tile_start, pends, hf, route, n_blocks, tm=tile)
    ys = _moe(pends, xs, p['w_gu'], p['b_gu'], p['w_dn'], p['b_dn'], n_blocks)
    return _combine(dest, ys, x2, gate_pad, row(p['final_norm_g']), tc=min(256, t))


def kernel(x, mem, norm_mix_g, w_in, conv_w, conv_b, lru_wx, lru_bx, lru_wa, lru_ba, lru_lambda, rwkv_mu, rwkv_w0, rwkv_w_up, rwkv_a0, rwkv_a_up, rwkv_g_up, rwkv_k_k, rwkv_k_a, rwkv_r_k, rwkv_lnx_g, rwkv_lnx_b, w_out, norm_xa_g, norm_mem_g, xa_wq, xa_wk, xa_wv, xa_wo, norm_ffn_g, w_router, b_router, w_gu, b_gu, w_dn, b_dn, final_norm_g):
    bsz, s, d = x.shape
    mlen = mem.shape[1]
    assert d == D_MODEL and w_in.shape[0] == 1
    p = dict(norm_mix_g=norm_mix_g[0], w_in=w_in[0], conv_w=conv_w[0], conv_b=conv_b[0], lru_wx=lru_wx[0],
             lru_bx=lru_bx[0], lru_wa=lru_wa[0], lru_ba=lru_ba[0], lru_lambda=lru_lambda[0],
             rwkv_mu=rwkv_mu[0], rwkv_w0=rwkv_w0[0], rwkv_w_up=rwkv_w_up[0], rwkv_a0=rwkv_a0[0],
             rwkv_a_up=rwkv_a_up[0], rwkv_g_up=rwkv_g_up[0], rwkv_k_k=rwkv_k_k[0], rwkv_k_a=rwkv_k_a[0],
             rwkv_r_k=rwkv_r_k[0].reshape(-1), rwkv_lnx_g=rwkv_lnx_g[0], rwkv_lnx_b=rwkv_lnx_b[0],
             w_out=w_out[0], norm_xa_g=norm_xa_g[0], norm_mem_g=norm_mem_g[0], xa_wq=xa_wq[0],
             xa_wk=xa_wk[0], xa_wv=xa_wv[0], xa_wo=xa_wo[0], norm_ffn_g=norm_ffn_g[0],
             w_router=w_router[0], b_router=b_router[0], w_gu=w_gu[0], b_gu=b_gu[0], w_dn=w_dn[0],
             b_dn=b_dn[0], final_norm_g=final_norm_g)
    out = _layer(x.reshape(bsz * s, d), mem.reshape(bsz * mlen, d), bsz, s, mlen, p)
    return out.reshape(bsz, s, d)
```

```python
import functools

import jax
import jax.numpy as jnp
from jax import lax
from jax.experimental import pallas as pl
from jax.experimental.pallas import tpu as pltpu

f32 = jnp.float32
bf16 = jnp.bfloat16
i32 = jnp.int32

D_MODEL = 1024
LRU_W = 512
RWKV_W = 512
LRU_BLOCKS = 8
LRU_BLOCK = 64
CONV_W = 4
LRU_C = 8.0
RWKV_HEAD = 64
DECAY_LORA = 64
AAA_LORA = 64
GATE_LORA = 128
RWKV_IN = 3 * RWKV_W + DECAY_LORA + AAA_LORA + GATE_LORA
XA_HEADS = 4
XA_HEAD = D_MODEL // XA_HEADS
N_EXPERTS = 32
TOP_K = 4
D_FF = D_MODEL
SWIGLU_LIMIT = 7.0
SWIGLU_ALPHA = 1.702
EPS = 1e-6
GN_EPS = 64e-5

LANES = 128
SUBLANES = 8
RWKV_CHUNK = 64
PAIR = 2 * RWKV_HEAD
N_PAIRS = RWKV_W // PAIR
MOE_BLOCK = 256
ROW_TILE = D_MODEL // LANES
X_TILE = ROW_TILE // 2
u32 = jnp.uint32
VMEM_LIMIT = 52 * 1024 * 1024


def _cparams(sem):
    return pltpu.CompilerParams(dimension_semantics=sem, vmem_limit_bytes=VMEM_LIMIT)


def _rms(x, g):
    return x * lax.rsqrt(jnp.mean(x * x, axis=-1, keepdims=True) + EPS) * g


def _full(shape):
    n = len(shape)
    return pl.BlockSpec(shape, lambda *a: (0,) * n)


def _shift_rows(x, prev8, d):
    xr = pltpu.roll(x, d, 0)
    tr = pltpu.roll(prev8, d, 0)
    row = lax.broadcasted_iota(i32, prev8.shape, 0)
    head = jnp.where(row < d, tr, xr[:SUBLANES])
    return jnp.concatenate([head, xr[SUBLANES:]], axis=0)


def _bf16_parts(x, n):
    parts = []
    for _ in range(n):
        piece = x.astype(bf16)
        parts.append(piece)
        x = x - piece.astype(f32)
    return parts


def _sum_dot(x, mask_bf, n_parts, mask_left=False):
    acc = None
    for piece in _bf16_parts(x, n_parts):
        d = (jnp.dot(mask_bf, piece, preferred_element_type=f32) if mask_left
             else jnp.dot(piece, mask_bf, preferred_element_type=f32))
        acc = d if acc is None else acc + d
    return acc


def _store_packed_rows(ref, val):
    n = val.shape[0]
    half = D_MODEL // 2
    lo = lax.bitcast_convert_type(val[:, :half].astype(bf16).astype(f32), u32) >> 16
    hi = lax.bitcast_convert_type(val[:, half:].astype(bf16).astype(f32), u32) & jnp.uint32(0xFFFF0000)
    words = hi | lo
    for j in range(X_TILE):
        ref[pl.ds(j, n, stride=X_TILE), :] = words[:, j * LANES:(j + 1) * LANES]


def _load_packed_rows(ref, n):
    ws = [ref[pl.ds(j, n, stride=X_TILE), :] for j in range(X_TILE)]
    lo = [lax.bitcast_convert_type(w << 16, f32).astype(bf16) for w in ws]
    hi = [lax.bitcast_convert_type(w & jnp.uint32(0xFFFF0000), f32).astype(bf16) for w in ws]
    return jnp.concatenate(lo + hi, axis=1)


def _store_row_tiles(ref, val, off=0):
    n = val.shape[0]
    for j in range(ROW_TILE):
        ref[pl.ds(off + j, n, stride=ROW_TILE), :] = val[:, j * LANES:(j + 1) * LANES]


def _load_row_tiles(ref, n, off=0):
    return jnp.concatenate([ref[pl.ds(off + j, n, stride=ROW_TILE), :] for j in range(ROW_TILE)], axis=1)


def _inproj_body(x_ref, g_ref, w_ref, xl_ref, gl_ref, ur_ref):
    h = _rms(x_ref[...], g_ref[...])
    u = jnp.dot(h.astype(bf16), w_ref[...], preferred_element_type=f32)
    xl_ref[...] = u[:, :LRU_W]
    gl_ref[...] = u[:, LRU_W:2 * LRU_W]
    ur_ref[...] = u[:, 2 * LRU_W:]


def _inproj(x2d, g, w_in_bf, tm):
    t = x2d.shape[0]
    return pl.pallas_call(
        _inproj_body,
        grid=(t // tm,),
        in_specs=[pl.BlockSpec((tm, D_MODEL), lambda i: (i, 0)), _full((1, D_MODEL)),
                  _full(w_in_bf.shape)],
        out_specs=[pl.BlockSpec((tm, LRU_W), lambda i: (i, 0)),
                   pl.BlockSpec((tm, LRU_W), lambda i: (i, 0)),
                   pl.BlockSpec((tm, RWKV_IN), lambda i: (i, 0))],
        out_shape=[jax.ShapeDtypeStruct((t, LRU_W), f32), jax.ShapeDtypeStruct((t, LRU_W), f32),
                   jax.ShapeDtypeStruct((t, RWKV_IN), f32)],
        compiler_params=_cparams(("parallel",)),
        name="inproj",
    )(x2d, g, w_in_bf)


def _lru_body(xl_ref, gl_ref, cw_ref, cb_ref, wg_ref, bg_ref, lam_ref, o_ref, tail_ref, h_ref):
    ts = xl_ref.shape[0]

    @pl.when(pl.program_id(1) == 0)
    def _():
        tail_ref[...] = jnp.zeros_like(tail_ref)
        h_ref[...] = jnp.zeros_like(h_ref)

    x = xl_ref[...]
    tail = tail_ref[...]
    cw = cw_ref[...]
    xc = cb_ref[...] + cw[CONV_W - 1:CONV_W] * x
    for d in range(1, CONV_W):
        xc = xc + cw[CONV_W - 1 - d:CONV_W - d] * _shift_rows(x, tail, d)
    tail_ref[...] = x[ts - SUBLANES:]

    gates = jax.nn.sigmoid(jnp.dot(xc.astype(bf16), wg_ref[...], preferred_element_type=f32) + bg_ref[...])
    gx = gates[:, :LRU_W]
    ga = gates[:, LRU_W:]
    log_a = -LRU_C * ga * jax.nn.softplus(-lam_ref[...])
    a = jnp.exp(log_a)
    b = jnp.sqrt(-jnp.tanh(log_a) * (a * a + 1.0)) * gx * xc

    row = lax.broadcasted_iota(i32, (ts, LRU_W), 0) % SUBLANES
    d = 1
    while d < SUBLANES:
        keep = row >= d
        a_s = jnp.where(keep, pltpu.roll(a, d, 0), 1.0)
        b_s = jnp.where(keep, pltpu.roll(b, d, 0), 0.0)
        b = a * b_s + b
        a = a * a_s
        d *= 2
    carry = h_ref[SUBLANES - 1:SUBLANES, :]
    groups = []
    for q in range(ts // SUBLANES):
        rows = slice(q * SUBLANES, (q + 1) * SUBLANES)
        hq = b[rows] + a[rows] * carry
        carry = hq[SUBLANES - 1:SUBLANES, :]
        groups.append(hq)
    h = jnp.concatenate(groups, axis=0)
    h_ref[...] = jnp.broadcast_to(carry, h_ref.shape)
    o_ref[...] = (h * jax.nn.gelu(gl_ref[...])).astype(o_ref.dtype)


def _lru(xl, gl, cw, cb, wg_bf, bg, lam, bsz, s, ts):
    nt = s // ts
    blk = pl.BlockSpec((ts, LRU_W), lambda b, i: (b * nt + i, 0))
    return pl.pallas_call(
        _lru_body,
        grid=(bsz, nt),
        in_specs=[blk, blk, _full(cw.shape), _full(cb.shape), _full(wg_bf.shape), _full(bg.shape),
                  _full(lam.shape)],
        out_specs=blk,
        out_shape=jax.ShapeDtypeStruct((bsz * s, LRU_W), bf16),
        scratch_shapes=[pltpu.VMEM((SUBLANES, LRU_W), f32), pltpu.VMEM((SUBLANES, LRU_W), f32)],
        compiler_params=_cparams(("parallel", "arbitrary")),
        name="lru",
    )(xl, gl, cw, cb, wg_bf, bg, lam)


def _mm_x3(a, b_parts):
    ah, al = _bf16_parts(a, 2)
    bh, bl = b_parts
    d = lambda x, y: jnp.dot(x, y, preferred_element_type=f32)
    return d(ah, bh) + (d(ah, bl) + d(al, bh))


def _mm(a, b):
    return jnp.dot(a.astype(bf16), b.astype(bf16), preferred_element_type=f32)


def _mm_nt(a, b):
    return lax.dot_general(a.astype(bf16), b.astype(bf16), (((1,), (1,)), ((), ())), preferred_element_type=f32)


def _mm_tn(a, b):
    return lax.dot_general(a.astype(bf16), b.astype(bf16), (((0,), (0,)), ((), ())), preferred_element_type=f32)


def _bd(x):
    m0 = lax.broadcasted_iota(i32, x.shape, 1) < RWKV_HEAD
    zero = jnp.zeros_like(x)
    return jnp.concatenate([jnp.where(m0, x, zero), jnp.where(m0, zero, x)], axis=0)


def _side_by_side(d):
    h = d.shape[0] // 2
    m0 = lax.broadcasted_iota(i32, (h, d.shape[1]), 1) < RWKV_HEAD
    return jnp.where(m0, d[:h], d[h:])


def _chunk_maps(chunks):
    c = RWKV_CHUNK
    ri = lax.broadcasted_iota(i32, (c, PAIR), 0)
    ji = lax.broadcasted_iota(i32, (c, PAIR), 1) % RWKV_HEAD
    strict = ji < ri
    incl = ji <= ri
    diag = ji == ri
    eye = jnp.where(diag, 1.0, 0.0).astype(f32)
    each = lambda f, *ls: [f(*xs) for xs in zip(*ls)]
    cat0 = lambda *xs: jnp.concatenate(xs, axis=0)
    cat1 = lambda *xs: jnp.concatenate(xs, axis=1)
    tb = lambda x: x.astype(bf16)

    ats, bts, kts, rts, vs, cls = [list(x) for x in zip(*chunks)]
    pcs = each(lambda cl: jnp.exp(cl[c - 1:c, :]), cls)
    bd_a = each(lambda x: tb(_bd(x)), ats)
    bd_v = each(lambda x: tb(_bd(x)), vs)

    aa = each(lambda a, r, b, k: _mm_nt(cat0(tb(a), tb(r)), cat0(tb(_bd(b)), tb(_bd(k)))), ats, rts, bts, kts)
    l_ab = each(lambda x: tb(jnp.where(strict, x[:c, :PAIR], 0.0)), aa)
    a_k = each(lambda x: tb(cat0(jnp.where(strict, x[:c, PAIR:], 0.0), jnp.where(incl, x[c:, PAIR:], 0.0))), aa)
    a_rb = each(lambda x: tb(jnp.where(incl, x[c:, :PAIR], 0.0)), aa)

    tinv = each(lambda x: eye + x, l_ab)
    lp = each(lambda x: tb(_mm(x, _bd(x))), l_ab)
    p = 2
    while 2 * p < c:
        x2 = each(lambda t, x: _mm(cat0(tb(t), x), _bd(x)), tinv, lp)
        tinv = each(lambda t, x: t + x[:c], tinv, x2)
        lp = each(lambda x: tb(x[c:]), x2)
        p *= 2
    tinv = each(lambda t, x: t + _mm(t, _bd(x)), tinv, lp)

    wy = each(_mm, a_k, bd_v)
    za = each(lambda t, w, a: _mm(t, cat1(tb(_bd(w[:c])), a)), tinv, wy, bd_a)
    zp = each(lambda x: x[:, :PAIR], za)
    ac = each(lambda x: x[:, PAIR:], za)
    y1 = each(lambda ar, z, a: _mm(ar, cat1(tb(_bd(z)), tb(_bd(a)))), a_rb, zp, ac)
    yp = each(lambda w, y: w[c:] + y[:, :PAIR], wy, y1)
    rc = each(lambda r, y: r + y[:, PAIR:], rts, y1)
    sm = each(lambda b, pc, z, a: _mm_tn(b * pc, cat1(z, a)), bts, pcs, zp, ac)
    kv = each(lambda k, pc, v: _mm_tn(k * pc, v), kts, pcs, vs)
    sp = each(lambda x, m: _side_by_side(x) + _side_by_side(m[:, :PAIR]), kv, sm)
    g = each(lambda pc, m: jnp.where(diag, jnp.broadcast_to(pc, (c, PAIR)), 0.0) + _side_by_side(m[:, PAIR:]), pcs, sm)
    return list(zip(g, sp, rc, yp))


N_UCOLS = 5


def _rwkv_chunk_body(ur_r, ur_k, ur_v, ur_lo, ur_dg, mu_ref, par_ref, wproj_ref, gup_ref, ones_ref, tri_ref,
                     g_ref, sp_ref, rc_ref, yp_ref, bon_ref, v_ref, gate_ref, prev_ref):
    ts = ur_r.shape[0]
    c = RWKV_CHUNK

    @pl.when(pl.program_id(2) == 0)
    def _():
        prev_ref[...] = jnp.zeros_like(prev_ref)

    mixed = []
    for j, ref in enumerate((ur_r, ur_k, ur_v, ur_lo, ur_dg)):
        u0 = ref[...]
        ls = slice(j * LANES, (j + 1) * LANES)
        us = _shift_rows(u0, prev_ref[:, ls], 1)
        prev_ref[:, ls] = u0[ts - SUBLANES:]
        mixed.append(u0 + (us - u0) * mu_ref[0, j:j + 1, :])
    r, k, v, lora, dg = mixed
    w0, a0, k_k, k_a, r_k = [par_ref[0, j:j + 1, :] for j in range(5)]

    lane = lax.broadcasted_iota(i32, lora.shape, 1)
    lora = jnp.where(lane < DECAY_LORA, jnp.tanh(lora), lora)
    proj = jnp.dot(lora.astype(bf16), wproj_ref[0], preferred_element_type=f32)
    w = -jax.nn.softplus(-(w0 + proj[:, :PAIR])) - 0.5
    lw = -jnp.exp(w)
    a = jax.nn.sigmoid(a0 + proj[:, PAIR:])
    gate_ref[...] = jnp.dot(jax.nn.sigmoid(dg).astype(bf16), gup_ref[0], preferred_element_type=f32)

    ones = ones_ref[...]
    kk = k * k_k
    ss = _sum_dot(kk * kk, ones, 2)
    kk = kk / jnp.maximum(jnp.sqrt(ss), 1e-12)
    k2 = k * (1.0 + (a - 1.0) * k_a)
    bon_ref[...] = _sum_dot(r * k2 * r_k, ones, 2)
    v_ref[...] = v

    tri = tri_ref[...]
    grp = tri.shape[0]
    cl = jnp.concatenate([_sum_dot(lw[q * grp:(q + 1) * grp], tri, 3, mask_left=True) for q in range(ts // grp)],
                         axis=0)
    e_neg = jnp.exp(-cl)
    at = -kk * jnp.exp(cl - lw)
    bt = kk * a * e_neg
    kt = k2 * e_neg
    rt = r * jnp.exp(cl)

    sls = [slice(j * c, (j + 1) * c) for j in range(ts // c)]
    outs = _chunk_maps([(at[sl], bt[sl], kt[sl], rt[sl], v[sl], cl[sl]) for sl in sls])
    for sl, (g, sp, rc, yp) in zip(sls, outs):
        g_ref[sl, :] = g
        sp_ref[sl, :] = sp
        rc_ref[sl, :] = rc
        yp_ref[sl, :] = yp


def _rwkv_chunk(ur, mu5, par5, wproj, gup, bsz, s, ts):
    t = bsz * s
    nt = s // ts
    col = lambda blk: pl.BlockSpec((ts, PAIR), lambda p, b, i, blk=blk: (b * nt + i, blk(p)))
    nrw = RWKV_W // PAIR
    ucols = [col(lambda p: p), col(lambda p: nrw + p), col(lambda p: 2 * nrw + p), col(lambda p: 3 * nrw),
             col(lambda p: 3 * nrw + 1)]
    per_pair = lambda a: pl.BlockSpec((1,) + a.shape[1:], lambda p, b, i: (p, 0, 0))
    ones_pair = _block_diag(jnp.ones((2, RWKV_HEAD, RWKV_HEAD), bf16))
    tri = _block_diag(jnp.tril(jnp.ones((2, RWKV_CHUNK, RWKV_CHUNK), bf16)))
    oblk = pl.BlockSpec((ts, PAIR), lambda p, b, i: (b * nt + i, p))
    osh = jax.ShapeDtypeStruct((t, RWKV_W), f32)
    return pl.pallas_call(
        _rwkv_chunk_body,
        grid=(N_PAIRS, bsz, nt),
        in_specs=ucols + [per_pair(mu5), per_pair(par5), per_pair(wproj), per_pair(gup), _full(ones_pair.shape),
                          _full(tri.shape)],
        out_specs=[oblk] * 7,
        out_shape=[osh] * 7,
        scratch_shapes=[pltpu.VMEM((SUBLANES, N_UCOLS * LANES), f32)],
        compiler_params=_cparams(("parallel", "parallel", "arbitrary")),
        name="rwkv_chunk",
    )(ur, ur, ur, ur, ur, mu5, par5, wproj, gup, ones_pair, tri)


def _rwkv_state_body(g_ref, sp_ref, rc_ref, yp_ref, y_ref, s_ref):
    @pl.when(pl.program_id(0) == 0)
    def _():
        s_ref[...] = jnp.zeros_like(s_ref)

    c = RWKV_CHUNK
    bsz = g_ref.shape[0]
    nck = g_ref.shape[1] // c
    chains = [(b, slice(p * PAIR, (p + 1) * PAIR)) for b in range(bsz) for p in range(N_PAIRS)]
    states = [s_ref[b, :, ls] for b, ls in chains]
    for j in range(nck):
        sl = slice(j * c, (j + 1) * c)
        prods = [_mm_x3(jnp.concatenate([rc_ref[b, sl, ls], g_ref[b, sl, ls]], axis=0), _bf16_parts(_bd(s), 2))
                 for (b, ls), s in zip(chains, states)]
        for (b, ls), pr in zip(chains, prods):
            y_ref[b, sl, ls] = yp_ref[b, sl, ls] + pr[:c]
        states = [pr[c:] + sp_ref[b, sl, ls] for (b, ls), pr in zip(chains, prods)]
    for (b, ls), s in zip(chains, states):
        s_ref[b, :, ls] = s


def _rwkv_state(g, sp, rc, yp, bsz, s, nck):
    rows = nck * RWKV_CHUNK
    rblk = pl.BlockSpec((bsz, rows, RWKV_W), lambda i: (0, i, 0))
    r3 = lambda a: a.reshape(bsz, s, RWKV_W)
    y = pl.pallas_call(
        _rwkv_state_body,
        grid=(s // rows,),
        in_specs=[rblk] * 4,
        out_specs=rblk,
        out_shape=jax.ShapeDtypeStruct((bsz, s, RWKV_W), f32),
        scratch_shapes=[pltpu.VMEM((bsz, RWKV_CHUNK, RWKV_W), f32)],
        compiler_params=_cparams(("arbitrary",)),
        name="rwkv_state",
    )(r3(g), r3(sp), r3(rc), r3(yp))
    return y.reshape(bsz * s, RWKV_W)


def _outproj_body(x_ref, yl_ref, ys_ref, bon_ref, v_ref, g_ref, lg_ref, lb_ref, ones_ref, w_ref, o_ref):
    y = ys_ref[...]
    ones = ones_ref[...]
    inv_n = 1.0 / RWKV_HEAD
    mean = _sum_dot(y, ones, 2) * inv_n
    yc = y - mean
    var = _sum_dot(yc * yc, ones, 2) * inv_n
    yn = yc * lax.rsqrt(var + GN_EPS) * lg_ref[...] + lb_ref[...]
    yr = (yn + bon_ref[...] * v_ref[...]) * g_ref[...]
    cat = jnp.concatenate([yl_ref[...], yr.astype(bf16)], axis=1)
    o_ref[...] = x_ref[...] + jnp.dot(cat, w_ref[...], preferred_element_type=f32)


def _outproj(x2d, y_lru, y_scan, bon, v, g, lnx_g, lnx_b, ones_blk, w_out_bf, tm):
    t = x2d.shape[0]
    xb = pl.BlockSpec((tm, D_MODEL), lambda i: (i, 0))
    hb = pl.BlockSpec((tm, RWKV_W), lambda i: (i, 0))
    return pl.pallas_call(
        _outproj_body,
        grid=(t // tm,),
        in_specs=[xb, hb, hb, hb, hb, hb, _full(lnx_g.shape), _full(lnx_b.shape), _full(ones_blk.shape),
                  _full(w_out_bf.shape)],
        out_specs=xb,
        out_shape=jax.ShapeDtypeStruct((t, D_MODEL), f32),
        compiler_params=_cparams(("parallel",)),
        name="outproj",
    )(x2d, y_lru, y_scan, bon, v, g, lnx_g, lnx_b, ones_blk, w_out_bf)


def _memkv_body(m_ref, g_ref, wk_ref, wv_ref, k_ref, v_ref):
    h = _rms(m_ref[...], g_ref[...]).astype(bf16)
    k_ref[...] = jnp.dot(h, wk_ref[...], preferred_element_type=f32).astype(bf16)
    v_ref[...] = jnp.dot(h, wv_ref[...], preferred_element_type=f32).astype(bf16)


def _memkv(mem2d, g, wk_bf, wv_bf, tm):
    t = mem2d.shape[0]
    blk = pl.BlockSpec((tm, D_MODEL), lambda i: (i, 0))
    sh = jax.ShapeDtypeStruct((t, D_MODEL), bf16)
    return pl.pallas_call(
        _memkv_body,
        grid=(t // tm,),
        in_specs=[blk, _full(g.shape), _full(wk_bf.shape), _full(wv_bf.shape)],
        out_specs=[blk, blk],
        out_shape=[sh, sh],
        compiler_params=_cparams(("parallel",)),
        name="memkv",
    )(mem2d, g, wk_bf, wv_bf)


def _xattn_body(x_ref, k_ref, v_ref, gx_ref, wq_ref, wo_ref, gf_ref, wr_ref, br_ref, upper_ref,
                x2_ref, hf_ref, idx_ref, gate_ref, cnt_ref, base_ref):
    x = x_ref[...]
    h = _rms(x, gx_ref[...]).astype(bf16)
    q = jnp.dot(h, wq_ref[...], preferred_element_type=f32).astype(bf16)
    k = k_ref[...]
    v = v_ref[...]
    heads = [slice(hd * XA_HEAD, (hd + 1) * XA_HEAD) for hd in range(XA_HEADS)]
    scs = [lax.dot_general(q[:, sl], k[:, sl], (((1,), (1,)), ((), ())), preferred_element_type=f32)
           * (XA_HEAD ** -0.5) for sl in heads]
    ps = []
    for sc in scs:
        e = jnp.exp(sc - jnp.max(sc, axis=-1, keepdims=True))
        ps.append((e / jnp.sum(e, axis=-1, keepdims=True)).astype(bf16))
    o = jnp.concatenate([jnp.dot(p, v[:, sl], preferred_element_type=f32).astype(bf16)
                         for p, sl in zip(ps, heads)], axis=1)
    x2 = x + jnp.dot(o, wo_ref[...], preferred_element_type=f32)
    x2_ref[...] = x2

    hf = _rms(x2, gf_ref[...])
    _store_packed_rows(hf_ref, hf)
    logits = lax.dot_general(wr_ref[...], hf.astype(bf16), (((1,), (1,)), ((), ())),
                             preferred_element_type=f32) + br_ref[...]
    erow = lax.broadcasted_iota(i32, logits.shape, 0)
    neg = jnp.float32(-jnp.inf)
    cur = logits
    vals = []
    idxs = []
    for _ in range(TOP_K):
        m = jnp.max(cur, axis=0, keepdims=True)
        am = jnp.min(jnp.where(cur == m, erow, N_EXPERTS), axis=0, keepdims=True)
        vals.append(m)
        idxs.append(am)
        cur = jnp.where(erow == am, neg, cur)
    es = [jnp.exp(vk - vals[0]) for vk in vals]
    den = es[0] + es[1] + es[2] + es[3]

    @pl.when(pl.program_id(0) == 0)
    def _():
        base_ref[...] = jnp.zeros_like(base_ref)

    onehot = [jnp.where(erow == am, 1.0, 0.0) for am in idxs]
    cnt = (onehot[0] + onehot[1]) + (onehot[2] + onehot[3])
    base = base_ref[:, 0:1]
    prior = jnp.dot(cnt.astype(bf16), upper_ref[...], preferred_element_type=f32) + base
    base_ref[...] = jnp.broadcast_to(base + jnp.sum(cnt, axis=1, keepdims=True), base_ref.shape)
    cnt_ref[...] = base_ref[...]

    orow = lax.broadcasted_iota(i32, idx_ref.shape, 0)
    idx_out = jnp.zeros(idx_ref.shape, i32)
    gate_out = jnp.zeros(gate_ref.shape, f32)
    for kq in range(TOP_K):
        rank = jnp.sum(prior * onehot[kq], axis=0, keepdims=True).astype(i32)
        idx_out = jnp.where(orow == kq, idxs[kq], idx_out)
        idx_out = jnp.where(orow == TOP_K + kq, rank, idx_out)
        gate_out = jnp.where(orow == kq, es[kq] / den, gate_out)
    idx_ref[...] = idx_out
    gate_ref[...] = gate_out


def _xattn(x1, kmem, vmem, g_xa, wq_bf, wo_bf, g_ffn, wr_t, br_col, bsz, s, mlen, tm):
    t = bsz * s
    nt = s // tm
    xb = pl.BlockSpec((tm, D_MODEL), lambda i: (i, 0))
    mb = pl.BlockSpec((mlen, D_MODEL), lambda i: (i // nt, 0))
    lb = pl.BlockSpec((2 * TOP_K, tm), lambda i: (0, i))
    upper = jnp.triu(jnp.ones((tm, tm), bf16), k=1)
    return pl.pallas_call(
        _xattn_body,
        grid=(t // tm,),
        in_specs=[xb, mb, mb, _full(g_xa.shape), _full(wq_bf.shape), _full(wo_bf.shape),
                  _full(g_ffn.shape), _full(wr_t.shape), _full(br_col.shape), _full(upper.shape)],
        out_specs=[xb, pl.BlockSpec((tm * X_TILE, LANES), lambda i: (i, 0)), lb, lb, _full((N_EXPERTS, LANES))],
        out_shape=[jax.ShapeDtypeStruct((t, D_MODEL), f32), jax.ShapeDtypeStruct((t * X_TILE, LANES), u32),
                   jax.ShapeDtypeStruct((2 * TOP_K, t), i32), jax.ShapeDtypeStruct((2 * TOP_K, t), f32),
                   jax.ShapeDtypeStruct((N_EXPERTS, LANES), f32)],
        scratch_shapes=[pltpu.VMEM((N_EXPERTS, LANES), f32)],
        compiler_params=_cparams(("arbitrary",)),
        name="xattn",
    )(x1, kmem, vmem, g_xa, wq_bf, wo_bf, g_ffn, wr_t, br_col, upper)


DMA_UNROLL = 8


def _tile_at(ref, row, tile=ROW_TILE):
    return ref.at[pl.ds(pl.multiple_of(row * tile, tile), tile), :]


def _dispatch_body(dest_ref, pend_ref, hf_ref, xs_hbm, zeros_ref, sem, zsem):
    i = pl.program_id(0)
    tm = hf_ref.shape[0] // X_TILE
    zrows = MOE_BLOCK * X_TILE

    def zero_copy(e):
        start = pl.multiple_of((pend_ref[e] - MOE_BLOCK) * X_TILE, X_TILE)
        return pltpu.make_async_copy(zeros_ref, xs_hbm.at[pl.ds(start, zrows), :], zsem)

    def nonempty(e):
        return pend_ref[e] > (pend_ref[e - 1] if e else 0)

    @pl.when(i == 0)
    def _():
        zeros_ref[...] = jnp.zeros_like(zeros_ref)
        for e in range(N_EXPERTS):
            @pl.when(nonempty(e))
            def _():
                zero_copy(e).start()
        for e in range(N_EXPERTS):
            @pl.when(nonempty(e))
            def _():
                zero_copy(e).wait()

        def tail_copy(b):
            return pltpu.make_async_copy(zeros_ref, xs_hbm.at[pl.ds(pl.multiple_of(b * zrows, zrows), zrows), :], zsem)

        def tail_start(b, carry):
            tail_copy(b).start()
            return carry

        def tail_wait(b, carry):
            tail_copy(b).wait()
            return carry
        n_used = pend_ref[N_EXPERTS - 1] // MOE_BLOCK
        n_all = xs_hbm.shape[0] // zrows
        lax.fori_loop(n_used, n_all, tail_start, 0)
        lax.fori_loop(n_used, n_all, tail_wait, 0)

    def body(q, carry):
        for u in range(DMA_UNROLL):
            r = q * DMA_UNROLL + u
            src = _tile_at(hf_ref, r, X_TILE)
            for kq in range(TOP_K):
                pltpu.make_async_copy(src, _tile_at(xs_hbm, dest_ref[(i * tm + r) * TOP_K + kq], X_TILE),
                                      sem).start(priority=kq % 2)
        return carry
    lax.fori_loop(0, tm // DMA_UNROLL, body, 0)
    for kq in range(TOP_K):
        pltpu.make_async_copy(hf_ref, xs_hbm.at[pl.ds(0, tm * X_TILE), :], sem).wait()


def _dispatch(dest_flat, pends, hf_tiles, n_blocks, tm):
    t = hf_tiles.shape[0] // X_TILE
    rows = n_blocks * MOE_BLOCK
    grid_spec = pltpu.PrefetchScalarGridSpec(
        num_scalar_prefetch=2,
        grid=(t // tm,),
        in_specs=[pl.BlockSpec((tm * X_TILE, LANES), lambda i, d, pe: (i, 0))],
        out_specs=pl.BlockSpec(memory_space=pl.ANY),
        scratch_shapes=[pltpu.VMEM((MOE_BLOCK * X_TILE, LANES), u32), pltpu.SemaphoreType.DMA(()),
                        pltpu.SemaphoreType.DMA(())],
    )
    return pl.pallas_call(
        _dispatch_body,
        grid_spec=grid_spec,
        out_shape=jax.ShapeDtypeStruct((rows * X_TILE, LANES), u32),
        compiler_params=_cparams(("arbitrary",)),
        name="dispatch",
    )(dest_flat, pends, hf_tiles)


def _moe_body(pend_ref, xs_hbm, wgu_ref, bgu_ref, wdn_ref, bdn_ref, ys_hbm, xbuf, ybuf, wgu_bf, wdn_bf,
              sem_in, sem_out):
    e = pl.program_id(0)
    blk = MOE_BLOCK * ROW_TILE
    end_blk = pend_ref[e] // MOE_BLOCK
    start_blk = jnp.where(e == 0, 0, pend_ref[jnp.maximum(e - 1, 0)] // MOE_BLOCK)
    nb = end_blk - start_blk

    def rows_of(j, n=blk):
        return pl.ds(pl.multiple_of((start_blk + j) * n, n), n)

    def x_copy(j, slot):
        return pltpu.make_async_copy(xs_hbm.at[rows_of(j, MOE_BLOCK * X_TILE), :], xbuf.at[slot], sem_in.at[slot])

    def y_copy(j, slot):
        return pltpu.make_async_copy(ybuf.at[slot], ys_hbm.at[rows_of(j), :], sem_out.at[slot])

    @pl.when(nb > 0)
    def _():
        x_copy(0, 0).start(priority=1)
        wgu_bf[...] = wgu_ref[0].astype(bf16)
        wdn_bf[...] = wdn_ref[0].astype(bf16)

    def block(j, carry):
        slot = j % 2

        @pl.when(j + 1 < nb)
        def _():
            x_copy(j + 1, 1 - slot).start(priority=1)

        x_copy(j, slot).wait()

        @pl.when(j >= 2)
        def _():
            y_copy(j - 2, slot).wait()

        xb = _load_packed_rows(xbuf.at[slot], MOE_BLOCK)
        gu = jnp.dot(xb, wgu_bf[...], preferred_element_type=f32) + bgu_ref[0]
        gate = jnp.minimum(gu[:, :D_FF], SWIGLU_LIMIT)
        up = jnp.clip(gu[:, D_FF:], -SWIGLU_LIMIT, SWIGLU_LIMIT)
        act = (up + 1.0) * (gate * jax.nn.sigmoid(SWIGLU_ALPHA * gate))
        y = jnp.dot(act.astype(bf16), wdn_bf[...], preferred_element_type=f32) + bdn_ref[0]
        _store_row_tiles(ybuf.at[slot], y)
        y_copy(j, slot).start(priority=1)
        return carry

    lax.fori_loop(0, nb, block, 0)

    @pl.when(nb >= 2)
    def _():
        y_copy(nb - 2, nb % 2).wait()

    @pl.when(nb >= 1)
    def _():
        y_copy(nb - 1, (nb - 1) % 2).wait()

    @pl.when(e == N_EXPERTS - 1)
    def _():
        n_all = ys_hbm.shape[0] // blk
        ybuf[0] = jnp.zeros(ybuf.shape[1:], ybuf.dtype)

        def tail_copy(b):
            return pltpu.make_async_copy(ybuf.at[0], ys_hbm.at[pl.ds(pl.multiple_of(b * blk, blk), blk), :],
                                         sem_out.at[0])

        def tail_start(b, carry):
            tail_copy(b).start()
            return carry

        def tail_wait(b, carry):
            tail_copy(b).wait()
            return carry
        lax.fori_loop(end_blk, n_all, tail_start, 0)
        lax.fori_loop(end_blk, n_all, tail_wait, 0)


def _moe(pends, xs_tiles, w_gu, b_gu, w_dn, b_dn, n_blocks):
    rows = n_blocks * MOE_BLOCK
    grid_spec = pltpu.PrefetchScalarGridSpec(
        num_scalar_prefetch=1,
        grid=(N_EXPERTS,),
        in_specs=[
            pl.BlockSpec(memory_space=pl.ANY),
            pl.BlockSpec((1, D_MODEL, 2 * D_FF), lambda e, pe: (e, 0, 0)),
            pl.BlockSpec((1, 1, 2 * D_FF), lambda e, pe: (e, 0, 0)),
            pl.BlockSpec((1, D_FF, D_MODEL), lambda e, pe: (e, 0, 0)),
            pl.BlockSpec((1, 1, D_MODEL), lambda e, pe: (e, 0, 0)),
        ],
        out_specs=pl.BlockSpec(memory_space=pl.ANY),
        scratch_shapes=[pltpu.VMEM((2, MOE_BLOCK * X_TILE, LANES), u32),
                        pltpu.VMEM((2, MOE_BLOCK * ROW_TILE, LANES), f32),
                        pltpu.VMEM((D_MODEL, 2 * D_FF), bf16), pltpu.VMEM((D_FF, D_MODEL), bf16),
                        pltpu.SemaphoreType.DMA((2,)), pltpu.SemaphoreType.DMA((2,))],
    )
    return pl.pallas_call(
        _moe_body,
        grid_spec=grid_spec,
        out_shape=jax.ShapeDtypeStruct((rows * ROW_TILE, LANES), f32),
        compiler_params=_cparams(("arbitrary",)),
        name="moe",
    )(pends, xs_tiles, w_gu, b_gu.reshape(N_EXPERTS, 1, 2 * D_FF), w_dn, b_dn.reshape(N_EXPERTS, 1, D_MODEL))


def _combine_body(pos_ref, ys_hbm, x_ref, gate_ref, g_ref, o_ref, buf, sem):
    i = pl.program_id(0)
    n = pl.num_programs(0)
    tc = x_ref.shape[0]
    slot = i % 2
    slot_rows = TOP_K * tc * ROW_TILE

    def start(step, sl):
        def body(q, carry):
            for u in range(DMA_UNROLL):
                r = q * DMA_UNROLL + u
                for kq in range(TOP_K):
                    dst = _tile_at(buf, (sl * TOP_K + kq) * tc + r)
                    pltpu.make_async_copy(_tile_at(ys_hbm, pos_ref[(step * tc + r) * TOP_K + kq]), dst,
                                          sem.at[sl]).start(priority=kq % 2)
            return carry
        lax.fori_loop(0, tc // DMA_UNROLL, body, 0)

    @pl.when(i == 0)
    def _():
        start(0, 0)

    @pl.when(i + 1 < n)
    def _():
        start(i + 1, 1 - slot)

    off = pl.multiple_of(slot * slot_rows, slot_rows)
    pltpu.make_async_copy(ys_hbm.at[pl.ds(0, slot_rows), :], buf.at[pl.ds(off, slot_rows), :], sem.at[slot]).wait()
    gates = gate_ref[...]
    acc = x_ref[...]
    for kq in range(TOP_K):
        rows = _load_row_tiles(buf, tc, off + kq * tc * ROW_TILE)
        acc = acc + gates[:, kq:kq + 1] * rows
    o_ref[...] = _rms(acc, g_ref[...])


def _combine(pos_flat, ys_tiles, x2, gate_pad, g_final, tc):
    t = x2.shape[0]
    grid_spec = pltpu.PrefetchScalarGridSpec(
        num_scalar_prefetch=1,
        grid=(t // tc,),
        in_specs=[pl.BlockSpec(memory_space=pl.ANY),
                  pl.BlockSpec((tc, D_MODEL), lambda i, p: (i, 0)),
                  pl.BlockSpec((tc, LANES), lambda i, p: (i, 0)),
                  pl.BlockSpec((1, D_MODEL), lambda i, p: (0, 0))],
        out_specs=pl.BlockSpec((tc, D_MODEL), lambda i, p: (i, 0)),
        scratch_shapes=[pltpu.VMEM((2 * TOP_K * tc * ROW_TILE, LANES), f32), pltpu.SemaphoreType.DMA((2,))],
    )
    return pl.pallas_call(
        _combine_body,
        grid_spec=grid_spec,
        out_shape=jax.ShapeDtypeStruct((t, D_MODEL), f32),
        compiler_params=_cparams(("arbitrary",)),
        name="combine",
    )(pos_flat, ys_tiles, x2, gate_pad, g_final)


def _routing(top_idx, rank, counts, t):
    n_assign = t * TOP_K
    experts = jnp.arange(N_EXPERTS, dtype=i32)
    padded = (counts + MOE_BLOCK - 1) // MOE_BLOCK * MOE_BLOCK
    pends = jnp.cumsum(padded).astype(i32)
    pstarts = pends - padded
    start_of = jnp.sum(jnp.where(top_idx[:, :, None] == experts, pstarts, 0), axis=-1)
    dest = (start_of + rank).astype(i32).reshape(n_assign)
    n_blocks = (n_assign + N_EXPERTS * (MOE_BLOCK - 1) + MOE_BLOCK - 1) // MOE_BLOCK
    return dest, pends, n_blocks


def _block_diag(w):
    n, bi, bj = w.shape
    eye = jnp.eye(n, dtype=w.dtype)
    return jnp.einsum('nij,nm->nimj', w, eye).reshape(n * bi, n * bj)


def _layer(x2d, mem2d, bsz, s, mlen, p):
    t = bsz * s
    row = lambda a: a.reshape(1, -1)
    ones_blk = _block_diag(jnp.ones((RWKV_W // RWKV_HEAD, RWKV_HEAD, RWKV_HEAD), bf16))

    xl, gl, ur = _inproj(x2d, row(p['norm_mix_g']), p['w_in'].astype(bf16), tm=min(512, t))

    wg = jnp.concatenate([_block_diag(p['lru_wx']), _block_diag(p['lru_wa'])], axis=1).astype(bf16)
    bg = jnp.concatenate([p['lru_bx'], p['lru_ba']]).reshape(1, -1)
    y_lru = _lru(xl, gl, p['conv_w'], row(p['conv_b']), wg, bg, row(p['lru_lambda']), bsz, s, ts=min(512, s))

    pairs = lambda a: a.reshape(N_PAIRS, PAIR)
    rows8 = lambda rows: jnp.pad(jnp.stack(rows, axis=1), ((0, 0), (0, SUBLANES - len(rows)), (0, 0)))
    mu = p['rwkv_mu']
    shared = lambda a: jnp.broadcast_to(a, (N_PAIRS, PAIR))
    mu5 = rows8([pairs(mu[:RWKV_W]), pairs(mu[RWKV_W:2 * RWKV_W]), pairs(mu[2 * RWKV_W:3 * RWKV_W]),
                 shared(mu[3 * RWKV_W:3 * RWKV_W + PAIR]), shared(mu[3 * RWKV_W + PAIR:])])
    par5 = rows8([pairs(p['rwkv_w0']), pairs(p['rwkv_a0']), pairs(p['rwkv_k_k']), pairs(p['rwkv_k_a']),
                  pairs(p['rwkv_r_k'])])
    zl = jnp.zeros((DECAY_LORA, RWKV_W), f32)
    w_dec = jnp.concatenate([p['rwkv_w_up'], zl], axis=0).reshape(PAIR, N_PAIRS, PAIR)
    w_icl = jnp.concatenate([zl, p['rwkv_a_up']], axis=0).reshape(PAIR, N_PAIRS, PAIR)
    wproj = jnp.concatenate([w_dec, w_icl], axis=2).transpose(1, 0, 2).astype(bf16)
    gup = p['rwkv_g_up'].reshape(GATE_LORA, N_PAIRS, PAIR).transpose(1, 0, 2).astype(bf16)
    gm, sp, rc, yp, bon, v, g = _rwkv_chunk(ur, mu5, par5, wproj, gup, bsz, s, ts=min(1024, s))
    y_scan = _rwkv_state(gm, sp, rc, yp, bsz, s, nck=min(4, s // RWKV_CHUNK))

    x1 = _outproj(x2d, y_lru, y_scan, bon, v, g, row(p['rwkv_lnx_g']), row(p['rwkv_lnx_b']), ones_blk,
                  p['w_out'].astype(bf16), tm=min(512, t))

    kmem, vmem = _memkv(mem2d, row(p['norm_mem_g']), p['xa_wk'].astype(bf16), p['xa_wv'].astype(bf16),
                        tm=min(512, bsz * mlen))
    x2, hf, route, gates, cnt_pad = _xattn(x1, kmem, vmem, row(p['norm_xa_g']), p['xa_wq'].astype(bf16),
                                           p['xa_wo'].astype(bf16), row(p['norm_ffn_g']),
                                           p['w_router'].T.astype(bf16), p['b_router'].reshape(-1, 1),
                                           bsz, s, mlen, tm=min(512, s))

    counts = cnt_pad[:, 0].astype(i32)
    dest, pends, n_blocks = _routing(route[:TOP_K].T, route[TOP_K:].T, counts, t)
    gate_pad = jnp.pad(gates[:TOP_K].T, ((0, 0), (0, LANES - TOP_K)))
    xs = _dispatch(dest, pends, hf, n_blocks, tm=min(256, t))
    ys = _moe(pends, xs, p['w_gu'], p['b_gu'], p['w_dn'], p['b_dn'], n_blocks)
    return _combine(dest, ys, x2, gate_pad, row(p['final_norm_g']), tc=min(256, t))


def kernel(x, mem, norm_mix_g, w_in, conv_w, conv_b, lru_wx, lru_bx, lru_wa, lru_ba, lru_lambda, rwkv_mu, rwkv_w0, rwkv_w_up, rwkv_a0, rwkv_a_up, rwkv_g_up, rwkv_k_k, rwkv_k_a, rwkv_r_k, rwkv_lnx_g, rwkv_lnx_b, w_out, norm_xa_g, norm_mem_g, xa_wq, xa_wk, xa_wv, xa_wo, norm_ffn_g, w_router, b_router, w_gu, b_gu, w_dn, b_dn, final_norm_g):
    bsz, s, d = x.shape
    mlen = mem.shape[1]
    assert d == D_MODEL and w_in.shape[0] == 1
    p = dict(norm_mix_g=norm_mix_g[0], w_in=w_in[0], conv_w=conv_w[0], conv_b=conv_b[0], lru_wx=lru_wx[0],
             lru_bx=lru_bx[0], lru_wa=lru_wa[0], lru_ba=lru_ba[0], lru_lambda=lru_lambda[0],
             rwkv_mu=rwkv_mu[0], rwkv_w0=rwkv_w0[0], rwkv_w_up=rwkv_w_up[0], rwkv_a0=rwkv_a0[0],
             rwkv_a_up=rwkv_a_up[0], rwkv_g_up=rwkv_g_up[0], rwkv_k_k=rwkv_k_k[0], rwkv_k_a=rwkv_k_a[0],
             rwkv_r_k=rwkv_r_k[0].reshape(-1), rwkv_lnx_g=rwkv_lnx_g[0], rwkv_lnx_b=rwkv_lnx_b[0],
             w_out=w_out[0], norm_xa_g=norm_xa_g[0], norm_mem_g=norm_mem_g[0], xa_wq=xa_wq[0],
             xa_wk=xa_wk[0], xa_wv=xa_wv[0], xa_wo=xa_wo[0], norm_ffn_g=norm_ffn_g[0],
             w_router=w_router[0], b_router=b_router[0], w_gu=w_gu[0], b_gu=b_gu[0], w_dn=w_dn[0],
             b_dn=b_dn[0], final_norm_g=final_norm_g)
    out = _layer(x.reshape(bsz * s, d), mem.reshape(bsz * mlen, d), bsz, s, mlen, p)
    return out.reshape(bsz, s, d)
```

```python
import functools

import jax
import jax.numpy as jnp
from jax import lax
from jax.experimental import pallas as pl
from jax.experimental.pallas import tpu as pltpu

f32 = jnp.float32
bf16 = jnp.bfloat16
i32 = jnp.int32

D_MODEL = 1024
LRU_W = 512
RWKV_W = 512
LRU_BLOCKS = 8
LRU_BLOCK = 64
CONV_W = 4
LRU_C = 8.0
RWKV_HEAD = 64
DECAY_LORA = 64
AAA_LORA = 64
GATE_LORA = 128
RWKV_IN = 3 * RWKV_W + DECAY_LORA + AAA_LORA + GATE_LORA
XA_HEADS = 4
XA_HEAD = D_MODEL // XA_HEADS
N_EXPERTS = 32
TOP_K = 4
D_FF = D_MODEL
SWIGLU_LIMIT = 7.0
SWIGLU_ALPHA = 1.702
EPS = 1e-6
GN_EPS = 64e-5

LANES = 128
SUBLANES = 8
RWKV_CHUNK = 64
PAIR = 2 * RWKV_HEAD
N_PAIRS = RWKV_W // PAIR
MOE_BLOCK = 256
X_TILE = D_MODEL // (2 * LANES)
u32 = jnp.uint32
VMEM_LIMIT = 52 * 1024 * 1024


def _cparams(sem):
    return pltpu.CompilerParams(dimension_semantics=sem, vmem_limit_bytes=VMEM_LIMIT)


def _rms(x, g):
    return x * lax.rsqrt(jnp.mean(x * x, axis=-1, keepdims=True) + EPS) * g


def _full(shape):
    n = len(shape)
    return pl.BlockSpec(shape, lambda *a: (0,) * n)


def _shift_rows(x, prev8, d):
    xr = pltpu.roll(x, d, 0)
    tr = pltpu.roll(prev8, d, 0)
    row = lax.broadcasted_iota(i32, prev8.shape, 0)
    head = jnp.where(row < d, tr, xr[:SUBLANES])
    return jnp.concatenate([head, xr[SUBLANES:]], axis=0)


def _bf16_parts(x, n):
    parts = []
    for _ in range(n):
        piece = x.astype(bf16)
        parts.append(piece)
        x = x - piece.astype(f32)
    return parts


def _sum_dot(x, mask_bf, n_parts, mask_left=False):
    acc = None
    for piece in _bf16_parts(x, n_parts):
        d = (jnp.dot(mask_bf, piece, preferred_element_type=f32) if mask_left
             else jnp.dot(piece, mask_bf, preferred_element_type=f32))
        acc = d if acc is None else acc + d
    return acc


def _store_packed_rows(ref, val):
    n = val.shape[0]
    half = D_MODEL // 2
    lo = lax.bitcast_convert_type(val[:, :half].astype(bf16).astype(f32), u32) >> 16
    hi = lax.bitcast_convert_type(val[:, half:].astype(bf16).astype(f32), u32) & jnp.uint32(0xFFFF0000)
    words = hi | lo
    for j in range(X_TILE):
        ref[pl.ds(j, n, stride=X_TILE), :] = words[:, j * LANES:(j + 1) * LANES]


def _load_packed_rows(ref, n, off=0):
    ws = [ref[pl.ds(off + j, n, stride=X_TILE), :] for j in range(X_TILE)]
    lo = [lax.bitcast_convert_type(w << 16, f32).astype(bf16) for w in ws]
    hi = [lax.bitcast_convert_type(w & jnp.uint32(0xFFFF0000), f32).astype(bf16) for w in ws]
    return jnp.concatenate(lo + hi, axis=1)


def _inproj_body(x_ref, g_ref, w_ref, xl_ref, gl_ref, ur_ref):
    h = _rms(x_ref[...], g_ref[...])
    u = jnp.dot(h.astype(bf16), w_ref[...], preferred_element_type=f32)
    xl_ref[...] = u[:, :LRU_W]
    gl_ref[...] = u[:, LRU_W:2 * LRU_W]
    ur_ref[...] = u[:, 2 * LRU_W:]


def _inproj(x2d, g, w_in_bf, tm):
    t = x2d.shape[0]
    return pl.pallas_call(
        _inproj_body,
        grid=(t // tm,),
        in_specs=[pl.BlockSpec((tm, D_MODEL), lambda i: (i, 0)), _full((1, D_MODEL)),
                  _full(w_in_bf.shape)],
        out_specs=[pl.BlockSpec((tm, LRU_W), lambda i: (i, 0)),
                   pl.BlockSpec((tm, LRU_W), lambda i: (i, 0)),
                   pl.BlockSpec((tm, RWKV_IN), lambda i: (i, 0))],
        out_shape=[jax.ShapeDtypeStruct((t, LRU_W), f32), jax.ShapeDtypeStruct((t, LRU_W), f32),
                   jax.ShapeDtypeStruct((t, RWKV_IN), f32)],
        compiler_params=_cparams(("parallel",)),
        name="inproj",
    )(x2d, g, w_in_bf)


def _lru_body(xl_ref, gl_ref, cw_ref, cb_ref, wg_ref, bg_ref, lam_ref, o_ref, tail_ref, h_ref):
    ts = xl_ref.shape[0]

    @pl.when(pl.program_id(1) == 0)
    def _():
        tail_ref[...] = jnp.zeros_like(tail_ref)
        h_ref[...] = jnp.zeros_like(h_ref)

    x = xl_ref[...]
    tail = tail_ref[...]
    cw = cw_ref[...]
    xc = cb_ref[...] + cw[CONV_W - 1:CONV_W] * x
    for d in range(1, CONV_W):
        xc = xc + cw[CONV_W - 1 - d:CONV_W - d] * _shift_rows(x, tail, d)
    tail_ref[...] = x[ts - SUBLANES:]

    gates = jax.nn.sigmoid(jnp.dot(xc.astype(bf16), wg_ref[...], preferred_element_type=f32) + bg_ref[...])
    gx = gates[:, :LRU_W]
    ga = gates[:, LRU_W:]
    log_a = -LRU_C * ga * jax.nn.softplus(-lam_ref[...])
    a = jnp.exp(log_a)
    b = jnp.sqrt(-jnp.tanh(log_a) * (a * a + 1.0)) * gx * xc

    row = lax.broadcasted_iota(i32, (ts, LRU_W), 0) % SUBLANES
    d = 1
    while d < SUBLANES:
        keep = row >= d
        a_s = jnp.where(keep, pltpu.roll(a, d, 0), 1.0)
        b_s = jnp.where(keep, pltpu.roll(b, d, 0), 0.0)
        b = a * b_s + b
        a = a * a_s
        d *= 2
    carry = h_ref[SUBLANES - 1:SUBLANES, :]
    groups = []
    for q in range(ts // SUBLANES):
        rows = slice(q * SUBLANES, (q + 1) * SUBLANES)
        hq = b[rows] + a[rows] * carry
        carry = hq[SUBLANES - 1:SUBLANES, :]
        groups.append(hq)
    h = jnp.concatenate(groups, axis=0)
    h_ref[...] = jnp.broadcast_to(carry, h_ref.shape)
    o_ref[...] = (h * jax.nn.gelu(gl_ref[...])).astype(o_ref.dtype)


def _lru(xl, gl, cw, cb, wg_bf, bg, lam, bsz, s, ts):
    nt = s // ts
    blk = pl.BlockSpec((ts, LRU_W), lambda b, i: (b * nt + i, 0))
    return pl.pallas_call(
        _lru_body,
        grid=(bsz, nt),
        in_specs=[blk, blk, _full(cw.shape), _full(cb.shape), _full(wg_bf.shape), _full(bg.shape),
                  _full(lam.shape)],
        out_specs=blk,
        out_shape=jax.ShapeDtypeStruct((bsz * s, LRU_W), bf16),
        scratch_shapes=[pltpu.VMEM((SUBLANES, LRU_W), f32), pltpu.VMEM((SUBLANES, LRU_W), f32)],
        compiler_params=_cparams(("parallel", "arbitrary")),
        name="lru",
    )(xl, gl, cw, cb, wg_bf, bg, lam)


def _mm_x3(a, b_parts):
    ah, al = _bf16_parts(a, 2)
    bh, bl = b_parts
    d = lambda x, y: jnp.dot(x, y, preferred_element_type=f32)
    return d(ah, bh) + (d(ah, bl) + d(al, bh))


def _mm(a, b):
    return jnp.dot(a.astype(bf16), b.astype(bf16), preferred_element_type=f32)


def _mm_nt(a, b):
    return lax.dot_general(a.astype(bf16), b.astype(bf16), (((1,), (1,)), ((), ())), preferred_element_type=f32)


def _mm_tn(a, b):
    return lax.dot_general(a.astype(bf16), b.astype(bf16), (((0,), (0,)), ((), ())), preferred_element_type=f32)


def _bd(x):
    m0 = lax.broadcasted_iota(i32, x.shape, 1) < RWKV_HEAD
    zero = jnp.zeros_like(x)
    return jnp.concatenate([jnp.where(m0, x, zero), jnp.where(m0, zero, x)], axis=0)


def _side_by_side(d):
    h = d.shape[0] // 2
    m0 = lax.broadcasted_iota(i32, (h, d.shape[1]), 1) < RWKV_HEAD
    return jnp.where(m0, d[:h], d[h:])


def _chunk_maps(chunks):
    c = RWKV_CHUNK
    ri = lax.broadcasted_iota(i32, (c, PAIR), 0)
    ji = lax.broadcasted_iota(i32, (c, PAIR), 1) % RWKV_HEAD
    strict = ji < ri
    incl = ji <= ri
    diag = ji == ri
    eye = jnp.where(diag, 1.0, 0.0).astype(f32)
    each = lambda f, *ls: [f(*xs) for xs in zip(*ls)]
    cat0 = lambda *xs: jnp.concatenate(xs, axis=0)
    cat1 = lambda *xs: jnp.concatenate(xs, axis=1)
    tb = lambda x: x.astype(bf16)

    ats, bts, kts, rts, vs, cls = [list(x) for x in zip(*chunks)]
    pcs = each(lambda cl: jnp.exp(cl[c - 1:c, :]), cls)
    bd_a = each(lambda x: tb(_bd(x)), ats)
    bd_v = each(lambda x: tb(_bd(x)), vs)

    aa = each(lambda a, r, b, k: _mm_nt(cat0(tb(a), tb(r)), cat0(tb(_bd(b)), tb(_bd(k)))), ats, rts, bts, kts)
    l_ab = each(lambda x: tb(jnp.where(strict, x[:c, :PAIR], 0.0)), aa)
    a_k = each(lambda x: tb(cat0(jnp.where(strict, x[:c, PAIR:], 0.0), jnp.where(incl, x[c:, PAIR:], 0.0))), aa)
    a_rb = each(lambda x: tb(jnp.where(incl, x[c:, :PAIR], 0.0)), aa)

    tinv = each(lambda x: eye + x, l_ab)
    lp = each(lambda x: tb(_mm(x, _bd(x))), l_ab)
    p = 2
    while 2 * p < c:
        x2 = each(lambda t, x: _mm(cat0(tb(t), x), _bd(x)), tinv, lp)
        tinv = each(lambda t, x: t + x[:c], tinv, x2)
        lp = each(lambda x: tb(x[c:]), x2)
        p *= 2
    tinv = each(lambda t, x: t + _mm(t, _bd(x)), tinv, lp)

    wy = each(_mm, a_k, bd_v)
    za = each(lambda t, w, a: _mm(t, cat1(tb(_bd(w[:c])), a)), tinv, wy, bd_a)
    zp = each(lambda x: x[:, :PAIR], za)
    ac = each(lambda x: x[:, PAIR:], za)
    y1 = each(lambda ar, z, a: _mm(ar, cat1(tb(_bd(z)), tb(_bd(a)))), a_rb, zp, ac)
    yp = each(lambda w, y: w[c:] + y[:, :PAIR], wy, y1)
    rc = each(lambda r, y: r + y[:, PAIR:], rts, y1)
    sm = each(lambda b, pc, z, a: _mm_tn(b * pc, cat1(z, a)), bts, pcs, zp, ac)
    kv = each(lambda k, pc, v: _mm_tn(k * pc, v), kts, pcs, vs)
    sp = each(lambda x, m: _side_by_side(x) + _side_by_side(m[:, :PAIR]), kv, sm)
    g = each(lambda pc, m: jnp.where(diag, jnp.broadcast_to(pc, (c, PAIR)), 0.0) + _side_by_side(m[:, PAIR:]), pcs, sm)
    return list(zip(g, sp, rc, yp))


N_UCOLS = 5


def _rwkv_chunk_body(ur_r, ur_k, ur_v, ur_lo, ur_dg, mu_ref, par_ref, wproj_ref, gup_ref, ones_ref, tri_ref,
                     g_ref, sp_ref, rc_ref, yp_ref, bon_ref, v_ref, gate_ref, prev_ref):
    ts = ur_r.shape[0]
    c = RWKV_CHUNK

    @pl.when(pl.program_id(2) == 0)
    def _():
        prev_ref[...] = jnp.zeros_like(prev_ref)

    mixed = []
    for j, ref in enumerate((ur_r, ur_k, ur_v, ur_lo, ur_dg)):
        u0 = ref[...]
        ls = slice(j * LANES, (j + 1) * LANES)
        us = _shift_rows(u0, prev_ref[:, ls], 1)
        prev_ref[:, ls] = u0[ts - SUBLANES:]
        mixed.append(u0 + (us - u0) * mu_ref[0, j:j + 1, :])
    r, k, v, lora, dg = mixed
    w0, a0, k_k, k_a, r_k = [par_ref[0, j:j + 1, :] for j in range(5)]

    lane = lax.broadcasted_iota(i32, lora.shape, 1)
    lora = jnp.where(lane < DECAY_LORA, jnp.tanh(lora), lora)
    proj = jnp.dot(lora.astype(bf16), wproj_ref[0], preferred_element_type=f32)
    w = -jax.nn.softplus(-(w0 + proj[:, :PAIR])) - 0.5
    lw = -jnp.exp(w)
    a = jax.nn.sigmoid(a0 + proj[:, PAIR:])
    gate_ref[...] = jnp.dot(jax.nn.sigmoid(dg).astype(bf16), gup_ref[0], preferred_element_type=f32)

    ones = ones_ref[...]
    kk = k * k_k
    ss = _sum_dot(kk * kk, ones, 2)
    kk = kk / jnp.maximum(jnp.sqrt(ss), 1e-12)
    k2 = k * (1.0 + (a - 1.0) * k_a)
    bon_ref[...] = _sum_dot(r * k2 * r_k, ones, 2)
    v_ref[...] = v

    tri = tri_ref[...]
    grp = tri.shape[0]
    cl = jnp.concatenate([_sum_dot(lw[q * grp:(q + 1) * grp], tri, 3, mask_left=True) for q in range(ts // grp)],
                         axis=0)
    e_neg = jnp.exp(-cl)
    at = -kk * jnp.exp(cl - lw)
    bt = kk * a * e_neg
    kt = k2 * e_neg
    rt = r * jnp.exp(cl)

    sls = [slice(j * c, (j + 1) * c) for j in range(ts // c)]
    outs = _chunk_maps([(at[sl], bt[sl], kt[sl], rt[sl], v[sl], cl[sl]) for sl in sls])
    for sl, (g, sp, rc, yp) in zip(sls, outs):
        g_ref[sl, :] = g
        sp_ref[sl, :] = sp
        rc_ref[sl, :] = rc
        yp_ref[sl, :] = yp


def _rwkv_chunk(ur, mu5, par5, wproj, gup, bsz, s, ts):
    t = bsz * s
    nt = s // ts
    col = lambda blk: pl.BlockSpec((ts, PAIR), lambda p, b, i, blk=blk: (b * nt + i, blk(p)))
    nrw = RWKV_W // PAIR
    ucols = [col(lambda p: p), col(lambda p: nrw + p), col(lambda p: 2 * nrw + p), col(lambda p: 3 * nrw),
             col(lambda p: 3 * nrw + 1)]
    per_pair = lambda a: pl.BlockSpec((1,) + a.shape[1:], lambda p, b, i: (p, 0, 0))
    ones_pair = _block_diag(jnp.ones((2, RWKV_HEAD, RWKV_HEAD), bf16))
    tri = _block_diag(jnp.tril(jnp.ones((2, RWKV_CHUNK, RWKV_CHUNK), bf16)))
    oblk = pl.BlockSpec((ts, PAIR), lambda p, b, i: (b * nt + i, p))
    osh = jax.ShapeDtypeStruct((t, RWKV_W), f32)
    return pl.pallas_call(
        _rwkv_chunk_body,
        grid=(N_PAIRS, bsz, nt),
        in_specs=ucols + [per_pair(mu5), per_pair(par5), per_pair(wproj), per_pair(gup), _full(ones_pair.shape),
                          _full(tri.shape)],
        out_specs=[oblk] * 7,
        out_shape=[osh] * 7,
        scratch_shapes=[pltpu.VMEM((SUBLANES, N_UCOLS * LANES), f32)],
        compiler_params=_cparams(("parallel", "parallel", "arbitrary")),
        name="rwkv_chunk",
    )(ur, ur, ur, ur, ur, mu5, par5, wproj, gup, ones_pair, tri)


def _rwkv_state_body(g_ref, sp_ref, rc_ref, yp_ref, y_ref, s_ref):
    @pl.when(pl.program_id(0) == 0)
    def _():
        s_ref[...] = jnp.zeros_like(s_ref)

    c = RWKV_CHUNK
    bsz = g_ref.shape[0]
    nck = g_ref.shape[1] // c
    chains = [(b, slice(p * PAIR, (p + 1) * PAIR)) for b in range(bsz) for p in range(N_PAIRS)]
    states = [s_ref[b, :, ls] for b, ls in chains]
    for j in range(nck):
        sl = slice(j * c, (j + 1) * c)
        prods = [_mm_x3(jnp.concatenate([rc_ref[b, sl, ls], g_ref[b, sl, ls]], axis=0), _bf16_parts(_bd(s), 2))
                 for (b, ls), s in zip(chains, states)]
        for (b, ls), pr in zip(chains, prods):
            y_ref[b, sl, ls] = yp_ref[b, sl, ls] + pr[:c]
        states = [pr[c:] + sp_ref[b, sl, ls] for (b, ls), pr in zip(chains, prods)]
    for (b, ls), s in zip(chains, states):
        s_ref[b, :, ls] = s


def _rwkv_state(g, sp, rc, yp, bsz, s, nck):
    rows = nck * RWKV_CHUNK
    rblk = pl.BlockSpec((bsz, rows, RWKV_W), lambda i: (0, i, 0))
    r3 = lambda a: a.reshape(bsz, s, RWKV_W)
    y = pl.pallas_call(
        _rwkv_state_body,
        grid=(s // rows,),
        in_specs=[rblk] * 4,
        out_specs=rblk,
        out_shape=jax.ShapeDtypeStruct((bsz, s, RWKV_W), f32),
        scratch_shapes=[pltpu.VMEM((bsz, RWKV_CHUNK, RWKV_W), f32)],
        compiler_params=_cparams(("arbitrary",)),
        name="rwkv_state",
    )(r3(g), r3(sp), r3(rc), r3(yp))
    return y.reshape(bsz * s, RWKV_W)


def _outproj_body(x_ref, yl_ref, ys_ref, bon_ref, v_ref, g_ref, lg_ref, lb_ref, ones_ref, w_ref, o_ref):
    y = ys_ref[...]
    ones = ones_ref[...]
    inv_n = 1.0 / RWKV_HEAD
    mean = _sum_dot(y, ones, 2) * inv_n
    yc = y - mean
    var = _sum_dot(yc * yc, ones, 2) * inv_n
    yn = yc * lax.rsqrt(var + GN_EPS) * lg_ref[...] + lb_ref[...]
    yr = (yn + bon_ref[...] * v_ref[...]) * g_ref[...]
    cat = jnp.concatenate([yl_ref[...], yr.astype(bf16)], axis=1)
    o_ref[...] = x_ref[...] + jnp.dot(cat, w_ref[...], preferred_element_type=f32)


def _outproj(x2d, y_lru, y_scan, bon, v, g, lnx_g, lnx_b, ones_blk, w_out_bf, tm):
    t = x2d.shape[0]
    xb = pl.BlockSpec((tm, D_MODEL), lambda i: (i, 0))
    hb = pl.BlockSpec((tm, RWKV_W), lambda i: (i, 0))
    return pl.pallas_call(
        _outproj_body,
        grid=(t // tm,),
        in_specs=[xb, hb, hb, hb, hb, hb, _full(lnx_g.shape), _full(lnx_b.shape), _full(ones_blk.shape),
                  _full(w_out_bf.shape)],
        out_specs=xb,
        out_shape=jax.ShapeDtypeStruct((t, D_MODEL), f32),
        compiler_params=_cparams(("parallel",)),
        name="outproj",
    )(x2d, y_lru, y_scan, bon, v, g, lnx_g, lnx_b, ones_blk, w_out_bf)


def _memkv_body(m_ref, g_ref, wk_ref, wv_ref, k_ref, v_ref):
    h = _rms(m_ref[...], g_ref[...]).astype(bf16)
    k_ref[...] = jnp.dot(h, wk_ref[...], preferred_element_type=f32).astype(bf16)
    v_ref[...] = jnp.dot(h, wv_ref[...], preferred_element_type=f32).astype(bf16)


def _memkv(mem2d, g, wk_bf, wv_bf, tm):
    t = mem2d.shape[0]
    blk = pl.BlockSpec((tm, D_MODEL), lambda i: (i, 0))
    sh = jax.ShapeDtypeStruct((t, D_MODEL), bf16)
    return pl.pallas_call(
        _memkv_body,
        grid=(t // tm,),
        in_specs=[blk, _full(g.shape), _full(wk_bf.shape), _full(wv_bf.shape)],
        out_specs=[blk, blk],
        out_shape=[sh, sh],
        compiler_params=_cparams(("parallel",)),
        name="memkv",
    )(mem2d, g, wk_bf, wv_bf)


def _xattn_body(x_ref, k_ref, v_ref, gx_ref, wq_ref, wo_ref, gf_ref, wr_ref, br_ref, upper_ref,
                x2_ref, hf_ref, idx_ref, gate_ref, cnt_ref, base_ref):
    x = x_ref[...]
    h = _rms(x, gx_ref[...]).astype(bf16)
    q = jnp.dot(h, wq_ref[...], preferred_element_type=f32).astype(bf16)
    k = k_ref[...]
    v = v_ref[...]
    heads = [slice(hd * XA_HEAD, (hd + 1) * XA_HEAD) for hd in range(XA_HEADS)]
    scs = [lax.dot_general(q[:, sl], k[:, sl], (((1,), (1,)), ((), ())), preferred_element_type=f32)
           * (XA_HEAD ** -0.5) for sl in heads]
    ps = []
    for sc in scs:
        e = jnp.exp(sc - jnp.max(sc, axis=-1, keepdims=True))
        ps.append((e / jnp.sum(e, axis=-1, keepdims=True)).astype(bf16))
    o = jnp.concatenate([jnp.dot(p, v[:, sl], preferred_element_type=f32).astype(bf16)
                         for p, sl in zip(ps, heads)], axis=1)
    x2 = x + jnp.dot(o, wo_ref[...], preferred_element_type=f32)
    x2_ref[...] = x2

    hf = _rms(x2, gf_ref[...])
    _store_packed_rows(hf_ref, hf)
    logits = lax.dot_general(wr_ref[...], hf.astype(bf16), (((1,), (1,)), ((), ())),
                             preferred_element_type=f32) + br_ref[...]
    erow = lax.broadcasted_iota(i32, logits.shape, 0)
    neg = jnp.float32(-jnp.inf)
    cur = logits
    vals = []
    idxs = []
    for _ in range(TOP_K):
        m = jnp.max(cur, axis=0, keepdims=True)
        am = jnp.min(jnp.where(cur == m, erow, N_EXPERTS), axis=0, keepdims=True)
        vals.append(m)
        idxs.append(am)
        cur = jnp.where(erow == am, neg, cur)
    es = [jnp.exp(vk - vals[0]) for vk in vals]
    den = es[0] + es[1] + es[2] + es[3]

    @pl.when(pl.program_id(0) == 0)
    def _():
        base_ref[...] = jnp.zeros_like(base_ref)

    onehot = [jnp.where(erow == am, 1.0, 0.0) for am in idxs]
    cnt = (onehot[0] + onehot[1]) + (onehot[2] + onehot[3])
    base = base_ref[:, 0:1]
    prior = jnp.dot(cnt.astype(bf16), upper_ref[...], preferred_element_type=f32) + base
    base_ref[...] = jnp.broadcast_to(base + jnp.sum(cnt, axis=1, keepdims=True), base_ref.shape)
    cnt_ref[...] = base_ref[...]

    orow = lax.broadcasted_iota(i32, idx_ref.shape, 0)
    idx_out = jnp.zeros(idx_ref.shape, i32)
    gate_out = jnp.zeros(gate_ref.shape, f32)
    for kq in range(TOP_K):
        rank = jnp.sum(prior * onehot[kq], axis=0, keepdims=True).astype(i32)
        idx_out = jnp.where(orow == kq, idxs[kq], idx_out)
        idx_out = jnp.where(orow == TOP_K + kq, rank, idx_out)
        gate_out = jnp.where(orow == kq, es[kq] / den, gate_out)
    idx_ref[...] = idx_out
    gate_ref[...] = gate_out


def _xattn(x1, kmem, vmem, g_xa, wq_bf, wo_bf, g_ffn, wr_t, br_col, bsz, s, mlen, tm):
    t = bsz * s
    nt = s // tm
    xb = pl.BlockSpec((tm, D_MODEL), lambda i: (i, 0))
    mb = pl.BlockSpec((mlen, D_MODEL), lambda i: (i // nt, 0))
    lb = pl.BlockSpec((2 * TOP_K, tm), lambda i: (0, i))
    upper = jnp.triu(jnp.ones((tm, tm), bf16), k=1)
    return pl.pallas_call(
        _xattn_body,
        grid=(t // tm,),
        in_specs=[xb, mb, mb, _full(g_xa.shape), _full(wq_bf.shape), _full(wo_bf.shape),
                  _full(g_ffn.shape), _full(wr_t.shape), _full(br_col.shape), _full(upper.shape)],
        out_specs=[xb, pl.BlockSpec((tm * X_TILE, LANES), lambda i: (i, 0)), lb, lb, _full((N_EXPERTS, LANES))],
        out_shape=[jax.ShapeDtypeStruct((t, D_MODEL), f32), jax.ShapeDtypeStruct((t * X_TILE, LANES), u32),
                   jax.ShapeDtypeStruct((2 * TOP_K, t), i32), jax.ShapeDtypeStruct((2 * TOP_K, t), f32),
                   jax.ShapeDtypeStruct((N_EXPERTS, LANES), f32)],
        scratch_shapes=[pltpu.VMEM((N_EXPERTS, LANES), f32)],
        compiler_params=_cparams(("arbitrary",)),
        name="xattn",
    )(x1, kmem, vmem, g_xa, wq_bf, wo_bf, g_ffn, wr_t, br_col, upper)


DMA_UNROLL = 8


def _tile_at(ref, row):
    return ref.at[pl.ds(pl.multiple_of(row * X_TILE, X_TILE), X_TILE), :]


def _dispatch_body(dest_ref, pend_ref, hf_ref, xs_hbm, zeros_ref, sem, zsem):
    i = pl.program_id(0)
    tm = hf_ref.shape[0] // X_TILE
    zrows = MOE_BLOCK * X_TILE

    def zero_copy(e):
        start = pl.multiple_of((pend_ref[e] - MOE_BLOCK) * X_TILE, X_TILE)
        return pltpu.make_async_copy(zeros_ref, xs_hbm.at[pl.ds(start, zrows), :], zsem)

    def nonempty(e):
        return pend_ref[e] > (pend_ref[e - 1] if e else 0)

    @pl.when(i == 0)
    def _():
        zeros_ref[...] = jnp.zeros_like(zeros_ref)
        for e in range(N_EXPERTS):
            @pl.when(nonempty(e))
            def _():
                zero_copy(e).start()
        for e in range(N_EXPERTS):
            @pl.when(nonempty(e))
            def _():
                zero_copy(e).wait()

        def tail_copy(b):
            return pltpu.make_async_copy(zeros_ref, xs_hbm.at[pl.ds(pl.multiple_of(b * zrows, zrows), zrows), :], zsem)

        def tail_start(b, carry):
            tail_copy(b).start()
            return carry

        def tail_wait(b, carry):
            tail_copy(b).wait()
            return carry
        n_used = pend_ref[N_EXPERTS - 1] // MOE_BLOCK
        n_all = xs_hbm.shape[0] // zrows
        lax.fori_loop(n_used, n_all, tail_start, 0)
        lax.fori_loop(n_used, n_all, tail_wait, 0)

    def body(q, carry):
        for u in range(DMA_UNROLL):
            r = q * DMA_UNROLL + u
            src = _tile_at(hf_ref, r)
            for kq in range(TOP_K):
                pltpu.make_async_copy(src, _tile_at(xs_hbm, dest_ref[(i * tm + r) * TOP_K + kq]),
                                      sem).start(priority=kq % 2)
        return carry
    lax.fori_loop(0, tm // DMA_UNROLL, body, 0)
    for kq in range(TOP_K):
        pltpu.make_async_copy(hf_ref, xs_hbm.at[pl.ds(0, tm * X_TILE), :], sem).wait()


def _dispatch(dest_flat, pends, hf_tiles, n_blocks, tm):
    t = hf_tiles.shape[0] // X_TILE
    rows = n_blocks * MOE_BLOCK
    grid_spec = pltpu.PrefetchScalarGridSpec(
        num_scalar_prefetch=2,
        grid=(t // tm,),
        in_specs=[pl.BlockSpec((tm * X_TILE, LANES), lambda i, d, pe: (i, 0))],
        out_specs=pl.BlockSpec(memory_space=pl.ANY),
        scratch_shapes=[pltpu.VMEM((MOE_BLOCK * X_TILE, LANES), u32), pltpu.SemaphoreType.DMA(()),
                        pltpu.SemaphoreType.DMA(())],
    )
    return pl.pallas_call(
        _dispatch_body,
        grid_spec=grid_spec,
        out_shape=jax.ShapeDtypeStruct((rows * X_TILE, LANES), u32),
        compiler_params=_cparams(("arbitrary",)),
        name="dispatch",
    )(dest_flat, pends, hf_tiles)


def _moe_body(pend_ref, xs_hbm, wgu_ref, bgu_ref, wdn_ref, bdn_ref, ys_hbm, xbuf, ybuf, wgu_bf, wdn_bf,
              sem_in, sem_out):
    e = pl.program_id(0)
    blk = MOE_BLOCK * X_TILE
    end_blk = pend_ref[e] // MOE_BLOCK
    start_blk = jnp.where(e == 0, 0, pend_ref[jnp.maximum(e - 1, 0)] // MOE_BLOCK)
    nb = end_blk - start_blk

    def rows_of(j):
        return pl.ds(pl.multiple_of((start_blk + j) * blk, blk), blk)

    def x_copy(j, slot):
        return pltpu.make_async_copy(xs_hbm.at[rows_of(j), :], xbuf.at[slot], sem_in.at[slot])

    def y_copy(j, slot):
        return pltpu.make_async_copy(ybuf.at[slot], ys_hbm.at[rows_of(j), :], sem_out.at[slot])

    @pl.when(nb > 0)
    def _():
        x_copy(0, 0).start(priority=1)
        wgu_bf[...] = wgu_ref[0].astype(bf16)
        wdn_bf[...] = wdn_ref[0].astype(bf16)

    def block(j, carry):
        slot = j % 2

        @pl.when(j + 1 < nb)
        def _():
            x_copy(j + 1, 1 - slot).start(priority=1)

        x_copy(j, slot).wait()

        @pl.when(j >= 2)
        def _():
            y_copy(j - 2, slot).wait()

        xb = _load_packed_rows(xbuf.at[slot], MOE_BLOCK)
        gu = jnp.dot(xb, wgu_bf[...], preferred_element_type=f32) + bgu_ref[0]
        gate = jnp.minimum(gu[:, :D_FF], SWIGLU_LIMIT)
        up = jnp.clip(gu[:, D_FF:], -SWIGLU_LIMIT, SWIGLU_LIMIT)
        act = (up + 1.0) * (gate * jax.nn.sigmoid(SWIGLU_ALPHA * gate))
        y = jnp.dot(act.astype(bf16), wdn_bf[...], preferred_element_type=f32) + bdn_ref[0]
        _store_packed_rows(ybuf.at[slot], y)
        y_copy(j, slot).start(priority=1)
        return carry

    lax.fori_loop(0, nb, block, 0)

    @pl.when(nb >= 2)
    def _():
        y_copy(nb - 2, nb % 2).wait()

    @pl.when(nb >= 1)
    def _():
        y_copy(nb - 1, (nb - 1) % 2).wait()

    @pl.when(e == N_EXPERTS - 1)
    def _():
        n_all = ys_hbm.shape[0] // blk
        ybuf[0] = jnp.zeros(ybuf.shape[1:], ybuf.dtype)

        def tail_copy(b):
            return pltpu.make_async_copy(ybuf.at[0], ys_hbm.at[pl.ds(pl.multiple_of(b * blk, blk), blk), :],
                                         sem_out.at[0])

        def tail_start(b, carry):
            tail_copy(b).start()
            return carry

        def tail_wait(b, carry):
            tail_copy(b).wait()
            return carry
        lax.fori_loop(end_blk, n_all, tail_start, 0)
        lax.fori_loop(end_blk, n_all, tail_wait, 0)


def _moe(pends, xs_tiles, w_gu, b_gu, w_dn, b_dn, n_blocks):
    rows = n_blocks * MOE_BLOCK
    grid_spec = pltpu.PrefetchScalarGridSpec(
        num_scalar_prefetch=1,
        grid=(N_EXPERTS,),
        in_specs=[
            pl.BlockSpec(memory_space=pl.ANY),
            pl.BlockSpec((1, D_MODEL, 2 * D_FF), lambda e, pe: (e, 0, 0)),
            pl.BlockSpec((1, 1, 2 * D_FF), lambda e, pe: (e, 0, 0)),
            pl.BlockSpec((1, D_FF, D_MODEL), lambda e, pe: (e, 0, 0)),
            pl.BlockSpec((1, 1, D_MODEL), lambda e, pe: (e, 0, 0)),
        ],
        out_specs=pl.BlockSpec(memory_space=pl.ANY),
        scratch_shapes=[pltpu.VMEM((2, MOE_BLOCK * X_TILE, LANES), u32),
                        pltpu.VMEM((2, MOE_BLOCK * X_TILE, LANES), u32),
                        pltpu.VMEM((D_MODEL, 2 * D_FF), bf16), pltpu.VMEM((D_FF, D_MODEL), bf16),
                        pltpu.SemaphoreType.DMA((2,)), pltpu.SemaphoreType.DMA((2,))],
    )
    return pl.pallas_call(
        _moe_body,
        grid_spec=grid_spec,
        out_shape=jax.ShapeDtypeStruct((rows * X_TILE, LANES), u32),
        compiler_params=_cparams(("arbitrary",)),
        name="moe",
    )(pends, xs_tiles, w_gu, b_gu.reshape(N_EXPERTS, 1, 2 * D_FF), w_dn, b_dn.reshape(N_EXPERTS, 1, D_MODEL))


def _combine_body(pos_ref, ys_hbm, x_ref, gate_ref, g_ref, o_ref, buf, sem):
    i = pl.program_id(0)
    n = pl.num_programs(0)
    tc = x_ref.shape[0]
    slot = i % 2
    slot_rows = TOP_K * tc * X_TILE

    def start(step, sl):
        def body(q, carry):
            for u in range(DMA_UNROLL):
                r = q * DMA_UNROLL + u
                for kq in range(TOP_K):
                    dst = _tile_at(buf, (sl * TOP_K + kq) * tc + r)
                    pltpu.make_async_copy(_tile_at(ys_hbm, pos_ref[(step * tc + r) * TOP_K + kq]), dst,
                                          sem.at[sl]).start(priority=kq % 2)
            return carry
        lax.fori_loop(0, tc // DMA_UNROLL, body, 0)

    @pl.when(i == 0)
    def _():
        start(0, 0)

    @pl.when(i + 1 < n)
    def _():
        start(i + 1, 1 - slot)

    off = pl.multiple_of(slot * slot_rows, slot_rows)
    pltpu.make_async_copy(ys_hbm.at[pl.ds(0, slot_rows), :], buf.at[pl.ds(off, slot_rows), :], sem.at[slot]).wait()
    gates = gate_ref[...]
    acc = x_ref[...]
    for kq in range(TOP_K):
        rows = _load_packed_rows(buf, tc, off + kq * tc * X_TILE)
        acc = acc + gates[:, kq:kq + 1] * rows.astype(f32)
    o_ref[...] = _rms(acc, g_ref[...])


def _combine(pos_flat, ys_tiles, x2, gate_pad, g_final, tc):
    t = x2.shape[0]
    grid_spec = pltpu.PrefetchScalarGridSpec(
        num_scalar_prefetch=1,
        grid=(t // tc,),
        in_specs=[pl.BlockSpec(memory_space=pl.ANY),
                  pl.BlockSpec((tc, D_MODEL), lambda i, p: (i, 0)),
                  pl.BlockSpec((tc, LANES), lambda i, p: (i, 0)),
                  pl.BlockSpec((1, D_MODEL), lambda i, p: (0, 0))],
        out_specs=pl.BlockSpec((tc, D_MODEL), lambda i, p: (i, 0)),
        scratch_shapes=[pltpu.VMEM((2 * TOP_K * tc * X_TILE, LANES), u32), pltpu.SemaphoreType.DMA((2,))],
    )
    return pl.pallas_call(
        _combine_body,
        grid_spec=grid_spec,
        out_shape=jax.ShapeDtypeStruct((t, D_MODEL), f32),
        compiler_params=_cparams(("arbitrary",)),
        name="combine",
    )(pos_flat, ys_tiles, x2, gate_pad, g_final)


def _routing(top_idx, rank, counts, t):
    n_assign = t * TOP_K
    experts = jnp.arange(N_EXPERTS, dtype=i32)
    padded = (counts + MOE_BLOCK - 1) // MOE_BLOCK * MOE_BLOCK
    pends = jnp.cumsum(padded).astype(i32)
    pstarts = pends - padded
    start_of = jnp.sum(jnp.where(top_idx[:, :, None] == experts, pstarts, 0), axis=-1)
    dest = (start_of + rank).astype(i32).reshape(n_assign)
    n_blocks = (n_assign + N_EXPERTS * (MOE_BLOCK - 1) + MOE_BLOCK - 1) // MOE_BLOCK
    return dest, pends, n_blocks


def _block_diag(w):
    n, bi, bj = w.shape
    eye = jnp.eye(n, dtype=w.dtype)
    return jnp.einsum('nij,nm->nimj', w, eye).reshape(n * bi, n * bj)


def _layer(x2d, mem2d, bsz, s, mlen, p):
    t = bsz * s
    row = lambda a: a.reshape(1, -1)
    ones_blk = _block_diag(jnp.ones((RWKV_W // RWKV_HEAD, RWKV_HEAD, RWKV_HEAD), bf16))

    xl, gl, ur = _inproj(x2d, row(p['norm_mix_g']), p['w_in'].astype(bf16), tm=min(512, t))

    wg = jnp.concatenate([_block_diag(p['lru_wx']), _block_diag(p['lru_wa'])], axis=1).astype(bf16)
    bg = jnp.concatenate([p['lru_bx'], p['lru_ba']]).reshape(1, -1)
    y_lru = _lru(xl, gl, p['conv_w'], row(p['conv_b']), wg, bg, row(p['lru_lambda']), bsz, s, ts=min(512, s))

    pairs = lambda a: a.reshape(N_PAIRS, PAIR)
    rows8 = lambda rows: jnp.pad(jnp.stack(rows, axis=1), ((0, 0), (0, SUBLANES - len(rows)), (0, 0)))
    mu = p['rwkv_mu']
    shared = lambda a: jnp.broadcast_to(a, (N_PAIRS, PAIR))
    mu5 = rows8([pairs(mu[:RWKV_W]), pairs(mu[RWKV_W:2 * RWKV_W]), pairs(mu[2 * RWKV_W:3 * RWKV_W]),
                 shared(mu[3 * RWKV_W:3 * RWKV_W + PAIR]), shared(mu[3 * RWKV_W + PAIR:])])
    par5 = rows8([pairs(p['rwkv_w0']), pairs(p['rwkv_a0']), pairs(p['rwkv_k_k']), pairs(p['rwkv_k_a']),
                  pairs(p['rwkv_r_k'])])
    zl = jnp.zeros((DECAY_LORA, RWKV_W), f32)
    w_dec = jnp.concatenate([p['rwkv_w_up'], zl], axis=0).reshape(PAIR, N_PAIRS, PAIR)
    w_icl = jnp.concatenate([zl, p['rwkv_a_up']], axis=0).reshape(PAIR, N_PAIRS, PAIR)
    wproj = jnp.concatenate([w_dec, w_icl], axis=2).transpose(1, 0, 2).astype(bf16)
    gup = p['rwkv_g_up'].reshape(GATE_LORA, N_PAIRS, PAIR).transpose(1, 0, 2).astype(bf16)
    gm, sp, rc, yp, bon, v, g = _rwkv_chunk(ur, mu5, par5, wproj, gup, bsz, s, ts=min(1024, s))
    y_scan = _rwkv_state(gm, sp, rc, yp, bsz, s, nck=min(4, s // RWKV_CHUNK))

    x1 = _outproj(x2d, y_lru, y_scan, bon, v, g, row(p['rwkv_lnx_g']), row(p['rwkv_lnx_b']), ones_blk,
                  p['w_out'].astype(bf16), tm=min(512, t))

    kmem, vmem = _memkv(mem2d, row(p['norm_mem_g']), p['xa_wk'].astype(bf16), p['xa_wv'].astype(bf16),
                        tm=min(512, bsz * mlen))
    x2, hf, route, gates, cnt_pad = _xattn(x1, kmem, vmem, row(p['norm_xa_g']), p['xa_wq'].astype(bf16),
                                           p['xa_wo'].astype(bf16), row(p['norm_ffn_g']),
                                           p['w_router'].T.astype(bf16), p['b_router'].reshape(-1, 1),
                                           bsz, s, mlen, tm=min(512, s))

    counts = cnt_pad[:, 0].astype(i32)
    dest, pends, n_blocks = _routing(route[:TOP_K].T, route[TOP_K:].T, counts, t)
    gate_pad = jnp.pad(gates[:TOP_K].T, ((0, 0), (0, LANES - TOP_K)))
    xs = _dispatch(dest, pends, hf, n_blocks, tm=min(256, t))
    ys = _moe(pends, xs, p['w_gu'], p['b_gu'], p['w_dn'], p['b_dn'], n_blocks)
    return _combine(dest, ys, x2, gate_pad, row(p['final_norm_g']), tc=min(256, t))


def kernel(x, mem, norm_mix_g, w_in, conv_w, conv_b, lru_wx, lru_bx, lru_wa, lru_ba, lru_lambda, rwkv_mu, rwkv_w0, rwkv_w_up, rwkv_a0, rwkv_a_up, rwkv_g_up, rwkv_k_k, rwkv_k_a, rwkv_r_k, rwkv_lnx_g, rwkv_lnx_b, w_out, norm_xa_g, norm_mem_g, xa_wq, xa_wk, xa_wv, xa_wo, norm_ffn_g, w_router, b_router, w_gu, b_gu, w_dn, b_dn, final_norm_g):
    bsz, s, d = x.shape
    mlen = mem.shape[1]
    assert d == D_MODEL and w_in.shape[0] == 1
    p = dict(norm_mix_g=norm_mix_g[0], w_in=w_in[0], conv_w=conv_w[0], conv_b=conv_b[0], lru_wx=lru_wx[0],
             lru_bx=lru_bx[0], lru_wa=lru_wa[0], lru_ba=lru_ba[0], lru_lambda=lru_lambda[0],
             rwkv_mu=rwkv_mu[0], rwkv_w0=rwkv_w0[0], rwkv_w_up=rwkv_w_up[0], rwkv_a0=rwkv_a0[0],
             rwkv_a_up=rwkv_a_up[0], rwkv_g_up=rwkv_g_up[0], rwkv_k_k=rwkv_k_k[0], rwkv_k_a=rwkv_k_a[0],
             rwkv_r_k=rwkv_r_k[0].reshape(-1), rwkv_lnx_g=rwkv_lnx_g[0], rwkv_lnx_b=rwkv_lnx_b[0],
             w_out=w_out[0], norm_xa_g=norm_xa_g[0], norm_mem_g=norm_mem_g[0], xa_wq=xa_wq[0],
             xa_wk=xa_wk[0], xa_wv=xa_wv[0], xa_wo=xa_wo[0], norm_ffn_g=norm_ffn_g[0],
             w_router=w_router[0], b_router=b_router[0], w_gu=w_gu[0], b_gu=b_gu[0], w_dn=w_dn[0],
             b_dn=b_dn[0], final_norm_g=final_norm_g)
    out = _layer(x.reshape(bsz * s, d), mem.reshape(bsz * mlen, d), bsz, s, mlen, p)
    return out.reshape(bsz, s, d)
```

```python
import functools

import jax
import jax.numpy as jnp
from jax import lax
from jax.experimental import pallas as pl
from jax.experimental.pallas import tpu as pltpu

f32 = jnp.float32
bf16 = jnp.bfloat16
i32 = jnp.int32

D_MODEL = 1024
LRU_W = 512
RWKV_W = 512
LRU_BLOCKS = 8
LRU_BLOCK = 64
CONV_W = 4
LRU_C = 8.0
RWKV_HEAD = 64
DECAY_LORA = 64
AAA_LORA = 64
GATE_LORA = 128
RWKV_IN = 3 * RWKV_W + DECAY_LORA + AAA_LORA + GATE_LORA
XA_HEADS = 4
XA_HEAD = D_MODEL // XA_HEADS
N_EXPERTS = 32
TOP_K = 4
D_FF = D_MODEL
SWIGLU_LIMIT = 7.0
SWIGLU_ALPHA = 1.702
EPS = 1e-6
GN_EPS = 64e-5

LANES = 128
SUBLANES = 8
RWKV_CHUNK = 64
PAIR = 2 * RWKV_HEAD
N_PAIRS = RWKV_W // PAIR
MOE_BLOCK = 256
X_TILE = D_MODEL // (2 * LANES)
u32 = jnp.uint32
VMEM_LIMIT = 52 * 1024 * 1024


def _cparams(sem):
    return pltpu.CompilerParams(dimension_semantics=sem, vmem_limit_bytes=VMEM_LIMIT)


def _rms(x, g):
    return x * lax.rsqrt(jnp.mean(x * x, axis=-1, keepdims=True) + EPS) * g


def _full(shape):
    n = len(shape)
    return pl.BlockSpec(shape, lambda *a: (0,) * n)


def _shift_rows(x, prev8, d):
    xr = pltpu.roll(x, d, 0)
    tr = pltpu.roll(prev8, d, 0)
    row = lax.broadcasted_iota(i32, prev8.shape, 0)
    head = jnp.where(row < d, tr, xr[:SUBLANES])
    return jnp.concatenate([head, xr[SUBLANES:]], axis=0)


def _bf16_parts(x, n):
    parts = []
    for _ in range(n):
        piece = x.astype(bf16)
        parts.append(piece)
        x = x - piece.astype(f32)
    return parts


def _sum_dot(x, mask_bf, n_parts, mask_left=False):
    acc = None
    for piece in _bf16_parts(x, n_parts):
        d = (jnp.dot(mask_bf, piece, preferred_element_type=f32) if mask_left
             else jnp.dot(piece, mask_bf, preferred_element_type=f32))
        acc = d if acc is None else acc + d
    return acc


def _store_packed_rows(ref, val):
    n = val.shape[0]
    half = D_MODEL // 2
    lo = lax.bitcast_convert_type(val[:, :half].astype(bf16).astype(f32), u32) >> 16
    hi = lax.bitcast_convert_type(val[:, half:].astype(bf16).astype(f32), u32) & jnp.uint32(0xFFFF0000)
    words = hi | lo
    for j in range(X_TILE):
        ref[pl.ds(j, n, stride=X_TILE), :] = words[:, j * LANES:(j + 1) * LANES]


def _load_packed_rows(ref, n, off=0):
    ws = [ref[pl.ds(off + j, n, stride=X_TILE), :] for j in range(X_TILE)]
    lo = [lax.bitcast_convert_type(w << 16, f32).astype(bf16) for w in ws]
    hi = [lax.bitcast_convert_type(w & jnp.uint32(0xFFFF0000), f32).astype(bf16) for w in ws]
    return jnp.concatenate(lo + hi, axis=1)


def _inproj_body(x_ref, g_ref, w_ref, xl_ref, gl_ref, ur_ref):
    h = _rms(x_ref[...], g_ref[...])
    u = jnp.dot(h.astype(bf16), w_ref[...], preferred_element_type=f32)
    xl_ref[...] = u[:, :LRU_W]
    gl_ref[...] = u[:, LRU_W:2 * LRU_W]
    ur_ref[...] = u[:, 2 * LRU_W:]


def _inproj(x2d, g, w_in_bf, tm):
    t = x2d.shape[0]
    return pl.pallas_call(
        _inproj_body,
        grid=(t // tm,),
        in_specs=[pl.BlockSpec((tm, D_MODEL), lambda i: (i, 0)), _full((1, D_MODEL)),
                  _full(w_in_bf.shape)],
        out_specs=[pl.BlockSpec((tm, LRU_W), lambda i: (i, 0)),
                   pl.BlockSpec((tm, LRU_W), lambda i: (i, 0)),
                   pl.BlockSpec((tm, RWKV_IN), lambda i: (i, 0))],
        out_shape=[jax.ShapeDtypeStruct((t, LRU_W), f32), jax.ShapeDtypeStruct((t, LRU_W), f32),
                   jax.ShapeDtypeStruct((t, RWKV_IN), f32)],
        compiler_params=_cparams(("parallel",)),
        name="inproj",
    )(x2d, g, w_in_bf)


def _lru_body(xl_ref, gl_ref, cw_ref, cb_ref, wg_ref, bg_ref, lam_ref, o_ref, tail_ref, h_ref):
    ts = xl_ref.shape[0]

    @pl.when(pl.program_id(1) == 0)
    def _():
        tail_ref[...] = jnp.zeros_like(tail_ref)
        h_ref[...] = jnp.zeros_like(h_ref)

    x = xl_ref[...]
    tail = tail_ref[...]
    cw = cw_ref[...]
    xc = cb_ref[...] + cw[CONV_W - 1:CONV_W] * x
    for d in range(1, CONV_W):
        xc = xc + cw[CONV_W - 1 - d:CONV_W - d] * _shift_rows(x, tail, d)
    tail_ref[...] = x[ts - SUBLANES:]

    gates = jax.nn.sigmoid(jnp.dot(xc.astype(bf16), wg_ref[...], preferred_element_type=f32) + bg_ref[...])
    gx = gates[:, :LRU_W]
    ga = gates[:, LRU_W:]
    log_a = -LRU_C * ga * jax.nn.softplus(-lam_ref[...])
    a = jnp.exp(log_a)
    b = jnp.sqrt(-jnp.tanh(log_a) * (a * a + 1.0)) * gx * xc

    row = lax.broadcasted_iota(i32, (ts, LRU_W), 0) % SUBLANES
    d = 1
    while d < SUBLANES:
        keep = row >= d
        a_s = jnp.where(keep, pltpu.roll(a, d, 0), 1.0)
        b_s = jnp.where(keep, pltpu.roll(b, d, 0), 0.0)
        b = a * b_s + b
        a = a * a_s
        d *= 2
    carry = h_ref[SUBLANES - 1:SUBLANES, :]
    groups = []
    for q in range(ts // SUBLANES):
        rows = slice(q * SUBLANES, (q + 1) * SUBLANES)
        hq = b[rows] + a[rows] * carry
        carry = hq[SUBLANES - 1:SUBLANES, :]
        groups.append(hq)
    h = jnp.concatenate(groups, axis=0)
    h_ref[...] = jnp.broadcast_to(carry, h_ref.shape)
    o_ref[...] = (h * jax.nn.gelu(gl_ref[...])).astype(o_ref.dtype)


def _lru(xl, gl, cw, cb, wg_bf, bg, lam, bsz, s, ts):
    nt = s // ts
    blk = pl.BlockSpec((ts, LRU_W), lambda b, i: (b * nt + i, 0))
    return pl.pallas_call(
        _lru_body,
        grid=(bsz, nt),
        in_specs=[blk, blk, _full(cw.shape), _full(cb.shape), _full(wg_bf.shape), _full(bg.shape),
                  _full(lam.shape)],
        out_specs=blk,
        out_shape=jax.ShapeDtypeStruct((bsz * s, LRU_W), bf16),
        scratch_shapes=[pltpu.VMEM((SUBLANES, LRU_W), f32), pltpu.VMEM((SUBLANES, LRU_W), f32)],
        compiler_params=_cparams(("parallel", "arbitrary")),
        name="lru",
    )(xl, gl, cw, cb, wg_bf, bg, lam)


def _mm_x3(a, b_parts):
    ah, al = _bf16_parts(a, 2)
    bh, bl = b_parts
    d = lambda x, y: jnp.dot(x, y, preferred_element_type=f32)
    return d(ah, bh) + (d(ah, bl) + d(al, bh))


def _mm(a, b):
    return jnp.dot(a.astype(bf16), b.astype(bf16), preferred_element_type=f32)


def _mm_nt(a, b):
    return lax.dot_general(a.astype(bf16), b.astype(bf16), (((1,), (1,)), ((), ())), preferred_element_type=f32)


def _mm_tn(a, b):
    return lax.dot_general(a.astype(bf16), b.astype(bf16), (((0,), (0,)), ((), ())), preferred_element_type=f32)


def _bd(x):
    m0 = lax.broadcasted_iota(i32, x.shape, 1) < RWKV_HEAD
    zero = jnp.zeros_like(x)
    return jnp.concatenate([jnp.where(m0, x, zero), jnp.where(m0, zero, x)], axis=0)


def _side_by_side(d):
    h = d.shape[0] // 2
    m0 = lax.broadcasted_iota(i32, (h, d.shape[1]), 1) < RWKV_HEAD
    return jnp.where(m0, d[:h], d[h:])


def _chunk_maps(chunks):
    c = RWKV_CHUNK
    ri = lax.broadcasted_iota(i32, (c, PAIR), 0)
    ji = lax.broadcasted_iota(i32, (c, PAIR), 1) % RWKV_HEAD
    strict = ji < ri
    incl = ji <= ri
    diag = ji == ri
    eye = jnp.where(diag, 1.0, 0.0).astype(f32)
    each = lambda f, *ls: [f(*xs) for xs in zip(*ls)]
    cat0 = lambda *xs: jnp.concatenate(xs, axis=0)
    cat1 = lambda *xs: jnp.concatenate(xs, axis=1)
    tb = lambda x: x.astype(bf16)

    ats, bts, kts, rts, vs, cls = [list(x) for x in zip(*chunks)]
    pcs = each(lambda cl: jnp.exp(cl[c - 1:c, :]), cls)
    bd_a = each(lambda x: tb(_bd(x)), ats)
    bd_v = each(lambda x: tb(_bd(x)), vs)

    aa = each(lambda a, r, b, k: _mm_nt(cat0(tb(a), tb(r)), cat0(tb(_bd(b)), tb(_bd(k)))), ats, rts, bts, kts)
    l_ab = each(lambda x: tb(jnp.where(strict, x[:c, :PAIR], 0.0)), aa)
    a_k = each(lambda x: tb(cat0(jnp.where(strict, x[:c, PAIR:], 0.0), jnp.where(incl, x[c:, PAIR:], 0.0))), aa)
    a_rb = each(lambda x: tb(jnp.where(incl, x[c:, :PAIR], 0.0)), aa)

    tinv = each(lambda x: eye + x, l_ab)
    lp = each(lambda x: tb(_mm(x, _bd(x))), l_ab)
    p = 2
    while 2 * p < c:
        x2 = each(lambda t, x: _mm(cat0(tb(t), x), _bd(x)), tinv, lp)
        tinv = each(lambda t, x: t + x[:c], tinv, x2)
        lp = each(lambda x: tb(x[c:]), x2)
        p *= 2
    tinv = each(lambda t, x: t + _mm(t, _bd(x)), tinv, lp)

    wy = each(_mm, a_k, bd_v)
    za = each(lambda t, w, a: _mm(t, cat1(tb(_bd(w[:c])), a)), tinv, wy, bd_a)
    zp = each(lambda x: x[:, :PAIR], za)
    ac = each(lambda x: x[:, PAIR:], za)
    y1 = each(lambda ar, z, a: _mm(ar, cat1(tb(_bd(z)), tb(_bd(a)))), a_rb, zp, ac)
    yp = each(lambda w, y: w[c:] + y[:, :PAIR], wy, y1)
    rc = each(lambda r, y: r + y[:, PAIR:], rts, y1)
    sm = each(lambda b, pc, z, a: _mm_tn(b * pc, cat1(z, a)), bts, pcs, zp, ac)
    kv = each(lambda k, pc, v: _mm_tn(k * pc, v), kts, pcs, vs)
    sp = each(lambda x, m: _side_by_side(x) + _side_by_side(m[:, :PAIR]), kv, sm)
    g = each(lambda pc, m: jnp.where(diag, jnp.broadcast_to(pc, (c, PAIR)), 0.0) + _side_by_side(m[:, PAIR:]), pcs, sm)
    return list(zip(g, sp, rc, yp))


N_UCOLS = 5


def _rwkv_chunk_body(ur_r, ur_k, ur_v, ur_lo, ur_dg, mu_ref, par_ref, wproj_ref, gup_ref, ones_ref, tri_ref,
                     g_ref, sp_ref, rc_ref, yp_ref, bon_ref, v_ref, gate_ref, prev_ref):
    ts = ur_r.shape[0]
    c = RWKV_CHUNK

    @pl.when(pl.program_id(2) == 0)
    def _():
        prev_ref[...] = jnp.zeros_like(prev_ref)

    mixed = []
    for j, ref in enumerate((ur_r, ur_k, ur_v, ur_lo, ur_dg)):
        u0 = ref[...]
        ls = slice(j * LANES, (j + 1) * LANES)
        us = _shift_rows(u0, prev_ref[:, ls], 1)
        prev_ref[:, ls] = u0[ts - SUBLANES:]
        mixed.append(u0 + (us - u0) * mu_ref[0, j:j + 1, :])
    r, k, v, lora, dg = mixed
    w0, a0, k_k, k_a, r_k = [par_ref[0, j:j + 1, :] for j in range(5)]

    lane = lax.broadcasted_iota(i32, lora.shape, 1)
    lora = jnp.where(lane < DECAY_LORA, jnp.tanh(lora), lora)
    proj = jnp.dot(lora.astype(bf16), wproj_ref[0], preferred_element_type=f32)
    w = -jax.nn.softplus(-(w0 + proj[:, :PAIR])) - 0.5
    lw = -jnp.exp(w)
    a = jax.nn.sigmoid(a0 + proj[:, PAIR:])
    gate_ref[...] = jnp.dot(jax.nn.sigmoid(dg).astype(bf16), gup_ref[0], preferred_element_type=f32)

    ones = ones_ref[...]
    kk = k * k_k
    ss = _sum_dot(kk * kk, ones, 1)
    kk = kk / jnp.maximum(jnp.sqrt(ss), 1e-12)
    k2 = k * (1.0 + (a - 1.0) * k_a)
    bon_ref[...] = _sum_dot(r * k2 * r_k, ones, 2)
    v_ref[...] = v

    tri = tri_ref[...]
    grp = tri.shape[0]
    cl = jnp.concatenate([_sum_dot(lw[q * grp:(q + 1) * grp], tri, 3, mask_left=True) for q in range(ts // grp)],
                         axis=0)
    e_neg = jnp.exp(-cl)
    at = -kk * jnp.exp(cl - lw)
    bt = kk * a * e_neg
    kt = k2 * e_neg
    rt = r * jnp.exp(cl)

    sls = [slice(j * c, (j + 1) * c) for j in range(ts // c)]
    outs = _chunk_maps([(at[sl], bt[sl], kt[sl], rt[sl], v[sl], cl[sl]) for sl in sls])
    for sl, (g, sp, rc, yp) in zip(sls, outs):
        g_ref[sl, :] = g
        sp_ref[sl, :] = sp
        rc_ref[sl, :] = rc
        yp_ref[sl, :] = yp


def _rwkv_chunk(ur, mu5, par5, wproj, gup, bsz, s, ts):
    t = bsz * s
    nt = s // ts
    col = lambda blk: pl.BlockSpec((ts, PAIR), lambda p, b, i, blk=blk: (b * nt + i, blk(p)))
    nrw = RWKV_W // PAIR
    ucols = [col(lambda p: p), col(lambda p: nrw + p), col(lambda p: 2 * nrw + p), col(lambda p: 3 * nrw),
             col(lambda p: 3 * nrw + 1)]
    per_pair = lambda a: pl.BlockSpec((1,) + a.shape[1:], lambda p, b, i: (p, 0, 0))
    ones_pair = _block_diag(jnp.ones((2, RWKV_HEAD, RWKV_HEAD), bf16))
    tri = _block_diag(jnp.tril(jnp.ones((2, RWKV_CHUNK, RWKV_CHUNK), bf16)))
    oblk = pl.BlockSpec((ts, PAIR), lambda p, b, i: (b * nt + i, p))
    osh = jax.ShapeDtypeStruct((t, RWKV_W), f32)
    return pl.pallas_call(
        _rwkv_chunk_body,
        grid=(N_PAIRS, bsz, nt),
        in_specs=ucols + [per_pair(mu5), per_pair(par5), per_pair(wproj), per_pair(gup), _full(ones_pair.shape),
                          _full(tri.shape)],
        out_specs=[oblk] * 7,
        out_shape=[osh] * 7,
        scratch_shapes=[pltpu.VMEM((SUBLANES, N_UCOLS * LANES), f32)],
        compiler_params=_cparams(("parallel", "parallel", "arbitrary")),
        name="rwkv_chunk",
    )(ur, ur, ur, ur, ur, mu5, par5, wproj, gup, ones_pair, tri)


def _rwkv_state_body(g_ref, sp_ref, rc_ref, yp_ref, y_ref, s_ref):
    @pl.when(pl.program_id(0) == 0)
    def _():
        s_ref[...] = jnp.zeros_like(s_ref)

    c = RWKV_CHUNK
    bsz = g_ref.shape[0]
    nck = g_ref.shape[1] // c
    chains = [(b, slice(p * PAIR, (p + 1) * PAIR)) for b in range(bsz) for p in range(N_PAIRS)]
    states = [s_ref[b, :, ls] for b, ls in chains]
    for j in range(nck):
        sl = slice(j * c, (j + 1) * c)
        prods = [_mm_x3(jnp.concatenate([rc_ref[b, sl, ls], g_ref[b, sl, ls]], axis=0), _bf16_parts(_bd(s), 2))
                 for (b, ls), s in zip(chains, states)]
        for (b, ls), pr in zip(chains, prods):
            y_ref[b, sl, ls] = yp_ref[b, sl, ls] + pr[:c]
        states = [pr[c:] + sp_ref[b, sl, ls] for (b, ls), pr in zip(chains, prods)]
    for (b, ls), s in zip(chains, states):
        s_ref[b, :, ls] = s


def _rwkv_state(g, sp, rc, yp, bsz, s, nck):
    rows = nck * RWKV_CHUNK
    rblk = pl.BlockSpec((bsz, rows, RWKV_W), lambda i: (0, i, 0))
    r3 = lambda a: a.reshape(bsz, s, RWKV_W)
    y = pl.pallas_call(
        _rwkv_state_body,
        grid=(s // rows,),
        in_specs=[rblk] * 4,
        out_specs=rblk,
        out_shape=jax.ShapeDtypeStruct((bsz, s, RWKV_W), f32),
        scratch_shapes=[pltpu.VMEM((bsz, RWKV_CHUNK, RWKV_W), f32)],
        compiler_params=_cparams(("arbitrary",)),
        name="rwkv_state",
    )(r3(g), r3(sp), r3(rc), r3(yp))
    return y.reshape(bsz * s, RWKV_W)


def _outproj_body(x_ref, yl_ref, ys_ref, bon_ref, v_ref, g_ref, lg_ref, lb_ref, ones_ref, w_ref, o_ref):
    y = ys_ref[...]
    ones = ones_ref[...]
    inv_n = 1.0 / RWKV_HEAD
    mean = _sum_dot(y, ones, 2) * inv_n
    yc = y - mean
    var = _sum_dot(yc * yc, ones, 1) * inv_n
    yn = yc * lax.rsqrt(var + GN_EPS) * lg_ref[...] + lb_ref[...]
    yr = (yn + bon_ref[...] * v_ref[...]) * g_ref[...]
    cat = jnp.concatenate([yl_ref[...], yr.astype(bf16)], axis=1)
    o_ref[...] = x_ref[...] + jnp.dot(cat, w_ref[...], preferred_element_type=f32)


def _outproj(x2d, y_lru, y_scan, bon, v, g, lnx_g, lnx_b, ones_blk, w_out_bf, tm):
    t = x2d.shape[0]
    xb = pl.BlockSpec((tm, D_MODEL), lambda i: (i, 0))
    hb = pl.BlockSpec((tm, RWKV_W), lambda i: (i, 0))
    return pl.pallas_call(
        _outproj_body,
        grid=(t // tm,),
        in_specs=[xb, hb, hb, hb, hb, hb, _full(lnx_g.shape), _full(lnx_b.shape), _full(ones_blk.shape),
                  _full(w_out_bf.shape)],
        out_specs=xb,
        out_shape=jax.ShapeDtypeStruct((t, D_MODEL), f32),
        compiler_params=_cparams(("parallel",)),
        name="outproj",
    )(x2d, y_lru, y_scan, bon, v, g, lnx_g, lnx_b, ones_blk, w_out_bf)


def _memkv_body(m_ref, g_ref, wk_ref, wv_ref, k_ref, v_ref):
    h = _rms(m_ref[...], g_ref[...]).astype(bf16)
    k_ref[...] = jnp.dot(h, wk_ref[...], preferred_element_type=f32).astype(bf16)
    v_ref[...] = jnp.dot(h, wv_ref[...], preferred_element_type=f32).astype(bf16)


def _memkv(mem2d, g, wk_bf, wv_bf, tm):
    t = mem2d.shape[0]
    blk = pl.BlockSpec((tm, D_MODEL), lambda i: (i, 0))
    sh = jax.ShapeDtypeStruct((t, D_MODEL), bf16)
    return pl.pallas_call(
        _memkv_body,
        grid=(t // tm,),
        in_specs=[blk, _full(g.shape), _full(wk_bf.shape), _full(wv_bf.shape)],
        out_specs=[blk, blk],
        out_shape=[sh, sh],
        compiler_params=_cparams(("parallel",)),
        name="memkv",
    )(mem2d, g, wk_bf, wv_bf)


def _xattn_body(x_ref, k_ref, v_ref, gx_ref, wq_ref, wo_ref, gf_ref, wr_ref, br_ref, upper_ref,
                x2_ref, hf_ref, idx_ref, gate_ref, cnt_ref, base_ref):
    x = x_ref[...]
    h = _rms(x, gx_ref[...]).astype(bf16)
    q = jnp.dot(h, wq_ref[...], preferred_element_type=f32).astype(bf16)
    k = k_ref[...]
    v = v_ref[...]
    heads = [slice(hd * XA_HEAD, (hd + 1) * XA_HEAD) for hd in range(XA_HEADS)]
    scs = [lax.dot_general(q[:, sl], k[:, sl], (((1,), (1,)), ((), ())), preferred_element_type=f32)
           * (XA_HEAD ** -0.5) for sl in heads]
    ps = []
    for sc in scs:
        e = jnp.exp(sc - jnp.max(sc, axis=-1, keepdims=True))
        ps.append((e / jnp.sum(e, axis=-1, keepdims=True)).astype(bf16))
    o = jnp.concatenate([jnp.dot(p, v[:, sl], preferred_element_type=f32).astype(bf16)
                         for p, sl in zip(ps, heads)], axis=1)
    x2 = x + jnp.dot(o, wo_ref[...], preferred_element_type=f32)
    x2_ref[...] = x2

    hf = _rms(x2, gf_ref[...])
    _store_packed_rows(hf_ref, hf)
    logits = lax.dot_general(wr_ref[...], hf.astype(bf16), (((1,), (1,)), ((), ())),
                             preferred_element_type=f32) + br_ref[...]
    erow = lax.broadcasted_iota(i32, logits.shape, 0)
    neg = jnp.float32(-jnp.inf)
    cur = logits
    vals = []
    idxs = []
    for _ in range(TOP_K):
        m = jnp.max(cur, axis=0, keepdims=True)
        am = jnp.min(jnp.where(cur == m, erow, N_EXPERTS), axis=0, keepdims=True)
        vals.append(m)
        idxs.append(am)
        cur = jnp.where(erow == am, neg, cur)
    es = [jnp.exp(vk - vals[0]) for vk in vals]
    den = es[0] + es[1] + es[2] + es[3]

    @pl.when(pl.program_id(0) == 0)
    def _():
        base_ref[...] = jnp.zeros_like(base_ref)

    onehot = [jnp.where(erow == am, 1.0, 0.0) for am in idxs]
    cnt = (onehot[0] + onehot[1]) + (onehot[2] + onehot[3])
    base = base_ref[:, 0:1]
    prior = jnp.dot(cnt.astype(bf16), upper_ref[...], preferred_element_type=f32) + base
    base_ref[...] = jnp.broadcast_to(base + jnp.sum(cnt, axis=1, keepdims=True), base_ref.shape)
    cnt_ref[...] = base_ref[...]

    orow = lax.broadcasted_iota(i32, idx_ref.shape, 0)
    idx_out = jnp.zeros(idx_ref.shape, i32)
    gate_out = jnp.zeros(gate_ref.shape, f32)
    for kq in range(TOP_K):
        rank = jnp.sum(prior * onehot[kq], axis=0, keepdims=True).astype(i32)
        idx_out = jnp.where(orow == kq, idxs[kq], idx_out)
        idx_out = jnp.where(orow == TOP_K + kq, rank, idx_out)
        gate_out = jnp.where(orow == kq, es[kq] / den, gate_out)
    idx_ref[...] = idx_out
    gate_ref[...] = gate_out


def _xattn(x1, kmem, vmem, g_xa, wq_bf, wo_bf, g_ffn, wr_t, br_col, bsz, s, mlen, tm):
    t = bsz * s
    nt = s // tm
    xb = pl.BlockSpec((tm, D_MODEL), lambda i: (i, 0))
    mb = pl.BlockSpec((mlen, D_MODEL), lambda i: (i // nt, 0))
    lb = pl.BlockSpec((2 * TOP_K, tm), lambda i: (0, i))
    upper = jnp.triu(jnp.ones((tm, tm), bf16), k=1)
    return pl.pallas_call(
        _xattn_body,
        grid=(t // tm,),
        in_specs=[xb, mb, mb, _full(g_xa.shape), _full(wq_bf.shape), _full(wo_bf.shape),
                  _full(g_ffn.shape), _full(wr_t.shape), _full(br_col.shape), _full(upper.shape)],
        out_specs=[xb, pl.BlockSpec((tm * X_TILE, LANES), lambda i: (i, 0)), lb, lb, _full((N_EXPERTS, LANES))],
        out_shape=[jax.ShapeDtypeStruct((t, D_MODEL), f32), jax.ShapeDtypeStruct((t * X_TILE, LANES), u32),
                   jax.ShapeDtypeStruct((2 * TOP_K, t), i32), jax.ShapeDtypeStruct((2 * TOP_K, t), f32),
                   jax.ShapeDtypeStruct((N_EXPERTS, LANES), f32)],
        scratch_shapes=[pltpu.VMEM((N_EXPERTS, LANES), f32)],
        compiler_params=_cparams(("arbitrary",)),
        name="xattn",
    )(x1, kmem, vmem, g_xa, wq_bf, wo_bf, g_ffn, wr_t, br_col, upper)


DMA_UNROLL = 8


def _tile_at(ref, row):
    return ref.at[pl.ds(pl.multiple_of(row * X_TILE, X_TILE), X_TILE), :]


def _dispatch_body(dest_ref, pend_ref, hf_ref, xs_hbm, zeros_ref, sem, zsem):
    i = pl.program_id(0)
    tm = hf_ref.shape[0] // X_TILE
    zrows = MOE_BLOCK * X_TILE

    def zero_copy(e):
        start = pl.multiple_of((pend_ref[e] - MOE_BLOCK) * X_TILE, X_TILE)
        return pltpu.make_async_copy(zeros_ref, xs_hbm.at[pl.ds(start, zrows), :], zsem)

    def nonempty(e):
        return pend_ref[e] > (pend_ref[e - 1] if e else 0)

    @pl.when(i == 0)
    def _():
        zeros_ref[...] = jnp.zeros_like(zeros_ref)
        for e in range(N_EXPERTS):
            @pl.when(nonempty(e))
            def _():
                zero_copy(e).start()
        for e in range(N_EXPERTS):
            @pl.when(nonempty(e))
            def _():
                zero_copy(e).wait()

        def tail_copy(b):
            return pltpu.make_async_copy(zeros_ref, xs_hbm.at[pl.ds(pl.multiple_of(b * zrows, zrows), zrows), :], zsem)

        def tail_start(b, carry):
            tail_copy(b).start()
            return carry

        def tail_wait(b, carry):
            tail_copy(b).wait()
            return carry
        n_used = pend_ref[N_EXPERTS - 1] // MOE_BLOCK
        n_all = xs_hbm.shape[0] // zrows
        lax.fori_loop(n_used, n_all, tail_start, 0)
        lax.fori_loop(n_used, n_all, tail_wait, 0)

    def body(q, carry):
        for u in range(DMA_UNROLL):
            r = q * DMA_UNROLL + u
            src = _tile_at(hf_ref, r)
            for kq in range(TOP_K):
                pltpu.make_async_copy(src, _tile_at(xs_hbm, dest_ref[(i * tm + r) * TOP_K + kq]),
                                      sem).start(priority=kq % 2)
        return carry
    lax.fori_loop(0, tm // DMA_UNROLL, body, 0)
    for kq in range(TOP_K):
        pltpu.make_async_copy(hf_ref, xs_hbm.at[pl.ds(0, tm * X_TILE), :], sem).wait()


def _dispatch(dest_flat, pends, hf_tiles, n_blocks, tm):
    t = hf_tiles.shape[0] // X_TILE
    rows = n_blocks * MOE_BLOCK
    grid_spec = pltpu.PrefetchScalarGridSpec(
        num_scalar_prefetch=2,
        grid=(t // tm,),
        in_specs=[pl.BlockSpec((tm * X_TILE, LANES), lambda i, d, pe: (i, 0))],
        out_specs=pl.BlockSpec(memory_space=pl.ANY),
        scratch_shapes=[pltpu.VMEM((MOE_BLOCK * X_TILE, LANES), u32), pltpu.SemaphoreType.DMA(()),
                        pltpu.SemaphoreType.DMA(())],
    )
    return pl.pallas_call(
        _dispatch_body,
        grid_spec=grid_spec,
        out_shape=jax.ShapeDtypeStruct((rows * X_TILE, LANES), u32),
        compiler_params=_cparams(("arbitrary",)),
        name="dispatch",
    )(dest_flat, pends, hf_tiles)


def _moe_body(pend_ref, xs_hbm, wgu_ref, bgu_ref, wdn_ref, bdn_ref, ys_hbm, xbuf, ybuf, wgu_bf, wdn_bf,
              sem_in, sem_out):
    e = pl.program_id(0)
    blk = MOE_BLOCK * X_TILE
    end_blk = pend_ref[e] // MOE_BLOCK
    start_blk = jnp.where(e == 0, 0, pend_ref[jnp.maximum(e - 1, 0)] // MOE_BLOCK)
    nb = end_blk - start_blk

    def rows_of(j):
        return pl.ds(pl.multiple_of((start_blk + j) * blk, blk), blk)

    def x_copy(j, slot):
        return pltpu.make_async_copy(xs_hbm.at[rows_of(j), :], xbuf.at[slot], sem_in.at[slot])

    def y_copy(j, slot):
        return pltpu.make_async_copy(ybuf.at[slot], ys_hbm.at[rows_of(j), :], sem_out.at[slot])

    @pl.when(nb > 0)
    def _():
        x_copy(0, 0).start(priority=1)
        wgu_bf[...] = wgu_ref[0].astype(bf16)
        wdn_bf[...] = wdn_ref[0].astype(bf16)

    def block(j, carry):
        slot = j % 2

        @pl.when(j + 1 < nb)
        def _():
            x_copy(j + 1, 1 - slot).start(priority=1)

        x_copy(j, slot).wait()

        @pl.when(j >= 2)
        def _():
            y_copy(j - 2, slot).wait()

        xb = _load_packed_rows(xbuf.at[slot], MOE_BLOCK)
        gu = jnp.dot(xb, wgu_bf[...], preferred_element_type=f32) + bgu_ref[0]
        gate = jnp.minimum(gu[:, :D_FF], SWIGLU_LIMIT)
        up = jnp.clip(gu[:, D_FF:], -SWIGLU_LIMIT, SWIGLU_LIMIT)
        act = (up + 1.0) * (gate * jax.nn.sigmoid(SWIGLU_ALPHA * gate))
        y = jnp.dot(act.astype(bf16), wdn_bf[...], preferred_element_type=f32) + bdn_ref[0]
        _store_packed_rows(ybuf.at[slot], y)
        y_copy(j, slot).start(priority=1)
        return carry

    lax.fori_loop(0, nb, block, 0)

    @pl.when(nb >= 2)
    def _():
        y_copy(nb - 2, nb % 2).wait()

    @pl.when(nb >= 1)
    def _():
        y_copy(nb - 1, (nb - 1) % 2).wait()

    @pl.when(e == N_EXPERTS - 1)
    def _():
        n_all = ys_hbm.shape[0] // blk
        ybuf[0] = jnp.zeros(ybuf.shape[1:], ybuf.dtype)

        def tail_copy(b):
            return pltpu.make_async_copy(ybuf.at[0], ys_hbm.at[pl.ds(pl.multiple_of(b * blk, blk), blk), :],
                                         sem_out.at[0])

        def tail_start(b, carry):
            tail_copy(b).start()
            return carry

        def tail_wait(b, carry):
            tail_copy(b).wait()
            return carry
        lax.fori_loop(end_blk, n_all, tail_start, 0)
        lax.fori_loop(end_blk, n_all, tail_wait, 0)


def _moe(pends, xs_tiles, w_gu, b_gu, w_dn, b_dn, n_blocks):
    rows = n_blocks * MOE_BLOCK
    grid_spec = pltpu.PrefetchScalarGridSpec(
        num_scalar_prefetch=1,
        grid=(N_EXPERTS,),
        in_specs=[
            pl.BlockSpec(memory_space=pl.ANY),
            pl.BlockSpec((1, D_MODEL, 2 * D_FF), lambda e, pe: (e, 0, 0)),
            pl.BlockSpec((1, 1, 2 * D_FF), lambda e, pe: (e, 0, 0)),
            pl.BlockSpec((1, D_FF, D_MODEL), lambda e, pe: (e, 0, 0)),
            pl.BlockSpec((1, 1, D_MODEL), lambda e, pe: (e, 0, 0)),
        ],
        out_specs=pl.BlockSpec(memory_space=pl.ANY),
        scratch_shapes=[pltpu.VMEM((2, MOE_BLOCK * X_TILE, LANES), u32),
                        pltpu.VMEM((2, MOE_BLOCK * X_TILE, LANES), u32),
                        pltpu.VMEM((D_MODEL, 2 * D_FF), bf16), pltpu.VMEM((D_FF, D_MODEL), bf16),
                        pltpu.SemaphoreType.DMA((2,)), pltpu.SemaphoreType.DMA((2,))],
    )
    return pl.pallas_call(
        _moe_body,
        grid_spec=grid_spec,
        out_shape=jax.ShapeDtypeStruct((rows * X_TILE, LANES), u32),
        compiler_params=_cparams(("arbitrary",)),
        name="moe",
    )(pends, xs_tiles, w_gu, b_gu.reshape(N_EXPERTS, 1, 2 * D_FF), w_dn, b_dn.reshape(N_EXPERTS, 1, D_MODEL))


def _combine_body(pos_ref, ys_hbm, x_ref, gate_ref, g_ref, o_ref, buf, sem):
    i = pl.program_id(0)
    n = pl.num_programs(0)
    tc = x_ref.shape[0]
    slot = i % 2
    slot_rows = TOP_K * tc * X_TILE

    def start(step, sl):
        def body(q, carry):
            for u in range(DMA_UNROLL):
                r = q * DMA_UNROLL + u
                for kq in range(TOP_K):
                    dst = _tile_at(buf, (sl * TOP_K + kq) * tc + r)
                    pltpu.make_async_copy(_tile_at(ys_hbm, pos_ref[(step * tc + r) * TOP_K + kq]), dst,
                                          sem.at[sl]).start(priority=kq % 2)
            return carry
        lax.fori_loop(0, tc // DMA_UNROLL, body, 0)

    @pl.when(i == 0)
    def _():
        start(0, 0)

    @pl.when(i + 1 < n)
    def _():
        start(i + 1, 1 - slot)

    off = pl.multiple_of(slot * slot_rows, slot_rows)
    pltpu.make_async_copy(ys_hbm.at[pl.ds(0, slot_rows), :], buf.at[pl.ds(off, slot_rows), :], sem.at[slot]).wait()
    gates = gate_ref[...]
    acc = x_ref[...]
    for kq in range(TOP_K):
        rows = _load_packed_rows(buf, tc, off + kq * tc * X_TILE)
        acc = acc + gates[:, kq:kq + 1] * rows.astype(f32)
    o_ref[...] = _rms(acc, g_ref[...])


def _combine(pos_flat, ys_tiles, x2, gate_pad, g_final, tc):
    t = x2.shape[0]
    grid_spec = pltpu.PrefetchScalarGridSpec(
        num_scalar_prefetch=1,
        grid=(t // tc,),
        in_specs=[pl.BlockSpec(memory_space=pl.ANY),
                  pl.BlockSpec((tc, D_MODEL), lambda i, p: (i, 0)),
                  pl.BlockSpec((tc, LANES), lambda i, p: (i, 0)),
                  pl.BlockSpec((1, D_MODEL), lambda i, p: (0, 0))],
        out_specs=pl.BlockSpec((tc, D_MODEL), lambda i, p: (i, 0)),
        scratch_shapes=[pltpu.VMEM((2 * TOP_K * tc * X_TILE, LANES), u32), pltpu.SemaphoreType.DMA((2,))],
    )
    return pl.pallas_call(
        _combine_body,
        grid_spec=grid_spec,
        out_shape=jax.ShapeDtypeStruct((t, D_MODEL), f32),
        compiler_params=_cparams(("arbitrary",)),
        name="combine",
    )(pos_flat, ys_tiles, x2, gate_pad, g_final)


def _routing(top_idx, rank, counts, t):
    n_assign = t * TOP_K
    experts = jnp.arange(N_EXPERTS, dtype=i32)
    padded = (counts + MOE_BLOCK - 1) // MOE_BLOCK * MOE_BLOCK
    pends = jnp.cumsum(padded).astype(i32)
    pstarts = pends - padded
    start_of = jnp.sum(jnp.where(top_idx[:, :, None] == experts, pstarts, 0), axis=-1)
    dest = (start_of + rank).astype(i32).reshape(n_assign)
    n_blocks = (n_assign + N_EXPERTS * (MOE_BLOCK - 1) + MOE_BLOCK - 1) // MOE_BLOCK
    return dest, pends, n_blocks


def _block_diag(w):
    n, bi, bj = w.shape
    eye = jnp.eye(n, dtype=w.dtype)
    return jnp.einsum('nij,nm->nimj', w, eye).reshape(n * bi, n * bj)


def _layer(x2d, mem2d, bsz, s, mlen, p):
    t = bsz * s
    row = lambda a: a.reshape(1, -1)
    ones_blk = _block_diag(jnp.ones((RWKV_W // RWKV_HEAD, RWKV_HEAD, RWKV_HEAD), bf16))

    xl, gl, ur = _inproj(x2d, row(p['norm_mix_g']), p['w_in'].astype(bf16), tm=min(512, t))

    wg = jnp.concatenate([_block_diag(p['lru_wx']), _block_diag(p['lru_wa'])], axis=1).astype(bf16)
    bg = jnp.concatenate([p['lru_bx'], p['lru_ba']]).reshape(1, -1)
    y_lru = _lru(xl, gl, p['conv_w'], row(p['conv_b']), wg, bg, row(p['lru_lambda']), bsz, s, ts=min(512, s))

    pairs = lambda a: a.reshape(N_PAIRS, PAIR)
    rows8 = lambda rows: jnp.pad(jnp.stack(rows, axis=1), ((0, 0), (0, SUBLANES - len(rows)), (0, 0)))
    mu = p['rwkv_mu']
    shared = lambda a: jnp.broadcast_to(a, (N_PAIRS, PAIR))
    mu5 = rows8([pairs(mu[:RWKV_W]), pairs(mu[RWKV_W:2 * RWKV_W]), pairs(mu[2 * RWKV_W:3 * RWKV_W]),
                 shared(mu[3 * RWKV_W:3 * RWKV_W + PAIR]), shared(mu[3 * RWKV_W + PAIR:])])
    par5 = rows8([pairs(p['rwkv_w0']), pairs(p['rwkv_a0']), pairs(p['rwkv_k_k']), pairs(p['rwkv_k_a']),
                  pairs(p['rwkv_r_k'])])
    zl = jnp.zeros((DECAY_LORA, RWKV_W), f32)
    w_dec = jnp.concatenate([p['rwkv_w_up'], zl], axis=0).reshape(PAIR, N_PAIRS, PAIR)
    w_icl = jnp.concatenate([zl, p['rwkv_a_up']], axis=0).reshape(PAIR, N_PAIRS, PAIR)
    wproj = jnp.concatenate([w_dec, w_icl], axis=2).transpose(1, 0, 2).astype(bf16)
    gup = p['rwkv_g_up'].reshape(GATE_LORA, N_PAIRS, PAIR).transpose(1, 0, 2).astype(bf16)
    gm, sp, rc, yp, bon, v, g = _rwkv_chunk(ur, mu5, par5, wproj, gup, bsz, s, ts=min(1024, s))
    y_scan = _rwkv_state(gm, sp, rc, yp, bsz, s, nck=min(4, s // RWKV_CHUNK))

    x1 = _outproj(x2d, y_lru, y_scan, bon, v, g, row(p['rwkv_lnx_g']), row(p['rwkv_lnx_b']), ones_blk,
                  p['w_out'].astype(bf16), tm=min(512, t))

    kmem, vmem = _memkv(mem2d, row(p['norm_mem_g']), p['xa_wk'].astype(bf16), p['xa_wv'].astype(bf16),
                        tm=min(512, bsz * mlen))
    x2, hf, route, gates, cnt_pad = _xattn(x1, kmem, vmem, row(p['norm_xa_g']), p['xa_wq'].astype(bf16),
                                           p['xa_wo'].astype(bf16), row(p['norm_ffn_g']),
                                           p['w_router'].T.astype(bf16), p['b_router'].reshape(-1, 1),
                                           bsz, s, mlen, tm=min(512, s))

    counts = cnt_pad[:, 0].astype(i32)
    dest, pends, n_blocks = _routing(route[:TOP_K].T, route[TOP_K:].T, counts, t)
    gate_pad = jnp.pad(gates[:TOP_K].T, ((0, 0), (0, LANES - TOP_K)))
    xs = _dispatch(dest, pends, hf, n_blocks, tm=min(256, t))
    ys = _moe(pends, xs, p['w_gu'], p['b_gu'], p['w_dn'], p['b_dn'], n_blocks)
    return _combine(dest, ys, x2, gate_pad, row(p['final_norm_g']), tc=min(256, t))


def kernel(x, mem, norm_mix_g, w_in, conv_w, conv_b, lru_wx, lru_bx, lru_wa, lru_ba, lru_lambda, rwkv_mu, rwkv_w0, rwkv_w_up, rwkv_a0, rwkv_a_up, rwkv_g_up, rwkv_k_k, rwkv_k_a, rwkv_r_k, rwkv_lnx_g, rwkv_lnx_b, w_out, norm_xa_g, norm_mem_g, xa_wq, xa_wk, xa_wv, xa_wo, norm_ffn_g, w_router, b_router, w_gu, b_gu, w_dn, b_dn, final_norm_g):
    bsz, s, d = x.shape
    mlen = mem.shape[1]
    assert d == D_MODEL and w_in.shape[0] == 1
    p = dict(norm_mix_g=norm_mix_g[0], w_in=w_in[0], conv_w=conv_w[0], conv_b=conv_b[0], lru_wx=lru_wx[0],
             lru_bx=lru_bx[0], lru_wa=lru_wa[0], lru_ba=lru_ba[0], lru_lambda=lru_lambda[0],
             rwkv_mu=rwkv_mu[0], rwkv_w0=rwkv_w0[0], rwkv_w_up=rwkv_w_up[0], rwkv_a0=rwkv_a0[0],
             rwkv_a_up=rwkv_a_up[0], rwkv_g_up=rwkv_g_up[0], rwkv_k_k=rwkv_k_k[0], rwkv_k_a=rwkv_k_a[0],
             rwkv_r_k=rwkv_r_k[0].reshape(-1), rwkv_lnx_g=rwkv_lnx_g[0], rwkv_lnx_b=rwkv_lnx_b[0],
             w_out=w_out[0], norm_xa_g=norm_xa_g[0], norm_mem_g=norm_mem_g[0], xa_wq=xa_wq[0],
             xa_wk=xa_wk[0], xa_wv=xa_wv[0], xa_wo=xa_wo[0], norm_ffn_g=norm_ffn_g[0],
             w_router=w_router[0], b_router=b_router[0], w_gu=w_gu[0], b_gu=b_gu[0], w_dn=w_dn[0],
             b_dn=b_dn[0], final_norm_g=final_norm_g)
    out = _layer(x.reshape(bsz * s, d), mem.reshape(bsz * mlen, d), bsz, s, mlen, p)
    return out.reshape(bsz, s, d)
```

```python
import functools

import jax
import jax.numpy as jnp
from jax import lax
from jax.experimental import pallas as pl
from jax.experimental.pallas import tpu as pltpu

f32 = jnp.float32
bf16 = jnp.bfloat16
i32 = jnp.int32

D_MODEL = 1024
LRU_W = 512
RWKV_W = 512
LRU_BLOCKS = 8
LRU_BLOCK = 64
CONV_W = 4
LRU_C = 8.0
RWKV_HEAD = 64
DECAY_LORA = 64
AAA_LORA = 64
GATE_LORA = 128
RWKV_IN = 3 * RWKV_W + DECAY_LORA + AAA_LORA + GATE_LORA
XA_HEADS = 4
XA_HEAD = D_MODEL // XA_HEADS
N_EXPERTS = 32
TOP_K = 4
D_FF = D_MODEL
SWIGLU_LIMIT = 7.0
SWIGLU_ALPHA = 1.702
EPS = 1e-6
GN_EPS = 64e-5

LANES = 128
SUBLANES = 8
RWKV_CHUNK = 64
PAIR = 2 * RWKV_HEAD
N_PAIRS = RWKV_W // PAIR
MOE_BLOCK = 256
X_TILE = D_MODEL // (2 * LANES)
u32 = jnp.uint32
VMEM_LIMIT = 52 * 1024 * 1024


def _cparams(sem):
    return pltpu.CompilerParams(dimension_semantics=sem, vmem_limit_bytes=VMEM_LIMIT)


def _rms(x, g):
    return x * lax.rsqrt(jnp.mean(x * x, axis=-1, keepdims=True) + EPS) * g


def _full(shape):
    n = len(shape)
    return pl.BlockSpec(shape, lambda *a: (0,) * n)


def _shift_rows(x, prev8, d):
    xr = pltpu.roll(x, d, 0)
    tr = pltpu.roll(prev8, d, 0)
    row = lax.broadcasted_iota(i32, prev8.shape, 0)
    head = jnp.where(row < d, tr, xr[:SUBLANES])
    return jnp.concatenate([head, xr[SUBLANES:]], axis=0)


def _bf16_parts(x, n):
    parts = []
    for _ in range(n):
        piece = x.astype(bf16)
        parts.append(piece)
        x = x - piece.astype(f32)
    return parts


def _sum_dot(x, mask_bf, n_parts, mask_left=False):
    acc = None
    for piece in _bf16_parts(x, n_parts):
        d = (jnp.dot(mask_bf, piece, preferred_element_type=f32) if mask_left
             else jnp.dot(piece, mask_bf, preferred_element_type=f32))
        acc = d if acc is None else acc + d
    return acc


def _store_packed_rows(ref, val):
    n = val.shape[0]
    half = D_MODEL // 2
    lo = lax.bitcast_convert_type(val[:, :half].astype(bf16).astype(f32), u32) >> 16
    hi = lax.bitcast_convert_type(val[:, half:].astype(bf16).astype(f32), u32) & jnp.uint32(0xFFFF0000)
    words = hi | lo
    for j in range(X_TILE):
        ref[pl.ds(j, n, stride=X_TILE), :] = words[:, j * LANES:(j + 1) * LANES]


def _load_packed_rows(ref, n, off=0):
    ws = [ref[pl.ds(off + j, n, stride=X_TILE), :] for j in range(X_TILE)]
    lo = [lax.bitcast_convert_type(w << 16, f32).astype(bf16) for w in ws]
    hi = [lax.bitcast_convert_type(w & jnp.uint32(0xFFFF0000), f32).astype(bf16) for w in ws]
    return jnp.concatenate(lo + hi, axis=1)


def _inproj_body(x_ref, g_ref, w_ref, xl_ref, gl_ref, ur_ref):
    h = _rms(x_ref[...], g_ref[...])
    u = jnp.dot(h.astype(bf16), w_ref[...], preferred_element_type=f32)
    xl_ref[...] = u[:, :LRU_W]
    gl_ref[...] = u[:, LRU_W:2 * LRU_W]
    ur_ref[...] = u[:, 2 * LRU_W:]


def _inproj(x2d, g, w_in_bf, tm):
    t = x2d.shape[0]
    return pl.pallas_call(
        _inproj_body,
        grid=(t // tm,),
        in_specs=[pl.BlockSpec((tm, D_MODEL), lambda i: (i, 0)), _full((1, D_MODEL)),
                  _full(w_in_bf.shape)],
        out_specs=[pl.BlockSpec((tm, LRU_W), lambda i: (i, 0)),
                   pl.BlockSpec((tm, LRU_W), lambda i: (i, 0)),
                   pl.BlockSpec((tm, RWKV_IN), lambda i: (i, 0))],
        out_shape=[jax.ShapeDtypeStruct((t, LRU_W), f32), jax.ShapeDtypeStruct((t, LRU_W), f32),
                   jax.ShapeDtypeStruct((t, RWKV_IN), f32)],
        compiler_params=_cparams(("parallel",)),
        name="inproj",
    )(x2d, g, w_in_bf)


def _lru_body(xl_ref, gl_ref, cw_ref, cb_ref, wg_ref, bg_ref, lam_ref, o_ref, tail_ref, h_ref):
    ts = xl_ref.shape[0]

    @pl.when(pl.program_id(1) == 0)
    def _():
        tail_ref[...] = jnp.zeros_like(tail_ref)
        h_ref[...] = jnp.zeros_like(h_ref)

    x = xl_ref[...]
    tail = tail_ref[...]
    cw = cw_ref[...]
    xc = cb_ref[...] + cw[CONV_W - 1:CONV_W] * x
    for d in range(1, CONV_W):
        xc = xc + cw[CONV_W - 1 - d:CONV_W - d] * _shift_rows(x, tail, d)
    tail_ref[...] = x[ts - SUBLANES:]

    gates = jax.nn.sigmoid(jnp.dot(xc.astype(bf16), wg_ref[...], preferred_element_type=f32) + bg_ref[...])
    gx = gates[:, :LRU_W]
    ga = gates[:, LRU_W:]
    log_a = -LRU_C * ga * jax.nn.softplus(-lam_ref[...])
    a = jnp.exp(log_a)
    b = jnp.sqrt(-jnp.tanh(log_a) * (a * a + 1.0)) * gx * xc

    row = lax.broadcasted_iota(i32, (ts, LRU_W), 0) % SUBLANES
    d = 1
    while d < SUBLANES:
        keep = row >= d
        a_s = jnp.where(keep, pltpu.roll(a, d, 0), 1.0)
        b_s = jnp.where(keep, pltpu.roll(b, d, 0), 0.0)
        b = a * b_s + b
        a = a * a_s
        d *= 2
    carry = h_ref[SUBLANES - 1:SUBLANES, :]
    groups = []
    for q in range(ts // SUBLANES):
        rows = slice(q * SUBLANES, (q + 1) * SUBLANES)
        hq = b[rows] + a[rows] * carry
        carry = hq[SUBLANES - 1:SUBLANES, :]
        groups.append(hq)
    h = jnp.concatenate(groups, axis=0)
    h_ref[...] = jnp.broadcast_to(carry, h_ref.shape)
    o_ref[...] = (h * jax.nn.gelu(gl_ref[...])).astype(o_ref.dtype)


def _lru(xl, gl, cw, cb, wg_bf, bg, lam, bsz, s, ts):
    nt = s // ts
    blk = pl.BlockSpec((ts, LRU_W), lambda b, i: (b * nt + i, 0))
    return pl.pallas_call(
        _lru_body,
        grid=(bsz, nt),
        in_specs=[blk, blk, _full(cw.shape), _full(cb.shape), _full(wg_bf.shape), _full(bg.shape),
                  _full(lam.shape)],
        out_specs=blk,
        out_shape=jax.ShapeDtypeStruct((bsz * s, LRU_W), bf16),
        scratch_shapes=[pltpu.VMEM((SUBLANES, LRU_W), f32), pltpu.VMEM((SUBLANES, LRU_W), f32)],
        compiler_params=_cparams(("parallel", "arbitrary")),
        name="lru",
    )(xl, gl, cw, cb, wg_bf, bg, lam)


def _mm_x3(a, b_parts):
    ah, al = _bf16_parts(a, 2)
    bh, bl = b_parts
    d = lambda x, y: jnp.dot(x, y, preferred_element_type=f32)
    return d(ah, bh) + (d(ah, bl) + d(al, bh))


def _mm(a, b):
    return jnp.dot(a.astype(bf16), b.astype(bf16), preferred_element_type=f32)


def _mm_nt(a, b):
    return lax.dot_general(a.astype(bf16), b.astype(bf16), (((1,), (1,)), ((), ())), preferred_element_type=f32)


def _mm_tn(a, b):
    return lax.dot_general(a.astype(bf16), b.astype(bf16), (((0,), (0,)), ((), ())), preferred_element_type=f32)


def _bd(x):
    m0 = lax.broadcasted_iota(i32, x.shape, 1) < RWKV_HEAD
    zero = jnp.zeros_like(x)
    return jnp.concatenate([jnp.where(m0, x, zero), jnp.where(m0, zero, x)], axis=0)


def _side_by_side(d):
    h = d.shape[0] // 2
    m0 = lax.broadcasted_iota(i32, (h, d.shape[1]), 1) < RWKV_HEAD
    return jnp.where(m0, d[:h], d[h:])


def _chunk_maps(chunks):
    c = RWKV_CHUNK
    ri = lax.broadcasted_iota(i32, (c, PAIR), 0)
    ji = lax.broadcasted_iota(i32, (c, PAIR), 1) % RWKV_HEAD
    strict = ji < ri
    incl = ji <= ri
    diag = ji == ri
    eye = jnp.where(diag, 1.0, 0.0).astype(f32)
    each = lambda f, *ls: [f(*xs) for xs in zip(*ls)]
    cat0 = lambda *xs: jnp.concatenate(xs, axis=0)
    cat1 = lambda *xs: jnp.concatenate(xs, axis=1)
    tb = lambda x: x.astype(bf16)

    ats, bts, kts, rts, vs, cls = [list(x) for x in zip(*chunks)]
    pcs = each(lambda cl: jnp.exp(cl[c - 1:c, :]), cls)
    bd_a = each(lambda x: tb(_bd(x)), ats)
    bd_v = each(lambda x: tb(_bd(x)), vs)

    aa = each(lambda a, r, b, k: _mm_nt(cat0(tb(a), tb(r)), cat0(tb(_bd(b)), tb(_bd(k)))), ats, rts, bts, kts)
    l_ab = each(lambda x: tb(jnp.where(strict, x[:c, :PAIR], 0.0)), aa)
    a_k = each(lambda x: tb(cat0(jnp.where(strict, x[:c, PAIR:], 0.0), jnp.where(incl, x[c:, PAIR:], 0.0))), aa)
    a_rb = each(lambda x: tb(jnp.where(incl, x[c:, :PAIR], 0.0)), aa)

    tinv = each(lambda x: eye + x, l_ab)
    lp = each(lambda x: tb(_mm(x, _bd(x))), l_ab)
    p = 2
    while 2 * p < c:
        x2 = each(lambda t, x: _mm(cat0(tb(t), x), _bd(x)), tinv, lp)
        tinv = each(lambda t, x: t + x[:c], tinv, x2)
        lp = each(lambda x: tb(x[c:]), x2)
        p *= 2
    tinv = each(lambda t, x: t + _mm(t, _bd(x)), tinv, lp)

    wy = each(_mm, a_k, bd_v)
    za = each(lambda t, w, a: _mm(t, cat1(tb(_bd(w[:c])), a)), tinv, wy, bd_a)
    zp = each(lambda x: x[:, :PAIR], za)
    ac = each(lambda x: x[:, PAIR:], za)
    y1 = each(lambda ar, z, a: _mm(ar, cat1(tb(_bd(z)), tb(_bd(a)))), a_rb, zp, ac)
    yp = each(lambda w, y: w[c:] + y[:, :PAIR], wy, y1)
    rc = each(lambda r, y: r + y[:, PAIR:], rts, y1)
    sm = each(lambda b, pc, z, a: _mm_tn(b * pc, cat1(z, a)), bts, pcs, zp, ac)
    kv = each(lambda k, pc, v: _mm_tn(k * pc, v), kts, pcs, vs)
    sp = each(lambda x, m: _side_by_side(x) + _side_by_side(m[:, :PAIR]), kv, sm)
    g = each(lambda pc, m: jnp.where(diag, jnp.broadcast_to(pc, (c, PAIR)), 0.0) + _side_by_side(m[:, PAIR:]), pcs, sm)
    return list(zip(g, sp, rc, yp))


N_UCOLS = 5


def _rwkv_chunk_body(ur_r, ur_k, ur_v, ur_lo, ur_dg, mu_ref, par_ref, wproj_ref, gup_ref, ones_ref, tri_ref,
                     g_ref, sp_ref, rc_ref, yp_ref, bon_ref, v_ref, gate_ref, prev_ref):
    ts = ur_r.shape[0]
    c = RWKV_CHUNK

    @pl.when(pl.program_id(2) == 0)
    def _():
        prev_ref[...] = jnp.zeros_like(prev_ref)

    mixed = []
    for j, ref in enumerate((ur_r, ur_k, ur_v, ur_lo, ur_dg)):
        u0 = ref[...]
        ls = slice(j * LANES, (j + 1) * LANES)
        us = _shift_rows(u0, prev_ref[:, ls], 1)
        prev_ref[:, ls] = u0[ts - SUBLANES:]
        mixed.append(u0 + (us - u0) * mu_ref[0, j:j + 1, :])
    r, k, v, lora, dg = mixed
    w0, a0, k_k, k_a, r_k = [par_ref[0, j:j + 1, :] for j in range(5)]

    lane = lax.broadcasted_iota(i32, lora.shape, 1)
    lora = jnp.where(lane < DECAY_LORA, jnp.tanh(lora), lora)
    proj = jnp.dot(lora.astype(bf16), wproj_ref[0], preferred_element_type=f32)
    w = -jax.nn.softplus(-(w0 + proj[:, :PAIR])) - 0.5
    lw = -jnp.exp(w)
    a = jax.nn.sigmoid(a0 + proj[:, PAIR:])
    gate_ref[...] = jnp.dot(jax.nn.sigmoid(dg).astype(bf16), gup_ref[0], preferred_element_type=f32)

    ones = ones_ref[...]
    kk = k * k_k
    ss = _sum_dot(kk * kk, ones, 2)
    kk = kk / jnp.maximum(jnp.sqrt(ss), 1e-12)
    k2 = k * (1.0 + (a - 1.0) * k_a)
    bon_ref[...] = _sum_dot(r * k2 * r_k, ones, 2)
    v_ref[...] = v

    tri = tri_ref[...]
    grp = tri.shape[0]
    cl = jnp.concatenate([_sum_dot(lw[q * grp:(q + 1) * grp], tri, 3, mask_left=True) for q in range(ts // grp)],
                         axis=0)
    e_neg = jnp.exp(-cl)
    at = -kk * jnp.exp(cl - lw)
    bt = kk * a * e_neg
    kt = k2 * e_neg
    rt = r * jnp.exp(cl)

    sls = [slice(j * c, (j + 1) * c) for j in range(ts // c)]
    outs = _chunk_maps([(at[sl], bt[sl], kt[sl], rt[sl], v[sl], cl[sl]) for sl in sls])
    for sl, (g, sp, rc, yp) in zip(sls, outs):
        g_ref[sl, :] = g
        sp_ref[sl, :] = sp
        rc_ref[sl, :] = rc
        yp_ref[sl, :] = yp


def _rwkv_chunk(ur, mu5, par5, wproj, gup, bsz, s, ts):
    t = bsz * s
    nt = s // ts
    col = lambda blk: pl.BlockSpec((ts, PAIR), lambda p, b, i, blk=blk: (b * nt + i, blk(p)))
    nrw = RWKV_W // PAIR
    ucols = [col(lambda p: p), col(lambda p: nrw + p), col(lambda p: 2 * nrw + p), col(lambda p: 3 * nrw),
             col(lambda p: 3 * nrw + 1)]
    per_pair = lambda a: pl.BlockSpec((1,) + a.shape[1:], lambda p, b, i: (p, 0, 0))
    ones_pair = _block_diag(jnp.ones((2, RWKV_HEAD, RWKV_HEAD), bf16))
    tri = _block_diag(jnp.tril(jnp.ones((2, RWKV_CHUNK, RWKV_CHUNK), bf16)))
    oblk = pl.BlockSpec((ts, PAIR), lambda p, b, i: (b * nt + i, p))
    osh = jax.ShapeDtypeStruct((t, RWKV_W), f32)
    return pl.pallas_call(
        _rwkv_chunk_body,
        grid=(N_PAIRS, bsz, nt),
        in_specs=ucols + [per_pair(mu5), per_pair(par5), per_pair(wproj), per_pair(gup), _full(ones_pair.shape),
                          _full(tri.shape)],
        out_specs=[oblk] * 7,
        out_shape=[osh] * 7,
        scratch_shapes=[pltpu.VMEM((SUBLANES, N_UCOLS * LANES), f32)],
        compiler_params=_cparams(("parallel", "parallel", "arbitrary")),
        name="rwkv_chunk",
    )(ur, ur, ur, ur, ur, mu5, par5, wproj, gup, ones_pair, tri)


def _rwkv_state_body(g_ref, sp_ref, rc_ref, yp_ref, y_ref, s_ref):
    @pl.when(pl.program_id(0) == 0)
    def _():
        s_ref[...] = jnp.zeros_like(s_ref)

    c = RWKV_CHUNK
    bsz = g_ref.shape[0]
    nck = g_ref.shape[1] // c
    chains = [(b, slice(p * PAIR, (p + 1) * PAIR)) for b in range(bsz) for p in range(N_PAIRS)]
    states = [s_ref[b, :, ls] for b, ls in chains]
    for j in range(nck):
        sl = slice(j * c, (j + 1) * c)
        prods = [_mm_x3(jnp.concatenate([rc_ref[b, sl, ls], g_ref[b, sl, ls]], axis=0), _bf16_parts(_bd(s), 2))
                 for (b, ls), s in zip(chains, states)]
        for (b, ls), pr in zip(chains, prods):
            y_ref[b, sl, ls] = yp_ref[b, sl, ls] + pr[:c]
        states = [pr[c:] + sp_ref[b, sl, ls] for (b, ls), pr in zip(chains, prods)]
    for (b, ls), s in zip(chains, states):
        s_ref[b, :, ls] = s


def _rwkv_state(g, sp, rc, yp, bsz, s, nck):
    rows = nck * RWKV_CHUNK
    rblk = pl.BlockSpec((bsz, rows, RWKV_W), lambda i: (0, i, 0))
    r3 = lambda a: a.reshape(bsz, s, RWKV_W)
    y = pl.pallas_call(
        _rwkv_state_body,
        grid=(s // rows,),
        in_specs=[rblk] * 4,
        out_specs=rblk,
        out_shape=jax.ShapeDtypeStruct((bsz, s, RWKV_W), f32),
        scratch_shapes=[pltpu.VMEM((bsz, RWKV_CHUNK, RWKV_W), f32)],
        compiler_params=_cparams(("arbitrary",)),
        name="rwkv_state",
    )(r3(g), r3(sp), r3(rc), r3(yp))
    return y.reshape(bsz * s, RWKV_W)


def _outproj_body(x_ref, yl_ref, ys_ref, bon_ref, v_ref, g_ref, lg_ref, lb_ref, ones_ref, w_ref, o_ref):
    y = ys_ref[...]
    ones = ones_ref[...]
    inv_n = 1.0 / RWKV_HEAD
    mean = _sum_dot(y, ones, 2) * inv_n
    yc = y - mean
    var = _sum_dot(yc * yc, ones, 2) * inv_n
    yn = yc * lax.rsqrt(var + GN_EPS) * lg_ref[...] + lb_ref[...]
    yr = (yn + bon_ref[...] * v_ref[...]) * g_ref[...]
    cat = jnp.concatenate([yl_ref[...], yr.astype(bf16)], axis=1)
    o_ref[...] = x_ref[...] + jnp.dot(cat, w_ref[...], preferred_element_type=f32)


def _outproj(x2d, y_lru, y_scan, bon, v, g, lnx_g, lnx_b, ones_blk, w_out_bf, tm):
    t = x2d.shape[0]
    xb = pl.BlockSpec((tm, D_MODEL), lambda i: (i, 0))
    hb = pl.BlockSpec((tm, RWKV_W), lambda i: (i, 0))
    return pl.pallas_call(
        _outproj_body,
        grid=(t // tm,),
        in_specs=[xb, hb, hb, hb, hb, hb, _full(lnx_g.shape), _full(lnx_b.shape), _full(ones_blk.shape),
                  _full(w_out_bf.shape)],
        out_specs=xb,
        out_shape=jax.ShapeDtypeStruct((t, D_MODEL), f32),
        compiler_params=_cparams(("parallel",)),
        name="outproj",
    )(x2d, y_lru, y_scan, bon, v, g, lnx_g, lnx_b, ones_blk, w_out_bf)


def _memkv_body(m_ref, g_ref, wk_ref, wv_ref, k_ref, v_ref):
    h = _rms(m_ref[...], g_ref[...]).astype(bf16)
    k_ref[...] = jnp.dot(h, wk_ref[...], preferred_element_type=f32).astype(bf16)
    v_ref[...] = jnp.dot(h, wv_ref[...], preferred_element_type=f32).astype(bf16)


def _memkv(mem2d, g, wk_bf, wv_bf, tm):
    t = mem2d.shape[0]
    blk = pl.BlockSpec((tm, D_MODEL), lambda i: (i, 0))
    sh = jax.ShapeDtypeStruct((t, D_MODEL), bf16)
    return pl.pallas_call(
        _memkv_body,
        grid=(t // tm,),
        in_specs=[blk, _full(g.shape), _full(wk_bf.shape), _full(wv_bf.shape)],
        out_specs=[blk, blk],
        out_shape=[sh, sh],
        compiler_params=_cparams(("parallel",)),
        name="memkv",
    )(mem2d, g, wk_bf, wv_bf)


def _xattn_body(x_ref, k_ref, v_ref, gx_ref, wq_ref, wo_ref, gf_ref, wr_ref, br_ref, upper_ref,
                x2_ref, hf_ref, idx_ref, gate_ref, cnt_ref, base_ref):
    x = x_ref[...]
    h = _rms(x, gx_ref[...]).astype(bf16)
    q = jnp.dot(h, wq_ref[...], preferred_element_type=f32).astype(bf16)
    k = k_ref[...]
    v = v_ref[...]
    heads = [slice(hd * XA_HEAD, (hd + 1) * XA_HEAD) for hd in range(XA_HEADS)]
    scs = [lax.dot_general(q[:, sl], k[:, sl], (((1,), (1,)), ((), ())), preferred_element_type=f32)
           * (XA_HEAD ** -0.5) for sl in heads]
    ps = []
    for sc in scs:
        e = jnp.exp(sc - jnp.max(sc, axis=-1, keepdims=True))
        ps.append((e / jnp.sum(e, axis=-1, keepdims=True)).astype(bf16))
    o = jnp.concatenate([jnp.dot(p, v[:, sl], preferred_element_type=f32).astype(bf16)
                         for p, sl in zip(ps, heads)], axis=1)
    x2 = x + jnp.dot(o, wo_ref[...], preferred_element_type=f32)
    x2_ref[...] = x2

    hf = _rms(x2, gf_ref[...])
    _store_packed_rows(hf_ref, hf)
    logits = lax.dot_general(wr_ref[...], hf.astype(bf16), (((1,), (1,)), ((), ())),
                             preferred_element_type=f32) + br_ref[...]
    erow = lax.broadcasted_iota(i32, logits.shape, 0)
    neg = jnp.float32(-jnp.inf)
    cur = logits
    vals = []
    idxs = []
    for _ in range(TOP_K):
        m = jnp.max(cur, axis=0, keepdims=True)
        am = jnp.min(jnp.where(cur == m, erow, N_EXPERTS), axis=0, keepdims=True)
        vals.append(m)
        idxs.append(am)
        cur = jnp.where(erow == am, neg, cur)
    es = [jnp.exp(vk - vals[0]) for vk in vals]
    den = es[0] + es[1] + es[2] + es[3]

    @pl.when(pl.program_id(0) == 0)
    def _():
        base_ref[...] = jnp.zeros_like(base_ref)

    onehot = [jnp.where(erow == am, 1.0, 0.0) for am in idxs]
    cnt = (onehot[0] + onehot[1]) + (onehot[2] + onehot[3])
    base = base_ref[:, 0:1]
    prior = jnp.dot(cnt.astype(bf16), upper_ref[...], preferred_element_type=f32) + base
    base_ref[...] = jnp.broadcast_to(base + jnp.sum(cnt, axis=1, keepdims=True), base_ref.shape)
    cnt_ref[...] = base_ref[...]

    orow = lax.broadcasted_iota(i32, idx_ref.shape, 0)
    idx_out = jnp.zeros(idx_ref.shape, i32)
    gate_out = jnp.zeros(gate_ref.shape, f32)
    for kq in range(TOP_K):
        rank = jnp.sum(prior * onehot[kq], axis=0, keepdims=True).astype(i32)
        idx_out = jnp.where(orow == kq, idxs[kq], idx_out)
        idx_out = jnp.where(orow == TOP_K + kq, rank, idx_out)
        gate_out = jnp.where(orow == kq, es[kq] / den, gate_out)
    idx_ref[...] = idx_out
    gate_ref[...] = gate_out


def _xattn(x1, kmem, vmem, g_xa, wq_bf, wo_bf, g_ffn, wr_t, br_col, bsz, s, mlen, tm):
    t = bsz * s
    nt = s // tm
    xb = pl.BlockSpec((tm, D_MODEL), lambda i: (i, 0))
    mb = pl.BlockSpec((mlen, D_MODEL), lambda i: (i // nt, 0))
    lb = pl.BlockSpec((2 * TOP_K, tm), lambda i: (0, i))
    upper = jnp.triu(jnp.ones((tm, tm), bf16), k=1)
    return pl.pallas_call(
        _xattn_body,
        grid=(t // tm,),
        in_specs=[xb, mb, mb, _full(g_xa.shape), _full(wq_bf.shape), _full(wo_bf.shape),
                  _full(g_ffn.shape), _full(wr_t.shape), _full(br_col.shape), _full(upper.shape)],
        out_specs=[xb, pl.BlockSpec((tm * X_TILE, LANES), lambda i: (i, 0)), lb, lb, _full((N_EXPERTS, LANES))],
        out_shape=[jax.ShapeDtypeStruct((t, D_MODEL), f32), jax.ShapeDtypeStruct((t * X_TILE, LANES), u32),
                   jax.ShapeDtypeStruct((2 * TOP_K, t), i32), jax.ShapeDtypeStruct((2 * TOP_K, t), f32),
                   jax.ShapeDtypeStruct((N_EXPERTS, LANES), f32)],
        scratch_shapes=[pltpu.VMEM((N_EXPERTS, LANES), f32)],
        compiler_params=_cparams(("arbitrary",)),
        name="xattn",
    )(x1, kmem, vmem, g_xa, wq_bf, wo_bf, g_ffn, wr_t, br_col, upper)


DMA_UNROLL = 8


def _tile_at(ref, row):
    return ref.at[pl.ds(pl.multiple_of(row * X_TILE, X_TILE), X_TILE), :]


def _dispatch_body(dest_ref, pend_ref, hf_ref, xs_hbm, zeros_ref, sem, zsem):
    i = pl.program_id(0)
    tm = hf_ref.shape[0] // X_TILE
    zrows = MOE_BLOCK * X_TILE

    def zero_copy(e):
        start = pl.multiple_of((pend_ref[e] - MOE_BLOCK) * X_TILE, X_TILE)
        return pltpu.make_async_copy(zeros_ref, xs_hbm.at[pl.ds(start, zrows), :], zsem)

    def nonempty(e):
        return pend_ref[e] > (pend_ref[e - 1] if e else 0)

    @pl.when(i == 0)
    def _():
        zeros_ref[...] = jnp.zeros_like(zeros_ref)
        for e in range(N_EXPERTS):
            @pl.when(nonempty(e))
            def _():
                zero_copy(e).start()
        for e in range(N_EXPERTS):
            @pl.when(nonempty(e))
            def _():
                zero_copy(e).wait()

        def tail_copy(b):
            return pltpu.make_async_copy(zeros_ref, xs_hbm.at[pl.ds(pl.multiple_of(b * zrows, zrows), zrows), :], zsem)

        def tail_start(b, carry):
            tail_copy(b).start()
            return carry

        def tail_wait(b, carry):
            tail_copy(b).wait()
            return carry
        n_used = pend_ref[N_EXPERTS - 1] // MOE_BLOCK
        n_all = xs_hbm.shape[0] // zrows
        lax.fori_loop(n_used, n_all, tail_start, 0)
        lax.fori_loop(n_used, n_all, tail_wait, 0)

    def body(q, carry):
        for u in range(DMA_UNROLL):
            r = q * DMA_UNROLL + u
            src = _tile_at(hf_ref, r)
            for kq in range(TOP_K):
                pltpu.make_async_copy(src, _tile_at(xs_hbm, dest_ref[(i * tm + r) * TOP_K + kq]),
                                      sem).start(priority=kq % 2)
        return carry
    lax.fori_loop(0, tm // DMA_UNROLL, body, 0)
    for kq in range(TOP_K):
        pltpu.make_async_copy(hf_ref, xs_hbm.at[pl.ds(0, tm * X_TILE), :], sem).wait()


def _dispatch(dest_flat, pends, hf_tiles, n_blocks, tm):
    t = hf_tiles.shape[0] // X_TILE
    rows = n_blocks * MOE_BLOCK
    grid_spec = pltpu.PrefetchScalarGridSpec(
        num_scalar_prefetch=2,
        grid=(t // tm,),
        in_specs=[pl.BlockSpec((tm * X_TILE, LANES), lambda i, d, pe: (i, 0))],
        out_specs=pl.BlockSpec(memory_space=pl.ANY),
        scratch_shapes=[pltpu.VMEM((MOE_BLOCK * X_TILE, LANES), u32), pltpu.SemaphoreType.DMA(()),
                        pltpu.SemaphoreType.DMA(())],
    )
    return pl.pallas_call(
        _dispatch_body,
        grid_spec=grid_spec,
        out_shape=jax.ShapeDtypeStruct((rows * X_TILE, LANES), u32),
        compiler_params=_cparams(("arbitrary",)),
        name="dispatch",
    )(dest_flat, pends, hf_tiles)


def _moe_body(pend_ref, xs_hbm, wgu_ref, bgu_ref, wdn_ref, bdn_ref, ys_hbm, xbuf, ybuf, wgu_bf, wdn_bf,
              sem_in, sem_out):
    e = pl.program_id(0)
    blk = MOE_BLOCK * X_TILE
    end_blk = pend_ref[e] // MOE_BLOCK
    start_blk = jnp.where(e == 0, 0, pend_ref[jnp.maximum(e - 1, 0)] // MOE_BLOCK)
    nb = end_blk - start_blk

    def rows_of(j):
        return pl.ds(pl.multiple_of((start_blk + j) * blk, blk), blk)

    def x_copy(j, slot):
        return pltpu.make_async_copy(xs_hbm.at[rows_of(j), :], xbuf.at[slot], sem_in.at[slot])

    def y_copy(j, slot):
        return pltpu.make_async_copy(ybuf.at[slot], ys_hbm.at[rows_of(j), :], sem_out.at[slot])

    @pl.when(nb > 0)
    def _():
        x_copy(0, 0).start(priority=1)
        wgu_bf[...] = wgu_ref[0].astype(bf16)
        wdn_bf[...] = wdn_ref[0].astype(bf16)

    def block(j, carry):
        slot = j % 2

        @pl.when(j + 1 < nb)
        def _():
            x_copy(j + 1, 1 - slot).start(priority=1)

        x_copy(j, slot).wait()

        @pl.when(j >= 2)
        def _():
            y_copy(j - 2, slot).wait()

        xb = _load_packed_rows(xbuf.at[slot], MOE_BLOCK)
        gu = jnp.dot(xb, wgu_bf[...], preferred_element_type=f32) + bgu_ref[0]
        gate = jnp.minimum(gu[:, :D_FF], SWIGLU_LIMIT)
        up = jnp.clip(gu[:, D_FF:], -SWIGLU_LIMIT, SWIGLU_LIMIT)
        act = (up + 1.0) * (gate * jax.nn.sigmoid(SWIGLU_ALPHA * gate))
        y = jnp.dot(act.astype(bf16), wdn_bf[...], preferred_element_type=f32) + bdn_ref[0]
        _store_packed_rows(ybuf.at[slot], y)
        y_copy(j, slot).start(priority=1)
        return carry

    lax.fori_loop(0, nb, block, 0)

    @pl.when(nb >= 2)
    def _():
        y_copy(nb - 2, nb % 2).wait()

    @pl.when(nb >= 1)
    def _():
        y_copy(nb - 1, (nb - 1) % 2).wait()

    @pl.when(e == N_EXPERTS - 1)
    def _():
        n_all = ys_hbm.shape[0] // blk
        ybuf[0] = jnp.zeros(ybuf.shape[1:], ybuf.dtype)

        def tail_copy(b):
            return pltpu.make_async_copy(ybuf.at[0], ys_hbm.at[pl.ds(pl.multiple_of(b * blk, blk), blk), :],
                                         sem_out.at[0])

        def tail_start(b, carry):
            tail_copy(b).start()
            return carry

        def tail_wait(b, carry):
            tail_copy(b).wait()
            return carry
        lax.fori_loop(end_blk, n_all, tail_start, 0)
        lax.fori_loop(end_blk, n_all, tail_wait, 0)


def _moe(pends, xs_tiles, w_gu, b_gu, w_dn, b_dn, n_blocks):
    rows = n_blocks * MOE_BLOCK
    grid_spec = pltpu.PrefetchScalarGridSpec(
        num_scalar_prefetch=1,
        grid=(N_EXPERTS,),
        in_specs=[
            pl.BlockSpec(memory_space=pl.ANY),
            pl.BlockSpec((1, D_MODEL, 2 * D_FF), lambda e, pe: (e, 0, 0)),
            pl.BlockSpec((1, 1, 2 * D_FF), lambda e, pe: (e, 0, 0)),
            pl.BlockSpec((1, D_FF, D_MODEL), lambda e, pe: (e, 0, 0)),
            pl.BlockSpec((1, 1, D_MODEL), lambda e, pe: (e, 0, 0)),
        ],
        out_specs=pl.BlockSpec(memory_space=pl.ANY),
        scratch_shapes=[pltpu.VMEM((2, MOE_BLOCK * X_TILE, LANES), u32),
                        pltpu.VMEM((2, MOE_BLOCK * X_TILE, LANES), u32),
                        pltpu.VMEM((D_MODEL, 2 * D_FF), bf16), pltpu.VMEM((D_FF, D_MODEL), bf16),
                        pltpu.SemaphoreType.DMA((2,)), pltpu.SemaphoreType.DMA((2,))],
    )
    return pl.pallas_call(
        _moe_body,
        grid_spec=grid_spec,
        out_shape=jax.ShapeDtypeStruct((rows * X_TILE, LANES), u32),
        compiler_params=_cparams(("arbitrary",)),
        name="moe",
    )(pends, xs_tiles, w_gu, b_gu.reshape(N_EXPERTS, 1, 2 * D_FF), w_dn, b_dn.reshape(N_EXPERTS, 1, D_MODEL))


def _combine_body(pos_ref, ys_hbm, x_ref, gate_ref, g_ref, o_ref, buf, sem):
    i = pl.program_id(0)
    n = pl.num_programs(0)
    tc = x_ref.shape[0]
    slot = i % 2
    slot_rows = TOP_K * tc * X_TILE

    def start(step, sl):
        def body(q, carry):
            for u in range(DMA_UNROLL):
                r = q * DMA_UNROLL + u
                for kq in range(TOP_K):
                    dst = _tile_at(buf, (sl * TOP_K + kq) * tc + r)
                    pltpu.make_async_copy(_tile_at(ys_hbm, pos_ref[(step * tc + r) * TOP_K + kq]), dst,
                                          sem.at[sl]).start(priority=kq % 2)
            return carry
        lax.fori_loop(0, tc // DMA_UNROLL, body, 0)

    @pl.when(i == 0)
    def _():
        start(0, 0)

    @pl.when(i + 1 < n)
    def _():
        start(i + 1, 1 - slot)

    off = pl.multiple_of(slot * slot_rows, slot_rows)
    pltpu.make_async_copy(ys_hbm.at[pl.ds(0, slot_rows), :], buf.at[pl.ds(off, slot_rows), :], sem.at[slot]).wait()
    gates = gate_ref[...]
    acc = x_ref[...]
    for kq in range(TOP_K):
        rows = _load_packed_rows(buf, tc, off + kq * tc * X_TILE)
        acc = acc + gates[:, kq:kq + 1] * rows.astype(f32)
    o_ref[...] = _rms(acc, g_ref[...])


def _combine(pos_flat, ys_tiles, x2, gate_pad, g_final, tc):
    t = x2.shape[0]
    grid_spec = pltpu.PrefetchScalarGridSpec(
        num_scalar_prefetch=1,
        grid=(t // tc,),
        in_specs=[pl.BlockSpec(memory_space=pl.ANY),
                  pl.BlockSpec((tc, D_MODEL), lambda i, p: (i, 0)),
                  pl.BlockSpec((tc, LANES), lambda i, p: (i, 0)),
                  pl.BlockSpec((1, D_MODEL), lambda i, p: (0, 0))],
        out_specs=pl.BlockSpec((tc, D_MODEL), lambda i, p: (i, 0)),
        scratch_shapes=[pltpu.VMEM((2 * TOP_K * tc * X_TILE, LANES), u32), pltpu.SemaphoreType.DMA((2,))],
    )
    return pl.pallas_call(
        _combine_body,
        grid_spec=grid_spec,
        out_shape=jax.ShapeDtypeStruct((t, D_MODEL), f32),
        compiler_params=_cparams(("arbitrary",)),
        name="combine",
    )(pos_flat, ys_tiles, x2, gate_pad, g_final)


def _routing(top_idx, rank, counts, t):
    n_assign = t * TOP_K
    experts = jnp.arange(N_EXPERTS, dtype=i32)
    padded = (counts + MOE_BLOCK - 1) // MOE_BLOCK * MOE_BLOCK
    pends = jnp.cumsum(padded).astype(i32)
    pstarts = pends - padded
    start_of = jnp.sum(jnp.where(top_idx[:, :, None] == experts, pstarts, 0), axis=-1)
    dest = (start_of + rank).astype(i32).reshape(n_assign)
    n_blocks = (n_assign + N_EXPERTS * (MOE_BLOCK - 1) + MOE_BLOCK - 1) // MOE_BLOCK
    return dest, pends, n_blocks


def _block_diag(w):
    n, bi, bj = w.shape
    eye = jnp.eye(n, dtype=w.dtype)
    return jnp.einsum('nij,nm->nimj', w, eye).reshape(n * bi, n * bj)


def _layer(x2d, mem2d, bsz, s, mlen, p):
    t = bsz * s
    row = lambda a: a.reshape(1, -1)
    ones_blk = _block_diag(jnp.ones((RWKV_W // RWKV_HEAD, RWKV_HEAD, RWKV_HEAD), bf16))

    xl, gl, ur = _inproj(x2d, row(p['norm_mix_g']), p['w_in'].astype(bf16), tm=min(512, t))

    wg = jnp.concatenate([_block_diag(p['lru_wx']), _block_diag(p['lru_wa'])], axis=1).astype(bf16)
    bg = jnp.concatenate([p['lru_bx'], p['lru_ba']]).reshape(1, -1)
    y_lru = _lru(xl, gl, p['conv_w'], row(p['conv_b']), wg, bg, row(p['lru_lambda']), bsz, s, ts=min(512, s))

    pairs = lambda a: a.reshape(N_PAIRS, PAIR)
    rows8 = lambda rows: jnp.pad(jnp.stack(rows, axis=1), ((0, 0), (0, SUBLANES - len(rows)), (0, 0)))
    mu = p['rwkv_mu']
    shared = lambda a: jnp.broadcast_to(a, (N_PAIRS, PAIR))
    mu5 = rows8([pairs(mu[:RWKV_W]), pairs(mu[RWKV_W:2 * RWKV_W]), pairs(mu[2 * RWKV_W:3 * RWKV_W]),
                 shared(mu[3 * RWKV_W:3 * RWKV_W + PAIR]), shared(mu[3 * RWKV_W + PAIR:])])
    par5 = rows8([pairs(p['rwkv_w0']), pairs(p['rwkv_a0']), pairs(p['rwkv_k_k']), pairs(p['rwkv_k_a']),
                  pairs(p['rwkv_r_k'])])
    zl = jnp.zeros((DECAY_LORA, RWKV_W), f32)
    w_dec = jnp.concatenate([p['rwkv_w_up'], zl], axis=0).reshape(PAIR, N_PAIRS, PAIR)
    w_icl = jnp.concatenate([zl, p['rwkv_a_up']], axis=0).reshape(PAIR, N_PAIRS, PAIR)
    wproj = jnp.concatenate([w_dec, w_icl], axis=2).transpose(1, 0, 2).astype(bf16)
    gup = p['rwkv_g_up'].reshape(GATE_LORA, N_PAIRS, PAIR).transpose(1, 0, 2).astype(bf16)
    gm, sp, rc, yp, bon, v, g = _rwkv_chunk(ur, mu5, par5, wproj, gup, bsz, s, ts=min(1024, s))
    y_scan = _rwkv_state(gm, sp, rc, yp, bsz, s, nck=min(4, s // RWKV_CHUNK))

    x1 = _outproj(x2d, y_lru, y_scan, bon, v, g, row(p['rwkv_lnx_g']), row(p['rwkv_lnx_b']), ones_blk,
                  p['w_out'].astype(bf16), tm=min(512, t))

    kmem, vmem = _memkv(mem2d, row(p['norm_mem_g']), p['xa_wk'].astype(bf16), p['xa_wv'].astype(bf16),
                        tm=min(512, bsz * mlen))
    x2, hf, route, gates, cnt_pad = _xattn(x1, kmem, vmem, row(p['norm_xa_g']), p['xa_wq'].astype(bf16),
                                           p['xa_wo'].astype(bf16), row(p['norm_ffn_g']),
                                           p['w_router'].T.astype(bf16), p['b_router'].reshape(-1, 1),
                                           bsz, s, mlen, tm=min(512, s))

    counts = cnt_pad[:, 0].astype(i32)
    dest, pends, n_blocks = _routing(route[:TOP_K].T, route[TOP_K:].T, counts, t)
    gate_pad = jnp.pad(gates[:TOP_K].T, ((0, 0), (0, LANES - TOP_K)))
    xs = _dispatch(dest, pends, hf, n_blocks, tm=min(4096, t))
    ys = _moe(pends, xs, p['w_gu'], p['b_gu'], p['w_dn'], p['b_dn'], n_blocks)
    return _combine(dest, ys, x2, gate_pad, row(p['final_norm_g']), tc=min(256, t))


def kernel(x, mem, norm_mix_g, w_in, conv_w, conv_b, lru_wx, lru_bx, lru_wa, lru_ba, lru_lambda, rwkv_mu, rwkv_w0, rwkv_w_up, rwkv_a0, rwkv_a_up, rwkv_g_up, rwkv_k_k, rwkv_k_a, rwkv_r_k, rwkv_lnx_g, rwkv_lnx_b, w_out, norm_xa_g, norm_mem_g, xa_wq, xa_wk, xa_wv, xa_wo, norm_ffn_g, w_router, b_router, w_gu, b_gu, w_dn, b_dn, final_norm_g):
    bsz, s, d = x.shape
    mlen = mem.shape[1]
    assert d == D_MODEL and w_in.shape[0] == 1
    p = dict(norm_mix_g=norm_mix_g[0], w_in=w_in[0], conv_w=conv_w[0], conv_b=conv_b[0], lru_wx=lru_wx[0],
             lru_bx=lru_bx[0], lru_wa=lru_wa[0], lru_ba=lru_ba[0], lru_lambda=lru_lambda[0],
             rwkv_mu=rwkv_mu[0], rwkv_w0=rwkv_w0[0], rwkv_w_up=rwkv_w_up[0], rwkv_a0=rwkv_a0[0],
             rwkv_a_up=rwkv_a_up[0], rwkv_g_up=rwkv_g_up[0], rwkv_k_k=rwkv_k_k[0], rwkv_k_a=rwkv_k_a[0],
             rwkv_r_k=rwkv_r_k[0].reshape(-1), rwkv_lnx_g=rwkv_lnx_g[0], rwkv_lnx_b=rwkv_lnx_b[0],
             w_out=w_out[0], norm_xa_g=norm_xa_g[0], norm_mem_g=norm_mem_g[0], xa_wq=xa_wq[0],
             xa_wk=xa_wk[0], xa_wv=xa_wv[0], xa_wo=xa_wo[0], norm_ffn_g=norm_ffn_g[0],
             w_router=w_router[0], b_router=b_router[0], w_gu=w_gu[0], b_gu=b_gu[0], w_dn=w_dn[0],
             b_dn=b_dn[0], final_norm_g=final_norm_g)
    out = _layer(x.reshape(bsz * s, d), mem.reshape(bsz * mlen, d), bsz, s, mlen, p)
    return out.reshape(bsz, s, d)
```

```python
import functools

import jax
import jax.numpy as jnp
from jax import lax
from jax.experimental import pallas as pl
from jax.experimental.pallas import tpu as pltpu

f32 = jnp.float32
bf16 = jnp.bfloat16
i32 = jnp.int32

D_MODEL = 1024
LRU_W = 512
RWKV_W = 512
LRU_BLOCKS = 8
LRU_BLOCK = 64
CONV_W = 4
LRU_C = 8.0
RWKV_HEAD = 64
DECAY_LORA = 64
AAA_LORA = 64
GATE_LORA = 128
RWKV_IN = 3 * RWKV_W + DECAY_LORA + AAA_LORA + GATE_LORA
XA_HEADS = 4
XA_HEAD = D_MODEL // XA_HEADS
N_EXPERTS = 32
TOP_K = 4
D_FF = D_MODEL
SWIGLU_LIMIT = 7.0
SWIGLU_ALPHA = 1.702
EPS = 1e-6
GN_EPS = 64e-5

LANES = 128
SUBLANES = 8
RWKV_CHUNK = 64
PAIR = 2 * RWKV_HEAD
N_PAIRS = RWKV_W // PAIR
MOE_BLOCK = 256
X_TILE = D_MODEL // (2 * LANES)
u32 = jnp.uint32
VMEM_LIMIT = 52 * 1024 * 1024


def _cparams(sem):
    return pltpu.CompilerParams(dimension_semantics=sem, vmem_limit_bytes=VMEM_LIMIT)


def _rms(x, g):
    return x * lax.rsqrt(jnp.mean(x * x, axis=-1, keepdims=True) + EPS) * g


def _full(shape):
    n = len(shape)
    return pl.BlockSpec(shape, lambda *a: (0,) * n)


def _shift_rows(x, prev8, d):
    xr = pltpu.roll(x, d, 0)
    tr = pltpu.roll(prev8, d, 0)
    row = lax.broadcasted_iota(i32, prev8.shape, 0)
    head = jnp.where(row < d, tr, xr[:SUBLANES])
    return jnp.concatenate([head, xr[SUBLANES:]], axis=0)


def _bf16_parts(x, n):
    parts = []
    for _ in range(n):
        piece = x.astype(bf16)
        parts.append(piece)
        x = x - piece.astype(f32)
    return parts


def _sum_dot(x, mask_bf, n_parts, mask_left=False):
    acc = None
    for piece in _bf16_parts(x, n_parts):
        d = (jnp.dot(mask_bf, piece, preferred_element_type=f32) if mask_left
             else jnp.dot(piece, mask_bf, preferred_element_type=f32))
        acc = d if acc is None else acc + d
    return acc


def _store_packed_rows(ref, val):
    n = val.shape[0]
    half = D_MODEL // 2
    lo = lax.bitcast_convert_type(val[:, :half].astype(bf16).astype(f32), u32) >> 16
    hi = lax.bitcast_convert_type(val[:, half:].astype(bf16).astype(f32), u32) & jnp.uint32(0xFFFF0000)
    words = hi | lo
    for j in range(X_TILE):
        ref[pl.ds(j, n, stride=X_TILE), :] = words[:, j * LANES:(j + 1) * LANES]


def _load_packed_rows(ref, n, off=0):
    ws = [ref[pl.ds(off + j, n, stride=X_TILE), :] for j in range(X_TILE)]
    lo = [lax.bitcast_convert_type(w << 16, f32).astype(bf16) for w in ws]
    hi = [lax.bitcast_convert_type(w & jnp.uint32(0xFFFF0000), f32).astype(bf16) for w in ws]
    return jnp.concatenate(lo + hi, axis=1)


def _inproj_body(x_ref, g_ref, w_ref, xl_ref, gl_ref, ur_ref):
    h = _rms(x_ref[...], g_ref[...])
    u = jnp.dot(h.astype(bf16), w_ref[...], preferred_element_type=f32)
    xl_ref[...] = u[:, :LRU_W]
    gl_ref[...] = u[:, LRU_W:2 * LRU_W]
    ur_ref[...] = u[:, 2 * LRU_W:]


def _inproj(x2d, g, w_in_bf, tm):
    t = x2d.shape[0]
    return pl.pallas_call(
        _inproj_body,
        grid=(t // tm,),
        in_specs=[pl.BlockSpec((tm, D_MODEL), lambda i: (i, 0)), _full((1, D_MODEL)),
                  _full(w_in_bf.shape)],
        out_specs=[pl.BlockSpec((tm, LRU_W), lambda i: (i, 0)),
                   pl.BlockSpec((tm, LRU_W), lambda i: (i, 0)),
                   pl.BlockSpec((tm, RWKV_IN), lambda i: (i, 0))],
        out_shape=[jax.ShapeDtypeStruct((t, LRU_W), f32), jax.ShapeDtypeStruct((t, LRU_W), f32),
                   jax.ShapeDtypeStruct((t, RWKV_IN), f32)],
        compiler_params=_cparams(("parallel",)),
        name="inproj",
    )(x2d, g, w_in_bf)


def _lru_body(xl_ref, gl_ref, cw_ref, cb_ref, wg_ref, bg_ref, lam_ref, o_ref, tail_ref, h_ref):
    ts = xl_ref.shape[0]

    @pl.when(pl.program_id(1) == 0)
    def _():
        tail_ref[...] = jnp.zeros_like(tail_ref)
        h_ref[...] = jnp.zeros_like(h_ref)

    x = xl_ref[...]
    tail = tail_ref[...]
    cw = cw_ref[...]
    xc = cb_ref[...] + cw[CONV_W - 1:CONV_W] * x
    for d in range(1, CONV_W):
        xc = xc + cw[CONV_W - 1 - d:CONV_W - d] * _shift_rows(x, tail, d)
    tail_ref[...] = x[ts - SUBLANES:]

    gates = jax.nn.sigmoid(jnp.dot(xc.astype(bf16), wg_ref[...], preferred_element_type=f32) + bg_ref[...])
    gx = gates[:, :LRU_W]
    ga = gates[:, LRU_W:]
    log_a = -LRU_C * ga * jax.nn.softplus(-lam_ref[...])
    a = jnp.exp(log_a)
    b = jnp.sqrt(-jnp.tanh(log_a) * (a * a + 1.0)) * gx * xc

    row = lax.broadcasted_iota(i32, (ts, LRU_W), 0) % SUBLANES
    d = 1
    while d < SUBLANES:
        keep = row >= d
        a_s = jnp.where(keep, pltpu.roll(a, d, 0), 1.0)
        b_s = jnp.where(keep, pltpu.roll(b, d, 0), 0.0)
        b = a * b_s + b
        a = a * a_s
        d *= 2
    carry = h_ref[SUBLANES - 1:SUBLANES, :]
    groups = []
    for q in range(ts // SUBLANES):
        rows = slice(q * SUBLANES, (q + 1) * SUBLANES)
        hq = b[rows] + a[rows] * carry
        carry = hq[SUBLANES - 1:SUBLANES, :]
        groups.append(hq)
    h = jnp.concatenate(groups, axis=0)
    h_ref[...] = jnp.broadcast_to(carry, h_ref.shape)
    o_ref[...] = (h * jax.nn.gelu(gl_ref[...])).astype(o_ref.dtype)


def _lru(xl, gl, cw, cb, wg_bf, bg, lam, bsz, s, ts):
    nt = s // ts
    blk = pl.BlockSpec((ts, LRU_W), lambda b, i: (b * nt + i, 0))
    return pl.pallas_call(
        _lru_body,
        grid=(bsz, nt),
        in_specs=[blk, blk, _full(cw.shape), _full(cb.shape), _full(wg_bf.shape), _full(bg.shape),
                  _full(lam.shape)],
        out_specs=blk,
        out_shape=jax.ShapeDtypeStruct((bsz * s, LRU_W), bf16),
        scratch_shapes=[pltpu.VMEM((SUBLANES, LRU_W), f32), pltpu.VMEM((SUBLANES, LRU_W), f32)],
        compiler_params=_cparams(("parallel", "arbitrary")),
        name="lru",
    )(xl, gl, cw, cb, wg_bf, bg, lam)


def _mm_x3(a, b_parts):
    ah, al = _bf16_parts(a, 2)
    bh, bl = b_parts
    d = lambda x, y: jnp.dot(x, y, preferred_element_type=f32)
    return d(ah, bh) + (d(ah, bl) + d(al, bh))


def _mm(a, b):
    return jnp.dot(a.astype(bf16), b.astype(bf16), preferred_element_type=f32)


def _mm_nt(a, b):
    return lax.dot_general(a.astype(bf16), b.astype(bf16), (((1,), (1,)), ((), ())), preferred_element_type=f32)


def _mm_tn(a, b):
    return lax.dot_general(a.astype(bf16), b.astype(bf16), (((0,), (0,)), ((), ())), preferred_element_type=f32)


def _bd(x):
    m0 = lax.broadcasted_iota(i32, x.shape, 1) < RWKV_HEAD
    zero = jnp.zeros_like(x)
    return jnp.concatenate([jnp.where(m0, x, zero), jnp.where(m0, zero, x)], axis=0)


def _side_by_side(d):
    h = d.shape[0] // 2
    m0 = lax.broadcasted_iota(i32, (h, d.shape[1]), 1) < RWKV_HEAD
    return jnp.where(m0, d[:h], d[h:])


def _chunk_maps(chunks):
    c = RWKV_CHUNK
    ri = lax.broadcasted_iota(i32, (c, PAIR), 0)
    ji = lax.broadcasted_iota(i32, (c, PAIR), 1) % RWKV_HEAD
    strict = ji < ri
    incl = ji <= ri
    diag = ji == ri
    eye = jnp.where(diag, 1.0, 0.0).astype(f32)
    each = lambda f, *ls: [f(*xs) for xs in zip(*ls)]
    cat0 = lambda *xs: jnp.concatenate(xs, axis=0)
    cat1 = lambda *xs: jnp.concatenate(xs, axis=1)
    tb = lambda x: x.astype(bf16)

    ats, bts, kts, rts, vs, cls = [list(x) for x in zip(*chunks)]
    pcs = each(lambda cl: jnp.exp(cl[c - 1:c, :]), cls)
    bd_a = each(lambda x: tb(_bd(x)), ats)
    bd_v = each(lambda x: tb(_bd(x)), vs)

    aa = each(lambda a, r, b, k: _mm_nt(cat0(tb(a), tb(r)), cat0(tb(_bd(b)), tb(_bd(k)))), ats, rts, bts, kts)
    l_ab = each(lambda x: tb(jnp.where(strict, x[:c, :PAIR], 0.0)), aa)
    a_k = each(lambda x: tb(cat0(jnp.where(strict, x[:c, PAIR:], 0.0), jnp.where(incl, x[c:, PAIR:], 0.0))), aa)
    a_rb = each(lambda x: tb(jnp.where(incl, x[c:, :PAIR], 0.0)), aa)

    tinv = each(lambda x: eye + x, l_ab)
    lp = each(lambda x: tb(_mm(x, _bd(x))), l_ab)
    p = 2
    while 2 * p < c:
        x2 = each(lambda t, x: _mm(cat0(tb(t), x), _bd(x)), tinv, lp)
        tinv = each(lambda t, x: t + x[:c], tinv, x2)
        lp = each(lambda x: tb(x[c:]), x2)
        p *= 2
    tinv = each(lambda t, x: t + _mm(t, _bd(x)), tinv, lp)

    wy = each(_mm, a_k, bd_v)
    za = each(lambda t, w, a: _mm(t, cat1(tb(_bd(w[:c])), a)), tinv, wy, bd_a)
    zp = each(lambda x: x[:, :PAIR], za)
    ac = each(lambda x: x[:, PAIR:], za)
    y1 = each(lambda ar, z, a: _mm(ar, cat1(tb(_bd(z)), tb(_bd(a)))), a_rb, zp, ac)
    yp = each(lambda w, y: w[c:] + y[:, :PAIR], wy, y1)
    rc = each(lambda r, y: r + y[:, PAIR:], rts, y1)
    sm = each(lambda b, pc, z, a: _mm_tn(b * pc, cat1(z, a)), bts, pcs, zp, ac)
    kv = each(lambda k, pc, v: _mm_tn(k * pc, v), kts, pcs, vs)
    sp = each(lambda x, m: _side_by_side(x) + _side_by_side(m[:, :PAIR]), kv, sm)
    g = each(lambda pc, m: jnp.where(diag, jnp.broadcast_to(pc, (c, PAIR)), 0.0) + _side_by_side(m[:, PAIR:]), pcs, sm)
    return list(zip(g, sp, rc, yp))


N_UCOLS = 5


def _rwkv_chunk_body(ur_r, ur_k, ur_v, ur_lo, ur_dg, mu_ref, par_ref, wproj_ref, gup_ref, ones_ref, tri_ref,
                     g_ref, sp_ref, rc_ref, yp_ref, bon_ref, v_ref, gate_ref, prev_ref):
    ts = ur_r.shape[0]
    c = RWKV_CHUNK

    @pl.when(pl.program_id(2) == 0)
    def _():
        prev_ref[...] = jnp.zeros_like(prev_ref)

    mixed = []
    for j, ref in enumerate((ur_r, ur_k, ur_v, ur_lo, ur_dg)):
        u0 = ref[...]
        ls = slice(j * LANES, (j + 1) * LANES)
        us = _shift_rows(u0, prev_ref[:, ls], 1)
        prev_ref[:, ls] = u0[ts - SUBLANES:]
        mixed.append(u0 + (us - u0) * mu_ref[0, j:j + 1, :])
    r, k, v, lora, dg = mixed
    w0, a0, k_k, k_a, r_k = [par_ref[0, j:j + 1, :] for j in range(5)]

    lane = lax.broadcasted_iota(i32, lora.shape, 1)
    lora = jnp.where(lane < DECAY_LORA, jnp.tanh(lora), lora)
    proj = jnp.dot(lora.astype(bf16), wproj_ref[0], preferred_element_type=f32)
    w = -jax.nn.softplus(-(w0 + proj[:, :PAIR])) - 0.5
    lw = -jnp.exp(w)
    a = jax.nn.sigmoid(a0 + proj[:, PAIR:])
    gate_ref[...] = jnp.dot(jax.nn.sigmoid(dg).astype(bf16), gup_ref[0], preferred_element_type=f32)

    ones = ones_ref[...]
    kk = k * k_k
    ss = _sum_dot(kk * kk, ones, 2)
    kk = kk / jnp.maximum(jnp.sqrt(ss), 1e-12)
    k2 = k * (1.0 + (a - 1.0) * k_a)
    bon_ref[...] = _sum_dot(r * k2 * r_k, ones, 2)
    v_ref[...] = v

    tri = tri_ref[...]
    grp = tri.shape[0]
    cl = jnp.concatenate([_sum_dot(lw[q * grp:(q + 1) * grp], tri, 3, mask_left=True) for q in range(ts // grp)],
                         axis=0)
    e_neg = jnp.exp(-cl)
    at = -kk * jnp.exp(cl - lw)
    bt = kk * a * e_neg
    kt = k2 * e_neg
    rt = r * jnp.exp(cl)

    sls = [slice(j * c, (j + 1) * c) for j in range(ts // c)]
    outs = _chunk_maps([(at[sl], bt[sl], kt[sl], rt[sl], v[sl], cl[sl]) for sl in sls])
    for sl, (g, sp, rc, yp) in zip(sls, outs):
        g_ref[sl, :] = g
        sp_ref[sl, :] = sp
        rc_ref[sl, :] = rc
        yp_ref[sl, :] = yp


def _rwkv_chunk(ur, mu5, par5, wproj, gup, bsz, s, ts):
    t = bsz * s
    nt = s // ts
    col = lambda blk: pl.BlockSpec((ts, PAIR), lambda p, b, i, blk=blk: (b * nt + i, blk(p)))
    nrw = RWKV_W // PAIR
    ucols = [col(lambda p: p), col(lambda p: nrw + p), col(lambda p: 2 * nrw + p), col(lambda p: 3 * nrw),
             col(lambda p: 3 * nrw + 1)]
    per_pair = lambda a: pl.BlockSpec((1,) + a.shape[1:], lambda p, b, i: (p, 0, 0))
    ones_pair = _block_diag(jnp.ones((2, RWKV_HEAD, RWKV_HEAD), bf16))
    tri = _block_diag(jnp.tril(jnp.ones((2, RWKV_CHUNK, RWKV_CHUNK), bf16)))
    oblk = pl.BlockSpec((ts, PAIR), lambda p, b, i: (b * nt + i, p))
    osh = jax.ShapeDtypeStruct((t, RWKV_W), f32)
    return pl.pallas_call(
        _rwkv_chunk_body,
        grid=(N_PAIRS, bsz, nt),
        in_specs=ucols + [per_pair(mu5), per_pair(par5), per_pair(wproj), per_pair(gup), _full(ones_pair.shape),
                          _full(tri.shape)],
        out_specs=[oblk] * 7,
        out_shape=[osh] * 7,
        scratch_shapes=[pltpu.VMEM((SUBLANES, N_UCOLS * LANES), f32)],
        compiler_params=_cparams(("parallel", "parallel", "arbitrary")),
        name="rwkv_chunk",
    )(ur, ur, ur, ur, ur, mu5, par5, wproj, gup, ones_pair, tri)


def _rwkv_state_body(g_ref, sp_ref, rc_ref, yp_ref, y_ref, s_ref):
    @pl.when(pl.program_id(0) == 0)
    def _():
        s_ref[...] = jnp.zeros_like(s_ref)

    c = RWKV_CHUNK
    bsz = g_ref.shape[0]
    nck = g_ref.shape[1] // c
    chains = [(b, slice(p * PAIR, (p + 1) * PAIR)) for b in range(bsz) for p in range(N_PAIRS)]
    states = [s_ref[b, :, ls] for b, ls in chains]
    for j in range(nck):
        sl = slice(j * c, (j + 1) * c)
        prods = [_mm_x3(jnp.concatenate([rc_ref[b, sl, ls], g_ref[b, sl, ls]], axis=0), _bf16_parts(_bd(s), 2))
                 for (b, ls), s in zip(chains, states)]
        for (b, ls), pr in zip(chains, prods):
            y_ref[b, sl, ls] = yp_ref[b, sl, ls] + pr[:c]
        states = [pr[c:] + sp_ref[b, sl, ls] for (b, ls), pr in zip(chains, prods)]
    for (b, ls), s in zip(chains, states):
        s_ref[b, :, ls] = s


def _rwkv_state(g, sp, rc, yp, bsz, s, nck):
    rows = nck * RWKV_CHUNK
    rblk = pl.BlockSpec((bsz, rows, RWKV_W), lambda i: (0, i, 0))
    r3 = lambda a: a.reshape(bsz, s, RWKV_W)
    y = pl.pallas_call(
        _rwkv_state_body,
        grid=(s // rows,),
        in_specs=[rblk] * 4,
        out_specs=rblk,
        out_shape=jax.ShapeDtypeStruct((bsz, s, RWKV_W), f32),
        scratch_shapes=[pltpu.VMEM((bsz, RWKV_CHUNK, RWKV_W), f32)],
        compiler_params=_cparams(("arbitrary",)),
        name="rwkv_state",
    )(r3(g), r3(sp), r3(rc), r3(yp))
    return y.reshape(bsz * s, RWKV_W)


def _outproj_body(x_ref, yl_ref, ys_ref, bon_ref, v_ref, g_ref, lg_ref, lb_ref, ones_ref, w_ref, o_ref):
    y = ys_ref[...]
    ones = ones_ref[...]
    inv_n = 1.0 / RWKV_HEAD
    mean = _sum_dot(y, ones, 2) * inv_n
    yc = y - mean
    var = _sum_dot(yc * yc, ones, 2) * inv_n
    yn = yc * lax.rsqrt(var + GN_EPS) * lg_ref[...] + lb_ref[...]
    yr = (yn + bon_ref[...] * v_ref[...]) * g_ref[...]
    cat = jnp.concatenate([yl_ref[...], yr.astype(bf16)], axis=1)
    o_ref[...] = x_ref[...] + jnp.dot(cat, w_ref[...], preferred_element_type=f32)


def _outproj(x2d, y_lru, y_scan, bon, v, g, lnx_g, lnx_b, ones_blk, w_out_bf, tm):
    t = x2d.shape[0]
    xb = pl.BlockSpec((tm, D_MODEL), lambda i: (i, 0))
    hb = pl.BlockSpec((tm, RWKV_W), lambda i: (i, 0))
    return pl.pallas_call(
        _outproj_body,
        grid=(t // tm,),
        in_specs=[xb, hb, hb, hb, hb, hb, _full(lnx_g.shape), _full(lnx_b.shape), _full(ones_blk.shape),
                  _full(w_out_bf.shape)],
        out_specs=xb,
        out_shape=jax.ShapeDtypeStruct((t, D_MODEL), f32),
        compiler_params=_cparams(("parallel",)),
        name="outproj",
    )(x2d, y_lru, y_scan, bon, v, g, lnx_g, lnx_b, ones_blk, w_out_bf)


def _memkv_body(m_ref, g_ref, wk_ref, wv_ref, k_ref, v_ref):
    h = _rms(m_ref[...], g_ref[...]).astype(bf16)
    k_ref[...] = jnp.dot(h, wk_ref[...], preferred_element_type=f32).astype(bf16)
    v_ref[...] = jnp.dot(h, wv_ref[...], preferred_element_type=f32).astype(bf16)


def _memkv(mem2d, g, wk_bf, wv_bf, tm):
    t = mem2d.shape[0]
    blk = pl.BlockSpec((tm, D_MODEL), lambda i: (i, 0))
    sh = jax.ShapeDtypeStruct((t, D_MODEL), bf16)
    return pl.pallas_call(
        _memkv_body,
        grid=(t // tm,),
        in_specs=[blk, _full(g.shape), _full(wk_bf.shape), _full(wv_bf.shape)],
        out_specs=[blk, blk],
        out_shape=[sh, sh],
        compiler_params=_cparams(("parallel",)),
        name="memkv",
    )(mem2d, g, wk_bf, wv_bf)


def _xattn_body(x_ref, k_ref, v_ref, gx_ref, wq_ref, wo_ref, gf_ref, wr_ref, br_ref, upper_ref,
                x2_ref, hf_ref, idx_ref, gate_ref, cnt_ref, base_ref):
    x = x_ref[...]
    h = _rms(x, gx_ref[...]).astype(bf16)
    q = jnp.dot(h, wq_ref[...], preferred_element_type=f32).astype(bf16)
    k = k_ref[...]
    v = v_ref[...]
    heads = [slice(hd * XA_HEAD, (hd + 1) * XA_HEAD) for hd in range(XA_HEADS)]
    scs = [lax.dot_general(q[:, sl], k[:, sl], (((1,), (1,)), ((), ())), preferred_element_type=f32)
           * (XA_HEAD ** -0.5) for sl in heads]
    ps = []
    for sc in scs:
        e = jnp.exp(sc - jnp.max(sc, axis=-1, keepdims=True))
        ps.append((e / jnp.sum(e, axis=-1, keepdims=True)).astype(bf16))
    o = jnp.concatenate([jnp.dot(p, v[:, sl], preferred_element_type=f32).astype(bf16)
                         for p, sl in zip(ps, heads)], axis=1)
    x2 = x + jnp.dot(o, wo_ref[...], preferred_element_type=f32)
    x2_ref[...] = x2

    hf = _rms(x2, gf_ref[...])
    _store_packed_rows(hf_ref, hf)
    logits = lax.dot_general(wr_ref[...], hf.astype(bf16), (((1,), (1,)), ((), ())),
                             preferred_element_type=f32) + br_ref[...]
    erow = lax.broadcasted_iota(i32, logits.shape, 0)
    neg = jnp.float32(-jnp.inf)
    cur = logits
    vals = []
    idxs = []
    for _ in range(TOP_K):
        m = jnp.max(cur, axis=0, keepdims=True)
        am = jnp.min(jnp.where(cur == m, erow, N_EXPERTS), axis=0, keepdims=True)
        vals.append(m)
        idxs.append(am)
        cur = jnp.where(erow == am, neg, cur)
    es = [jnp.exp(vk - vals[0]) for vk in vals]
    den = es[0] + es[1] + es[2] + es[3]

    @pl.when(pl.program_id(0) == 0)
    def _():
        base_ref[...] = jnp.zeros_like(base_ref)

    onehot = [jnp.where(erow == am, 1.0, 0.0) for am in idxs]
    cnt = (onehot[0] + onehot[1]) + (onehot[2] + onehot[3])
    base = base_ref[:, 0:1]
    prior = jnp.dot(cnt.astype(bf16), upper_ref[...], preferred_element_type=f32) + base
    base_ref[...] = jnp.broadcast_to(base + jnp.sum(cnt, axis=1, keepdims=True), base_ref.shape)
    cnt_ref[...] = base_ref[...]

    orow = lax.broadcasted_iota(i32, idx_ref.shape, 0)
    idx_out = jnp.zeros(idx_ref.shape, i32)
    gate_out = jnp.zeros(gate_ref.shape, f32)
    for kq in range(TOP_K):
        rank = jnp.sum(prior * onehot[kq], axis=0, keepdims=True).astype(i32)
        idx_out = jnp.where(orow == kq, idxs[kq], idx_out)
        idx_out = jnp.where(orow == TOP_K + kq, rank, idx_out)
        gate_out = jnp.where(orow == kq, es[kq] / den, gate_out)
    idx_ref[...] = idx_out
    gate_ref[...] = gate_out


def _xattn(x1, kmem, vmem, g_xa, wq_bf, wo_bf, g_ffn, wr_t, br_col, bsz, s, mlen, tm):
    t = bsz * s
    nt = s // tm
    xb = pl.BlockSpec((tm, D_MODEL), lambda i: (i, 0))
    mb = pl.BlockSpec((mlen, D_MODEL), lambda i: (i // nt, 0))
    lb = pl.BlockSpec((2 * TOP_K, tm), lambda i: (0, i))
    upper = jnp.triu(jnp.ones((tm, tm), bf16), k=1)
    return pl.pallas_call(
        _xattn_body,
        grid=(t // tm,),
        in_specs=[xb, mb, mb, _full(g_xa.shape), _full(wq_bf.shape), _full(wo_bf.shape),
                  _full(g_ffn.shape), _full(wr_t.shape), _full(br_col.shape), _full(upper.shape)],
        out_specs=[xb, pl.BlockSpec((tm * X_TILE, LANES), lambda i: (i, 0)), lb, lb, _full((N_EXPERTS, LANES))],
        out_shape=[jax.ShapeDtypeStruct((t, D_MODEL), f32), jax.ShapeDtypeStruct((t * X_TILE, LANES), u32),
                   jax.ShapeDtypeStruct((2 * TOP_K, t), i32), jax.ShapeDtypeStruct((2 * TOP_K, t), f32),
                   jax.ShapeDtypeStruct((N_EXPERTS, LANES), f32)],
        scratch_shapes=[pltpu.VMEM((N_EXPERTS, LANES), f32)],
        compiler_params=_cparams(("arbitrary",)),
        name="xattn",
    )(x1, kmem, vmem, g_xa, wq_bf, wo_bf, g_ffn, wr_t, br_col, upper)


DMA_UNROLL = 8


def _tile_at(ref, row):
    return ref.at[pl.ds(pl.multiple_of(row * X_TILE, X_TILE), X_TILE), :]


def _dispatch_body(dest_ref, pend_ref, hf_ref, xs_hbm, zeros_ref, sem, zsem):
    i = pl.program_id(0)
    tm = hf_ref.shape[0] // X_TILE
    zrows = MOE_BLOCK * X_TILE

    def zero_copy(e):
        start = pl.multiple_of((pend_ref[e] - MOE_BLOCK) * X_TILE, X_TILE)
        return pltpu.make_async_copy(zeros_ref, xs_hbm.at[pl.ds(start, zrows), :], zsem)

    def nonempty(e):
        return pend_ref[e] > (pend_ref[e - 1] if e else 0)

    @pl.when(i == 0)
    def _():
        zeros_ref[...] = jnp.zeros_like(zeros_ref)
        for e in range(N_EXPERTS):
            @pl.when(nonempty(e))
            def _():
                zero_copy(e).start()
        for e in range(N_EXPERTS):
            @pl.when(nonempty(e))
            def _():
                zero_copy(e).wait()

        def tail_copy(b):
            return pltpu.make_async_copy(zeros_ref, xs_hbm.at[pl.ds(pl.multiple_of(b * zrows, zrows), zrows), :], zsem)

        def tail_start(b, carry):
            tail_copy(b).start()
            return carry

        def tail_wait(b, carry):
            tail_copy(b).wait()
            return carry
        n_used = pend_ref[N_EXPERTS - 1] // MOE_BLOCK
        n_all = xs_hbm.shape[0] // zrows
        lax.fori_loop(n_used, n_all, tail_start, 0)
        lax.fori_loop(n_used, n_all, tail_wait, 0)

    def body(q, carry):
        for u in range(DMA_UNROLL):
            r = q * DMA_UNROLL + u
            src = _tile_at(hf_ref, r)
            for kq in range(TOP_K):
                pltpu.make_async_copy(src, _tile_at(xs_hbm, dest_ref[(i * tm + r) * TOP_K + kq]),
                                      sem).start(priority=kq % 2)
        return carry
    lax.fori_loop(0, tm // DMA_UNROLL, body, 0)
    for kq in range(TOP_K):
        pltpu.make_async_copy(hf_ref, xs_hbm.at[pl.ds(0, tm * X_TILE), :], sem).wait()


def _dispatch(dest_flat, pends, hf_tiles, n_blocks, tm):
    t = hf_tiles.shape[0] // X_TILE
    rows = n_blocks * MOE_BLOCK
    grid_spec = pltpu.PrefetchScalarGridSpec(
        num_scalar_prefetch=2,
        grid=(t // tm,),
        in_specs=[pl.BlockSpec((tm * X_TILE, LANES), lambda i, d, pe: (i, 0))],
        out_specs=pl.BlockSpec(memory_space=pl.ANY),
        scratch_shapes=[pltpu.VMEM((MOE_BLOCK * X_TILE, LANES), u32), pltpu.SemaphoreType.DMA(()),
                        pltpu.SemaphoreType.DMA(())],
    )
    return pl.pallas_call(
        _dispatch_body,
        grid_spec=grid_spec,
        out_shape=jax.ShapeDtypeStruct((rows * X_TILE, LANES), u32),
        compiler_params=_cparams(("arbitrary",)),
        name="dispatch",
    )(dest_flat, pends, hf_tiles)


def _moe_body(pend_ref, xs_hbm, wgu_ref, bgu_ref, wdn_ref, bdn_ref, ys_hbm, xbuf, ybuf, wgu_bf, wdn_bf,
              sem_in, sem_out):
    e = pl.program_id(0)
    blk = MOE_BLOCK * X_TILE
    end_blk = pend_ref[e] // MOE_BLOCK
    start_blk = jnp.where(e == 0, 0, pend_ref[jnp.maximum(e - 1, 0)] // MOE_BLOCK)
    nb = end_blk - start_blk

    n_total = pend_ref[N_EXPERTS - 1] // MOE_BLOCK

    def rows_of(b):
        return pl.ds(pl.multiple_of(b * blk, blk), blk)

    def x_copy(b):
        return pltpu.make_async_copy(xs_hbm.at[rows_of(b), :], xbuf.at[b % 2], sem_in.at[b % 2])

    def y_copy(b):
        return pltpu.make_async_copy(ybuf.at[b % 2], ys_hbm.at[rows_of(b), :], sem_out.at[b % 2])

    @pl.when((e == 0) & (n_total > 0))
    def _():
        x_copy(0).start(priority=1)

    @pl.when(nb > 0)
    def _():
        wgu_bf[...] = wgu_ref[0].astype(bf16)
        wdn_bf[...] = wdn_ref[0].astype(bf16)

    def block(j, carry):
        b = start_blk + j
        slot = b % 2

        @pl.when(b + 1 < n_total)
        def _():
            x_copy(b + 1).start(priority=1)

        x_copy(b).wait()

        @pl.when(b >= 2)
        def _():
            y_copy(b - 2).wait()

        xb = _load_packed_rows(xbuf.at[slot], MOE_BLOCK)
        gu = jnp.dot(xb, wgu_bf[...], preferred_element_type=f32) + bgu_ref[0]
        gate = jnp.minimum(gu[:, :D_FF], SWIGLU_LIMIT)
        up = jnp.clip(gu[:, D_FF:], -SWIGLU_LIMIT, SWIGLU_LIMIT)
        act = (up + 1.0) * (gate * jax.nn.sigmoid(SWIGLU_ALPHA * gate))
        y = jnp.dot(act.astype(bf16), wdn_bf[...], preferred_element_type=f32) + bdn_ref[0]
        _store_packed_rows(ybuf.at[slot], y)
        y_copy(b).start(priority=1)
        return carry

    lax.fori_loop(0, nb, block, 0)

    @pl.when(e == N_EXPERTS - 1)
    def _():
        @pl.when(n_total >= 2)
        def _():
            y_copy(n_total - 2).wait()

        @pl.when(n_total >= 1)
        def _():
            y_copy(n_total - 1).wait()

        n_all = ys_hbm.shape[0] // blk
        ybuf[0] = jnp.zeros(ybuf.shape[1:], ybuf.dtype)

        def tail_copy(b):
            return pltpu.make_async_copy(ybuf.at[0], ys_hbm.at[pl.ds(pl.multiple_of(b * blk, blk), blk), :],
                                         sem_out.at[0])

        def tail_start(b, carry):
            tail_copy(b).start()
            return carry

        def tail_wait(b, carry):
            tail_copy(b).wait()
            return carry
        lax.fori_loop(end_blk, n_all, tail_start, 0)
        lax.fori_loop(end_blk, n_all, tail_wait, 0)


def _moe(pends, xs_tiles, w_gu, b_gu, w_dn, b_dn, n_blocks):
    rows = n_blocks * MOE_BLOCK
    grid_spec = pltpu.PrefetchScalarGridSpec(
        num_scalar_prefetch=1,
        grid=(N_EXPERTS,),
        in_specs=[
            pl.BlockSpec(memory_space=pl.ANY),
            pl.BlockSpec((1, D_MODEL, 2 * D_FF), lambda e, pe: (e, 0, 0)),
            pl.BlockSpec((1, 1, 2 * D_FF), lambda e, pe: (e, 0, 0)),
            pl.BlockSpec((1, D_FF, D_MODEL), lambda e, pe: (e, 0, 0)),
            pl.BlockSpec((1, 1, D_MODEL), lambda e, pe: (e, 0, 0)),
        ],
        out_specs=pl.BlockSpec(memory_space=pl.ANY),
        scratch_shapes=[pltpu.VMEM((2, MOE_BLOCK * X_TILE, LANES), u32),
                        pltpu.VMEM((2, MOE_BLOCK * X_TILE, LANES), u32),
                        pltpu.VMEM((D_MODEL, 2 * D_FF), bf16), pltpu.VMEM((D_FF, D_MODEL), bf16),
                        pltpu.SemaphoreType.DMA((2,)), pltpu.SemaphoreType.DMA((2,))],
    )
    return pl.pallas_call(
        _moe_body,
        grid_spec=grid_spec,
        out_shape=jax.ShapeDtypeStruct((rows * X_TILE, LANES), u32),
        compiler_params=_cparams(("arbitrary",)),
        name="moe",
    )(pends, xs_tiles, w_gu, b_gu.reshape(N_EXPERTS, 1, 2 * D_FF), w_dn, b_dn.reshape(N_EXPERTS, 1, D_MODEL))


def _combine_body(pos_ref, ys_hbm, x_ref, gate_ref, g_ref, o_ref, buf, sem):
    i = pl.program_id(0)
    n = pl.num_programs(0)
    tc = x_ref.shape[0]
    slot = i % 2
    slot_rows = TOP_K * tc * X_TILE

    def start(step, sl):
        def body(q, carry):
            for u in range(DMA_UNROLL):
                r = q * DMA_UNROLL + u
                for kq in range(TOP_K):
                    dst = _tile_at(buf, (sl * TOP_K + kq) * tc + r)
                    pltpu.make_async_copy(_tile_at(ys_hbm, pos_ref[(step * tc + r) * TOP_K + kq]), dst,
                                          sem.at[sl]).start(priority=kq % 2)
            return carry
        lax.fori_loop(0, tc // DMA_UNROLL, body, 0)

    @pl.when(i == 0)
    def _():
        start(0, 0)

    @pl.when(i + 1 < n)
    def _():
        start(i + 1, 1 - slot)

    off = pl.multiple_of(slot * slot_rows, slot_rows)
    pltpu.make_async_copy(ys_hbm.at[pl.ds(0, slot_rows), :], buf.at[pl.ds(off, slot_rows), :], sem.at[slot]).wait()
    gates = gate_ref[...]
    acc = x_ref[...]
    for kq in range(TOP_K):
        rows = _load_packed_rows(buf, tc, off + kq * tc * X_TILE)
        acc = acc + gates[:, kq:kq + 1] * rows.astype(f32)
    o_ref[...] = _rms(acc, g_ref[...])


def _combine(pos_flat, ys_tiles, x2, gate_pad, g_final, tc):
    t = x2.shape[0]
    grid_spec = pltpu.PrefetchScalarGridSpec(
        num_scalar_prefetch=1,
        grid=(t // tc,),
        in_specs=[pl.BlockSpec(memory_space=pl.ANY),
                  pl.BlockSpec((tc, D_MODEL), lambda i, p: (i, 0)),
                  pl.BlockSpec((tc, LANES), lambda i, p: (i, 0)),
                  pl.BlockSpec((1, D_MODEL), lambda i, p: (0, 0))],
        out_specs=pl.BlockSpec((tc, D_MODEL), lambda i, p: (i, 0)),
        scratch_shapes=[pltpu.VMEM((2 * TOP_K * tc * X_TILE, LANES), u32), pltpu.SemaphoreType.DMA((2,))],
    )
    return pl.pallas_call(
        _combine_body,
        grid_spec=grid_spec,
        out_shape=jax.ShapeDtypeStruct((t, D_MODEL), f32),
        compiler_params=_cparams(("arbitrary",)),
        name="combine",
    )(pos_flat, ys_tiles, x2, gate_pad, g_final)


def _routing(top_idx, rank, counts, t):
    n_assign = t * TOP_K
    experts = jnp.arange(N_EXPERTS, dtype=i32)
    padded = (counts + MOE_BLOCK - 1) // MOE_BLOCK * MOE_BLOCK
    pends = jnp.cumsum(padded).astype(i32)
    pstarts = pends - padded
    start_of = jnp.sum(jnp.where(top_idx[:, :, None] == experts, pstarts, 0), axis=-1)
    dest = (start_of + rank).astype(i32).reshape(n_assign)
    n_blocks = (n_assign + N_EXPERTS * (MOE_BLOCK - 1) + MOE_BLOCK - 1) // MOE_BLOCK
    return dest, pends, n_blocks


def _block_diag(w):
    n, bi, bj = w.shape
    eye = jnp.eye(n, dtype=w.dtype)
    return jnp.einsum('nij,nm->nimj', w, eye).reshape(n * bi, n * bj)


def _layer(x2d, mem2d, bsz, s, mlen, p):
    t = bsz * s
    row = lambda a: a.reshape(1, -1)
    ones_blk = _block_diag(jnp.ones((RWKV_W // RWKV_HEAD, RWKV_HEAD, RWKV_HEAD), bf16))

    xl, gl, ur = _inproj(x2d, row(p['norm_mix_g']), p['w_in'].astype(bf16), tm=min(512, t))

    wg = jnp.concatenate([_block_diag(p['lru_wx']), _block_diag(p['lru_wa'])], axis=1).astype(bf16)
    bg = jnp.concatenate([p['lru_bx'], p['lru_ba']]).reshape(1, -1)
    y_lru = _lru(xl, gl, p['conv_w'], row(p['conv_b']), wg, bg, row(p['lru_lambda']), bsz, s, ts=min(512, s))

    pairs = lambda a: a.reshape(N_PAIRS, PAIR)
    rows8 = lambda rows: jnp.pad(jnp.stack(rows, axis=1), ((0, 0), (0, SUBLANES - len(rows)), (0, 0)))
    mu = p['rwkv_mu']
    shared = lambda a: jnp.broadcast_to(a, (N_PAIRS, PAIR))
    mu5 = rows8([pairs(mu[:RWKV_W]), pairs(mu[RWKV_W:2 * RWKV_W]), pairs(mu[2 * RWKV_W:3 * RWKV_W]),
                 shared(mu[3 * RWKV_W:3 * RWKV_W + PAIR]), shared(mu[3 * RWKV_W + PAIR:])])
    par5 = rows8([pairs(p['rwkv_w0']), pairs(p['rwkv_a0']), pairs(p['rwkv_k_k']), pairs(p['rwkv_k_a']),
                  pairs(p['rwkv_r_k'])])
    zl = jnp.zeros((DECAY_LORA, RWKV_W), f32)
    w_dec = jnp.concatenate([p['rwkv_w_up'], zl], axis=0).reshape(PAIR, N_PAIRS, PAIR)
    w_icl = jnp.concatenate([zl, p['rwkv_a_up']], axis=0).reshape(PAIR, N_PAIRS, PAIR)
    wproj = jnp.concatenate([w_dec, w_icl], axis=2).transpose(1, 0, 2).astype(bf16)
    gup = p['rwkv_g_up'].reshape(GATE_LORA, N_PAIRS, PAIR).transpose(1, 0, 2).astype(bf16)
    gm, sp, rc, yp, bon, v, g = _rwkv_chunk(ur, mu5, par5, wproj, gup, bsz, s, ts=min(1024, s))
    y_scan = _rwkv_state(gm, sp, rc, yp, bsz, s, nck=min(4, s // RWKV_CHUNK))

    x1 = _outproj(x2d, y_lru, y_scan, bon, v, g, row(p['rwkv_lnx_g']), row(p['rwkv_lnx_b']), ones_blk,
                  p['w_out'].astype(bf16), tm=min(512, t))

    kmem, vmem = _memkv(mem2d, row(p['norm_mem_g']), p['xa_wk'].astype(bf16), p['xa_wv'].astype(bf16),
                        tm=min(512, bsz * mlen))
    x2, hf, route, gates, cnt_pad = _xattn(x1, kmem, vmem, row(p['norm_xa_g']), p['xa_wq'].astype(bf16),
                                           p['xa_wo'].astype(bf16), row(p['norm_ffn_g']),
                                           p['w_router'].T.astype(bf16), p['b_router'].reshape(-1, 1),
                                           bsz, s, mlen, tm=min(512, s))

    counts = cnt_pad[:, 0].astype(i32)
    dest, pends, n_blocks = _routing(route[:TOP_K].T, route[TOP_K:].T, counts, t)
    gate_pad = jnp.pad(gates[:TOP_K].T, ((0, 0), (0, LANES - TOP_K)))
    xs = _dispatch(dest, pends, hf, n_blocks, tm=min(4096, t))
    ys = _moe(pends, xs, p['w_gu'], p['b_gu'], p['w_dn'], p['b_dn'], n_blocks)
    return _combine(dest, ys, x2, gate_pad, row(p['final_norm_g']), tc=min(256, t))


def kernel(x, mem, norm_mix_g, w_in, conv_w, conv_b, lru_wx, lru_bx, lru_wa, lru_ba, lru_lambda, rwkv_mu, rwkv_w0, rwkv_w_up, rwkv_a0, rwkv_a_up, rwkv_g_up, rwkv_k_k, rwkv_k_a, rwkv_r_k, rwkv_lnx_g, rwkv_lnx_b, w_out, norm_xa_g, norm_mem_g, xa_wq, xa_wk, xa_wv, xa_wo, norm_ffn_g, w_router, b_router, w_gu, b_gu, w_dn, b_dn, final_norm_g):
    bsz, s, d = x.shape
    mlen = mem.shape[1]
    assert d == D_MODEL and w_in.shape[0] == 1
    p = dict(norm_mix_g=norm_mix_g[0], w_in=w_in[0], conv_w=conv_w[0], conv_b=conv_b[0], lru_wx=lru_wx[0],
             lru_bx=lru_bx[0], lru_wa=lru_wa[0], lru_ba=lru_ba[0], lru_lambda=lru_lambda[0],
             rwkv_mu=rwkv_mu[0], rwkv_w0=rwkv_w0[0], rwkv_w_up=rwkv_w_up[0], rwkv_a0=rwkv_a0[0],
             rwkv_a_up=rwkv_a_up[0], rwkv_g_up=rwkv_g_up[0], rwkv_k_k=rwkv_k_k[0], rwkv_k_a=rwkv_k_a[0],
             rwkv_r_k=rwkv_r_k[0].reshape(-1), rwkv_lnx_g=rwkv_lnx_g[0], rwkv_lnx_b=rwkv_lnx_b[0],
             w_out=w_out[0], norm_xa_g=norm_xa_g[0], norm_mem_g=norm_mem_g[0], xa_wq=xa_wq[0],
             xa_wk=xa_wk[0], xa_wv=xa_wv[0], xa_wo=xa_wo[0], norm_ffn_g=norm_ffn_g[0],
             w_router=w_router[0], b_router=b_router[0], w_gu=w_gu[0], b_gu=b_gu[0], w_dn=w_dn[0],
             b_dn=b_dn[0], final_norm_g=final_norm_g)
    out = _layer(x.reshape(bsz * s, d), mem.reshape(bsz * mlen, d), bsz, s, mlen, p)
    return out.reshape(bsz, s, d)
```

```python
import jax
import jax.numpy as jnp
from jax import lax
from jax.experimental import pallas as pl
from jax.experimental.pallas import tpu as pltpu

f32 = jnp.float32
bf16 = jnp.bfloat16
i32 = jnp.int32
u32 = jnp.uint32

D_MODEL = 1024
LRU_W = 512
RWKV_W = 512
CONV_W = 4
LRU_C = 8.0
RWKV_HEAD = 64
DECAY_LORA = 64
AAA_LORA = 64
GATE_LORA = 128
RWKV_IN = 3 * RWKV_W + DECAY_LORA + AAA_LORA + GATE_LORA
XA_HEADS = 4
XA_HEAD = D_MODEL // XA_HEADS
N_EXPERTS = 32
TOP_K = 4
D_FF = D_MODEL
SWIGLU_LIMIT = 7.0
SWIGLU_ALPHA = 1.702
EPS = 1e-6
GN_EPS = 64e-5

LANES = 128
SUBLANES = 8
RWKV_CHUNK = 64
PAIR = 2 * RWKV_HEAD
N_PAIRS = RWKV_W // PAIR
MOE_BLOCK = 256
X_TILE = D_MODEL // (2 * LANES)
V7X_VMEM_BYTES = 64 * 1024 * 1024
VMEM_LIMIT = V7X_VMEM_BYTES - 12 * 1024 * 1024


def _cparams(sem):
    return pltpu.CompilerParams(dimension_semantics=sem, vmem_limit_bytes=VMEM_LIMIT)


def _rms(x, g):
    return x * lax.rsqrt(jnp.mean(x * x, axis=-1, keepdims=True) + EPS) * g


def _full(shape):
    n = len(shape)
    return pl.BlockSpec(shape, lambda *a: (0,) * n)


def _shift_rows(x, prev8, d):
    xr = pltpu.roll(x, d, 0)
    tr = pltpu.roll(prev8, d, 0)
    row = lax.broadcasted_iota(i32, prev8.shape, 0)
    head = jnp.where(row < d, tr, xr[:SUBLANES])
    return jnp.concatenate([head, xr[SUBLANES:]], axis=0)


def _bf16_parts(x, n):
    parts = []
    for _ in range(n):
        piece = x.astype(bf16)
        parts.append(piece)
        x = x - piece.astype(f32)
    return parts


def _sum_dot(x, mask_bf, n_parts, mask_left=False):
    acc = None
    for piece in _bf16_parts(x, n_parts):
        d = (jnp.dot(mask_bf, piece, preferred_element_type=f32) if mask_left
             else jnp.dot(piece, mask_bf, preferred_element_type=f32))
        acc = d if acc is None else acc + d
    return acc


def _store_packed_rows(ref, val):
    n = val.shape[0]
    half = D_MODEL // 2
    lo = lax.bitcast_convert_type(val[:, :half].astype(bf16).astype(f32), u32) >> 16
    hi = lax.bitcast_convert_type(val[:, half:].astype(bf16).astype(f32), u32) & jnp.uint32(0xFFFF0000)
    words = hi | lo
    for j in range(X_TILE):
        ref[pl.ds(j, n, stride=X_TILE), :] = words[:, j * LANES:(j + 1) * LANES]


def _load_packed_rows(ref, n, off=0):
    ws = [ref[pl.ds(off + j, n, stride=X_TILE), :] for j in range(X_TILE)]
    lo = [lax.bitcast_convert_type(w << 16, f32).astype(bf16) for w in ws]
    hi = [lax.bitcast_convert_type(w & jnp.uint32(0xFFFF0000), f32).astype(bf16) for w in ws]
    return jnp.concatenate(lo + hi, axis=1)


def _inproj_body(x_ref, g_ref, w_ref, xl_ref, gl_ref, ur_ref):
    h = _rms(x_ref[...], g_ref[...])
    u = jnp.dot(h.astype(bf16), w_ref[...], preferred_element_type=f32)
    xl_ref[...] = u[:, :LRU_W]
    gl_ref[...] = u[:, LRU_W:2 * LRU_W]
    ur_ref[...] = u[:, 2 * LRU_W:]


def _inproj(x2d, g, w_in_bf, tm):
    t = x2d.shape[0]
    return pl.pallas_call(
        _inproj_body,
        grid=(t // tm,),
        in_specs=[pl.BlockSpec((tm, D_MODEL), lambda i: (i, 0)), _full((1, D_MODEL)),
                  _full(w_in_bf.shape)],
        out_specs=[pl.BlockSpec((tm, LRU_W), lambda i: (i, 0)),
                   pl.BlockSpec((tm, LRU_W), lambda i: (i, 0)),
                   pl.BlockSpec((tm, RWKV_IN), lambda i: (i, 0))],
        out_shape=[jax.ShapeDtypeStruct((t, LRU_W), f32), jax.ShapeDtypeStruct((t, LRU_W), f32),
                   jax.ShapeDtypeStruct((t, RWKV_IN), f32)],
        compiler_params=_cparams(("parallel",)),
        name="inproj",
    )(x2d, g, w_in_bf)


def _lru_body(xl_ref, gl_ref, cw_ref, cb_ref, wg_ref, bg_ref, lam_ref, o_ref, tail_ref, h_ref):
    ts = xl_ref.shape[0]

    @pl.when(pl.program_id(1) == 0)
    def _():
        tail_ref[...] = jnp.zeros_like(tail_ref)
        h_ref[...] = jnp.zeros_like(h_ref)

    x = xl_ref[...]
    tail = tail_ref[...]
    cw = cw_ref[...]
    xc = cb_ref[...] + cw[CONV_W - 1:CONV_W] * x
    for d in range(1, CONV_W):
        xc = xc + cw[CONV_W - 1 - d:CONV_W - d] * _shift_rows(x, tail, d)
    tail_ref[...] = x[ts - SUBLANES:]

    gates = jax.nn.sigmoid(jnp.dot(xc.astype(bf16), wg_ref[...], preferred_element_type=f32) + bg_ref[...])
    gx = gates[:, :LRU_W]
    ga = gates[:, LRU_W:]
    log_a = -LRU_C * ga * jax.nn.softplus(-lam_ref[...])
    a = jnp.exp(log_a)
    b = jnp.sqrt(-jnp.tanh(log_a) * (a * a + 1.0)) * gx * xc

    row = lax.broadcasted_iota(i32, (ts, LRU_W), 0) % SUBLANES
    d = 1
    while d < SUBLANES:
        keep = row >= d
        a_s = jnp.where(keep, pltpu.roll(a, d, 0), 1.0)
        b_s = jnp.where(keep, pltpu.roll(b, d, 0), 0.0)
        b = a * b_s + b
        a = a * a_s
        d *= 2
    carry = h_ref[SUBLANES - 1:SUBLANES, :]
    groups = []
    for q in range(ts // SUBLANES):
        rows = slice(q * SUBLANES, (q + 1) * SUBLANES)
        hq = b[rows] + a[rows] * carry
        carry = hq[SUBLANES - 1:SUBLANES, :]
        groups.append(hq)
    h = jnp.concatenate(groups, axis=0)
    h_ref[...] = jnp.broadcast_to(carry, h_ref.shape)
    o_ref[...] = (h * jax.nn.gelu(gl_ref[...])).astype(o_ref.dtype)


def _lru(xl, gl, cw, cb, wg_bf, bg, lam, bsz, s, ts):
    nt = s // ts
    blk = pl.BlockSpec((ts, LRU_W), lambda b, i: (b * nt + i, 0))
    return pl.pallas_call(
        _lru_body,
        grid=(bsz, nt),
        in_specs=[blk, blk, _full(cw.shape), _full(cb.shape), _full(wg_bf.shape), _full(bg.shape),
                  _full(lam.shape)],
        out_specs=blk,
        out_shape=jax.ShapeDtypeStruct((bsz * s, LRU_W), bf16),
        scratch_shapes=[pltpu.VMEM((SUBLANES, LRU_W), f32), pltpu.VMEM((SUBLANES, LRU_W), f32)],
        compiler_params=_cparams(("parallel", "arbitrary")),
        name="lru",
    )(xl, gl, cw, cb, wg_bf, bg, lam)


def _mm_x3(a, b_parts):
    ah, al = _bf16_parts(a, 2)
    bh, bl = b_parts
    d = lambda x, y: jnp.dot(x, y, preferred_element_type=f32)
    return d(ah, bh) + (d(ah, bl) + d(al, bh))


def _mm(a, b):
    return jnp.dot(a.astype(bf16), b.astype(bf16), preferred_element_type=f32)


def _mm_nt(a, b):
    return lax.dot_general(a.astype(bf16), b.astype(bf16), (((1,), (1,)), ((), ())), preferred_element_type=f32)


def _mm_tn(a, b):
    return lax.dot_general(a.astype(bf16), b.astype(bf16), (((0,), (0,)), ((), ())), preferred_element_type=f32)


def _bd(x):
    m0 = lax.broadcasted_iota(i32, x.shape, 1) < RWKV_HEAD
    zero = jnp.zeros_like(x)
    return jnp.concatenate([jnp.where(m0, x, zero), jnp.where(m0, zero, x)], axis=0)


def _side_by_side(d):
    h = d.shape[0] // 2
    m0 = lax.broadcasted_iota(i32, (h, d.shape[1]), 1) < RWKV_HEAD
    return jnp.where(m0, d[:h], d[h:])


def _chunk_maps(chunks):
    c = RWKV_CHUNK
    ri = lax.broadcasted_iota(i32, (c, PAIR), 0)
    ji = lax.broadcasted_iota(i32, (c, PAIR), 1) % RWKV_HEAD
    strict = ji < ri
    incl = ji <= ri
    diag = ji == ri
    eye = jnp.where(diag, 1.0, 0.0).astype(f32)
    each = lambda f, *ls: [f(*xs) for xs in zip(*ls)]
    cat0 = lambda *xs: jnp.concatenate(xs, axis=0)
    cat1 = lambda *xs: jnp.concatenate(xs, axis=1)
    tb = lambda x: x.astype(bf16)

    ats, bts, kts, rts, vs, cls = [list(x) for x in zip(*chunks)]
    pcs = each(lambda cl: jnp.exp(cl[c - 1:c, :]), cls)
    bd_a = each(lambda x: tb(_bd(x)), ats)
    bd_v = each(lambda x: tb(_bd(x)), vs)

    aa = each(lambda a, r, b, k: _mm_nt(cat0(tb(a), tb(r)), cat0(tb(_bd(b)), tb(_bd(k)))), ats, rts, bts, kts)
    l_ab = each(lambda x: tb(jnp.where(strict, x[:c, :PAIR], 0.0)), aa)
    a_k = each(lambda x: tb(cat0(jnp.where(strict, x[:c, PAIR:], 0.0), jnp.where(incl, x[c:, PAIR:], 0.0))), aa)
    a_rb = each(lambda x: tb(jnp.where(incl, x[c:, :PAIR], 0.0)), aa)

    tinv = each(lambda x: eye + x, l_ab)
    lp = each(lambda x: tb(_mm(x, _bd(x))), l_ab)
    p = 2
    while 2 * p < c:
        x2 = each(lambda t, x: _mm(cat0(tb(t), x), _bd(x)), tinv, lp)
        tinv = each(lambda t, x: t + x[:c], tinv, x2)
        lp = each(lambda x: tb(x[c:]), x2)
        p *= 2
    tinv = each(lambda t, x: t + _mm(t, _bd(x)), tinv, lp)

    wy = each(_mm, a_k, bd_v)
    za = each(lambda t, w, a: _mm(t, cat1(tb(_bd(w[:c])), a)), tinv, wy, bd_a)
    zp = each(lambda x: x[:, :PAIR], za)
    ac = each(lambda x: x[:, PAIR:], za)
    y1 = each(lambda ar, z, a: _mm(ar, cat1(tb(_bd(z)), tb(_bd(a)))), a_rb, zp, ac)
    yp = each(lambda w, y: w[c:] + y[:, :PAIR], wy, y1)
    rc = each(lambda r, y: r + y[:, PAIR:], rts, y1)
    sm = each(lambda b, pc, z, a: _mm_tn(b * pc, cat1(z, a)), bts, pcs, zp, ac)
    kv = each(lambda k, pc, v: _mm_tn(k * pc, v), kts, pcs, vs)
    sp = each(lambda x, m: _side_by_side(x) + _side_by_side(m[:, :PAIR]), kv, sm)
    g = each(lambda pc, m: jnp.where(diag, jnp.broadcast_to(pc, (c, PAIR)), 0.0) + _side_by_side(m[:, PAIR:]), pcs, sm)
    return list(zip(g, sp, rc, yp))


N_UCOLS = 5


def _rwkv_chunk_body(ur_r, ur_k, ur_v, ur_lo, ur_dg, mu_ref, par_ref, wproj_ref, gup_ref, ones_ref, tri_ref,
                     g_ref, sp_ref, rc_ref, yp_ref, bon_ref, v_ref, gate_ref, prev_ref):
    ts = ur_r.shape[0]
    c = RWKV_CHUNK

    @pl.when(pl.program_id(2) == 0)
    def _():
        prev_ref[...] = jnp.zeros_like(prev_ref)

    mixed = []
    for j, ref in enumerate((ur_r, ur_k, ur_v, ur_lo, ur_dg)):
        u0 = ref[...]
        ls = slice(j * LANES, (j + 1) * LANES)
        us = _shift_rows(u0, prev_ref[:, ls], 1)
        prev_ref[:, ls] = u0[ts - SUBLANES:]
        mixed.append(u0 + (us - u0) * mu_ref[0, j:j + 1, :])
    r, k, v, lora, dg = mixed
    w0, a0, k_k, k_a, r_k = [par_ref[0, j:j + 1, :] for j in range(5)]

    lane = lax.broadcasted_iota(i32, lora.shape, 1)
    lora = jnp.where(lane < DECAY_LORA, jnp.tanh(lora), lora)
    proj = jnp.dot(lora.astype(bf16), wproj_ref[0], preferred_element_type=f32)
    w = -jax.nn.softplus(-(w0 + proj[:, :PAIR])) - 0.5
    lw = -jnp.exp(w)
    a = jax.nn.sigmoid(a0 + proj[:, PAIR:])
    gate_ref[...] = jnp.dot(jax.nn.sigmoid(dg).astype(bf16), gup_ref[0], preferred_element_type=f32)

    ones = ones_ref[...]
    kk = k * k_k
    ss = _sum_dot(kk * kk, ones, 2)
    kk = kk / jnp.maximum(jnp.sqrt(ss), 1e-12)
    k2 = k * (1.0 + (a - 1.0) * k_a)
    bon_ref[...] = _sum_dot(r * k2 * r_k, ones, 2)
    v_ref[...] = v

    tri = tri_ref[...]
    grp = tri.shape[0]
    cl = jnp.concatenate([_sum_dot(lw[q * grp:(q + 1) * grp], tri, 3, mask_left=True) for q in range(ts // grp)],
                         axis=0)
    e_neg = jnp.exp(-cl)
    at = -kk * jnp.exp(cl - lw)
    bt = kk * a * e_neg
    kt = k2 * e_neg
    rt = r * jnp.exp(cl)

    sls = [slice(j * c, (j + 1) * c) for j in range(ts // c)]
    outs = _chunk_maps([(at[sl], bt[sl], kt[sl], rt[sl], v[sl], cl[sl]) for sl in sls])
    for sl, (g, sp, rc, yp) in zip(sls, outs):
        g_ref[sl, :] = g
        sp_ref[sl, :] = sp
        rc_ref[sl, :] = rc
        yp_ref[sl, :] = yp


def _rwkv_chunk(ur, mu5, par5, wproj, gup, bsz, s, ts):
    t = bsz * s
    nt = s // ts
    col = lambda blk: pl.BlockSpec((ts, PAIR), lambda p, b, i, blk=blk: (b * nt + i, blk(p)))
    nrw = RWKV_W // PAIR
    ucols = [col(lambda p: p), col(lambda p: nrw + p), col(lambda p: 2 * nrw + p), col(lambda p: 3 * nrw),
             col(lambda p: 3 * nrw + 1)]
    per_pair = lambda a: pl.BlockSpec((1,) + a.shape[1:], lambda p, b, i: (p, 0, 0))
    ones_pair = _block_diag(jnp.ones((2, RWKV_HEAD, RWKV_HEAD), bf16))
    tri = _block_diag(jnp.tril(jnp.ones((2, RWKV_CHUNK, RWKV_CHUNK), bf16)))
    oblk = pl.BlockSpec((ts, PAIR), lambda p, b, i: (b * nt + i, p))
    osh = jax.ShapeDtypeStruct((t, RWKV_W), f32)
    return pl.pallas_call(
        _rwkv_chunk_body,
        grid=(N_PAIRS, bsz, nt),
        in_specs=ucols + [per_pair(mu5), per_pair(par5), per_pair(wproj), per_pair(gup), _full(ones_pair.shape),
                          _full(tri.shape)],
        out_specs=[oblk] * 7,
        out_shape=[osh] * 7,
        scratch_shapes=[pltpu.VMEM((SUBLANES, N_UCOLS * LANES), f32)],
        compiler_params=_cparams(("parallel", "parallel", "arbitrary")),
        name="rwkv_chunk",
    )(ur, ur, ur, ur, ur, mu5, par5, wproj, gup, ones_pair, tri)


def _rwkv_state_body(g_ref, sp_ref, rc_ref, yp_ref, y_ref, s_ref):
    @pl.when(pl.program_id(0) == 0)
    def _():
        s_ref[...] = jnp.zeros_like(s_ref)

    c = RWKV_CHUNK
    bsz = g_ref.shape[0]
    nck = g_ref.shape[1] // c
    chains = [(b, slice(p * PAIR, (p + 1) * PAIR)) for b in range(bsz) for p in range(N_PAIRS)]
    states = [s_ref[b, :, ls] for b, ls in chains]
    for j in range(nck):
        sl = slice(j * c, (j + 1) * c)
        prods = [_mm_x3(jnp.concatenate([rc_ref[b, sl, ls], g_ref[b, sl, ls]], axis=0), _bf16_parts(_bd(s), 2))
                 for (b, ls), s in zip(chains, states)]
        for (b, ls), pr in zip(chains, prods):
            y_ref[b, sl, ls] = yp_ref[b, sl, ls] + pr[:c]
        states = [pr[c:] + sp_ref[b, sl, ls] for (b, ls), pr in zip(chains, prods)]
    for (b, ls), s in zip(chains, states):
        s_ref[b, :, ls] = s


def _rwkv_state(g, sp, rc, yp, bsz, s, nck):
    rows = nck * RWKV_CHUNK
    rblk = pl.BlockSpec((bsz, rows, RWKV_W), lambda i: (0, i, 0))
    r3 = lambda a: a.reshape(bsz, s, RWKV_W)
    y = pl.pallas_call(
        _rwkv_state_body,
        grid=(s // rows,),
        in_specs=[rblk] * 4,
        out_specs=rblk,
        out_shape=jax.ShapeDtypeStruct((bsz, s, RWKV_W), f32),
        scratch_shapes=[pltpu.VMEM((bsz, RWKV_CHUNK, RWKV_W), f32)],
        compiler_params=_cparams(("arbitrary",)),
        name="rwkv_state",
    )(r3(g), r3(sp), r3(rc), r3(yp))
    return y.reshape(bsz * s, RWKV_W)


def _outproj_body(x_ref, yl_ref, ys_ref, bon_ref, v_ref, g_ref, lg_ref, lb_ref, ones_ref, w_ref, o_ref):
    y = ys_ref[...]
    ones = ones_ref[...]
    inv_n = 1.0 / RWKV_HEAD
    mean = _sum_dot(y, ones, 2) * inv_n
    yc = y - mean
    var = _sum_dot(yc * yc, ones, 2) * inv_n
    yn = yc * lax.rsqrt(var + GN_EPS) * lg_ref[...] + lb_ref[...]
    yr = (yn + bon_ref[...] * v_ref[...]) * g_ref[...]
    cat = jnp.concatenate([yl_ref[...], yr.astype(bf16)], axis=1)
    o_ref[...] = x_ref[...] + jnp.dot(cat, w_ref[...], preferred_element_type=f32)


def _outproj(x2d, y_lru, y_scan, bon, v, g, lnx_g, lnx_b, ones_blk, w_out_bf, tm):
    t = x2d.shape[0]
    xb = pl.BlockSpec((tm, D_MODEL), lambda i: (i, 0))
    hb = pl.BlockSpec((tm, RWKV_W), lambda i: (i, 0))
    return pl.pallas_call(
        _outproj_body,
        grid=(t // tm,),
        in_specs=[xb, hb, hb, hb, hb, hb, _full(lnx_g.shape), _full(lnx_b.shape), _full(ones_blk.shape),
                  _full(w_out_bf.shape)],
        out_specs=xb,
        out_shape=jax.ShapeDtypeStruct((t, D_MODEL), f32),
        compiler_params=_cparams(("parallel",)),
        name="outproj",
    )(x2d, y_lru, y_scan, bon, v, g, lnx_g, lnx_b, ones_blk, w_out_bf)


def _memkv_body(m_ref, g_ref, wk_ref, wv_ref, k_ref, v_ref):
    h = _rms(m_ref[...], g_ref[...]).astype(bf16)
    k_ref[...] = jnp.dot(h, wk_ref[...], preferred_element_type=f32).astype(bf16)
    v_ref[...] = jnp.dot(h, wv_ref[...], preferred_element_type=f32).astype(bf16)


def _memkv(mem2d, g, wk_bf, wv_bf, tm):
    t = mem2d.shape[0]
    blk = pl.BlockSpec((tm, D_MODEL), lambda i: (i, 0))
    sh = jax.ShapeDtypeStruct((t, D_MODEL), bf16)
    return pl.pallas_call(
        _memkv_body,
        grid=(t // tm,),
        in_specs=[blk, _full(g.shape), _full(wk_bf.shape), _full(wv_bf.shape)],
        out_specs=[blk, blk],
        out_shape=[sh, sh],
        compiler_params=_cparams(("parallel",)),
        name="memkv",
    )(mem2d, g, wk_bf, wv_bf)


def _xattn_body(x_ref, k_ref, v_ref, gx_ref, wq_ref, wo_ref, gf_ref, wr_ref, br_ref, upper_ref,
                x2_ref, hf_ref, idx_ref, gate_ref, cnt_ref, base_ref):
    x = x_ref[...]
    h = _rms(x, gx_ref[...]).astype(bf16)
    q = jnp.dot(h, wq_ref[...], preferred_element_type=f32).astype(bf16)
    k = k_ref[...]
    v = v_ref[...]
    heads = [slice(hd * XA_HEAD, (hd + 1) * XA_HEAD) for hd in range(XA_HEADS)]
    scs = [lax.dot_general(q[:, sl], k[:, sl], (((1,), (1,)), ((), ())), preferred_element_type=f32)
           * (XA_HEAD ** -0.5) for sl in heads]
    ps = []
    for sc in scs:
        e = jnp.exp(sc - jnp.max(sc, axis=-1, keepdims=True))
        ps.append((e / jnp.sum(e, axis=-1, keepdims=True)).astype(bf16))
    o = jnp.concatenate([jnp.dot(p, v[:, sl], preferred_element_type=f32).astype(bf16)
                         for p, sl in zip(ps, heads)], axis=1)
    x2 = x + jnp.dot(o, wo_ref[...], preferred_element_type=f32)
    x2_ref[...] = x2

    hf = _rms(x2, gf_ref[...])
    _store_packed_rows(hf_ref, hf)
    logits = lax.dot_general(wr_ref[...], hf.astype(bf16), (((1,), (1,)), ((), ())),
                             preferred_element_type=f32) + br_ref[...]
    erow = lax.broadcasted_iota(i32, logits.shape, 0)
    neg = jnp.float32(-jnp.inf)
    cur = logits
    vals = []
    idxs = []
    for _ in range(TOP_K):
        m = jnp.max(cur, axis=0, keepdims=True)
        am = jnp.min(jnp.where(cur == m, erow, N_EXPERTS), axis=0, keepdims=True)
        vals.append(m)
        idxs.append(am)
        cur = jnp.where(erow == am, neg, cur)
    es = [jnp.exp(vk - vals[0]) for vk in vals]
    den = es[0] + es[1] + es[2] + es[3]

    @pl.when(pl.program_id(0) == 0)
    def _():
        base_ref[...] = jnp.zeros_like(base_ref)

    onehot = [jnp.where(erow == am, 1.0, 0.0) for am in idxs]
    cnt = (onehot[0] + onehot[1]) + (onehot[2] + onehot[3])
    base = base_ref[:, 0:1]
    prior = jnp.dot(cnt.astype(bf16), upper_ref[...], preferred_element_type=f32) + base
    base_ref[...] = jnp.broadcast_to(base + jnp.sum(cnt, axis=1, keepdims=True), base_ref.shape)
    cnt_ref[...] = base_ref[...]

    orow = lax.broadcasted_iota(i32, idx_ref.shape, 0)
    idx_out = jnp.zeros(idx_ref.shape, i32)
    gate_out = jnp.zeros(gate_ref.shape, f32)
    for kq in range(TOP_K):
        rank = jnp.sum(prior * onehot[kq], axis=0, keepdims=True).astype(i32)
        idx_out = jnp.where(orow == kq, idxs[kq], idx_out)
        idx_out = jnp.where(orow == TOP_K + kq, rank, idx_out)
        gate_out = jnp.where(orow == kq, es[kq] / den, gate_out)
    idx_ref[...] = idx_out
    gate_ref[...] = gate_out


def _xattn(x1, kmem, vmem, g_xa, wq_bf, wo_bf, g_ffn, wr_t, br_col, bsz, s, mlen, tm):
    t = bsz * s
    nt = s // tm
    xb = pl.BlockSpec((tm, D_MODEL), lambda i: (i, 0))
    mb = pl.BlockSpec((mlen, D_MODEL), lambda i: (i // nt, 0))
    lb = pl.BlockSpec((2 * TOP_K, tm), lambda i: (0, i))
    upper = jnp.triu(jnp.ones((tm, tm), bf16), k=1)
    return pl.pallas_call(
        _xattn_body,
        grid=(t // tm,),
        in_specs=[xb, mb, mb, _full(g_xa.shape), _full(wq_bf.shape), _full(wo_bf.shape),
                  _full(g_ffn.shape), _full(wr_t.shape), _full(br_col.shape), _full(upper.shape)],
        out_specs=[xb, pl.BlockSpec((tm * X_TILE, LANES), lambda i: (i, 0)), lb, lb, _full((N_EXPERTS, LANES))],
        out_shape=[jax.ShapeDtypeStruct((t, D_MODEL), f32), jax.ShapeDtypeStruct((t * X_TILE, LANES), u32),
                   jax.ShapeDtypeStruct((2 * TOP_K, t), i32), jax.ShapeDtypeStruct((2 * TOP_K, t), f32),
                   jax.ShapeDtypeStruct((N_EXPERTS, LANES), f32)],
        scratch_shapes=[pltpu.VMEM((N_EXPERTS, LANES), f32)],
        compiler_params=_cparams(("arbitrary",)),
        name="xattn",
    )(x1, kmem, vmem, g_xa, wq_bf, wo_bf, g_ffn, wr_t, br_col, upper)


DMA_UNROLL = 8


def _tile_at(ref, row):
    return ref.at[pl.ds(pl.multiple_of(row * X_TILE, X_TILE), X_TILE), :]


def _dispatch_body(dest_ref, pend_ref, hf_ref, xs_hbm, zeros_ref, sem, zsem):
    i = pl.program_id(0)
    tm = hf_ref.shape[0] // X_TILE
    zrows = MOE_BLOCK * X_TILE

    def zero_copy(e):
        start = pl.multiple_of((pend_ref[e] - MOE_BLOCK) * X_TILE, X_TILE)
        return pltpu.make_async_copy(zeros_ref, xs_hbm.at[pl.ds(start, zrows), :], zsem)

    def nonempty(e):
        return pend_ref[e] > (pend_ref[e - 1] if e else 0)

    @pl.when(i == 0)
    def _():
        zeros_ref[...] = jnp.zeros_like(zeros_ref)
        for e in range(N_EXPERTS):
            @pl.when(nonempty(e))
            def _():
                zero_copy(e).start()
        for e in range(N_EXPERTS):
            @pl.when(nonempty(e))
            def _():
                zero_copy(e).wait()

        def tail_copy(b):
            return pltpu.make_async_copy(zeros_ref, xs_hbm.at[pl.ds(pl.multiple_of(b * zrows, zrows), zrows), :], zsem)

        def tail_start(b, carry):
            tail_copy(b).start()
            return carry

        def tail_wait(b, carry):
            tail_copy(b).wait()
            return carry
        n_used = pend_ref[N_EXPERTS - 1] // MOE_BLOCK
        n_all = xs_hbm.shape[0] // zrows
        lax.fori_loop(n_used, n_all, tail_start, 0)
        lax.fori_loop(n_used, n_all, tail_wait, 0)

    def body(q, carry):
        for u in range(DMA_UNROLL):
            r = q * DMA_UNROLL + u
            src = _tile_at(hf_ref, r)
            for kq in range(TOP_K):
                pltpu.make_async_copy(src, _tile_at(xs_hbm, dest_ref[(i * tm + r) * TOP_K + kq]),
                                      sem).start(priority=kq % 2)
        return carry
    lax.fori_loop(0, tm // DMA_UNROLL, body, 0)
    for kq in range(TOP_K):
        pltpu.make_async_copy(hf_ref, xs_hbm.at[pl.ds(0, tm * X_TILE), :], sem).wait()


def _dispatch(dest_flat, pends, hf_tiles, n_blocks, tm):
    t = hf_tiles.shape[0] // X_TILE
    rows = n_blocks * MOE_BLOCK
    grid_spec = pltpu.PrefetchScalarGridSpec(
        num_scalar_prefetch=2,
        grid=(t // tm,),
        in_specs=[pl.BlockSpec((tm * X_TILE, LANES), lambda i, d, pe: (i, 0))],
        out_specs=pl.BlockSpec(memory_space=pl.ANY),
        scratch_shapes=[pltpu.VMEM((MOE_BLOCK * X_TILE, LANES), u32), pltpu.SemaphoreType.DMA(()),
                        pltpu.SemaphoreType.DMA(())],
    )
    return pl.pallas_call(
        _dispatch_body,
        grid_spec=grid_spec,
        out_shape=jax.ShapeDtypeStruct((rows * X_TILE, LANES), u32),
        compiler_params=_cparams(("arbitrary",)),
        name="dispatch",
    )(dest_flat, pends, hf_tiles)


def _moe_body(pend_ref, xs_hbm, wgu_ref, bgu_ref, wdn_ref, bdn_ref, ys_hbm, xbuf, ybuf, wgu_bf, wdn_bf,
              sem_in, sem_out):
    e = pl.program_id(0)
    blk = MOE_BLOCK * X_TILE
    end_blk = pend_ref[e] // MOE_BLOCK
    start_blk = jnp.where(e == 0, 0, pend_ref[jnp.maximum(e - 1, 0)] // MOE_BLOCK)
    nb = end_blk - start_blk

    n_total = pend_ref[N_EXPERTS - 1] // MOE_BLOCK

    def rows_of(b):
        return pl.ds(pl.multiple_of(b * blk, blk), blk)

    def x_copy(b):
        return pltpu.make_async_copy(xs_hbm.at[rows_of(b), :], xbuf.at[b % 2], sem_in.at[b % 2])

    def y_copy(b):
        return pltpu.make_async_copy(ybuf.at[b % 2], ys_hbm.at[rows_of(b), :], sem_out.at[b % 2])

    @pl.when((e == 0) & (n_total > 0))
    def _():
        x_copy(0).start(priority=1)

    @pl.when(nb > 0)
    def _():
        wgu_bf[...] = wgu_ref[0].astype(bf16)
        wdn_bf[...] = wdn_ref[0].astype(bf16)

    def block(j, carry):
        b = start_blk + j
        slot = b % 2

        @pl.when(b + 1 < n_total)
        def _():
            x_copy(b + 1).start(priority=1)

        x_copy(b).wait()

        @pl.when(b >= 2)
        def _():
            y_copy(b - 2).wait()

        xb = _load_packed_rows(xbuf.at[slot], MOE_BLOCK)
        gu = jnp.dot(xb, wgu_bf[...], preferred_element_type=f32) + bgu_ref[0]
        gate = jnp.minimum(gu[:, :D_FF], SWIGLU_LIMIT)
        up = jnp.clip(gu[:, D_FF:], -SWIGLU_LIMIT, SWIGLU_LIMIT)
        act = (up + 1.0) * (gate * jax.nn.sigmoid(SWIGLU_ALPHA * gate))
        y = jnp.dot(act.astype(bf16), wdn_bf[...], preferred_element_type=f32) + bdn_ref[0]
        _store_packed_rows(ybuf.at[slot], y)
        y_copy(b).start(priority=1)
        return carry

    lax.fori_loop(0, nb, block, 0)

    @pl.when(e == N_EXPERTS - 1)
    def _():
        @pl.when(n_total >= 2)
        def _():
            y_copy(n_total - 2).wait()

        @pl.when(n_total >= 1)
        def _():
            y_copy(n_total - 1).wait()

        n_all = ys_hbm.shape[0] // blk
        ybuf[0] = jnp.zeros(ybuf.shape[1:], ybuf.dtype)

        def tail_copy(b):
            return pltpu.make_async_copy(ybuf.at[0], ys_hbm.at[pl.ds(pl.multiple_of(b * blk, blk), blk), :],
                                         sem_out.at[0])

        def tail_start(b, carry):
            tail_copy(b).start()
            return carry

        def tail_wait(b, carry):
            tail_copy(b).wait()
            return carry
        lax.fori_loop(end_blk, n_all, tail_start, 0)
        lax.fori_loop(end_blk, n_all, tail_wait, 0)


def _moe(pends, xs_tiles, w_gu, b_gu, w_dn, b_dn, n_blocks):
    rows = n_blocks * MOE_BLOCK
    grid_spec = pltpu.PrefetchScalarGridSpec(
        num_scalar_prefetch=1,
        grid=(N_EXPERTS,),
        in_specs=[
            pl.BlockSpec(memory_space=pl.ANY),
            pl.BlockSpec((1, D_MODEL, 2 * D_FF), lambda e, pe: (e, 0, 0)),
            pl.BlockSpec((1, 1, 2 * D_FF), lambda e, pe: (e, 0, 0)),
            pl.BlockSpec((1, D_FF, D_MODEL), lambda e, pe: (e, 0, 0)),
            pl.BlockSpec((1, 1, D_MODEL), lambda e, pe: (e, 0, 0)),
        ],
        out_specs=pl.BlockSpec(memory_space=pl.ANY),
        scratch_shapes=[pltpu.VMEM((2, MOE_BLOCK * X_TILE, LANES), u32),
                        pltpu.VMEM((2, MOE_BLOCK * X_TILE, LANES), u32),
                        pltpu.VMEM((D_MODEL, 2 * D_FF), bf16), pltpu.VMEM((D_FF, D_MODEL), bf16),
                        pltpu.SemaphoreType.DMA((2,)), pltpu.SemaphoreType.DMA((2,))],
    )
    return pl.pallas_call(
        _moe_body,
        grid_spec=grid_spec,
        out_shape=jax.ShapeDtypeStruct((rows * X_TILE, LANES), u32),
        compiler_params=_cparams(("arbitrary",)),
        name="moe",
    )(pends, xs_tiles, w_gu, b_gu.reshape(N_EXPERTS, 1, 2 * D_FF), w_dn, b_dn.reshape(N_EXPERTS, 1, D_MODEL))


def _combine_body(pos_ref, ys_hbm, x_ref, gate_ref, g_ref, o_ref, buf, sem):
    i = pl.program_id(0)
    n = pl.num_programs(0)
    tc = x_ref.shape[0]
    slot = i % 2
    slot_rows = TOP_K * tc * X_TILE

    def start(step, sl):
        def body(q, carry):
            for u in range(DMA_UNROLL):
                r = q * DMA_UNROLL + u
                for kq in range(TOP_K):
                    dst = _tile_at(buf, (sl * TOP_K + kq) * tc + r)
                    pltpu.make_async_copy(_tile_at(ys_hbm, pos_ref[(step * tc + r) * TOP_K + kq]), dst,
                                          sem.at[sl]).start(priority=kq % 2)
            return carry
        lax.fori_loop(0, tc // DMA_UNROLL, body, 0)

    @pl.when(i == 0)
    def _():
        start(0, 0)

    @pl.when(i + 1 < n)
    def _():
        start(i + 1, 1 - slot)

    off = pl.multiple_of(slot * slot_rows, slot_rows)
    pltpu.make_async_copy(ys_hbm.at[pl.ds(0, slot_rows), :], buf.at[pl.ds(off, slot_rows), :], sem.at[slot]).wait()
    gates = gate_ref[...]
    acc = x_ref[...]
    for kq in range(TOP_K):
        rows = _load_packed_rows(buf, tc, off + kq * tc * X_TILE)
        acc = acc + gates[:, kq:kq + 1] * rows.astype(f32)
    o_ref[...] = _rms(acc, g_ref[...])


def _combine(pos_flat, ys_tiles, x2, gate_pad, g_final, tc):
    t = x2.shape[0]
    grid_spec = pltpu.PrefetchScalarGridSpec(
        num_scalar_prefetch=1,
        grid=(t // tc,),
        in_specs=[pl.BlockSpec(memory_space=pl.ANY),
                  pl.BlockSpec((tc, D_MODEL), lambda i, p: (i, 0)),
                  pl.BlockSpec((tc, LANES), lambda i, p: (i, 0)),
                  pl.BlockSpec((1, D_MODEL), lambda i, p: (0, 0))],
        out_specs=pl.BlockSpec((tc, D_MODEL), lambda i, p: (i, 0)),
        scratch_shapes=[pltpu.VMEM((2 * TOP_K * tc * X_TILE, LANES), u32), pltpu.SemaphoreType.DMA((2,))],
    )
    return pl.pallas_call(
        _combine_body,
        grid_spec=grid_spec,
        out_shape=jax.ShapeDtypeStruct((t, D_MODEL), f32),
        compiler_params=_cparams(("arbitrary",)),
        name="combine",
    )(pos_flat, ys_tiles, x2, gate_pad, g_final)


def _routing(top_idx, rank, counts, t):
    n_assign = t * TOP_K
    experts = jnp.arange(N_EXPERTS, dtype=i32)
    padded = (counts + MOE_BLOCK - 1) // MOE_BLOCK * MOE_BLOCK
    pends = jnp.cumsum(padded).astype(i32)
    pstarts = pends - padded
    start_of = jnp.sum(jnp.where(top_idx[:, :, None] == experts, pstarts, 0), axis=-1)
    dest = (start_of + rank).astype(i32).reshape(n_assign)
    n_blocks = (n_assign + N_EXPERTS * (MOE_BLOCK - 1) + MOE_BLOCK - 1) // MOE_BLOCK
    return dest, pends, n_blocks


def _block_diag(w):
    n, bi, bj = w.shape
    eye = jnp.eye(n, dtype=w.dtype)
    return jnp.einsum('nij,nm->nimj', w, eye).reshape(n * bi, n * bj)


def _layer(x2d, mem2d, bsz, s, mlen, p):
    t = bsz * s
    row = lambda a: a.reshape(1, -1)
    ones_blk = _block_diag(jnp.ones((RWKV_W // RWKV_HEAD, RWKV_HEAD, RWKV_HEAD), bf16))

    xl, gl, ur = _inproj(x2d, row(p['norm_mix_g']), p['w_in'].astype(bf16), tm=min(1024, t))

    wg = jnp.concatenate([_block_diag(p['lru_wx']), _block_diag(p['lru_wa'])], axis=1).astype(bf16)
    bg = jnp.concatenate([p['lru_bx'], p['lru_ba']]).reshape(1, -1)
    y_lru = _lru(xl, gl, p['conv_w'], row(p['conv_b']), wg, bg, row(p['lru_lambda']), bsz, s, ts=min(512, s))

    pairs = lambda a: a.reshape(N_PAIRS, PAIR)
    rows8 = lambda rows: jnp.pad(jnp.stack(rows, axis=1), ((0, 0), (0, SUBLANES - len(rows)), (0, 0)))
    mu = p['rwkv_mu']
    shared = lambda a: jnp.broadcast_to(a, (N_PAIRS, PAIR))
    mu5 = rows8([pairs(mu[:RWKV_W]), pairs(mu[RWKV_W:2 * RWKV_W]), pairs(mu[2 * RWKV_W:3 * RWKV_W]),
                 shared(mu[3 * RWKV_W:3 * RWKV_W + PAIR]), shared(mu[3 * RWKV_W + PAIR:])])
    par5 = rows8([pairs(p['rwkv_w0']), pairs(p['rwkv_a0']), pairs(p['rwkv_k_k']), pairs(p['rwkv_k_a']),
                  pairs(p['rwkv_r_k'])])
    zl = jnp.zeros((DECAY_LORA, RWKV_W), f32)
    w_dec = jnp.concatenate([p['rwkv_w_up'], zl], axis=0).reshape(PAIR, N_PAIRS, PAIR)
    w_icl = jnp.concatenate([zl, p['rwkv_a_up']], axis=0).reshape(PAIR, N_PAIRS, PAIR)
    wproj = jnp.concatenate([w_dec, w_icl], axis=2).transpose(1, 0, 2).astype(bf16)
    gup = p['rwkv_g_up'].reshape(GATE_LORA, N_PAIRS, PAIR).transpose(1, 0, 2).astype(bf16)
    gm, sp, rc, yp, bon, v, g = _rwkv_chunk(ur, mu5, par5, wproj, gup, bsz, s, ts=min(1024, s))
    y_scan = _rwkv_state(gm, sp, rc, yp, bsz, s, nck=min(4, s // RWKV_CHUNK))

    x1 = _outproj(x2d, y_lru, y_scan, bon, v, g, row(p['rwkv_lnx_g']), row(p['rwkv_lnx_b']), ones_blk,
                  p['w_out'].astype(bf16), tm=min(1024, t))

    kmem, vmem = _memkv(mem2d, row(p['norm_mem_g']), p['xa_wk'].astype(bf16), p['xa_wv'].astype(bf16),
                        tm=min(512, bsz * mlen))
    x2, hf, route, gates, cnt_pad = _xattn(x1, kmem, vmem, row(p['norm_xa_g']), p['xa_wq'].astype(bf16),
                                           p['xa_wo'].astype(bf16), row(p['norm_ffn_g']),
                                           p['w_router'].T.astype(bf16), p['b_router'].reshape(-1, 1),
                                           bsz, s, mlen, tm=min(1024, s))

    counts = cnt_pad[:, 0].astype(i32)
    dest, pends, n_blocks = _routing(route[:TOP_K].T, route[TOP_K:].T, counts, t)
    gate_pad = jnp.pad(gates[:TOP_K].T, ((0, 0), (0, LANES - TOP_K)))
    xs = _dispatch(dest, pends, hf, n_blocks, tm=min(4096, t))
    ys = _moe(pends, xs, p['w_gu'], p['b_gu'], p['w_dn'], p['b_dn'], n_blocks)
    return _combine(dest, ys, x2, gate_pad, row(p['final_norm_g']), tc=min(256, t))


def kernel(x, mem, norm_mix_g, w_in, conv_w, conv_b, lru_wx, lru_bx, lru_wa, lru_ba, lru_lambda, rwkv_mu, rwkv_w0, rwkv_w_up, rwkv_a0, rwkv_a_up, rwkv_g_up, rwkv_k_k, rwkv_k_a, rwkv_r_k, rwkv_lnx_g, rwkv_lnx_b, w_out, norm_xa_g, norm_mem_g, xa_wq, xa_wk, xa_wv, xa_wo, norm_ffn_g, w_router, b_router, w_gu, b_gu, w_dn, b_dn, final_norm_g):
    bsz, s, d = x.shape
    mlen = mem.shape[1]
    assert d == D_MODEL and w_in.shape[0] == 1
    p = dict(norm_mix_g=norm_mix_g[0], w_in=w_in[0], conv_w=conv_w[0], conv_b=conv_b[0], lru_wx=lru_wx[0],
             lru_bx=lru_bx[0], lru_wa=lru_wa[0], lru_ba=lru_ba[0], lru_lambda=lru_lambda[0],
             rwkv_mu=rwkv_mu[0], rwkv_w0=rwkv_w0[0], rwkv_w_up=rwkv_w_up[0], rwkv_a0=rwkv_a0[0],
             rwkv_a_up=rwkv_a_up[0], rwkv_g_up=rwkv_g_up[0], rwkv_k_k=rwkv_k_k[0], rwkv_k_a=rwkv_k_a[0],
             rwkv_r_k=rwkv_r_k[0].reshape(-1), rwkv_lnx_g=rwkv_lnx_g[0], rwkv_lnx_b=rwkv_lnx_b[0],
             w_out=w_out[0], norm_xa_g=norm_xa_g[0], norm_mem_g=norm_mem_g[0], xa_wq=xa_wq[0],
             xa_wk=xa_wk[0], xa_wv=xa_wv[0], xa_wo=xa_wo[0], norm_ffn_g=norm_ffn_g[0],
             w_router=w_router[0], b_router=b_router[0], w_gu=w_gu[0], b_gu=b_gu[0], w_dn=w_dn[0],
             b_dn=b_dn[0], final_norm_g=final_norm_g)
    out = _layer(x.reshape(bsz * s, d), mem.reshape(bsz * mlen, d), bsz, s, mlen, p)
    return out.reshape(bsz, s, d)
```

```python
import jax
import jax.numpy as jnp
from jax import lax
from jax.experimental import pallas as pl
from jax.experimental.pallas import tpu as pltpu

f32 = jnp.float32
bf16 = jnp.bfloat16
i32 = jnp.int32
u32 = jnp.uint32

D_MODEL = 1024
LRU_W = 512
RWKV_W = 512
CONV_W = 4
LRU_C = 8.0
RWKV_HEAD = 64
DECAY_LORA = 64
AAA_LORA = 64
GATE_LORA = 128
RWKV_IN = 3 * RWKV_W + DECAY_LORA + AAA_LORA + GATE_LORA
XA_HEADS = 4
XA_HEAD = D_MODEL // XA_HEADS
N_EXPERTS = 32
TOP_K = 4
D_FF = D_MODEL
SWIGLU_LIMIT = 7.0
SWIGLU_ALPHA = 1.702
EPS = 1e-6
GN_EPS = 64e-5

LANES = 128
SUBLANES = 8
RWKV_CHUNK = 64
PAIR = 2 * RWKV_HEAD
N_PAIRS = RWKV_W // PAIR
MOE_BLOCK = 256
X_TILE = D_MODEL // (2 * LANES)
V7X_VMEM_BYTES = 64 * 1024 * 1024
VMEM_LIMIT = V7X_VMEM_BYTES - 12 * 1024 * 1024


def _cparams(sem):
    return pltpu.CompilerParams(dimension_semantics=sem, vmem_limit_bytes=VMEM_LIMIT)


def _rms(x, g):
    return x * lax.rsqrt(jnp.mean(x * x, axis=-1, keepdims=True) + EPS) * g


def _full(shape):
    n = len(shape)
    return pl.BlockSpec(shape, lambda *a: (0,) * n)


def _shift_rows(x, prev8, d):
    xr = pltpu.roll(x, d, 0)
    tr = pltpu.roll(prev8, d, 0)
    row = lax.broadcasted_iota(i32, prev8.shape, 0)
    head = jnp.where(row < d, tr, xr[:SUBLANES])
    return jnp.concatenate([head, xr[SUBLANES:]], axis=0)


def _bf16_parts(x, n):
    parts = []
    for _ in range(n):
        piece = x.astype(bf16)
        parts.append(piece)
        x = x - piece.astype(f32)
    return parts


def _sum_dot(x, mask_bf, n_parts, mask_left=False):
    acc = None
    for piece in _bf16_parts(x, n_parts):
        d = (jnp.dot(mask_bf, piece, preferred_element_type=f32) if mask_left
             else jnp.dot(piece, mask_bf, preferred_element_type=f32))
        acc = d if acc is None else acc + d
    return acc


def _store_packed_rows(ref, val):
    n = val.shape[0]
    half = D_MODEL // 2
    lo = lax.bitcast_convert_type(val[:, :half].astype(bf16).astype(f32), u32) >> 16
    hi = lax.bitcast_convert_type(val[:, half:].astype(bf16).astype(f32), u32) & jnp.uint32(0xFFFF0000)
    words = hi | lo
    for j in range(X_TILE):
        ref[pl.ds(j, n, stride=X_TILE), :] = words[:, j * LANES:(j + 1) * LANES]


def _load_packed_rows(ref, n, off=0):
    ws = [ref[pl.ds(off + j, n, stride=X_TILE), :] for j in range(X_TILE)]
    lo = [lax.bitcast_convert_type(w << 16, f32).astype(bf16) for w in ws]
    hi = [lax.bitcast_convert_type(w & jnp.uint32(0xFFFF0000), f32).astype(bf16) for w in ws]
    return jnp.concatenate(lo + hi, axis=1)


def _inproj_body(x_ref, g_ref, w_ref, xl_ref, gl_ref, ur_ref):
    h = _rms(x_ref[...], g_ref[...])
    u = jnp.dot(h.astype(bf16), w_ref[...], preferred_element_type=f32)
    xl_ref[...] = u[:, :LRU_W]
    gl_ref[...] = u[:, LRU_W:2 * LRU_W]
    ur_ref[...] = u[:, 2 * LRU_W:]


def _inproj(x2d, g, w_in_bf, tm):
    t = x2d.shape[0]
    return pl.pallas_call(
        _inproj_body,
        grid=(t // tm,),
        in_specs=[pl.BlockSpec((tm, D_MODEL), lambda i: (i, 0)), _full((1, D_MODEL)),
                  _full(w_in_bf.shape)],
        out_specs=[pl.BlockSpec((tm, LRU_W), lambda i: (i, 0)),
                   pl.BlockSpec((tm, LRU_W), lambda i: (i, 0)),
                   pl.BlockSpec((tm, RWKV_IN), lambda i: (i, 0))],
        out_shape=[jax.ShapeDtypeStruct((t, LRU_W), f32), jax.ShapeDtypeStruct((t, LRU_W), f32),
                   jax.ShapeDtypeStruct((t, RWKV_IN), f32)],
        compiler_params=_cparams(("parallel",)),
        name="inproj",
    )(x2d, g, w_in_bf)


def _lru_body(xl_ref, gl_ref, cw_ref, cb_ref, wg_ref, bg_ref, lam_ref, o_ref, tail_ref, h_ref):
    ts = xl_ref.shape[0]

    @pl.when(pl.program_id(1) == 0)
    def _():
        tail_ref[...] = jnp.zeros_like(tail_ref)
        h_ref[...] = jnp.zeros_like(h_ref)

    x = xl_ref[...]
    tail = tail_ref[...]
    cw = cw_ref[...]
    xc = cb_ref[...] + cw[CONV_W - 1:CONV_W] * x
    for d in range(1, CONV_W):
        xc = xc + cw[CONV_W - 1 - d:CONV_W - d] * _shift_rows(x, tail, d)
    tail_ref[...] = x[ts - SUBLANES:]

    gates = jax.nn.sigmoid(jnp.dot(xc.astype(bf16), wg_ref[...], preferred_element_type=f32) + bg_ref[...])
    gx = gates[:, :LRU_W]
    ga = gates[:, LRU_W:]
    log_a = -LRU_C * ga * jax.nn.softplus(-lam_ref[...])
    a = jnp.exp(log_a)
    b = jnp.sqrt(-jnp.tanh(log_a) * (a * a + 1.0)) * gx * xc

    row = lax.broadcasted_iota(i32, (ts, LRU_W), 0) % SUBLANES
    d = 1
    while d < SUBLANES:
        keep = row >= d
        a_s = jnp.where(keep, pltpu.roll(a, d, 0), 1.0)
        b_s = jnp.where(keep, pltpu.roll(b, d, 0), 0.0)
        b = a * b_s + b
        a = a * a_s
        d *= 2
    carry = h_ref[SUBLANES - 1:SUBLANES, :]
    groups = []
    for q in range(ts // SUBLANES):
        rows = slice(q * SUBLANES, (q + 1) * SUBLANES)
        hq = b[rows] + a[rows] * carry
        carry = hq[SUBLANES - 1:SUBLANES, :]
        groups.append(hq)
    h = jnp.concatenate(groups, axis=0)
    h_ref[...] = jnp.broadcast_to(carry, h_ref.shape)
    o_ref[...] = (h * jax.nn.gelu(gl_ref[...])).astype(o_ref.dtype)


def _lru(xl, gl, cw, cb, wg_bf, bg, lam, bsz, s, ts):
    nt = s // ts
    blk = pl.BlockSpec((ts, LRU_W), lambda b, i: (b * nt + i, 0))
    return pl.pallas_call(
        _lru_body,
        grid=(bsz, nt),
        in_specs=[blk, blk, _full(cw.shape), _full(cb.shape), _full(wg_bf.shape), _full(bg.shape),
                  _full(lam.shape)],
        out_specs=blk,
        out_shape=jax.ShapeDtypeStruct((bsz * s, LRU_W), bf16),
        scratch_shapes=[pltpu.VMEM((SUBLANES, LRU_W), f32), pltpu.VMEM((SUBLANES, LRU_W), f32)],
        compiler_params=_cparams(("parallel", "arbitrary")),
        name="lru",
    )(xl, gl, cw, cb, wg_bf, bg, lam)


def _mm_x3(a, b_parts):
    ah, al = _bf16_parts(a, 2)
    bh, bl = b_parts
    d = lambda x, y: jnp.dot(x, y, preferred_element_type=f32)
    return d(ah, bh) + (d(ah, bl) + d(al, bh))


def _mm(a, b):
    return jnp.dot(a.astype(bf16), b.astype(bf16), preferred_element_type=f32)


def _mm_nt(a, b):
    return lax.dot_general(a.astype(bf16), b.astype(bf16), (((1,), (1,)), ((), ())), preferred_element_type=f32)


def _mm_tn(a, b):
    return lax.dot_general(a.astype(bf16), b.astype(bf16), (((0,), (0,)), ((), ())), preferred_element_type=f32)


def _bd(x):
    m0 = lax.broadcasted_iota(i32, x.shape, 1) < RWKV_HEAD
    zero = jnp.zeros_like(x)
    return jnp.concatenate([jnp.where(m0, x, zero), jnp.where(m0, zero, x)], axis=0)


def _side_by_side(d):
    h = d.shape[0] // 2
    m0 = lax.broadcasted_iota(i32, (h, d.shape[1]), 1) < RWKV_HEAD
    return jnp.where(m0, d[:h], d[h:])


def _chunk_maps(chunks):
    c = RWKV_CHUNK
    ri = lax.broadcasted_iota(i32, (c, PAIR), 0)
    ji = lax.broadcasted_iota(i32, (c, PAIR), 1) % RWKV_HEAD
    strict = ji < ri
    incl = ji <= ri
    diag = ji == ri
    eye = jnp.where(diag, 1.0, 0.0).astype(f32)
    each = lambda f, *ls: [f(*xs) for xs in zip(*ls)]
    cat0 = lambda *xs: jnp.concatenate(xs, axis=0)
    cat1 = lambda *xs: jnp.concatenate(xs, axis=1)
    tb = lambda x: x.astype(bf16)

    ats, bts, kts, rts, vs, cls = [list(x) for x in zip(*chunks)]
    pcs = each(lambda cl: jnp.exp(cl[c - 1:c, :]), cls)
    bd_a = each(lambda x: tb(_bd(x)), ats)
    bd_v = each(lambda x: tb(_bd(x)), vs)

    aa = each(lambda a, r, b, k: _mm_nt(cat0(tb(a), tb(r)), cat0(tb(_bd(b)), tb(_bd(k)))), ats, rts, bts, kts)
    l_ab = each(lambda x: tb(jnp.where(strict, x[:c, :PAIR], 0.0)), aa)
    a_k = each(lambda x: tb(cat0(jnp.where(strict, x[:c, PAIR:], 0.0), jnp.where(incl, x[c:, PAIR:], 0.0))), aa)
    a_rb = each(lambda x: tb(jnp.where(incl, x[c:, :PAIR], 0.0)), aa)

    tinv = each(lambda x: eye + x, l_ab)
    lp = each(lambda x: tb(_mm(x, _bd(x))), l_ab)
    p = 2
    while 2 * p < c:
        x2 = each(lambda t, x: _mm(cat0(tb(t), x), _bd(x)), tinv, lp)
        tinv = each(lambda t, x: t + x[:c], tinv, x2)
        lp = each(lambda x: tb(x[c:]), x2)
        p *= 2
    tinv = each(lambda t, x: t + _mm(t, _bd(x)), tinv, lp)

    wy = each(_mm, a_k, bd_v)
    za = each(lambda t, w, a: _mm(t, cat1(tb(_bd(w[:c])), a)), tinv, wy, bd_a)
    zp = each(lambda x: x[:, :PAIR], za)
    ac = each(lambda x: x[:, PAIR:], za)
    y1 = each(lambda ar, z, a: _mm(ar, cat1(tb(_bd(z)), tb(_bd(a)))), a_rb, zp, ac)
    yp = each(lambda w, y: w[c:] + y[:, :PAIR], wy, y1)
    rc = each(lambda r, y: r + y[:, PAIR:], rts, y1)
    sm = each(lambda b, pc, z, a: _mm_tn(b * pc, cat1(z, a)), bts, pcs, zp, ac)
    kv = each(lambda k, pc, v: _mm_tn(k * pc, v), kts, pcs, vs)
    sp = each(lambda x, m: _side_by_side(x) + _side_by_side(m[:, :PAIR]), kv, sm)
    g = each(lambda pc, m: jnp.where(diag, jnp.broadcast_to(pc, (c, PAIR)), 0.0) + _side_by_side(m[:, PAIR:]), pcs, sm)
    return list(zip(g, sp, rc, yp))


N_UCOLS = 5


def _rwkv_chunk_body(ur_r, ur_k, ur_v, ur_lo, ur_dg, mu_ref, par_ref, wproj_ref, gup_ref, ones_ref, tri_ref,
                     g_ref, sp_ref, rc_ref, yp_ref, bon_ref, v_ref, gate_ref, prev_ref):
    ts = ur_r.shape[0]
    c = RWKV_CHUNK

    @pl.when(pl.program_id(2) == 0)
    def _():
        prev_ref[...] = jnp.zeros_like(prev_ref)

    mixed = []
    for j, ref in enumerate((ur_r, ur_k, ur_v, ur_lo, ur_dg)):
        u0 = ref[...]
        ls = slice(j * LANES, (j + 1) * LANES)
        us = _shift_rows(u0, prev_ref[:, ls], 1)
        prev_ref[:, ls] = u0[ts - SUBLANES:]
        mixed.append(u0 + (us - u0) * mu_ref[0, j:j + 1, :])
    r, k, v, lora, dg = mixed
    w0, a0, k_k, k_a, r_k = [par_ref[0, j:j + 1, :] for j in range(5)]

    lane = lax.broadcasted_iota(i32, lora.shape, 1)
    lora = jnp.where(lane < DECAY_LORA, jnp.tanh(lora), lora)
    proj = jnp.dot(lora.astype(bf16), wproj_ref[0], preferred_element_type=f32)
    w = -jax.nn.softplus(-(w0 + proj[:, :PAIR])) - 0.5
    lw = -jnp.exp(w)
    a = jax.nn.sigmoid(a0 + proj[:, PAIR:])
    gate_ref[...] = jnp.dot(jax.nn.sigmoid(dg).astype(bf16), gup_ref[0], preferred_element_type=f32)

    ones = ones_ref[...]
    kk = k * k_k
    ss = _sum_dot(kk * kk, ones, 2)
    kk = kk / jnp.maximum(jnp.sqrt(ss), 1e-12)
    k2 = k * (1.0 + (a - 1.0) * k_a)
    bon_ref[...] = _sum_dot(r * k2 * r_k, ones, 2)
    v_ref[...] = v

    tri = tri_ref[...]
    grp = tri.shape[0]
    cl = jnp.concatenate([_sum_dot(lw[q * grp:(q + 1) * grp], tri, 3, mask_left=True) for q in range(ts // grp)],
                         axis=0)
    e_neg = jnp.exp(-cl)
    at = -kk * jnp.exp(cl - lw)
    bt = kk * a * e_neg
    kt = k2 * e_neg
    rt = r * jnp.exp(cl)

    sls = [slice(j * c, (j + 1) * c) for j in range(ts // c)]
    outs = _chunk_maps([(at[sl], bt[sl], kt[sl], rt[sl], v[sl], cl[sl]) for sl in sls])
    for sl, (g, sp, rc, yp) in zip(sls, outs):
        g_ref[sl, :] = g
        sp_ref[sl, :] = sp
        rc_ref[sl, :] = rc
        yp_ref[sl, :] = yp


def _rwkv_chunk(ur, mu5, par5, wproj, gup, bsz, s, ts):
    t = bsz * s
    nt = s // ts
    col = lambda blk: pl.BlockSpec((ts, PAIR), lambda p, b, i, blk=blk: (b * nt + i, blk(p)))
    nrw = RWKV_W // PAIR
    ucols = [col(lambda p: p), col(lambda p: nrw + p), col(lambda p: 2 * nrw + p), col(lambda p: 3 * nrw),
             col(lambda p: 3 * nrw + 1)]
    per_pair = lambda a: pl.BlockSpec((1,) + a.shape[1:], lambda p, b, i: (p, 0, 0))
    ones_pair = _block_diag(jnp.ones((2, RWKV_HEAD, RWKV_HEAD), bf16))
    tri = _block_diag(jnp.tril(jnp.ones((2, RWKV_CHUNK, RWKV_CHUNK), bf16)))
    oblk = pl.BlockSpec((ts, PAIR), lambda p, b, i: (b * nt + i, p))
    osh = jax.ShapeDtypeStruct((t, RWKV_W), f32)
    return pl.pallas_call(
        _rwkv_chunk_body,
        grid=(N_PAIRS, bsz, nt),
        in_specs=ucols + [per_pair(mu5), per_pair(par5), per_pair(wproj), per_pair(gup), _full(ones_pair.shape),
                          _full(tri.shape)],
        out_specs=[oblk] * 7,
        out_shape=[osh] * 7,
        scratch_shapes=[pltpu.VMEM((SUBLANES, N_UCOLS * LANES), f32)],
        compiler_params=_cparams(("parallel", "parallel", "arbitrary")),
        name="rwkv_chunk",
    )(ur, ur, ur, ur, ur, mu5, par5, wproj, gup, ones_pair, tri)


def _rwkv_state_body(g_ref, sp_ref, rc_ref, yp_ref, y_ref, s_ref):
    @pl.when(pl.program_id(0) == 0)
    def _():
        s_ref[...] = jnp.zeros_like(s_ref)

    c = RWKV_CHUNK
    bsz = g_ref.shape[0]
    nck = g_ref.shape[1] // c
    chains = [(b, slice(p * PAIR, (p + 1) * PAIR)) for b in range(bsz) for p in range(N_PAIRS)]
    states = [s_ref[b, :, ls] for b, ls in chains]
    for j in range(nck):
        sl = slice(j * c, (j + 1) * c)
        prods = [_mm_x3(jnp.concatenate([rc_ref[b, sl, ls], g_ref[b, sl, ls]], axis=0), _bf16_parts(_bd(s), 2))
                 for (b, ls), s in zip(chains, states)]
        for (b, ls), pr in zip(chains, prods):
            y_ref[b, sl, ls] = yp_ref[b, sl, ls] + pr[:c]
        states = [pr[c:] + sp_ref[b, sl, ls] for (b, ls), pr in zip(chains, prods)]
    for (b, ls), s in zip(chains, states):
        s_ref[b, :, ls] = s


def _rwkv_state(g, sp, rc, yp, bsz, s, nck):
    rows = nck * RWKV_CHUNK
    rblk = pl.BlockSpec((bsz, rows, RWKV_W), lambda i: (0, i, 0))
    r3 = lambda a: a.reshape(bsz, s, RWKV_W)
    y = pl.pallas_call(
        _rwkv_state_body,
        grid=(s // rows,),
        in_specs=[rblk] * 4,
        out_specs=rblk,
        out_shape=jax.ShapeDtypeStruct((bsz, s, RWKV_W), f32),
        scratch_shapes=[pltpu.VMEM((bsz, RWKV_CHUNK, RWKV_W), f32)],
        compiler_params=_cparams(("arbitrary",)),
        name="rwkv_state",
    )(r3(g), r3(sp), r3(rc), r3(yp))
    return y.reshape(bsz * s, RWKV_W)


def _outproj_body(x_ref, yl_ref, ys_ref, bon_ref, v_ref, g_ref, lg_ref, lb_ref, ones_ref, w_ref, o_ref):
    y = ys_ref[...]
    ones = ones_ref[...]
    inv_n = 1.0 / RWKV_HEAD
    mean = _sum_dot(y, ones, 2) * inv_n
    yc = y - mean
    var = _sum_dot(yc * yc, ones, 2) * inv_n
    yn = yc * lax.rsqrt(var + GN_EPS) * lg_ref[...] + lb_ref[...]
    yr = (yn + bon_ref[...] * v_ref[...]) * g_ref[...]
    cat = jnp.concatenate([yl_ref[...], yr.astype(bf16)], axis=1)
    o_ref[...] = x_ref[...] + jnp.dot(cat, w_ref[...], preferred_element_type=f32)


def _outproj(x2d, y_lru, y_scan, bon, v, g, lnx_g, lnx_b, ones_blk, w_out_bf, tm):
    t = x2d.shape[0]
    xb = pl.BlockSpec((tm, D_MODEL), lambda i: (i, 0))
    hb = pl.BlockSpec((tm, RWKV_W), lambda i: (i, 0))
    return pl.pallas_call(
        _outproj_body,
        grid=(t // tm,),
        in_specs=[xb, hb, hb, hb, hb, hb, _full(lnx_g.shape), _full(lnx_b.shape), _full(ones_blk.shape),
                  _full(w_out_bf.shape)],
        out_specs=xb,
        out_shape=jax.ShapeDtypeStruct((t, D_MODEL), f32),
        compiler_params=_cparams(("parallel",)),
        name="outproj",
    )(x2d, y_lru, y_scan, bon, v, g, lnx_g, lnx_b, ones_blk, w_out_bf)


def _memkv_body(m_ref, g_ref, wk_ref, wv_ref, k_ref, v_ref):
    h = _rms(m_ref[...], g_ref[...]).astype(bf16)
    k_ref[...] = jnp.dot(h, wk_ref[...], preferred_element_type=f32).astype(bf16)
    v_ref[...] = jnp.dot(h, wv_ref[...], preferred_element_type=f32).astype(bf16)


def _memkv(mem2d, g, wk_bf, wv_bf, tm):
    t = mem2d.shape[0]
    blk = pl.BlockSpec((tm, D_MODEL), lambda i: (i, 0))
    sh = jax.ShapeDtypeStruct((t, D_MODEL), bf16)
    return pl.pallas_call(
        _memkv_body,
        grid=(t // tm,),
        in_specs=[blk, _full(g.shape), _full(wk_bf.shape), _full(wv_bf.shape)],
        out_specs=[blk, blk],
        out_shape=[sh, sh],
        compiler_params=_cparams(("parallel",)),
        name="memkv",
    )(mem2d, g, wk_bf, wv_bf)


def _xattn_body(x_ref, k_ref, v_ref, gx_ref, wq_ref, wo_ref, gf_ref, wr_ref, br_ref, upper_ref,
                x2_ref, hf_ref, idx_ref, gate_ref, cnt_ref, base_ref):
    x = x_ref[...]
    h = _rms(x, gx_ref[...]).astype(bf16)
    q = jnp.dot(h, wq_ref[...], preferred_element_type=f32).astype(bf16)
    k = k_ref[...]
    v = v_ref[...]
    heads = [slice(hd * XA_HEAD, (hd + 1) * XA_HEAD) for hd in range(XA_HEADS)]
    scs = [lax.dot_general(q[:, sl], k[:, sl], (((1,), (1,)), ((), ())), preferred_element_type=f32)
           * (XA_HEAD ** -0.5) for sl in heads]
    ps = []
    for sc in scs:
        e = jnp.exp(sc - jnp.max(sc, axis=-1, keepdims=True))
        ps.append((e / jnp.sum(e, axis=-1, keepdims=True)).astype(bf16))
    o = jnp.concatenate([jnp.dot(p, v[:, sl], preferred_element_type=f32).astype(bf16)
                         for p, sl in zip(ps, heads)], axis=1)
    x2 = x + jnp.dot(o, wo_ref[...], preferred_element_type=f32)
    x2_ref[...] = x2

    hf = _rms(x2, gf_ref[...])
    _store_packed_rows(hf_ref, hf)
    logits = lax.dot_general(wr_ref[...], hf.astype(bf16), (((1,), (1,)), ((), ())),
                             preferred_element_type=f32) + br_ref[...]
    erow = lax.broadcasted_iota(i32, logits.shape, 0)
    neg = jnp.float32(-jnp.inf)
    cur = logits
    vals = []
    idxs = []
    for _ in range(TOP_K):
        m = jnp.max(cur, axis=0, keepdims=True)
        am = jnp.min(jnp.where(cur == m, erow, N_EXPERTS), axis=0, keepdims=True)
        vals.append(m)
        idxs.append(am)
        cur = jnp.where(erow == am, neg, cur)
    es = [jnp.exp(vk - vals[0]) for vk in vals]
    den = es[0] + es[1] + es[2] + es[3]

    @pl.when(pl.program_id(0) == 0)
    def _():
        base_ref[...] = jnp.zeros_like(base_ref)

    onehot = [jnp.where(erow == am, 1.0, 0.0) for am in idxs]
    cnt = (onehot[0] + onehot[1]) + (onehot[2] + onehot[3])
    base = base_ref[:, 0:1]
    prior = jnp.dot(cnt.astype(bf16), upper_ref[...], preferred_element_type=f32) + base
    base_ref[...] = jnp.broadcast_to(base + jnp.sum(cnt, axis=1, keepdims=True), base_ref.shape)
    cnt_ref[...] = base_ref[...]

    orow = lax.broadcasted_iota(i32, idx_ref.shape, 0)
    idx_out = jnp.zeros(idx_ref.shape, i32)
    gate_out = jnp.zeros(gate_ref.shape, f32)
    for kq in range(TOP_K):
        rank = jnp.sum(prior * onehot[kq], axis=0, keepdims=True).astype(i32)
        idx_out = jnp.where(orow == kq, idxs[kq], idx_out)
        idx_out = jnp.where(orow == TOP_K + kq, rank, idx_out)
        gate_out = jnp.where(orow == kq, es[kq] / den, gate_out)
    idx_ref[...] = idx_out
    gate_ref[...] = gate_out


def _xattn(x1, kmem, vmem, g_xa, wq_bf, wo_bf, g_ffn, wr_t, br_col, bsz, s, mlen, tm):
    t = bsz * s
    nt = s // tm
    xb = pl.BlockSpec((tm, D_MODEL), lambda i: (i, 0))
    mb = pl.BlockSpec((mlen, D_MODEL), lambda i: (i // nt, 0))
    lb = pl.BlockSpec((2 * TOP_K, tm), lambda i: (0, i))
    upper = jnp.triu(jnp.ones((tm, tm), bf16), k=1)
    return pl.pallas_call(
        _xattn_body,
        grid=(t // tm,),
        in_specs=[xb, mb, mb, _full(g_xa.shape), _full(wq_bf.shape), _full(wo_bf.shape),
                  _full(g_ffn.shape), _full(wr_t.shape), _full(br_col.shape), _full(upper.shape)],
        out_specs=[xb, pl.BlockSpec((tm * X_TILE, LANES), lambda i: (i, 0)), lb, lb, _full((N_EXPERTS, LANES))],
        out_shape=[jax.ShapeDtypeStruct((t, D_MODEL), f32), jax.ShapeDtypeStruct((t * X_TILE, LANES), u32),
                   jax.ShapeDtypeStruct((2 * TOP_K, t), i32), jax.ShapeDtypeStruct((2 * TOP_K, t), f32),
                   jax.ShapeDtypeStruct((N_EXPERTS, LANES), f32)],
        scratch_shapes=[pltpu.VMEM((N_EXPERTS, LANES), f32)],
        compiler_params=_cparams(("arbitrary",)),
        name="xattn",
    )(x1, kmem, vmem, g_xa, wq_bf, wo_bf, g_ffn, wr_t, br_col, upper)


DMA_UNROLL = 8


def _tile_at(ref, row):
    return ref.at[pl.ds(pl.multiple_of(row * X_TILE, X_TILE), X_TILE), :]


def _dispatch_body(dest_ref, pend_ref, hf_ref, xs_hbm, zeros_ref, sem, zsem):
    i = pl.program_id(0)
    tm = hf_ref.shape[0] // X_TILE
    zrows = MOE_BLOCK * X_TILE

    def zero_copy(e):
        start = pl.multiple_of((pend_ref[e] - MOE_BLOCK) * X_TILE, X_TILE)
        return pltpu.make_async_copy(zeros_ref, xs_hbm.at[pl.ds(start, zrows), :], zsem)

    def nonempty(e):
        return pend_ref[e] > (pend_ref[e - 1] if e else 0)

    @pl.when(i == 0)
    def _():
        zeros_ref[...] = jnp.zeros_like(zeros_ref)
        for e in range(N_EXPERTS):
            @pl.when(nonempty(e))
            def _():
                zero_copy(e).start()
        for e in range(N_EXPERTS):
            @pl.when(nonempty(e))
            def _():
                zero_copy(e).wait()

        def tail_copy(b):
            return pltpu.make_async_copy(zeros_ref, xs_hbm.at[pl.ds(pl.multiple_of(b * zrows, zrows), zrows), :], zsem)

        def tail_start(b, carry):
            tail_copy(b).start()
            return carry

        def tail_wait(b, carry):
            tail_copy(b).wait()
            return carry
        n_used = pend_ref[N_EXPERTS - 1] // MOE_BLOCK
        n_all = xs_hbm.shape[0] // zrows
        lax.fori_loop(n_used, n_all, tail_start, 0)
        lax.fori_loop(n_used, n_all, tail_wait, 0)

    def body(q, carry):
        for u in range(DMA_UNROLL):
            r = q * DMA_UNROLL + u
            src = _tile_at(hf_ref, r)
            for kq in range(TOP_K):
                pltpu.make_async_copy(src, _tile_at(xs_hbm, dest_ref[(i * tm + r) * TOP_K + kq]),
                                      sem).start(priority=kq % 2)
        return carry
    lax.fori_loop(0, tm // DMA_UNROLL, body, 0)
    for kq in range(TOP_K):
        pltpu.make_async_copy(hf_ref, xs_hbm.at[pl.ds(0, tm * X_TILE), :], sem).wait()


def _dispatch(dest_flat, pends, hf_tiles, n_blocks, tm):
    t = hf_tiles.shape[0] // X_TILE
    rows = n_blocks * MOE_BLOCK
    grid_spec = pltpu.PrefetchScalarGridSpec(
        num_scalar_prefetch=2,
        grid=(t // tm,),
        in_specs=[pl.BlockSpec((tm * X_TILE, LANES), lambda i, d, pe: (i, 0))],
        out_specs=pl.BlockSpec(memory_space=pl.ANY),
        scratch_shapes=[pltpu.VMEM((MOE_BLOCK * X_TILE, LANES), u32), pltpu.SemaphoreType.DMA(()),
                        pltpu.SemaphoreType.DMA(())],
    )
    return pl.pallas_call(
        _dispatch_body,
        grid_spec=grid_spec,
        out_shape=jax.ShapeDtypeStruct((rows * X_TILE, LANES), u32),
        compiler_params=_cparams(("arbitrary",)),
        name="dispatch",
    )(dest_flat, pends, hf_tiles)


def _moe_body(pend_ref, xs_hbm, wgu_ref, bgu_ref, wdn_ref, bdn_ref, ys_hbm, xbuf, ybuf, wgu_bf, wdn_bf,
              sem_in, sem_out):
    e = pl.program_id(0)
    blk = MOE_BLOCK * X_TILE
    end_blk = pend_ref[e] // MOE_BLOCK
    start_blk = jnp.where(e == 0, 0, pend_ref[jnp.maximum(e - 1, 0)] // MOE_BLOCK)
    nb = end_blk - start_blk

    n_total = pend_ref[N_EXPERTS - 1] // MOE_BLOCK

    def rows_of(b):
        return pl.ds(pl.multiple_of(b * blk, blk), blk)

    def x_copy(b):
        return pltpu.make_async_copy(xs_hbm.at[rows_of(b), :], xbuf.at[b % 2], sem_in.at[b % 2])

    def y_copy(b):
        return pltpu.make_async_copy(ybuf.at[b % 2], ys_hbm.at[rows_of(b), :], sem_out.at[b % 2])

    @pl.when((e == 0) & (n_total > 0))
    def _():
        x_copy(0).start()

    @pl.when(nb > 0)
    def _():
        wgu_bf[...] = wgu_ref[0].astype(bf16)
        wdn_bf[...] = wdn_ref[0].astype(bf16)

    def block(j, carry):
        b = start_blk + j
        slot = b % 2

        @pl.when(b + 1 < n_total)
        def _():
            x_copy(b + 1).start()

        x_copy(b).wait()

        @pl.when(b >= 2)
        def _():
            y_copy(b - 2).wait()

        xb = _load_packed_rows(xbuf.at[slot], MOE_BLOCK)
        gu = jnp.dot(xb, wgu_bf[...], preferred_element_type=f32) + bgu_ref[0]
        gate = jnp.minimum(gu[:, :D_FF], SWIGLU_LIMIT)
        up = jnp.clip(gu[:, D_FF:], -SWIGLU_LIMIT, SWIGLU_LIMIT)
        act = (up + 1.0) * (gate * jax.nn.sigmoid(SWIGLU_ALPHA * gate))
        y = jnp.dot(act.astype(bf16), wdn_bf[...], preferred_element_type=f32) + bdn_ref[0]
        _store_packed_rows(ybuf.at[slot], y)
        y_copy(b).start()
        return carry

    lax.fori_loop(0, nb, block, 0)

    @pl.when(e == N_EXPERTS - 1)
    def _():
        @pl.when(n_total >= 2)
        def _():
            y_copy(n_total - 2).wait()

        @pl.when(n_total >= 1)
        def _():
            y_copy(n_total - 1).wait()

        n_all = ys_hbm.shape[0] // blk
        ybuf[0] = jnp.zeros(ybuf.shape[1:], ybuf.dtype)

        def tail_copy(b):
            return pltpu.make_async_copy(ybuf.at[0], ys_hbm.at[pl.ds(pl.multiple_of(b * blk, blk), blk), :],
                                         sem_out.at[0])

        def tail_start(b, carry):
            tail_copy(b).start()
            return carry

        def tail_wait(b, carry):
            tail_copy(b).wait()
            return carry
        lax.fori_loop(end_blk, n_all, tail_start, 0)
        lax.fori_loop(end_blk, n_all, tail_wait, 0)


def _moe(pends, xs_tiles, w_gu, b_gu, w_dn, b_dn, n_blocks):
    rows = n_blocks * MOE_BLOCK
    grid_spec = pltpu.PrefetchScalarGridSpec(
        num_scalar_prefetch=1,
        grid=(N_EXPERTS,),
        in_specs=[
            pl.BlockSpec(memory_space=pl.ANY),
            pl.BlockSpec((1, D_MODEL, 2 * D_FF), lambda e, pe: (e, 0, 0)),
            pl.BlockSpec((1, 1, 2 * D_FF), lambda e, pe: (e, 0, 0)),
            pl.BlockSpec((1, D_FF, D_MODEL), lambda e, pe: (e, 0, 0)),
            pl.BlockSpec((1, 1, D_MODEL), lambda e, pe: (e, 0, 0)),
        ],
        out_specs=pl.BlockSpec(memory_space=pl.ANY),
        scratch_shapes=[pltpu.VMEM((2, MOE_BLOCK * X_TILE, LANES), u32),
                        pltpu.VMEM((2, MOE_BLOCK * X_TILE, LANES), u32),
                        pltpu.VMEM((D_MODEL, 2 * D_FF), bf16), pltpu.VMEM((D_FF, D_MODEL), bf16),
                        pltpu.SemaphoreType.DMA((2,)), pltpu.SemaphoreType.DMA((2,))],
    )
    return pl.pallas_call(
        _moe_body,
        grid_spec=grid_spec,
        out_shape=jax.ShapeDtypeStruct((rows * X_TILE, LANES), u32),
        compiler_params=_cparams(("arbitrary",)),
        name="moe",
    )(pends, xs_tiles, w_gu, b_gu.reshape(N_EXPERTS, 1, 2 * D_FF), w_dn, b_dn.reshape(N_EXPERTS, 1, D_MODEL))


def _combine_body(pos_ref, ys_hbm, x_ref, gate_ref, g_ref, o_ref, buf, sem):
    i = pl.program_id(0)
    n = pl.num_programs(0)
    tc = x_ref.shape[0]
    slot = i % 2
    slot_rows = TOP_K * tc * X_TILE

    def start(step, sl):
        def body(q, carry):
            for u in range(DMA_UNROLL):
                r = q * DMA_UNROLL + u
                for kq in range(TOP_K):
                    dst = _tile_at(buf, (sl * TOP_K + kq) * tc + r)
                    pltpu.make_async_copy(_tile_at(ys_hbm, pos_ref[(step * tc + r) * TOP_K + kq]), dst,
                                          sem.at[sl]).start(priority=kq % 2)
            return carry
        lax.fori_loop(0, tc // DMA_UNROLL, body, 0)

    @pl.when(i == 0)
    def _():
        start(0, 0)

    @pl.when(i + 1 < n)
    def _():
        start(i + 1, 1 - slot)

    off = pl.multiple_of(slot * slot_rows, slot_rows)
    pltpu.make_async_copy(ys_hbm.at[pl.ds(0, slot_rows), :], buf.at[pl.ds(off, slot_rows), :], sem.at[slot]).wait()
    gates = gate_ref[...]
    acc = x_ref[...]
    for kq in range(TOP_K):
        rows = _load_packed_rows(buf, tc, off + kq * tc * X_TILE)
        acc = acc + gates[:, kq:kq + 1] * rows.astype(f32)
    o_ref[...] = _rms(acc, g_ref[...])


def _combine(pos_flat, ys_tiles, x2, gate_pad, g_final, tc):
    t = x2.shape[0]
    grid_spec = pltpu.PrefetchScalarGridSpec(
        num_scalar_prefetch=1,
        grid=(t // tc,),
        in_specs=[pl.BlockSpec(memory_space=pl.ANY),
                  pl.BlockSpec((tc, D_MODEL), lambda i, p: (i, 0)),
                  pl.BlockSpec((tc, LANES), lambda i, p: (i, 0)),
                  pl.BlockSpec((1, D_MODEL), lambda i, p: (0, 0))],
        out_specs=pl.BlockSpec((tc, D_MODEL), lambda i, p: (i, 0)),
        scratch_shapes=[pltpu.VMEM((2 * TOP_K * tc * X_TILE, LANES), u32), pltpu.SemaphoreType.DMA((2,))],
    )
    return pl.pallas_call(
        _combine_body,
        grid_spec=grid_spec,
        out_shape=jax.ShapeDtypeStruct((t, D_MODEL), f32),
        compiler_params=_cparams(("arbitrary",)),
        name="combine",
    )(pos_flat, ys_tiles, x2, gate_pad, g_final)


def _routing(top_idx, rank, counts, t):
    n_assign = t * TOP_K
    experts = jnp.arange(N_EXPERTS, dtype=i32)
    padded = (counts + MOE_BLOCK - 1) // MOE_BLOCK * MOE_BLOCK
    pends = jnp.cumsum(padded).astype(i32)
    pstarts = pends - padded
    start_of = jnp.sum(jnp.where(top_idx[:, :, None] == experts, pstarts, 0), axis=-1)
    dest = (start_of + rank).astype(i32).reshape(n_assign)
    n_blocks = (n_assign + N_EXPERTS * (MOE_BLOCK - 1) + MOE_BLOCK - 1) // MOE_BLOCK
    return dest, pends, n_blocks


def _block_diag(w):
    n, bi, bj = w.shape
    eye = jnp.eye(n, dtype=w.dtype)
    return jnp.einsum('nij,nm->nimj', w, eye).reshape(n * bi, n * bj)


def _layer(x2d, mem2d, bsz, s, mlen, p):
    t = bsz * s
    row = lambda a: a.reshape(1, -1)
    ones_blk = _block_diag(jnp.ones((RWKV_W // RWKV_HEAD, RWKV_HEAD, RWKV_HEAD), bf16))

    xl, gl, ur = _inproj(x2d, row(p['norm_mix_g']), p['w_in'].astype(bf16), tm=min(1024, t))

    wg = jnp.concatenate([_block_diag(p['lru_wx']), _block_diag(p['lru_wa'])], axis=1).astype(bf16)
    bg = jnp.concatenate([p['lru_bx'], p['lru_ba']]).reshape(1, -1)
    y_lru = _lru(xl, gl, p['conv_w'], row(p['conv_b']), wg, bg, row(p['lru_lambda']), bsz, s, ts=min(1024, s))

    pairs = lambda a: a.reshape(N_PAIRS, PAIR)
    rows8 = lambda rows: jnp.pad(jnp.stack(rows, axis=1), ((0, 0), (0, SUBLANES - len(rows)), (0, 0)))
    mu = p['rwkv_mu']
    shared = lambda a: jnp.broadcast_to(a, (N_PAIRS, PAIR))
    mu5 = rows8([pairs(mu[:RWKV_W]), pairs(mu[RWKV_W:2 * RWKV_W]), pairs(mu[2 * RWKV_W:3 * RWKV_W]),
                 shared(mu[3 * RWKV_W:3 * RWKV_W + PAIR]), shared(mu[3 * RWKV_W + PAIR:])])
    par5 = rows8([pairs(p['rwkv_w0']), pairs(p['rwkv_a0']), pairs(p['rwkv_k_k']), pairs(p['rwkv_k_a']),
                  pairs(p['rwkv_r_k'])])
    zl = jnp.zeros((DECAY_LORA, RWKV_W), f32)
    w_dec = jnp.concatenate([p['rwkv_w_up'], zl], axis=0).reshape(PAIR, N_PAIRS, PAIR)
    w_icl = jnp.concatenate([zl, p['rwkv_a_up']], axis=0).reshape(PAIR, N_PAIRS, PAIR)
    wproj = jnp.concatenate([w_dec, w_icl], axis=2).transpose(1, 0, 2).astype(bf16)
    gup = p['rwkv_g_up'].reshape(GATE_LORA, N_PAIRS, PAIR).transpose(1, 0, 2).astype(bf16)
    gm, sp, rc, yp, bon, v, g = _rwkv_chunk(ur, mu5, par5, wproj, gup, bsz, s, ts=min(2048, s))
    y_scan = _rwkv_state(gm, sp, rc, yp, bsz, s, nck=min(4, s // RWKV_CHUNK))

    x1 = _outproj(x2d, y_lru, y_scan, bon, v, g, row(p['rwkv_lnx_g']), row(p['rwkv_lnx_b']), ones_blk,
                  p['w_out'].astype(bf16), tm=min(1024, t))

    kmem, vmem = _memkv(mem2d, row(p['norm_mem_g']), p['xa_wk'].astype(bf16), p['xa_wv'].astype(bf16),
                        tm=min(512, bsz * mlen))
    x2, hf, route, gates, cnt_pad = _xattn(x1, kmem, vmem, row(p['norm_xa_g']), p['xa_wq'].astype(bf16),
                                           p['xa_wo'].astype(bf16), row(p['norm_ffn_g']),
                                           p['w_router'].T.astype(bf16), p['b_router'].reshape(-1, 1),
                                           bsz, s, mlen, tm=min(1024, s))

    counts = cnt_pad[:, 0].astype(i32)
    dest, pends, n_blocks = _routing(route[:TOP_K].T, route[TOP_K:].T, counts, t)
    gate_pad = jnp.pad(gates[:TOP_K].T, ((0, 0), (0, LANES - TOP_K)))
    xs = _dispatch(dest, pends, hf, n_blocks, tm=min(4096, t))
    ys = _moe(pends, xs, p['w_gu'], p['b_gu'], p['w_dn'], p['b_dn'], n_blocks)
    return _combine(dest, ys, x2, gate_pad, row(p['final_norm_g']), tc=min(256, t))


def kernel(x, mem, norm_mix_g, w_in, conv_w, conv_b, lru_wx, lru_bx, lru_wa, lru_ba, lru_lambda, rwkv_mu, rwkv_w0, rwkv_w_up, rwkv_a0, rwkv_a_up, rwkv_g_up, rwkv_k_k, rwkv_k_a, rwkv_r_k, rwkv_lnx_g, rwkv_lnx_b, w_out, norm_xa_g, norm_mem_g, xa_wq, xa_wk, xa_wv, xa_wo, norm_ffn_g, w_router, b_router, w_gu, b_gu, w_dn, b_dn, final_norm_g):
    bsz, s, d = x.shape
    mlen = mem.shape[1]
    assert d == D_MODEL and w_in.shape[0] == 1
    p = dict(norm_mix_g=norm_mix_g[0], w_in=w_in[0], conv_w=conv_w[0], conv_b=conv_b[0], lru_wx=lru_wx[0],
             lru_bx=lru_bx[0], lru_wa=lru_wa[0], lru_ba=lru_ba[0], lru_lambda=lru_lambda[0],
             rwkv_mu=rwkv_mu[0], rwkv_w0=rwkv_w0[0], rwkv_w_up=rwkv_w_up[0], rwkv_a0=rwkv_a0[0],
             rwkv_a_up=rwkv_a_up[0], rwkv_g_up=rwkv_g_up[0], rwkv_k_k=rwkv_k_k[0], rwkv_k_a=rwkv_k_a[0],
             rwkv_r_k=rwkv_r_k[0].reshape(-1), rwkv_lnx_g=rwkv_lnx_g[0], rwkv_lnx_b=rwkv_lnx_b[0],
             w_out=w_out[0], norm_xa_g=norm_xa_g[0], norm_mem_g=norm_mem_g[0], xa_wq=xa_wq[0],
             xa_wk=xa_wk[0], xa_wv=xa_wv[0], xa_wo=xa_wo[0], norm_ffn_g=norm_ffn_g[0],
             w_router=w_router[0], b_router=b_router[0], w_gu=w_gu[0], b_gu=b_gu[0], w_dn=w_dn[0],
             b_dn=b_dn[0], final_norm_g=final_norm_g)
    out = _layer(x.reshape(bsz * s, d), mem.reshape(bsz * mlen, d), bsz, s, mlen, p)
    return out.reshape(bsz, s, d)
```

```python
import jax
import jax.numpy as jnp
from jax import lax
from jax.experimental import pallas as pl
from jax.experimental.pallas import tpu as pltpu

f32 = jnp.float32
bf16 = jnp.bfloat16
i32 = jnp.int32
u32 = jnp.uint32

D_MODEL = 1024
LRU_W = 512
RWKV_W = 512
CONV_W = 4
LRU_C = 8.0
RWKV_HEAD = 64
DECAY_LORA = 64
AAA_LORA = 64
GATE_LORA = 128
RWKV_IN = 3 * RWKV_W + DECAY_LORA + AAA_LORA + GATE_LORA
XA_HEADS = 4
XA_HEAD = D_MODEL // XA_HEADS
N_EXPERTS = 32
TOP_K = 4
D_FF = D_MODEL
SWIGLU_LIMIT = 7.0
SWIGLU_ALPHA = 1.702
EPS = 1e-6
GN_EPS = 64e-5

LANES = 128
SUBLANES = 8
RWKV_CHUNK = 64
PAIR = 2 * RWKV_HEAD
N_PAIRS = RWKV_W // PAIR
MOE_BLOCK = 512
X_TILE = D_MODEL // (2 * LANES)
V7X_VMEM_BYTES = 64 * 1024 * 1024
VMEM_LIMIT = V7X_VMEM_BYTES - 12 * 1024 * 1024


def _cparams(sem):
    return pltpu.CompilerParams(dimension_semantics=sem, vmem_limit_bytes=VMEM_LIMIT)


def _rms(x, g):
    return x * lax.rsqrt(jnp.mean(x * x, axis=-1, keepdims=True) + EPS) * g


def _full(shape):
    n = len(shape)
    return pl.BlockSpec(shape, lambda *a: (0,) * n)


def _shift_rows(x, prev8, d):
    xr = pltpu.roll(x, d, 0)
    tr = pltpu.roll(prev8, d, 0)
    row = lax.broadcasted_iota(i32, prev8.shape, 0)
    head = jnp.where(row < d, tr, xr[:SUBLANES])
    return jnp.concatenate([head, xr[SUBLANES:]], axis=0)


def _bf16_parts(x, n):
    parts = []
    for _ in range(n):
        piece = x.astype(bf16)
        parts.append(piece)
        x = x - piece.astype(f32)
    return parts


def _sum_dot(x, mask_bf, n_parts, mask_left=False):
    acc = None
    for piece in _bf16_parts(x, n_parts):
        d = (jnp.dot(mask_bf, piece, preferred_element_type=f32) if mask_left
             else jnp.dot(piece, mask_bf, preferred_element_type=f32))
        acc = d if acc is None else acc + d
    return acc


def _store_packed_rows(ref, val):
    n = val.shape[0]
    half = D_MODEL // 2
    lo = lax.bitcast_convert_type(val[:, :half].astype(bf16).astype(f32), u32) >> 16
    hi = lax.bitcast_convert_type(val[:, half:].astype(bf16).astype(f32), u32) & jnp.uint32(0xFFFF0000)
    words = hi | lo
    for j in range(X_TILE):
        ref[pl.ds(j, n, stride=X_TILE), :] = words[:, j * LANES:(j + 1) * LANES]


def _load_packed_rows(ref, n, off=0):
    ws = [ref[pl.ds(off + j, n, stride=X_TILE), :] for j in range(X_TILE)]
    lo = [lax.bitcast_convert_type(w << 16, f32).astype(bf16) for w in ws]
    hi = [lax.bitcast_convert_type(w & jnp.uint32(0xFFFF0000), f32).astype(bf16) for w in ws]
    return jnp.concatenate(lo + hi, axis=1)


def _inproj_body(x_ref, g_ref, w_ref, xl_ref, gl_ref, ur_ref):
    h = _rms(x_ref[...], g_ref[...])
    u = jnp.dot(h.astype(bf16), w_ref[...], preferred_element_type=f32)
    xl_ref[...] = u[:, :LRU_W]
    gl_ref[...] = u[:, LRU_W:2 * LRU_W]
    ur_ref[...] = u[:, 2 * LRU_W:]


def _inproj(x2d, g, w_in_bf, tm):
    t = x2d.shape[0]
    return pl.pallas_call(
        _inproj_body,
        grid=(t // tm,),
        in_specs=[pl.BlockSpec((tm, D_MODEL), lambda i: (i, 0)), _full((1, D_MODEL)),
                  _full(w_in_bf.shape)],
        out_specs=[pl.BlockSpec((tm, LRU_W), lambda i: (i, 0)),
                   pl.BlockSpec((tm, LRU_W), lambda i: (i, 0)),
                   pl.BlockSpec((tm, RWKV_IN), lambda i: (i, 0))],
        out_shape=[jax.ShapeDtypeStruct((t, LRU_W), f32), jax.ShapeDtypeStruct((t, LRU_W), f32),
                   jax.ShapeDtypeStruct((t, RWKV_IN), f32)],
        compiler_params=_cparams(("parallel",)),
        name="inproj",
    )(x2d, g, w_in_bf)


def _lru_body(xl_ref, gl_ref, cw_ref, cb_ref, wg_ref, bg_ref, lam_ref, o_ref, tail_ref, h_ref):
    ts = xl_ref.shape[0]

    @pl.when(pl.program_id(1) == 0)
    def _():
        tail_ref[...] = jnp.zeros_like(tail_ref)
        h_ref[...] = jnp.zeros_like(h_ref)

    x = xl_ref[...]
    tail = tail_ref[...]
    cw = cw_ref[...]
    xc = cb_ref[...] + cw[CONV_W - 1:CONV_W] * x
    for d in range(1, CONV_W):
        xc = xc + cw[CONV_W - 1 - d:CONV_W - d] * _shift_rows(x, tail, d)
    tail_ref[...] = x[ts - SUBLANES:]

    gates = jax.nn.sigmoid(jnp.dot(xc.astype(bf16), wg_ref[...], preferred_element_type=f32) + bg_ref[...])
    gx = gates[:, :LRU_W]
    ga = gates[:, LRU_W:]
    log_a = -LRU_C * ga * jax.nn.softplus(-lam_ref[...])
    a = jnp.exp(log_a)
    b = jnp.sqrt(-jnp.tanh(log_a) * (a * a + 1.0)) * gx * xc

    row = lax.broadcasted_iota(i32, (ts, LRU_W), 0) % SUBLANES
    d = 1
    while d < SUBLANES:
        keep = row >= d
        a_s = jnp.where(keep, pltpu.roll(a, d, 0), 1.0)
        b_s = jnp.where(keep, pltpu.roll(b, d, 0), 0.0)
        b = a * b_s + b
        a = a * a_s
        d *= 2
    carry = h_ref[SUBLANES - 1:SUBLANES, :]
    groups = []
    for q in range(ts // SUBLANES):
        rows = slice(q * SUBLANES, (q + 1) * SUBLANES)
        hq = b[rows] + a[rows] * carry
        carry = hq[SUBLANES - 1:SUBLANES, :]
        groups.append(hq)
    h = jnp.concatenate(groups, axis=0)
    h_ref[...] = jnp.broadcast_to(carry, h_ref.shape)
    o_ref[...] = (h * jax.nn.gelu(gl_ref[...])).astype(o_ref.dtype)


def _lru(xl, gl, cw, cb, wg_bf, bg, lam, bsz, s, ts):
    nt = s // ts
    blk = pl.BlockSpec((ts, LRU_W), lambda b, i: (b * nt + i, 0))
    return pl.pallas_call(
        _lru_body,
        grid=(bsz, nt),
        in_specs=[blk, blk, _full(cw.shape), _full(cb.shape), _full(wg_bf.shape), _full(bg.shape),
                  _full(lam.shape)],
        out_specs=blk,
        out_shape=jax.ShapeDtypeStruct((bsz * s, LRU_W), bf16),
        scratch_shapes=[pltpu.VMEM((SUBLANES, LRU_W), f32), pltpu.VMEM((SUBLANES, LRU_W), f32)],
        compiler_params=_cparams(("parallel", "arbitrary")),
        name="lru",
    )(xl, gl, cw, cb, wg_bf, bg, lam)


def _mm_x3(a, b_parts):
    ah, al = _bf16_parts(a, 2)
    bh, bl = b_parts
    d = lambda x, y: jnp.dot(x, y, preferred_element_type=f32)
    return d(ah, bh) + (d(ah, bl) + d(al, bh))


def _mm(a, b):
    return jnp.dot(a.astype(bf16), b.astype(bf16), preferred_element_type=f32)


def _mm_nt(a, b):
    return lax.dot_general(a.astype(bf16), b.astype(bf16), (((1,), (1,)), ((), ())), preferred_element_type=f32)


def _mm_tn(a, b):
    return lax.dot_general(a.astype(bf16), b.astype(bf16), (((0,), (0,)), ((), ())), preferred_element_type=f32)


def _bd(x):
    m0 = lax.broadcasted_iota(i32, x.shape, 1) < RWKV_HEAD
    zero = jnp.zeros_like(x)
    return jnp.concatenate([jnp.where(m0, x, zero), jnp.where(m0, zero, x)], axis=0)


def _side_by_side(d):
    h = d.shape[0] // 2
    m0 = lax.broadcasted_iota(i32, (h, d.shape[1]), 1) < RWKV_HEAD
    return jnp.where(m0, d[:h], d[h:])


def _chunk_maps(chunks):
    c = RWKV_CHUNK
    ri = lax.broadcasted_iota(i32, (c, PAIR), 0)
    ji = lax.broadcasted_iota(i32, (c, PAIR), 1) % RWKV_HEAD
    strict = ji < ri
    incl = ji <= ri
    diag = ji == ri
    eye = jnp.where(diag, 1.0, 0.0).astype(f32)
    each = lambda f, *ls: [f(*xs) for xs in zip(*ls)]
    cat0 = lambda *xs: jnp.concatenate(xs, axis=0)
    cat1 = lambda *xs: jnp.concatenate(xs, axis=1)
    tb = lambda x: x.astype(bf16)

    ats, bts, kts, rts, vs, cls = [list(x) for x in zip(*chunks)]
    pcs = each(lambda cl: jnp.exp(cl[c - 1:c, :]), cls)
    bd_a = each(lambda x: tb(_bd(x)), ats)
    bd_v = each(lambda x: tb(_bd(x)), vs)

    aa = each(lambda a, r, b, k: _mm_nt(cat0(tb(a), tb(r)), cat0(tb(_bd(b)), tb(_bd(k)))), ats, rts, bts, kts)
    l_ab = each(lambda x: tb(jnp.where(strict, x[:c, :PAIR], 0.0)), aa)
    a_k = each(lambda x: tb(cat0(jnp.where(strict, x[:c, PAIR:], 0.0), jnp.where(incl, x[c:, PAIR:], 0.0))), aa)
    a_rb = each(lambda x: tb(jnp.where(incl, x[c:, :PAIR], 0.0)), aa)

    tinv = each(lambda x: eye + x, l_ab)
    lp = each(lambda x: tb(_mm(x, _bd(x))), l_ab)
    p = 2
    while 2 * p < c:
        x2 = each(lambda t, x: _mm(cat0(tb(t), x), _bd(x)), tinv, lp)
        tinv = each(lambda t, x: t + x[:c], tinv, x2)
        lp = each(lambda x: tb(x[c:]), x2)
        p *= 2
    tinv = each(lambda t, x: t + _mm(t, _bd(x)), tinv, lp)

    wy = each(_mm, a_k, bd_v)
    za = each(lambda t, w, a: _mm(t, cat1(tb(_bd(w[:c])), a)), tinv, wy, bd_a)
    zp = each(lambda x: x[:, :PAIR], za)
    ac = each(lambda x: x[:, PAIR:], za)
    y1 = each(lambda ar, z, a: _mm(ar, cat1(tb(_bd(z)), tb(_bd(a)))), a_rb, zp, ac)
    yp = each(lambda w, y: w[c:] + y[:, :PAIR], wy, y1)
    rc = each(lambda r, y: r + y[:, PAIR:], rts, y1)
    sm = each(lambda b, pc, z, a: _mm_tn(b * pc, cat1(z, a)), bts, pcs, zp, ac)
    kv = each(lambda k, pc, v: _mm_tn(k * pc, v), kts, pcs, vs)
    sp = each(lambda x, m: _side_by_side(x) + _side_by_side(m[:, :PAIR]), kv, sm)
    g = each(lambda pc, m: jnp.where(diag, jnp.broadcast_to(pc, (c, PAIR)), 0.0) + _side_by_side(m[:, PAIR:]), pcs, sm)
    return list(zip(g, sp, rc, yp))


N_UCOLS = 5


def _rwkv_chunk_body(ur_r, ur_k, ur_v, ur_lo, ur_dg, mu_ref, par_ref, wproj_ref, gup_ref, ones_ref, tri_ref,
                     g_ref, sp_ref, rc_ref, yp_ref, bon_ref, v_ref, gate_ref, prev_ref):
    ts = ur_r.shape[0]
    c = RWKV_CHUNK

    @pl.when(pl.program_id(2) == 0)
    def _():
        prev_ref[...] = jnp.zeros_like(prev_ref)

    mixed = []
    for j, ref in enumerate((ur_r, ur_k, ur_v, ur_lo, ur_dg)):
        u0 = ref[...]
        ls = slice(j * LANES, (j + 1) * LANES)
        us = _shift_rows(u0, prev_ref[:, ls], 1)
        prev_ref[:, ls] = u0[ts - SUBLANES:]
        mixed.append(u0 + (us - u0) * mu_ref[0, j:j + 1, :])
    r, k, v, lora, dg = mixed
    w0, a0, k_k, k_a, r_k = [par_ref[0, j:j + 1, :] for j in range(5)]

    lane = lax.broadcasted_iota(i32, lora.shape, 1)
    lora = jnp.where(lane < DECAY_LORA, jnp.tanh(lora), lora)
    proj = jnp.dot(lora.astype(bf16), wproj_ref[0], preferred_element_type=f32)
    w = -jax.nn.softplus(-(w0 + proj[:, :PAIR])) - 0.5
    lw = -jnp.exp(w)
    a = jax.nn.sigmoid(a0 + proj[:, PAIR:])
    gate_ref[...] = jnp.dot(jax.nn.sigmoid(dg).astype(bf16), gup_ref[0], preferred_element_type=f32)

    ones = ones_ref[...]
    kk = k * k_k
    ss = _sum_dot(kk * kk, ones, 2)
    kk = kk / jnp.maximum(jnp.sqrt(ss), 1e-12)
    k2 = k * (1.0 + (a - 1.0) * k_a)
    bon_ref[...] = _sum_dot(r * k2 * r_k, ones, 2)
    v_ref[...] = v

    tri = tri_ref[...]
    grp = tri.shape[0]
    cl = jnp.concatenate([_sum_dot(lw[q * grp:(q + 1) * grp], tri, 3, mask_left=True) for q in range(ts // grp)],
                         axis=0)
    e_neg = jnp.exp(-cl)
    at = -kk * jnp.exp(cl - lw)
    bt = kk * a * e_neg
    kt = k2 * e_neg
    rt = r * jnp.exp(cl)

    sls = [slice(j * c, (j + 1) * c) for j in range(ts // c)]
    outs = _chunk_maps([(at[sl], bt[sl], kt[sl], rt[sl], v[sl], cl[sl]) for sl in sls])
    for sl, (g, sp, rc, yp) in zip(sls, outs):
        g_ref[sl, :] = g
        sp_ref[sl, :] = sp
        rc_ref[sl, :] = rc
        yp_ref[sl, :] = yp


def _rwkv_chunk(ur, mu5, par5, wproj, gup, bsz, s, ts):
    t = bsz * s
    nt = s // ts
    col = lambda blk: pl.BlockSpec((ts, PAIR), lambda p, b, i, blk=blk: (b * nt + i, blk(p)))
    nrw = RWKV_W // PAIR
    ucols = [col(lambda p: p), col(lambda p: nrw + p), col(lambda p: 2 * nrw + p), col(lambda p: 3 * nrw),
             col(lambda p: 3 * nrw + 1)]
    per_pair = lambda a: pl.BlockSpec((1,) + a.shape[1:], lambda p, b, i: (p, 0, 0))
    ones_pair = _block_diag(jnp.ones((2, RWKV_HEAD, RWKV_HEAD), bf16))
    tri = _block_diag(jnp.tril(jnp.ones((2, RWKV_CHUNK, RWKV_CHUNK), bf16)))
    oblk = pl.BlockSpec((ts, PAIR), lambda p, b, i: (b * nt + i, p))
    osh = jax.ShapeDtypeStruct((t, RWKV_W), f32)
    return pl.pallas_call(
        _rwkv_chunk_body,
        grid=(N_PAIRS, bsz, nt),
        in_specs=ucols + [per_pair(mu5), per_pair(par5), per_pair(wproj), per_pair(gup), _full(ones_pair.shape),
                          _full(tri.shape)],
        out_specs=[oblk] * 7,
        out_shape=[osh] * 7,
        scratch_shapes=[pltpu.VMEM((SUBLANES, N_UCOLS * LANES), f32)],
        compiler_params=_cparams(("parallel", "parallel", "arbitrary")),
        name="rwkv_chunk",
    )(ur, ur, ur, ur, ur, mu5, par5, wproj, gup, ones_pair, tri)


def _rwkv_state_body(g_ref, sp_ref, rc_ref, yp_ref, y_ref, s_ref):
    @pl.when(pl.program_id(0) == 0)
    def _():
        s_ref[...] = jnp.zeros_like(s_ref)

    c = RWKV_CHUNK
    bsz = g_ref.shape[0]
    nck = g_ref.shape[1] // c
    chains = [(b, slice(p * PAIR, (p + 1) * PAIR)) for b in range(bsz) for p in range(N_PAIRS)]
    states = [s_ref[b, :, ls] for b, ls in chains]
    for j in range(nck):
        sl = slice(j * c, (j + 1) * c)
        prods = [_mm_x3(jnp.concatenate([rc_ref[b, sl, ls], g_ref[b, sl, ls]], axis=0), _bf16_parts(_bd(s), 2))
                 for (b, ls), s in zip(chains, states)]
        for (b, ls), pr in zip(chains, prods):
            y_ref[b, sl, ls] = yp_ref[b, sl, ls] + pr[:c]
        states = [pr[c:] + sp_ref[b, sl, ls] for (b, ls), pr in zip(chains, prods)]
    for (b, ls), s in zip(chains, states):
        s_ref[b, :, ls] = s


def _rwkv_state(g, sp, rc, yp, bsz, s, nck):
    rows = nck * RWKV_CHUNK
    rblk = pl.BlockSpec((bsz, rows, RWKV_W), lambda i: (0, i, 0))
    r3 = lambda a: a.reshape(bsz, s, RWKV_W)
    y = pl.pallas_call(
        _rwkv_state_body,
        grid=(s // rows,),
        in_specs=[rblk] * 4,
        out_specs=rblk,
        out_shape=jax.ShapeDtypeStruct((bsz, s, RWKV_W), f32),
        scratch_shapes=[pltpu.VMEM((bsz, RWKV_CHUNK, RWKV_W), f32)],
        compiler_params=_cparams(("arbitrary",)),
        name="rwkv_state",
    )(r3(g), r3(sp), r3(rc), r3(yp))
    return y.reshape(bsz * s, RWKV_W)


def _outproj_body(x_ref, yl_ref, ys_ref, bon_ref, v_ref, g_ref, lg_ref, lb_ref, ones_ref, w_ref, o_ref):
    y = ys_ref[...]
    ones = ones_ref[...]
    inv_n = 1.0 / RWKV_HEAD
    mean = _sum_dot(y, ones, 2) * inv_n
    yc = y - mean
    var = _sum_dot(yc * yc, ones, 2) * inv_n
    yn = yc * lax.rsqrt(var + GN_EPS) * lg_ref[...] + lb_ref[...]
    yr = (yn + bon_ref[...] * v_ref[...]) * g_ref[...]
    cat = jnp.concatenate([yl_ref[...], yr.astype(bf16)], axis=1)
    o_ref[...] = x_ref[...] + jnp.dot(cat, w_ref[...], preferred_element_type=f32)


def _outproj(x2d, y_lru, y_scan, bon, v, g, lnx_g, lnx_b, ones_blk, w_out_bf, tm):
    t = x2d.shape[0]
    xb = pl.BlockSpec((tm, D_MODEL), lambda i: (i, 0))
    hb = pl.BlockSpec((tm, RWKV_W), lambda i: (i, 0))
    return pl.pallas_call(
        _outproj_body,
        grid=(t // tm,),
        in_specs=[xb, hb, hb, hb, hb, hb, _full(lnx_g.shape), _full(lnx_b.shape), _full(ones_blk.shape),
                  _full(w_out_bf.shape)],
        out_specs=xb,
        out_shape=jax.ShapeDtypeStruct((t, D_MODEL), f32),
        compiler_params=_cparams(("parallel",)),
        name="outproj",
    )(x2d, y_lru, y_scan, bon, v, g, lnx_g, lnx_b, ones_blk, w_out_bf)


def _memkv_body(m_ref, g_ref, wk_ref, wv_ref, k_ref, v_ref):
    h = _rms(m_ref[...], g_ref[...]).astype(bf16)
    k_ref[...] = jnp.dot(h, wk_ref[...], preferred_element_type=f32).astype(bf16)
    v_ref[...] = jnp.dot(h, wv_ref[...], preferred_element_type=f32).astype(bf16)


def _memkv(mem2d, g, wk_bf, wv_bf, tm):
    t = mem2d.shape[0]
    blk = pl.BlockSpec((tm, D_MODEL), lambda i: (i, 0))
    sh = jax.ShapeDtypeStruct((t, D_MODEL), bf16)
    return pl.pallas_call(
        _memkv_body,
        grid=(t // tm,),
        in_specs=[blk, _full(g.shape), _full(wk_bf.shape), _full(wv_bf.shape)],
        out_specs=[blk, blk],
        out_shape=[sh, sh],
        compiler_params=_cparams(("parallel",)),
        name="memkv",
    )(mem2d, g, wk_bf, wv_bf)


def _xattn_body(x_ref, k_ref, v_ref, gx_ref, wq_ref, wo_ref, gf_ref, wr_ref, br_ref, upper_ref,
                x2_ref, hf_ref, idx_ref, gate_ref, cnt_ref, base_ref):
    x = x_ref[...]
    h = _rms(x, gx_ref[...]).astype(bf16)
    q = jnp.dot(h, wq_ref[...], preferred_element_type=f32).astype(bf16)
    k = k_ref[...]
    v = v_ref[...]
    heads = [slice(hd * XA_HEAD, (hd + 1) * XA_HEAD) for hd in range(XA_HEADS)]
    scs = [lax.dot_general(q[:, sl], k[:, sl], (((1,), (1,)), ((), ())), preferred_element_type=f32)
           * (XA_HEAD ** -0.5) for sl in heads]
    ps = []
    for sc in scs:
        e = jnp.exp(sc - jnp.max(sc, axis=-1, keepdims=True))
        ps.append((e / jnp.sum(e, axis=-1, keepdims=True)).astype(bf16))
    o = jnp.concatenate([jnp.dot(p, v[:, sl], preferred_element_type=f32).astype(bf16)
                         for p, sl in zip(ps, heads)], axis=1)
    x2 = x + jnp.dot(o, wo_ref[...], preferred_element_type=f32)
    x2_ref[...] = x2

    hf = _rms(x2, gf_ref[...])
    _store_packed_rows(hf_ref, hf)
    logits = lax.dot_general(wr_ref[...], hf.astype(bf16), (((1,), (1,)), ((), ())),
                             preferred_element_type=f32) + br_ref[...]
    erow = lax.broadcasted_iota(i32, logits.shape, 0)
    neg = jnp.float32(-jnp.inf)
    cur = logits
    vals = []
    idxs = []
    for _ in range(TOP_K):
        m = jnp.max(cur, axis=0, keepdims=True)
        am = jnp.min(jnp.where(cur == m, erow, N_EXPERTS), axis=0, keepdims=True)
        vals.append(m)
        idxs.append(am)
        cur = jnp.where(erow == am, neg, cur)
    es = [jnp.exp(vk - vals[0]) for vk in vals]
    den = es[0] + es[1] + es[2] + es[3]

    @pl.when(pl.program_id(0) == 0)
    def _():
        base_ref[...] = jnp.zeros_like(base_ref)

    onehot = [jnp.where(erow == am, 1.0, 0.0) for am in idxs]
    cnt = (onehot[0] + onehot[1]) + (onehot[2] + onehot[3])
    base = base_ref[:, 0:1]
    prior = jnp.dot(cnt.astype(bf16), upper_ref[...], preferred_element_type=f32) + base
    base_ref[...] = jnp.broadcast_to(base + jnp.sum(cnt, axis=1, keepdims=True), base_ref.shape)
    cnt_ref[...] = base_ref[...]

    orow = lax.broadcasted_iota(i32, idx_ref.shape, 0)
    idx_out = jnp.zeros(idx_ref.shape, i32)
    gate_out = jnp.zeros(gate_ref.shape, f32)
    for kq in range(TOP_K):
        rank = jnp.sum(prior * onehot[kq], axis=0, keepdims=True).astype(i32)
        idx_out = jnp.where(orow == kq, idxs[kq], idx_out)
        idx_out = jnp.where(orow == TOP_K + kq, rank, idx_out)
        gate_out = jnp.where(orow == kq, es[kq] / den, gate_out)
    idx_ref[...] = idx_out
    gate_ref[...] = gate_out


def _xattn(x1, kmem, vmem, g_xa, wq_bf, wo_bf, g_ffn, wr_t, br_col, bsz, s, mlen, tm):
    t = bsz * s
    nt = s // tm
    xb = pl.BlockSpec((tm, D_MODEL), lambda i: (i, 0))
    mb = pl.BlockSpec((mlen, D_MODEL), lambda i: (i // nt, 0))
    lb = pl.BlockSpec((2 * TOP_K, tm), lambda i: (0, i))
    upper = jnp.triu(jnp.ones((tm, tm), bf16), k=1)
    return pl.pallas_call(
        _xattn_body,
        grid=(t // tm,),
        in_specs=[xb, mb, mb, _full(g_xa.shape), _full(wq_bf.shape), _full(wo_bf.shape),
                  _full(g_ffn.shape), _full(wr_t.shape), _full(br_col.shape), _full(upper.shape)],
        out_specs=[xb, pl.BlockSpec((tm * X_TILE, LANES), lambda i: (i, 0)), lb, lb, _full((N_EXPERTS, LANES))],
        out_shape=[jax.ShapeDtypeStruct((t, D_MODEL), f32), jax.ShapeDtypeStruct((t * X_TILE, LANES), u32),
                   jax.ShapeDtypeStruct((2 * TOP_K, t), i32), jax.ShapeDtypeStruct((2 * TOP_K, t), f32),
                   jax.ShapeDtypeStruct((N_EXPERTS, LANES), f32)],
        scratch_shapes=[pltpu.VMEM((N_EXPERTS, LANES), f32)],
        compiler_params=_cparams(("arbitrary",)),
        name="xattn",
    )(x1, kmem, vmem, g_xa, wq_bf, wo_bf, g_ffn, wr_t, br_col, upper)


DMA_UNROLL = 8


def _tile_at(ref, row):
    return ref.at[pl.ds(pl.multiple_of(row * X_TILE, X_TILE), X_TILE), :]


def _dispatch_body(dest_ref, pend_ref, hf_ref, xs_hbm, zeros_ref, sem, zsem):
    i = pl.program_id(0)
    tm = hf_ref.shape[0] // X_TILE
    zrows = MOE_BLOCK * X_TILE

    def zero_copy(e):
        start = pl.multiple_of((pend_ref[e] - MOE_BLOCK) * X_TILE, X_TILE)
        return pltpu.make_async_copy(zeros_ref, xs_hbm.at[pl.ds(start, zrows), :], zsem)

    def nonempty(e):
        return pend_ref[e] > (pend_ref[e - 1] if e else 0)

    @pl.when(i == 0)
    def _():
        zeros_ref[...] = jnp.zeros_like(zeros_ref)
        for e in range(N_EXPERTS):
            @pl.when(nonempty(e))
            def _():
                zero_copy(e).start()
        for e in range(N_EXPERTS):
            @pl.when(nonempty(e))
            def _():
                zero_copy(e).wait()

        def tail_copy(b):
            return pltpu.make_async_copy(zeros_ref, xs_hbm.at[pl.ds(pl.multiple_of(b * zrows, zrows), zrows), :], zsem)

        def tail_start(b, carry):
            tail_copy(b).start()
            return carry

        def tail_wait(b, carry):
            tail_copy(b).wait()
            return carry
        n_used = pend_ref[N_EXPERTS - 1] // MOE_BLOCK
        n_all = xs_hbm.shape[0] // zrows
        lax.fori_loop(n_used, n_all, tail_start, 0)
        lax.fori_loop(n_used, n_all, tail_wait, 0)

    def body(q, carry):
        for u in range(DMA_UNROLL):
            r = q * DMA_UNROLL + u
            src = _tile_at(hf_ref, r)
            for kq in range(TOP_K):
                pltpu.make_async_copy(src, _tile_at(xs_hbm, dest_ref[(i * tm + r) * TOP_K + kq]),
                                      sem).start(priority=kq % 2)
        return carry
    lax.fori_loop(0, tm // DMA_UNROLL, body, 0)
    for kq in range(TOP_K):
        pltpu.make_async_copy(hf_ref, xs_hbm.at[pl.ds(0, tm * X_TILE), :], sem).wait()


def _dispatch(dest_flat, pends, hf_tiles, n_blocks, tm):
    t = hf_tiles.shape[0] // X_TILE
    rows = n_blocks * MOE_BLOCK
    grid_spec = pltpu.PrefetchScalarGridSpec(
        num_scalar_prefetch=2,
        grid=(t // tm,),
        in_specs=[pl.BlockSpec((tm * X_TILE, LANES), lambda i, d, pe: (i, 0))],
        out_specs=pl.BlockSpec(memory_space=pl.ANY),
        scratch_shapes=[pltpu.VMEM((MOE_BLOCK * X_TILE, LANES), u32), pltpu.SemaphoreType.DMA(()),
                        pltpu.SemaphoreType.DMA(())],
    )
    return pl.pallas_call(
        _dispatch_body,
        grid_spec=grid_spec,
        out_shape=jax.ShapeDtypeStruct((rows * X_TILE, LANES), u32),
        compiler_params=_cparams(("arbitrary",)),
        name="dispatch",
    )(dest_flat, pends, hf_tiles)


def _moe_body(pend_ref, xs_hbm, wgu_ref, bgu_ref, wdn_ref, bdn_ref, ys_hbm, xbuf, ybuf, wgu_bf, wdn_bf,
              sem_in, sem_out):
    e = pl.program_id(0)
    blk = MOE_BLOCK * X_TILE
    end_blk = pend_ref[e] // MOE_BLOCK
    start_blk = jnp.where(e == 0, 0, pend_ref[jnp.maximum(e - 1, 0)] // MOE_BLOCK)
    nb = end_blk - start_blk

    n_total = pend_ref[N_EXPERTS - 1] // MOE_BLOCK

    def rows_of(b):
        return pl.ds(pl.multiple_of(b * blk, blk), blk)

    def x_copy(b):
        return pltpu.make_async_copy(xs_hbm.at[rows_of(b), :], xbuf.at[b % 2], sem_in.at[b % 2])

    def y_copy(b):
        return pltpu.make_async_copy(ybuf.at[b % 2], ys_hbm.at[rows_of(b), :], sem_out.at[b % 2])

    @pl.when((e == 0) & (n_total > 0))
    def _():
        x_copy(0).start()

    @pl.when(nb > 0)
    def _():
        wgu_bf[...] = wgu_ref[0].astype(bf16)
        wdn_bf[...] = wdn_ref[0].astype(bf16)

    def block(j, carry):
        b = start_blk + j
        slot = b % 2

        @pl.when(b + 1 < n_total)
        def _():
            x_copy(b + 1).start()

        x_copy(b).wait()

        @pl.when(b >= 2)
        def _():
            y_copy(b - 2).wait()

        xb = _load_packed_rows(xbuf.at[slot], MOE_BLOCK)
        gu = jnp.dot(xb, wgu_bf[...], preferred_element_type=f32) + bgu_ref[0]
        gate = jnp.minimum(gu[:, :D_FF], SWIGLU_LIMIT)
        up = jnp.clip(gu[:, D_FF:], -SWIGLU_LIMIT, SWIGLU_LIMIT)
        act = (up + 1.0) * (gate * jax.nn.sigmoid(SWIGLU_ALPHA * gate))
        y = jnp.dot(act.astype(bf16), wdn_bf[...], preferred_element_type=f32) + bdn_ref[0]
        _store_packed_rows(ybuf.at[slot], y)
        y_copy(b).start()
        return carry

    lax.fori_loop(0, nb, block, 0)

    @pl.when(e == N_EXPERTS - 1)
    def _():
        @pl.when(n_total >= 2)
        def _():
            y_copy(n_total - 2).wait()

        @pl.when(n_total >= 1)
        def _():
            y_copy(n_total - 1).wait()

        n_all = ys_hbm.shape[0] // blk
        ybuf[0] = jnp.zeros(ybuf.shape[1:], ybuf.dtype)

        def tail_copy(b):
            return pltpu.make_async_copy(ybuf.at[0], ys_hbm.at[pl.ds(pl.multiple_of(b * blk, blk), blk), :],
                                         sem_out.at[0])

        def tail_start(b, carry):
            tail_copy(b).start()
            return carry

        def tail_wait(b, carry):
            tail_copy(b).wait()
            return carry
        lax.fori_loop(end_blk, n_all, tail_start, 0)
        lax.fori_loop(end_blk, n_all, tail_wait, 0)


def _moe(pends, xs_tiles, w_gu, b_gu, w_dn, b_dn, n_blocks):
    rows = n_blocks * MOE_BLOCK
    grid_spec = pltpu.PrefetchScalarGridSpec(
        num_scalar_prefetch=1,
        grid=(N_EXPERTS,),
        in_specs=[
            pl.BlockSpec(memory_space=pl.ANY),
            pl.BlockSpec((1, D_MODEL, 2 * D_FF), lambda e, pe: (e, 0, 0)),
            pl.BlockSpec((1, 1, 2 * D_FF), lambda e, pe: (e, 0, 0)),
            pl.BlockSpec((1, D_FF, D_MODEL), lambda e, pe: (e, 0, 0)),
            pl.BlockSpec((1, 1, D_MODEL), lambda e, pe: (e, 0, 0)),
        ],
        out_specs=pl.BlockSpec(memory_space=pl.ANY),
        scratch_shapes=[pltpu.VMEM((2, MOE_BLOCK * X_TILE, LANES), u32),
                        pltpu.VMEM((2, MOE_BLOCK * X_TILE, LANES), u32),
                        pltpu.VMEM((D_MODEL, 2 * D_FF), bf16), pltpu.VMEM((D_FF, D_MODEL), bf16),
                        pltpu.SemaphoreType.DMA((2,)), pltpu.SemaphoreType.DMA((2,))],
    )
    return pl.pallas_call(
        _moe_body,
        grid_spec=grid_spec,
        out_shape=jax.ShapeDtypeStruct((rows * X_TILE, LANES), u32),
        compiler_params=_cparams(("arbitrary",)),
        name="moe",
    )(pends, xs_tiles, w_gu, b_gu.reshape(N_EXPERTS, 1, 2 * D_FF), w_dn, b_dn.reshape(N_EXPERTS, 1, D_MODEL))


def _combine_body(pos_ref, ys_hbm, x_ref, gate_ref, g_ref, o_ref, buf, sem):
    i = pl.program_id(0)
    n = pl.num_programs(0)
    tc = x_ref.shape[0]
    slot = i % 2
    slot_rows = TOP_K * tc * X_TILE

    def start(step, sl):
        def body(q, carry):
            for u in range(DMA_UNROLL):
                r = q * DMA_UNROLL + u
                for kq in range(TOP_K):
                    dst = _tile_at(buf, (sl * TOP_K + kq) * tc + r)
                    pltpu.make_async_copy(_tile_at(ys_hbm, pos_ref[(step * tc + r) * TOP_K + kq]), dst,
                                          sem.at[sl]).start(priority=kq % 2)
            return carry
        lax.fori_loop(0, tc // DMA_UNROLL, body, 0)

    @pl.when(i == 0)
    def _():
        start(0, 0)

    @pl.when(i + 1 < n)
    def _():
        start(i + 1, 1 - slot)

    off = pl.multiple_of(slot * slot_rows, slot_rows)
    pltpu.make_async_copy(ys_hbm.at[pl.ds(0, slot_rows), :], buf.at[pl.ds(off, slot_rows), :], sem.at[slot]).wait()
    gates = gate_ref[...]
    acc = x_ref[...]
    for kq in range(TOP_K):
        rows = _load_packed_rows(buf, tc, off + kq * tc * X_TILE)
        acc = acc + gates[:, kq:kq + 1] * rows.astype(f32)
    o_ref[...] = _rms(acc, g_ref[...])


def _combine(pos_flat, ys_tiles, x2, gate_pad, g_final, tc):
    t = x2.shape[0]
    grid_spec = pltpu.PrefetchScalarGridSpec(
        num_scalar_prefetch=1,
        grid=(t // tc,),
        in_specs=[pl.BlockSpec(memory_space=pl.ANY),
                  pl.BlockSpec((tc, D_MODEL), lambda i, p: (i, 0)),
                  pl.BlockSpec((tc, LANES), lambda i, p: (i, 0)),
                  pl.BlockSpec((1, D_MODEL), lambda i, p: (0, 0))],
        out_specs=pl.BlockSpec((tc, D_MODEL), lambda i, p: (i, 0)),
        scratch_shapes=[pltpu.VMEM((2 * TOP_K * tc * X_TILE, LANES), u32), pltpu.SemaphoreType.DMA((2,))],
    )
    return pl.pallas_call(
        _combine_body,
        grid_spec=grid_spec,
        out_shape=jax.ShapeDtypeStruct((t, D_MODEL), f32),
        compiler_params=_cparams(("arbitrary",)),
        name="combine",
    )(pos_flat, ys_tiles, x2, gate_pad, g_final)


def _routing(top_idx, rank, counts, t):
    n_assign = t * TOP_K
    experts = jnp.arange(N_EXPERTS, dtype=i32)
    padded = (counts + MOE_BLOCK - 1) // MOE_BLOCK * MOE_BLOCK
    pends = jnp.cumsum(padded).astype(i32)
    pstarts = pends - padded
    start_of = jnp.sum(jnp.where(top_idx[:, :, None] == experts, pstarts, 0), axis=-1)
    dest = (start_of + rank).astype(i32).reshape(n_assign)
    n_blocks = (n_assign + N_EXPERTS * (MOE_BLOCK - 1) + MOE_BLOCK - 1) // MOE_BLOCK
    return dest, pends, n_blocks


def _block_diag(w):
    n, bi, bj = w.shape
    eye = jnp.eye(n, dtype=w.dtype)
    return jnp.einsum('nij,nm->nimj', w, eye).reshape(n * bi, n * bj)


def _layer(x2d, mem2d, bsz, s, mlen, p):
    t = bsz * s
    row = lambda a: a.reshape(1, -1)
    ones_blk = _block_diag(jnp.ones((RWKV_W // RWKV_HEAD, RWKV_HEAD, RWKV_HEAD), bf16))

    xl, gl, ur = _inproj(x2d, row(p['norm_mix_g']), p['w_in'].astype(bf16), tm=min(1024, t))

    wg = jnp.concatenate([_block_diag(p['lru_wx']), _block_diag(p['lru_wa'])], axis=1).astype(bf16)
    bg = jnp.concatenate([p['lru_bx'], p['lru_ba']]).reshape(1, -1)
    y_lru = _lru(xl, gl, p['conv_w'], row(p['conv_b']), wg, bg, row(p['lru_lambda']), bsz, s, ts=min(1024, s))

    pairs = lambda a: a.reshape(N_PAIRS, PAIR)
    rows8 = lambda rows: jnp.pad(jnp.stack(rows, axis=1), ((0, 0), (0, SUBLANES - len(rows)), (0, 0)))
    mu = p['rwkv_mu']
    shared = lambda a: jnp.broadcast_to(a, (N_PAIRS, PAIR))
    mu5 = rows8([pairs(mu[:RWKV_W]), pairs(mu[RWKV_W:2 * RWKV_W]), pairs(mu[2 * RWKV_W:3 * RWKV_W]),
                 shared(mu[3 * RWKV_W:3 * RWKV_W + PAIR]), shared(mu[3 * RWKV_W + PAIR:])])
    par5 = rows8([pairs(p['rwkv_w0']), pairs(p['rwkv_a0']), pairs(p['rwkv_k_k']), pairs(p['rwkv_k_a']),
                  pairs(p['rwkv_r_k'])])
    zl = jnp.zeros((DECAY_LORA, RWKV_W), f32)
    w_dec = jnp.concatenate([p['rwkv_w_up'], zl], axis=0).reshape(PAIR, N_PAIRS, PAIR)
    w_icl = jnp.concatenate([zl, p['rwkv_a_up']], axis=0).reshape(PAIR, N_PAIRS, PAIR)
    wproj = jnp.concatenate([w_dec, w_icl], axis=2).transpose(1, 0, 2).astype(bf16)
    gup = p['rwkv_g_up'].reshape(GATE_LORA, N_PAIRS, PAIR).transpose(1, 0, 2).astype(bf16)
    gm, sp, rc, yp, bon, v, g = _rwkv_chunk(ur, mu5, par5, wproj, gup, bsz, s, ts=min(2048, s))
    y_scan = _rwkv_state(gm, sp, rc, yp, bsz, s, nck=min(4, s // RWKV_CHUNK))

    x1 = _outproj(x2d, y_lru, y_scan, bon, v, g, row(p['rwkv_lnx_g']), row(p['rwkv_lnx_b']), ones_blk,
                  p['w_out'].astype(bf16), tm=min(1024, t))

    kmem, vmem = _memkv(mem2d, row(p['norm_mem_g']), p['xa_wk'].astype(bf16), p['xa_wv'].astype(bf16),
                        tm=min(512, bsz * mlen))
    x2, hf, route, gates, cnt_pad = _xattn(x1, kmem, vmem, row(p['norm_xa_g']), p['xa_wq'].astype(bf16),
                                           p['xa_wo'].astype(bf16), row(p['norm_ffn_g']),
                                           p['w_router'].T.astype(bf16), p['b_router'].reshape(-1, 1),
                                           bsz, s, mlen, tm=min(1024, s))

    counts = cnt_pad[:, 0].astype(i32)
    dest, pends, n_blocks = _routing(route[:TOP_K].T, route[TOP_K:].T, counts, t)
    gate_pad = jnp.pad(gates[:TOP_K].T, ((0, 0), (0, LANES - TOP_K)))
    xs = _dispatch(dest, pends, hf, n_blocks, tm=min(4096, t))
    ys = _moe(pends, xs, p['w_gu'], p['b_gu'], p['w_dn'], p['b_dn'], n_blocks)
    return _combine(dest, ys, x2, gate_pad, row(p['final_norm_g']), tc=min(256, t))


def kernel(x, mem, norm_mix_g, w_in, conv_w, conv_b, lru_wx, lru_bx, lru_wa, lru_ba, lru_lambda, rwkv_mu, rwkv_w0, rwkv_w_up, rwkv_a0, rwkv_a_up, rwkv_g_up, rwkv_k_k, rwkv_k_a, rwkv_r_k, rwkv_lnx_g, rwkv_lnx_b, w_out, norm_xa_g, norm_mem_g, xa_wq, xa_wk, xa_wv, xa_wo, norm_ffn_g, w_router, b_router, w_gu, b_gu, w_dn, b_dn, final_norm_g):
    bsz, s, d = x.shape
    mlen = mem.shape[1]
    assert d == D_MODEL and w_in.shape[0] == 1
    p = dict(norm_mix_g=norm_mix_g[0], w_in=w_in[0], conv_w=conv_w[0], conv_b=conv_b[0], lru_wx=lru_wx[0],
             lru_bx=lru_bx[0], lru_wa=lru_wa[0], lru_ba=lru_ba[0], lru_lambda=lru_lambda[0],
             rwkv_mu=rwkv_mu[0], rwkv_w0=rwkv_w0[0], rwkv_w_up=rwkv_w_up[0], rwkv_a0=rwkv_a0[0],
             rwkv_a_up=rwkv_a_up[0], rwkv_g_up=rwkv_g_up[0], rwkv_k_k=rwkv_k_k[0], rwkv_k_a=rwkv_k_a[0],
             rwkv_r_k=rwkv_r_k[0].reshape(-1), rwkv_lnx_g=rwkv_lnx_g[0], rwkv_lnx_b=rwkv_lnx_b[0],
             w_out=w_out[0], norm_xa_g=norm_xa_g[0], norm_mem_g=norm_mem_g[0], xa_wq=xa_wq[0],
             xa_wk=xa_wk[0], xa_wv=xa_wv[0], xa_wo=xa_wo[0], norm_ffn_g=norm_ffn_g[0],
             w_router=w_router[0], b_router=b_router[0], w_gu=w_gu[0], b_gu=b_gu[0], w_dn=w_dn[0],
             b_dn=b_dn[0], final_norm_g=final_norm_g)
    out = _layer(x.reshape(bsz * s, d), mem.reshape(bsz * mlen, d), bsz, s, mlen, p)
    return out.reshape(bsz, s, d)
```

```python
import jax
import jax.numpy as jnp
from jax import lax
from jax.experimental import pallas as pl
from jax.experimental.pallas import tpu as pltpu

f32 = jnp.float32
bf16 = jnp.bfloat16
i32 = jnp.int32
u32 = jnp.uint32

D_MODEL = 1024
LRU_W = 512
RWKV_W = 512
CONV_W = 4
LRU_C = 8.0
RWKV_HEAD = 64
DECAY_LORA = 64
AAA_LORA = 64
GATE_LORA = 128
RWKV_IN = 3 * RWKV_W + DECAY_LORA + AAA_LORA + GATE_LORA
XA_HEADS = 4
XA_HEAD = D_MODEL // XA_HEADS
N_EXPERTS = 32
TOP_K = 4
D_FF = D_MODEL
SWIGLU_LIMIT = 7.0
SWIGLU_ALPHA = 1.702
EPS = 1e-6
GN_EPS = 64e-5

LANES = 128
SUBLANES = 8
RWKV_CHUNK = 64
PAIR = 2 * RWKV_HEAD
N_PAIRS = RWKV_W // PAIR
MOE_BLOCK = 512
X_TILE = D_MODEL // (2 * LANES)
V7X_VMEM_BYTES = 64 * 1024 * 1024
VMEM_LIMIT = V7X_VMEM_BYTES - 12 * 1024 * 1024


def _cparams(sem):
    return pltpu.CompilerParams(dimension_semantics=sem, vmem_limit_bytes=VMEM_LIMIT)


def _rms(x, g):
    return x * lax.rsqrt(jnp.mean(x * x, axis=-1, keepdims=True) + EPS) * g


def _full(shape):
    n = len(shape)
    return pl.BlockSpec(shape, lambda *a: (0,) * n)


def _shift_rows(x, prev8, d):
    xr = pltpu.roll(x, d, 0)
    tr = pltpu.roll(prev8, d, 0)
    row = lax.broadcasted_iota(i32, prev8.shape, 0)
    head = jnp.where(row < d, tr, xr[:SUBLANES])
    return jnp.concatenate([head, xr[SUBLANES:]], axis=0)


def _bf16_parts(x, n):
    parts = []
    for _ in range(n):
        piece = x.astype(bf16)
        parts.append(piece)
        x = x - piece.astype(f32)
    return parts


def _sum_dot(x, mask_bf, n_parts, mask_left=False):
    acc = None
    for piece in _bf16_parts(x, n_parts):
        d = (jnp.dot(mask_bf, piece, preferred_element_type=f32) if mask_left
             else jnp.dot(piece, mask_bf, preferred_element_type=f32))
        acc = d if acc is None else acc + d
    return acc


def _store_packed_rows(ref, val):
    n = val.shape[0]
    half = D_MODEL // 2
    lo = lax.bitcast_convert_type(val[:, :half].astype(bf16).astype(f32), u32) >> 16
    hi = lax.bitcast_convert_type(val[:, half:].astype(bf16).astype(f32), u32) & jnp.uint32(0xFFFF0000)
    words = hi | lo
    for j in range(X_TILE):
        ref[pl.ds(j, n, stride=X_TILE), :] = words[:, j * LANES:(j + 1) * LANES]


def _load_packed_rows(ref, n, off=0):
    ws = [ref[pl.ds(off + j, n, stride=X_TILE), :] for j in range(X_TILE)]
    lo = [lax.bitcast_convert_type(w << 16, f32).astype(bf16) for w in ws]
    hi = [lax.bitcast_convert_type(w & jnp.uint32(0xFFFF0000), f32).astype(bf16) for w in ws]
    return jnp.concatenate(lo + hi, axis=1)


def _inproj_body(x_ref, g_ref, w_ref, xl_ref, gl_ref, ur_ref):
    h = _rms(x_ref[...], g_ref[...])
    u = jnp.dot(h.astype(bf16), w_ref[...], preferred_element_type=f32)
    xl_ref[...] = u[:, :LRU_W]
    gl_ref[...] = u[:, LRU_W:2 * LRU_W]
    ur_ref[...] = u[:, 2 * LRU_W:]


def _inproj(x2d, g, w_in_bf, tm):
    t = x2d.shape[0]
    return pl.pallas_call(
        _inproj_body,
        grid=(t // tm,),
        in_specs=[pl.BlockSpec((tm, D_MODEL), lambda i: (i, 0)), _full((1, D_MODEL)),
                  _full(w_in_bf.shape)],
        out_specs=[pl.BlockSpec((tm, LRU_W), lambda i: (i, 0)),
                   pl.BlockSpec((tm, LRU_W), lambda i: (i, 0)),
                   pl.BlockSpec((tm, RWKV_IN), lambda i: (i, 0))],
        out_shape=[jax.ShapeDtypeStruct((t, LRU_W), f32), jax.ShapeDtypeStruct((t, LRU_W), f32),
                   jax.ShapeDtypeStruct((t, RWKV_IN), f32)],
        compiler_params=_cparams(("parallel",)),
        name="inproj",
    )(x2d, g, w_in_bf)


def _lru_body(xl_ref, gl_ref, cw_ref, cb_ref, wg_ref, bg_ref, lam_ref, o_ref, tail_ref, h_ref):
    ts = xl_ref.shape[0]

    @pl.when(pl.program_id(1) == 0)
    def _():
        tail_ref[...] = jnp.zeros_like(tail_ref)
        h_ref[...] = jnp.zeros_like(h_ref)

    x = xl_ref[...]
    tail = tail_ref[...]
    cw = cw_ref[...]
    xc = cb_ref[...] + cw[CONV_W - 1:CONV_W] * x
    for d in range(1, CONV_W):
        xc = xc + cw[CONV_W - 1 - d:CONV_W - d] * _shift_rows(x, tail, d)
    tail_ref[...] = x[ts - SUBLANES:]

    gates = jax.nn.sigmoid(jnp.dot(xc.astype(bf16), wg_ref[...], preferred_element_type=f32) + bg_ref[...])
    gx = gates[:, :LRU_W]
    ga = gates[:, LRU_W:]
    log_a = -LRU_C * ga * jax.nn.softplus(-lam_ref[...])
    a = jnp.exp(log_a)
    b = jnp.sqrt(-jnp.tanh(log_a) * (a * a + 1.0)) * gx * xc

    row = lax.broadcasted_iota(i32, (ts, LRU_W), 0) % SUBLANES
    d = 1
    while d < SUBLANES:
        keep = row >= d
        a_s = jnp.where(keep, pltpu.roll(a, d, 0), 1.0)
        b_s = jnp.where(keep, pltpu.roll(b, d, 0), 0.0)
        b = a * b_s + b
        a = a * a_s
        d *= 2
    carry = h_ref[SUBLANES - 1:SUBLANES, :]
    groups = []
    for q in range(ts // SUBLANES):
        rows = slice(q * SUBLANES, (q + 1) * SUBLANES)
        hq = b[rows] + a[rows] * carry
        carry = hq[SUBLANES - 1:SUBLANES, :]
        groups.append(hq)
    h = jnp.concatenate(groups, axis=0)
    h_ref[...] = jnp.broadcast_to(carry, h_ref.shape)
    o_ref[...] = (h * jax.nn.gelu(gl_ref[...])).astype(o_ref.dtype)


def _lru(xl, gl, cw, cb, wg_bf, bg, lam, bsz, s, ts):
    nt = s // ts
    blk = pl.BlockSpec((ts, LRU_W), lambda b, i: (b * nt + i, 0))
    return pl.pallas_call(
        _lru_body,
        grid=(bsz, nt),
        in_specs=[blk, blk, _full(cw.shape), _full(cb.shape), _full(wg_bf.shape), _full(bg.shape),
                  _full(lam.shape)],
        out_specs=blk,
        out_shape=jax.ShapeDtypeStruct((bsz * s, LRU_W), bf16),
        scratch_shapes=[pltpu.VMEM((SUBLANES, LRU_W), f32), pltpu.VMEM((SUBLANES, LRU_W), f32)],
        compiler_params=_cparams(("parallel", "arbitrary")),
        name="lru",
    )(xl, gl, cw, cb, wg_bf, bg, lam)


def _mm_x3(a, b_parts):
    ah, al = _bf16_parts(a, 2)
    bh, bl = b_parts
    d = lambda x, y: jnp.dot(x, y, preferred_element_type=f32)
    return d(ah, bh) + (d(ah, bl) + d(al, bh))


def _mm(a, b):
    return jnp.dot(a.astype(bf16), b.astype(bf16), preferred_element_type=f32)


def _mm_nt(a, b):
    return lax.dot_general(a.astype(bf16), b.astype(bf16), (((1,), (1,)), ((), ())), preferred_element_type=f32)


def _mm_tn(a, b):
    return lax.dot_general(a.astype(bf16), b.astype(bf16), (((0,), (0,)), ((), ())), preferred_element_type=f32)


def _bd(x):
    m0 = lax.broadcasted_iota(i32, x.shape, 1) < RWKV_HEAD
    zero = jnp.zeros_like(x)
    return jnp.concatenate([jnp.where(m0, x, zero), jnp.where(m0, zero, x)], axis=0)


def _side_by_side(d):
    h = d.shape[0] // 2
    m0 = lax.broadcasted_iota(i32, (h, d.shape[1]), 1) < RWKV_HEAD
    return jnp.where(m0, d[:h], d[h:])


def _chunk_maps(chunks):
    c = RWKV_CHUNK
    ri = lax.broadcasted_iota(i32, (c, PAIR), 0)
    ji = lax.broadcasted_iota(i32, (c, PAIR), 1) % RWKV_HEAD
    strict = ji < ri
    incl = ji <= ri
    diag = ji == ri
    eye = jnp.where(diag, 1.0, 0.0).astype(f32)
    each = lambda f, *ls: [f(*xs) for xs in zip(*ls)]
    cat0 = lambda *xs: jnp.concatenate(xs, axis=0)
    cat1 = lambda *xs: jnp.concatenate(xs, axis=1)
    tb = lambda x: x.astype(bf16)

    ats, bts, kts, rts, vs, cls = [list(x) for x in zip(*chunks)]
    pcs = each(lambda cl: jnp.exp(cl[c - 1:c, :]), cls)
    bd_a = each(lambda x: tb(_bd(x)), ats)
    bd_v = each(lambda x: tb(_bd(x)), vs)

    aa = each(lambda a, r, b, k: _mm_nt(cat0(tb(a), tb(r)), cat0(tb(_bd(b)), tb(_bd(k)))), ats, rts, bts, kts)
    l_ab = each(lambda x: tb(jnp.where(strict, x[:c, :PAIR], 0.0)), aa)
    a_k = each(lambda x: tb(cat0(jnp.where(strict, x[:c, PAIR:], 0.0), jnp.where(incl, x[c:, PAIR:], 0.0))), aa)
    a_rb = each(lambda x: tb(jnp.where(incl, x[c:, :PAIR], 0.0)), aa)

    tinv = each(lambda x: eye + x, l_ab)
    lp = each(lambda x: tb(_mm(x, _bd(x))), l_ab)
    p = 2
    while 2 * p < c:
        x2 = each(lambda t, x: _mm(cat0(tb(t), x), _bd(x)), tinv, lp)
        tinv = each(lambda t, x: t + x[:c], tinv, x2)
        lp = each(lambda x: tb(x[c:]), x2)
        p *= 2
    tinv = each(lambda t, x: t + _mm(t, _bd(x)), tinv, lp)

    wy = each(_mm, a_k, bd_v)
    za = each(lambda t, w, a: _mm(t, cat1(tb(_bd(w[:c])), a)), tinv, wy, bd_a)
    zp = each(lambda x: x[:, :PAIR], za)
    ac = each(lambda x: x[:, PAIR:], za)
    y1 = each(lambda ar, z, a: _mm(ar, cat1(tb(_bd(z)), tb(_bd(a)))), a_rb, zp, ac)
    yp = each(lambda w, y: w[c:] + y[:, :PAIR], wy, y1)
    rc = each(lambda r, y: r + y[:, PAIR:], rts, y1)
    sm = each(lambda b, pc, z, a: _mm_tn(b * pc, cat1(z, a)), bts, pcs, zp, ac)
    kv = each(lambda k, pc, v: _mm_tn(k * pc, v), kts, pcs, vs)
    sp = each(lambda x, m: _side_by_side(x) + _side_by_side(m[:, :PAIR]), kv, sm)
    g = each(lambda pc, m: jnp.where(diag, jnp.broadcast_to(pc, (c, PAIR)), 0.0) + _side_by_side(m[:, PAIR:]), pcs, sm)
    return list(zip(g, sp, rc, yp))


N_UCOLS = 5


def _rwkv_chunk_body(ur_r, ur_k, ur_v, ur_lo, ur_dg, mu_ref, par_ref, wproj_ref, gup_ref, ones_ref, tri_ref,
                     g_ref, sp_ref, rc_ref, yp_ref, bon_ref, v_ref, gate_ref, prev_ref):
    ts = ur_r.shape[0]
    c = RWKV_CHUNK

    @pl.when(pl.program_id(2) == 0)
    def _():
        prev_ref[...] = jnp.zeros_like(prev_ref)

    mixed = []
    for j, ref in enumerate((ur_r, ur_k, ur_v, ur_lo, ur_dg)):
        u0 = ref[...]
        ls = slice(j * LANES, (j + 1) * LANES)
        us = _shift_rows(u0, prev_ref[:, ls], 1)
        prev_ref[:, ls] = u0[ts - SUBLANES:]
        mixed.append(u0 + (us - u0) * mu_ref[0, j:j + 1, :])
    r, k, v, lora, dg = mixed
    w0, a0, k_k, k_a, r_k = [par_ref[0, j:j + 1, :] for j in range(5)]

    lane = lax.broadcasted_iota(i32, lora.shape, 1)
    lora = jnp.where(lane < DECAY_LORA, jnp.tanh(lora), lora)
    proj = jnp.dot(lora.astype(bf16), wproj_ref[0], preferred_element_type=f32)
    w = -jax.nn.softplus(-(w0 + proj[:, :PAIR])) - 0.5
    lw = -jnp.exp(w)
    a = jax.nn.sigmoid(a0 + proj[:, PAIR:])
    gate_ref[...] = jnp.dot(jax.nn.sigmoid(dg).astype(bf16), gup_ref[0], preferred_element_type=f32)

    ones = ones_ref[...]
    kk = k * k_k
    ss = _sum_dot(kk * kk, ones, 2)
    kk = kk / jnp.maximum(jnp.sqrt(ss), 1e-12)
    k2 = k * (1.0 + (a - 1.0) * k_a)
    bon_ref[...] = _sum_dot(r * k2 * r_k, ones, 2)
    v_ref[...] = v

    tri = tri_ref[...]
    grp = tri.shape[0]
    cl = jnp.concatenate([_sum_dot(lw[q * grp:(q + 1) * grp], tri, 3, mask_left=True) for q in range(ts // grp)],
                         axis=0)
    e_neg = jnp.exp(-cl)
    at = -kk * jnp.exp(cl - lw)
    bt = kk * a * e_neg
    kt = k2 * e_neg
    rt = r * jnp.exp(cl)

    sls = [slice(j * c, (j + 1) * c) for j in range(ts // c)]
    outs = _chunk_maps([(at[sl], bt[sl], kt[sl], rt[sl], v[sl], cl[sl]) for sl in sls])
    for sl, (g, sp, rc, yp) in zip(sls, outs):
        g_ref[sl, :] = g
        sp_ref[sl, :] = sp
        rc_ref[sl, :] = rc
        yp_ref[sl, :] = yp


def _rwkv_chunk(ur, mu5, par5, wproj, gup, bsz, s, ts):
    t = bsz * s
    nt = s // ts
    col = lambda blk: pl.BlockSpec((ts, PAIR), lambda p, b, i, blk=blk: (b * nt + i, blk(p)))
    nrw = RWKV_W // PAIR
    ucols = [col(lambda p: p), col(lambda p: nrw + p), col(lambda p: 2 * nrw + p), col(lambda p: 3 * nrw),
             col(lambda p: 3 * nrw + 1)]
    per_pair = lambda a: pl.BlockSpec((1,) + a.shape[1:], lambda p, b, i: (p, 0, 0))
    ones_pair = _block_diag(jnp.ones((2, RWKV_HEAD, RWKV_HEAD), bf16))
    tri = _block_diag(jnp.tril(jnp.ones((2, RWKV_CHUNK, RWKV_CHUNK), bf16)))
    oblk = pl.BlockSpec((ts, PAIR), lambda p, b, i: (b * nt + i, p))
    osh = jax.ShapeDtypeStruct((t, RWKV_W), f32)
    return pl.pallas_call(
        _rwkv_chunk_body,
        grid=(N_PAIRS, bsz, nt),
        in_specs=ucols + [per_pair(mu5), per_pair(par5), per_pair(wproj), per_pair(gup), _full(ones_pair.shape),
                          _full(tri.shape)],
        out_specs=[oblk] * 7,
        out_shape=[osh] * 7,
        scratch_shapes=[pltpu.VMEM((SUBLANES, N_UCOLS * LANES), f32)],
        compiler_params=_cparams(("parallel", "parallel", "arbitrary")),
        name="rwkv_chunk",
    )(ur, ur, ur, ur, ur, mu5, par5, wproj, gup, ones_pair, tri)


def _rwkv_state_body(g_ref, sp_ref, rc_ref, yp_ref, y_ref, s_ref):
    @pl.when(pl.program_id(0) == 0)
    def _():
        s_ref[...] = jnp.zeros_like(s_ref)

    c = RWKV_CHUNK
    bsz = g_ref.shape[0]
    nck = g_ref.shape[1] // c
    chains = [(b, slice(p * PAIR, (p + 1) * PAIR)) for b in range(bsz) for p in range(N_PAIRS)]
    states = [s_ref[b, :, ls] for b, ls in chains]
    for j in range(nck):
        sl = slice(j * c, (j + 1) * c)
        prods = [_mm_x3(jnp.concatenate([rc_ref[b, sl, ls], g_ref[b, sl, ls]], axis=0), _bf16_parts(_bd(s), 2))
                 for (b, ls), s in zip(chains, states)]
        for (b, ls), pr in zip(chains, prods):
            y_ref[b, sl, ls] = yp_ref[b, sl, ls] + pr[:c]
        states = [pr[c:] + sp_ref[b, sl, ls] for (b, ls), pr in zip(chains, prods)]
    for (b, ls), s in zip(chains, states):
        s_ref[b, :, ls] = s


def _rwkv_state(g, sp, rc, yp, bsz, s, nck):
    rows = nck * RWKV_CHUNK
    rblk = pl.BlockSpec((bsz, rows, RWKV_W), lambda i: (0, i, 0))
    r3 = lambda a: a.reshape(bsz, s, RWKV_W)
    y = pl.pallas_call(
        _rwkv_state_body,
        grid=(s // rows,),
        in_specs=[rblk] * 4,
        out_specs=rblk,
        out_shape=jax.ShapeDtypeStruct((bsz, s, RWKV_W), f32),
        scratch_shapes=[pltpu.VMEM((bsz, RWKV_CHUNK, RWKV_W), f32)],
        compiler_params=_cparams(("arbitrary",)),
        name="rwkv_state",
    )(r3(g), r3(sp), r3(rc), r3(yp))
    return y.reshape(bsz * s, RWKV_W)


def _outproj_body(x_ref, yl_ref, ys_ref, bon_ref, v_ref, g_ref, lg_ref, lb_ref, ones_ref, w_ref, o_ref):
    y = ys_ref[...]
    ones = ones_ref[...]
    inv_n = 1.0 / RWKV_HEAD
    mean = _sum_dot(y, ones, 2) * inv_n
    yc = y - mean
    var = _sum_dot(yc * yc, ones, 2) * inv_n
    yn = yc * lax.rsqrt(var + GN_EPS) * lg_ref[...] + lb_ref[...]
    yr = (yn + bon_ref[...] * v_ref[...]) * g_ref[...]
    cat = jnp.concatenate([yl_ref[...], yr.astype(bf16)], axis=1)
    o_ref[...] = x_ref[...] + jnp.dot(cat, w_ref[...], preferred_element_type=f32)


def _outproj(x2d, y_lru, y_scan, bon, v, g, lnx_g, lnx_b, ones_blk, w_out_bf, tm):
    t = x2d.shape[0]
    xb = pl.BlockSpec((tm, D_MODEL), lambda i: (i, 0))
    hb = pl.BlockSpec((tm, RWKV_W), lambda i: (i, 0))
    return pl.pallas_call(
        _outproj_body,
        grid=(t // tm,),
        in_specs=[xb, hb, hb, hb, hb, hb, _full(lnx_g.shape), _full(lnx_b.shape), _full(ones_blk.shape),
                  _full(w_out_bf.shape)],
        out_specs=xb,
        out_shape=jax.ShapeDtypeStruct((t, D_MODEL), f32),
        compiler_params=_cparams(("parallel",)),
        name="outproj",
    )(x2d, y_lru, y_scan, bon, v, g, lnx_g, lnx_b, ones_blk, w_out_bf)


def _memkv_body(m_ref, g_ref, wk_ref, wv_ref, k_ref, v_ref):
    h = _rms(m_ref[...], g_ref[...]).astype(bf16)
    k_ref[...] = jnp.dot(h, wk_ref[...], preferred_element_type=f32).astype(bf16)
    v_ref[...] = jnp.dot(h, wv_ref[...], preferred_element_type=f32).astype(bf16)


def _memkv(mem2d, g, wk_bf, wv_bf, tm):
    t = mem2d.shape[0]
    blk = pl.BlockSpec((tm, D_MODEL), lambda i: (i, 0))
    sh = jax.ShapeDtypeStruct((t, D_MODEL), bf16)
    return pl.pallas_call(
        _memkv_body,
        grid=(t // tm,),
        in_specs=[blk, _full(g.shape), _full(wk_bf.shape), _full(wv_bf.shape)],
        out_specs=[blk, blk],
        out_shape=[sh, sh],
        compiler_params=_cparams(("parallel",)),
        name="memkv",
    )(mem2d, g, wk_bf, wv_bf)


def _xattn_body(x_ref, k_ref, v_ref, gx_ref, wq_ref, wo_ref, gf_ref, wr_ref, br_ref, upper_ref,
                x2_ref, hf_ref, idx_ref, gate_ref, cnt_ref, base_ref):
    x = x_ref[...]
    h = _rms(x, gx_ref[...]).astype(bf16)
    q = jnp.dot(h, wq_ref[...], preferred_element_type=f32).astype(bf16)
    k = k_ref[...]
    v = v_ref[...]
    heads = [slice(hd * XA_HEAD, (hd + 1) * XA_HEAD) for hd in range(XA_HEADS)]
    scs = [lax.dot_general(q[:, sl], k[:, sl], (((1,), (1,)), ((), ())), preferred_element_type=f32)
           * (XA_HEAD ** -0.5) for sl in heads]
    ps = []
    for sc in scs:
        e = jnp.exp(sc - jnp.max(sc, axis=-1, keepdims=True))
        ps.append((e / jnp.sum(e, axis=-1, keepdims=True)).astype(bf16))
    o = jnp.concatenate([jnp.dot(p, v[:, sl], preferred_element_type=f32).astype(bf16)
                         for p, sl in zip(ps, heads)], axis=1)
    x2 = x + jnp.dot(o, wo_ref[...], preferred_element_type=f32)
    x2_ref[...] = x2

    hf = _rms(x2, gf_ref[...])
    _store_packed_rows(hf_ref, hf)
    logits = lax.dot_general(wr_ref[...], hf.astype(bf16), (((1,), (1,)), ((), ())),
                             preferred_element_type=f32) + br_ref[...]
    erow = lax.broadcasted_iota(i32, logits.shape, 0)
    neg = jnp.float32(-jnp.inf)
    cur = logits
    vals = []
    idxs = []
    for _ in range(TOP_K):
        m = jnp.max(cur, axis=0, keepdims=True)
        am = jnp.min(jnp.where(cur == m, erow, N_EXPERTS), axis=0, keepdims=True)
        vals.append(m)
        idxs.append(am)
        cur = jnp.where(erow == am, neg, cur)
    es = [jnp.exp(vk - vals[0]) for vk in vals]
    den = es[0] + es[1] + es[2] + es[3]

    @pl.when(pl.program_id(0) == 0)
    def _():
        base_ref[...] = jnp.zeros_like(base_ref)

    onehot = [jnp.where(erow == am, 1.0, 0.0) for am in idxs]
    cnt = (onehot[0] + onehot[1]) + (onehot[2] + onehot[3])
    base = base_ref[:, 0:1]
    prior = jnp.dot(cnt.astype(bf16), upper_ref[...], preferred_element_type=f32) + base
    base_ref[...] = jnp.broadcast_to(base + jnp.sum(cnt, axis=1, keepdims=True), base_ref.shape)
    cnt_ref[...] = base_ref[...]

    orow = lax.broadcasted_iota(i32, idx_ref.shape, 0)
    idx_out = jnp.zeros(idx_ref.shape, i32)
    gate_out = jnp.zeros(gate_ref.shape, f32)
    for kq in range(TOP_K):
        rank = jnp.sum(prior * onehot[kq], axis=0, keepdims=True).astype(i32)
        idx_out = jnp.where(orow == kq, idxs[kq], idx_out)
        idx_out = jnp.where(orow == TOP_K + kq, rank, idx_out)
        gate_out = jnp.where(orow == kq, es[kq] / den, gate_out)
    idx_ref[...] = idx_out
    gate_ref[...] = gate_out


def _xattn(x1, kmem, vmem, g_xa, wq_bf, wo_bf, g_ffn, wr_t, br_col, bsz, s, mlen, tm):
    t = bsz * s
    nt = s // tm
    xb = pl.BlockSpec((tm, D_MODEL), lambda i: (i, 0))
    mb = pl.BlockSpec((mlen, D_MODEL), lambda i: (i // nt, 0))
    lb = pl.BlockSpec((2 * TOP_K, tm), lambda i: (0, i))
    upper = jnp.triu(jnp.ones((tm, tm), bf16), k=1)
    return pl.pallas_call(
        _xattn_body,
        grid=(t // tm,),
        in_specs=[xb, mb, mb, _full(g_xa.shape), _full(wq_bf.shape), _full(wo_bf.shape),
                  _full(g_ffn.shape), _full(wr_t.shape), _full(br_col.shape), _full(upper.shape)],
        out_specs=[xb, pl.BlockSpec((tm * X_TILE, LANES), lambda i: (i, 0)), lb, lb, _full((N_EXPERTS, LANES))],
        out_shape=[jax.ShapeDtypeStruct((t, D_MODEL), f32), jax.ShapeDtypeStruct((t * X_TILE, LANES), u32),
                   jax.ShapeDtypeStruct((2 * TOP_K, t), i32), jax.ShapeDtypeStruct((2 * TOP_K, t), f32),
                   jax.ShapeDtypeStruct((N_EXPERTS, LANES), f32)],
        scratch_shapes=[pltpu.VMEM((N_EXPERTS, LANES), f32)],
        compiler_params=_cparams(("arbitrary",)),
        name="xattn",
    )(x1, kmem, vmem, g_xa, wq_bf, wo_bf, g_ffn, wr_t, br_col, upper)


DMA_UNROLL = 8


def _tile_at(ref, row):
    return ref.at[pl.ds(pl.multiple_of(row * X_TILE, X_TILE), X_TILE), :]


def _dispatch_body(dest_ref, pend_ref, hf_ref, xs_hbm, zeros_ref, sem, zsem):
    i = pl.program_id(0)
    tm = hf_ref.shape[0] // X_TILE
    zrows = MOE_BLOCK * X_TILE

    def zero_copy(e):
        start = pl.multiple_of((pend_ref[e] - MOE_BLOCK) * X_TILE, X_TILE)
        return pltpu.make_async_copy(zeros_ref, xs_hbm.at[pl.ds(start, zrows), :], zsem)

    def nonempty(e):
        return pend_ref[e] > (pend_ref[e - 1] if e else 0)

    @pl.when(i == 0)
    def _():
        zeros_ref[...] = jnp.zeros_like(zeros_ref)
        for e in range(N_EXPERTS):
            @pl.when(nonempty(e))
            def _():
                zero_copy(e).start()
        for e in range(N_EXPERTS):
            @pl.when(nonempty(e))
            def _():
                zero_copy(e).wait()

        def tail_copy(b):
            return pltpu.make_async_copy(zeros_ref, xs_hbm.at[pl.ds(pl.multiple_of(b * zrows, zrows), zrows), :], zsem)

        def tail_start(b, carry):
            tail_copy(b).start()
            return carry

        def tail_wait(b, carry):
            tail_copy(b).wait()
            return carry
        n_used = pend_ref[N_EXPERTS - 1] // MOE_BLOCK
        n_all = xs_hbm.shape[0] // zrows
        lax.fori_loop(n_used, n_all, tail_start, 0)
        lax.fori_loop(n_used, n_all, tail_wait, 0)

    def body(q, carry):
        for u in range(DMA_UNROLL):
            r = q * DMA_UNROLL + u
            src = _tile_at(hf_ref, r)
            for kq in range(TOP_K):
                pltpu.make_async_copy(src, _tile_at(xs_hbm, dest_ref[(i * tm + r) * TOP_K + kq]),
                                      sem).start(priority=kq % 2)
        return carry
    lax.fori_loop(0, tm // DMA_UNROLL, body, 0)
    for kq in range(TOP_K):
        pltpu.make_async_copy(hf_ref, xs_hbm.at[pl.ds(0, tm * X_TILE), :], sem).wait()


def _dispatch(dest_flat, pends, hf_tiles, n_blocks, tm):
    t = hf_tiles.shape[0] // X_TILE
    rows = n_blocks * MOE_BLOCK
    grid_spec = pltpu.PrefetchScalarGridSpec(
        num_scalar_prefetch=2,
        grid=(t // tm,),
        in_specs=[pl.BlockSpec((tm * X_TILE, LANES), lambda i, d, pe: (i, 0))],
        out_specs=pl.BlockSpec(memory_space=pl.ANY),
        scratch_shapes=[pltpu.VMEM((MOE_BLOCK * X_TILE, LANES), u32), pltpu.SemaphoreType.DMA(()),
                        pltpu.SemaphoreType.DMA(())],
    )
    return pl.pallas_call(
        _dispatch_body,
        grid_spec=grid_spec,
        out_shape=jax.ShapeDtypeStruct((rows * X_TILE, LANES), u32),
        compiler_params=_cparams(("arbitrary",)),
        name="dispatch",
    )(dest_flat, pends, hf_tiles)


def _moe_body(pend_ref, xs_hbm, wgu_ref, bgu_ref, wdn_ref, bdn_ref, ys_hbm, xbuf, ybuf, wgu_bf, wdn_bf,
              sem_in, sem_out):
    e = pl.program_id(0)
    blk = MOE_BLOCK * X_TILE
    end_blk = pend_ref[e] // MOE_BLOCK
    start_blk = jnp.where(e == 0, 0, pend_ref[jnp.maximum(e - 1, 0)] // MOE_BLOCK)
    nb = end_blk - start_blk

    n_total = pend_ref[N_EXPERTS - 1] // MOE_BLOCK

    def rows_of(b):
        return pl.ds(pl.multiple_of(b * blk, blk), blk)

    def x_copy(b):
        return pltpu.make_async_copy(xs_hbm.at[rows_of(b), :], xbuf.at[b % 2], sem_in.at[b % 2])

    def y_copy(b):
        return pltpu.make_async_copy(ybuf.at[b % 2], ys_hbm.at[rows_of(b), :], sem_out.at[b % 2])

    @pl.when((e == 0) & (n_total > 0))
    def _():
        x_copy(0).start()

    @pl.when(nb > 0)
    def _():
        wgu_bf[...] = wgu_ref[0].astype(bf16)
        wdn_bf[...] = wdn_ref[0].astype(bf16)

    def block(j, carry):
        b = start_blk + j
        slot = b % 2

        @pl.when(b + 1 < n_total)
        def _():
            x_copy(b + 1).start()

        x_copy(b).wait()

        @pl.when(b >= 2)
        def _():
            y_copy(b - 2).wait()

        xb = _load_packed_rows(xbuf.at[slot], MOE_BLOCK)
        half = D_FF // 2
        y = bdn_ref[0]
        for h in range(2):
            gc = slice(h * half, (h + 1) * half)
            uc = slice(D_FF + h * half, D_FF + (h + 1) * half)
            gate = jnp.dot(xb, wgu_bf[:, gc], preferred_element_type=f32) + bgu_ref[0, :, gc]
            up = jnp.dot(xb, wgu_bf[:, uc], preferred_element_type=f32) + bgu_ref[0, :, uc]
            gate = jnp.minimum(gate, SWIGLU_LIMIT)
            up = jnp.clip(up, -SWIGLU_LIMIT, SWIGLU_LIMIT)
            act = (up + 1.0) * (gate * jax.nn.sigmoid(SWIGLU_ALPHA * gate))
            y = y + jnp.dot(act.astype(bf16), wdn_bf[gc, :], preferred_element_type=f32)
        _store_packed_rows(ybuf.at[slot], y)
        y_copy(b).start()
        return carry

    lax.fori_loop(0, nb, block, 0)

    @pl.when(e == N_EXPERTS - 1)
    def _():
        @pl.when(n_total >= 2)
        def _():
            y_copy(n_total - 2).wait()

        @pl.when(n_total >= 1)
        def _():
            y_copy(n_total - 1).wait()

        n_all = ys_hbm.shape[0] // blk
        ybuf[0] = jnp.zeros(ybuf.shape[1:], ybuf.dtype)

        def tail_copy(b):
            return pltpu.make_async_copy(ybuf.at[0], ys_hbm.at[pl.ds(pl.multiple_of(b * blk, blk), blk), :],
                                         sem_out.at[0])

        def tail_start(b, carry):
            tail_copy(b).start()
            return carry

        def tail_wait(b, carry):
            tail_copy(b).wait()
            return carry
        lax.fori_loop(end_blk, n_all, tail_start, 0)
        lax.fori_loop(end_blk, n_all, tail_wait, 0)


def _moe(pends, xs_tiles, w_gu, b_gu, w_dn, b_dn, n_blocks):
    rows = n_blocks * MOE_BLOCK
    grid_spec = pltpu.PrefetchScalarGridSpec(
        num_scalar_prefetch=1,
        grid=(N_EXPERTS,),
        in_specs=[
            pl.BlockSpec(memory_space=pl.ANY),
            pl.BlockSpec((1, D_MODEL, 2 * D_FF), lambda e, pe: (e, 0, 0)),
            pl.BlockSpec((1, 1, 2 * D_FF), lambda e, pe: (e, 0, 0)),
            pl.BlockSpec((1, D_FF, D_MODEL), lambda e, pe: (e, 0, 0)),
            pl.BlockSpec((1, 1, D_MODEL), lambda e, pe: (e, 0, 0)),
        ],
        out_specs=pl.BlockSpec(memory_space=pl.ANY),
        scratch_shapes=[pltpu.VMEM((2, MOE_BLOCK * X_TILE, LANES), u32),
                        pltpu.VMEM((2, MOE_BLOCK * X_TILE, LANES), u32),
                        pltpu.VMEM((D_MODEL, 2 * D_FF), bf16), pltpu.VMEM((D_FF, D_MODEL), bf16),
                        pltpu.SemaphoreType.DMA((2,)), pltpu.SemaphoreType.DMA((2,))],
    )
    return pl.pallas_call(
        _moe_body,
        grid_spec=grid_spec,
        out_shape=jax.ShapeDtypeStruct((rows * X_TILE, LANES), u32),
        compiler_params=_cparams(("arbitrary",)),
        name="moe",
    )(pends, xs_tiles, w_gu, b_gu.reshape(N_EXPERTS, 1, 2 * D_FF), w_dn, b_dn.reshape(N_EXPERTS, 1, D_MODEL))


def _combine_body(pos_ref, ys_hbm, x_ref, gate_ref, g_ref, o_ref, buf, sem):
    i = pl.program_id(0)
    n = pl.num_programs(0)
    tc = x_ref.shape[0]
    slot = i % 2
    slot_rows = TOP_K * tc * X_TILE

    def start(step, sl):
        def body(q, carry):
            for u in range(DMA_UNROLL):
                r = q * DMA_UNROLL + u
                for kq in range(TOP_K):
                    dst = _tile_at(buf, (sl * TOP_K + kq) * tc + r)
                    pltpu.make_async_copy(_tile_at(ys_hbm, pos_ref[(step * tc + r) * TOP_K + kq]), dst,
                                          sem.at[sl]).start(priority=kq % 2)
            return carry
        lax.fori_loop(0, tc // DMA_UNROLL, body, 0)

    @pl.when(i == 0)
    def _():
        start(0, 0)

    @pl.when(i + 1 < n)
    def _():
        start(i + 1, 1 - slot)

    off = pl.multiple_of(slot * slot_rows, slot_rows)
    pltpu.make_async_copy(ys_hbm.at[pl.ds(0, slot_rows), :], buf.at[pl.ds(off, slot_rows), :], sem.at[slot]).wait()
    gates = gate_ref[...]
    acc = x_ref[...]
    for kq in range(TOP_K):
        rows = _load_packed_rows(buf, tc, off + kq * tc * X_TILE)
        acc = acc + gates[:, kq:kq + 1] * rows.astype(f32)
    o_ref[...] = _rms(acc, g_ref[...])


def _combine(pos_flat, ys_tiles, x2, gate_pad, g_final, tc):
    t = x2.shape[0]
    grid_spec = pltpu.PrefetchScalarGridSpec(
        num_scalar_prefetch=1,
        grid=(t // tc,),
        in_specs=[pl.BlockSpec(memory_space=pl.ANY),
                  pl.BlockSpec((tc, D_MODEL), lambda i, p: (i, 0)),
                  pl.BlockSpec((tc, LANES), lambda i, p: (i, 0)),
                  pl.BlockSpec((1, D_MODEL), lambda i, p: (0, 0))],
        out_specs=pl.BlockSpec((tc, D_MODEL), lambda i, p: (i, 0)),
        scratch_shapes=[pltpu.VMEM((2 * TOP_K * tc * X_TILE, LANES), u32), pltpu.SemaphoreType.DMA((2,))],
    )
    return pl.pallas_call(
        _combine_body,
        grid_spec=grid_spec,
        out_shape=jax.ShapeDtypeStruct((t, D_MODEL), f32),
        compiler_params=_cparams(("arbitrary",)),
        name="combine",
    )(pos_flat, ys_tiles, x2, gate_pad, g_final)


def _routing(top_idx, rank, counts, t):
    n_assign = t * TOP_K
    experts = jnp.arange(N_EXPERTS, dtype=i32)
    padded = (counts + MOE_BLOCK - 1) // MOE_BLOCK * MOE_BLOCK
    pends = jnp.cumsum(padded).astype(i32)
    pstarts = pends - padded
    start_of = jnp.sum(jnp.where(top_idx[:, :, None] == experts, pstarts, 0), axis=-1)
    dest = (start_of + rank).astype(i32).reshape(n_assign)
    n_blocks = (n_assign + N_EXPERTS * (MOE_BLOCK - 1) + MOE_BLOCK - 1) // MOE_BLOCK
    return dest, pends, n_blocks


def _block_diag(w):
    n, bi, bj = w.shape
    eye = jnp.eye(n, dtype=w.dtype)
    return jnp.einsum('nij,nm->nimj', w, eye).reshape(n * bi, n * bj)


def _layer(x2d, mem2d, bsz, s, mlen, p):
    t = bsz * s
    row = lambda a: a.reshape(1, -1)
    ones_blk = _block_diag(jnp.ones((RWKV_W // RWKV_HEAD, RWKV_HEAD, RWKV_HEAD), bf16))

    xl, gl, ur = _inproj(x2d, row(p['norm_mix_g']), p['w_in'].astype(bf16), tm=min(1024, t))

    wg = jnp.concatenate([_block_diag(p['lru_wx']), _block_diag(p['lru_wa'])], axis=1).astype(bf16)
    bg = jnp.concatenate([p['lru_bx'], p['lru_ba']]).reshape(1, -1)
    y_lru = _lru(xl, gl, p['conv_w'], row(p['conv_b']), wg, bg, row(p['lru_lambda']), bsz, s, ts=min(1024, s))

    pairs = lambda a: a.reshape(N_PAIRS, PAIR)
    rows8 = lambda rows: jnp.pad(jnp.stack(rows, axis=1), ((0, 0), (0, SUBLANES - len(rows)), (0, 0)))
    mu = p['rwkv_mu']
    shared = lambda a: jnp.broadcast_to(a, (N_PAIRS, PAIR))
    mu5 = rows8([pairs(mu[:RWKV_W]), pairs(mu[RWKV_W:2 * RWKV_W]), pairs(mu[2 * RWKV_W:3 * RWKV_W]),
                 shared(mu[3 * RWKV_W:3 * RWKV_W + PAIR]), shared(mu[3 * RWKV_W + PAIR:])])
    par5 = rows8([pairs(p['rwkv_w0']), pairs(p['rwkv_a0']), pairs(p['rwkv_k_k']), pairs(p['rwkv_k_a']),
                  pairs(p['rwkv_r_k'])])
    zl = jnp.zeros((DECAY_LORA, RWKV_W), f32)
    w_dec = jnp.concatenate([p['rwkv_w_up'], zl], axis=0).reshape(PAIR, N_PAIRS, PAIR)
    w_icl = jnp.concatenate([zl, p['rwkv_a_up']], axis=0).reshape(PAIR, N_PAIRS, PAIR)
    wproj = jnp.concatenate([w_dec, w_icl], axis=2).transpose(1, 0, 2).astype(bf16)
    gup = p['rwkv_g_up'].reshape(GATE_LORA, N_PAIRS, PAIR).transpose(1, 0, 2).astype(bf16)
    gm, sp, rc, yp, bon, v, g = _rwkv_chunk(ur, mu5, par5, wproj, gup, bsz, s, ts=min(2048, s))
    y_scan = _rwkv_state(gm, sp, rc, yp, bsz, s, nck=min(4, s // RWKV_CHUNK))

    x1 = _outproj(x2d, y_lru, y_scan, bon, v, g, row(p['rwkv_lnx_g']), row(p['rwkv_lnx_b']), ones_blk,
                  p['w_out'].astype(bf16), tm=min(1024, t))

    kmem, vmem = _memkv(mem2d, row(p['norm_mem_g']), p['xa_wk'].astype(bf16), p['xa_wv'].astype(bf16),
                        tm=min(512, bsz * mlen))
    x2, hf, route, gates, cnt_pad = _xattn(x1, kmem, vmem, row(p['norm_xa_g']), p['xa_wq'].astype(bf16),
                                           p['xa_wo'].astype(bf16), row(p['norm_ffn_g']),
                                           p['w_router'].T.astype(bf16), p['b_router'].reshape(-1, 1),
                                           bsz, s, mlen, tm=min(1024, s))

    counts = cnt_pad[:, 0].astype(i32)
    dest, pends, n_blocks = _routing(route[:TOP_K].T, route[TOP_K:].T, counts, t)
    gate_pad = jnp.pad(gates[:TOP_K].T, ((0, 0), (0, LANES - TOP_K)))
    xs = _dispatch(dest, pends, hf, n_blocks, tm=min(4096, t))
    ys = _moe(pends, xs, p['w_gu'], p['b_gu'], p['w_dn'], p['b_dn'], n_blocks)
    return _combine(dest, ys, x2, gate_pad, row(p['final_norm_g']), tc=min(256, t))


def kernel(x, mem, norm_mix_g, w_in, conv_w, conv_b, lru_wx, lru_bx, lru_wa, lru_ba, lru_lambda, rwkv_mu, rwkv_w0, rwkv_w_up, rwkv_a0, rwkv_a_up, rwkv_g_up, rwkv_k_k, rwkv_k_a, rwkv_r_k, rwkv_lnx_g, rwkv_lnx_b, w_out, norm_xa_g, norm_mem_g, xa_wq, xa_wk, xa_wv, xa_wo, norm_ffn_g, w_router, b_router, w_gu, b_gu, w_dn, b_dn, final_norm_g):
    bsz, s, d = x.shape
    mlen = mem.shape[1]
    assert d == D_MODEL and w_in.shape[0] == 1
    p = dict(norm_mix_g=norm_mix_g[0], w_in=w_in[0], conv_w=conv_w[0], conv_b=conv_b[0], lru_wx=lru_wx[0],
             lru_bx=lru_bx[0], lru_wa=lru_wa[0], lru_ba=lru_ba[0], lru_lambda=lru_lambda[0],
             rwkv_mu=rwkv_mu[0], rwkv_w0=rwkv_w0[0], rwkv_w_up=rwkv_w_up[0], rwkv_a0=rwkv_a0[0],
             rwkv_a_up=rwkv_a_up[0], rwkv_g_up=rwkv_g_up[0], rwkv_k_k=rwkv_k_k[0], rwkv_k_a=rwkv_k_a[0],
             rwkv_r_k=rwkv_r_k[0].reshape(-1), rwkv_lnx_g=rwkv_lnx_g[0], rwkv_lnx_b=rwkv_lnx_b[0],
             w_out=w_out[0], norm_xa_g=norm_xa_g[0], norm_mem_g=norm_mem_g[0], xa_wq=xa_wq[0],
             xa_wk=xa_wk[0], xa_wv=xa_wv[0], xa_wo=xa_wo[0], norm_ffn_g=norm_ffn_g[0],
             w_router=w_router[0], b_router=b_router[0], w_gu=w_gu[0], b_gu=b_gu[0], w_dn=w_dn[0],
             b_dn=b_dn[0], final_norm_g=final_norm_g)
    out = _layer(x.reshape(bsz * s, d), mem.reshape(bsz * mlen, d), bsz, s, mlen, p)
    return out.reshape(bsz, s, d)
```
